```python
import math
import jax, jax.numpy as jnp
from jax import lax
import numpy as np

D_MODEL = 1024
BATCH = 4
SEQ = 8192
DEPTH = 4

GRID_W = 64
CTX_LEN = 256
N_MOD = 6
MLP_HIDDEN = 4 * D_MODEL
NORM_EPS = 1e-6

S5_WIDTH = D_MODEL // 2
S5_GROUP_CH = 16
S5_GROUPS = S5_WIDTH // S5_GROUP_CH
S5_STATE = 64

M2_INNER = D_MODEL
M2_HEAD_DIM = 64
M2_HEADS = M2_INNER // M2_HEAD_DIM
M2_GROUPS = 4
M2_STATE = 128
M2_BC = M2_GROUPS * M2_STATE
M2_XBC = M2_INNER + 2 * M2_BC
M2_CONV = 3
SSD_CHUNK = 128

EVEN_IN = S5_WIDTH + M2_INNER + M2_XBC + 2 * M2_HEADS
EVEN_MIX = S5_WIDTH + M2_INNER

HY_WIDTH = D_MODEL
HY_ORDER = 2
HY_CONV = 3
HY_BANDS = 16
HY_EMB = 2 * HY_BANDS + 1
HY_HIDDEN = 64
HY_MIN_DECAY = math.log(1e-2) / 1.5
HY_MAX_DECAY = math.log(1e-2) / 0.3

N_EVEN = (DEPTH + 1) // 2
N_ODD = DEPTH // 2

kernel_name = 'hybrid_s5_ssd_hyena_dit_block'


def rmsnorm(x, g):
    xf = x.astype(jnp.float32)
    y = xf * lax.rsqrt(jnp.mean(xf * xf, axis=-1, keepdims=True) + NORM_EPS)
    return (y * g.astype(jnp.float32)).astype(x.dtype)


def adaln(x, g, shift, scale):
    return rmsnorm(x, g) * (1 + scale) + shift


def sq_relu_mlp(h, w1, w2):
    return jnp.square(jax.nn.relu(h @ w1)) @ w2


def dwconv_centred(u, w, b):
    k = w.shape[0]
    y = lax.conv_general_dilated(u, w[:, None, :].astype(u.dtype), window_strides=(1,),
                                 padding=[(k // 2, k // 2)], dimension_numbers=('NWC', 'WIO', 'NWC'),
                                 feature_group_count=u.shape[-1])
    return y + b.astype(u.dtype)


def grid_sincos(n, dm):
    rows = n // GRID_W
    quarter = dm // 4
    omega = 1.0 / (10000.0 ** (jnp.arange(quarter, dtype=jnp.float32) / quarter))
    ang_r = jnp.arange(rows, dtype=jnp.float32)[:, None] * omega
    ang_c = jnp.arange(GRID_W, dtype=jnp.float32)[:, None] * omega
    emb_r = jnp.concatenate([jnp.sin(ang_r), jnp.cos(ang_r)], axis=-1)
    emb_c = jnp.concatenate([jnp.sin(ang_c), jnp.cos(ang_c)], axis=-1)
    half = emb_r.shape[-1]
    pe = jnp.concatenate([jnp.broadcast_to(emb_r[:, None, :], (rows, GRID_W, half)),
                          jnp.broadcast_to(emb_c[None, :, :], (rows, GRID_W, half))], axis=-1)
    return pe.reshape(rows * GRID_W, 2 * half)


def _complex_affine_combine(e1, e2):
    a1r, a1i, b1r, b1i = e1
    a2r, a2i, b2r, b2i = e2
    return (a2r * a1r - a2i * a1i,
            a2r * a1i + a2i * a1r,
            a2r * b1r - a2i * b1i + b2r,
            a2r * b1i + a2i * b1r + b2i)


def s5_direction(u, lam_re, lam_im, log_dt, b_re, b_im, c_re, c_im, h0):
    f32 = jnp.float32
    lam_re = lam_re.astype(f32)
    lam_im = lam_im.astype(f32)
    b_re = b_re.astype(f32)
    b_im = b_im.astype(f32)
    c_re = c_re.astype(f32)
    c_im = c_im.astype(f32)
    step = jnp.exp(log_dt.astype(f32))[:, None]
    mag = jnp.exp(lam_re * step)
    ar = mag * jnp.cos(lam_im * step)
    ai = mag * jnp.sin(lam_im * step)
    den = lam_re * lam_re + lam_im * lam_im
    fr = ((ar - 1.0) * lam_re + ai * lam_im) / den
    fi = (ai * lam_re - (ar - 1.0) * lam_im) / den
    bbr = fr[..., None] * b_re - fi[..., None] * b_im
    bbi = fr[..., None] * b_im + fi[..., None] * b_re
    bu_r = jnp.einsum('gpk,blgk->blgp', bbr, u)
    bu_i = jnp.einsum('gpk,blgk->blgp', bbi, u)
    l = u.shape[1]
    a_r = jnp.broadcast_to(ar[None, None], (1, l) + ar.shape)
    a_i = jnp.broadcast_to(ai[None, None], (1, l) + ai.shape)
    cum_r, cum_i, h_r, h_i = lax.associative_scan(_complex_affine_combine, (a_r, a_i, bu_r, bu_i), axis=1)
    h0_r = h0[0][:, None]
    h0_i = h0[1][:, None]
    h_r, h_i = h_r + cum_r * h0_r - cum_i * h0_i, h_i + cum_r * h0_i + cum_i * h0_r
    y = jnp.einsum('gkp,blgp->blgk', c_re, h_r) - jnp.einsum('gkp,blgp->blgk', c_im, h_i)
    return y, (h_r[:, -1], h_i[:, -1])


def s5_mixer(u, lam_re, lam_im, log_dt, b_re, b_im, c_re, c_im, d_skip, glu_w, glu_b, h0_f, h0_b):
    f32 = jnp.float32
    bsz, l, _ = u.shape
    uf = u.astype(f32)
    ug = uf.reshape(bsz, l, S5_GROUPS, S5_GROUP_CH)
    y_f, s_f = s5_direction(ug, lam_re[0], lam_im[0], log_dt[0], b_re, b_im, c_re, c_im, h0_f)
    y_b, s_b = s5_direction(ug[:, ::-1], lam_re[1], lam_im[1], log_dt[1], b_re, b_im, c_re, c_im, h0_b)
    y = (y_f + y_b[:, ::-1]).reshape(bsz, l, S5_WIDTH) + d_skip.astype(f32) * uf
    y = jax.nn.gelu(y)
    y = y * jax.nn.sigmoid(y @ glu_w.astype(f32) + glu_b.astype(f32))
    return y.astype(u.dtype), s_f, s_b


def segsum(a):
    cs = jnp.cumsum(a, axis=-1)
    d = cs[..., :, None] - cs[..., None, :]
    t = a.shape[-1]
    mask = jnp.tril(jnp.ones((t, t), dtype=bool))
    return jnp.where(mask, d, -jnp.inf)


def ssd_scan(x, dt, a_coef, bm, cm, h0):
    bsz, l, nh, hp = x.shape
    ng, ns = bm.shape[2], bm.shape[3]
    nr = nh // ng
    nc = l // SSD_CHUNK
    xc = x.reshape(bsz, nc, SSD_CHUNK, ng, nr, hp)
    dtc = dt.reshape(bsz, nc, SSD_CHUNK, ng, nr)
    bc = bm.reshape(bsz, nc, SSD_CHUNK, ng, ns)
    cc = cm.reshape(bsz, nc, SSD_CHUNK, ng, ns)
    a = jnp.transpose(dtc * a_coef.reshape(ng, nr), (0, 3, 4, 1, 2))
    a_cum = jnp.cumsum(a, axis=-1)
    xdt = xc * dtc[..., None]
    lmat = jnp.exp(segsum(a))
    cb = jnp.einsum('bcqgn,bcsgn->bgcqs', cc, bc)
    y_diag = jnp.einsum('bgcqs,bgrcqs,bcsgrp->bcqgrp', cb, lmat, xdt)
    decay_states = jnp.exp(a_cum[..., -1:] - a_cum)
    states = jnp.einsum('bcqgn,bgrcq,bcqgrp->bcgrpn', bc, decay_states, xdt)
    states = jnp.concatenate([h0.reshape(bsz, 1, ng, nr, hp, ns), states], axis=1)
    chunk_a = jnp.pad(a_cum[..., -1], ((0, 0), (0, 0), (0, 0), (1, 0)))
    decay_chunk = jnp.exp(segsum(chunk_a))
    new_states = jnp.einsum('bgrzc,bcgrpn->bzgrpn', decay_chunk, states)
    prev_states = new_states[:, :-1]
    final = new_states[:, -1]
    y_off = jnp.einsum('bcqgn,bcgrpn,bgrcq->bcqgrp', cc, prev_states, jnp.exp(a_cum))
    y = (y_diag + y_off).reshape(bsz, l, nh, hp)
    return y, final.reshape(bsz, nh, hp, ns)


def mamba2_mixer(z, xbc, dt_raw, conv_w, conv_b, dt_bias, a_log, d_skip, norm_g, h0_f, h0_b):
    f32 = jnp.float32
    bsz, l, _ = z.shape
    xbc = jax.nn.silu(dwconv_centred(xbc, conv_w, conv_b)).astype(f32)
    xs, bm, cm = jnp.split(xbc, [M2_INNER, M2_INNER + M2_BC], axis=-1)
    xs = xs.reshape(bsz, l, M2_HEADS, M2_HEAD_DIM)
    bm = bm.reshape(bsz, l, M2_GROUPS, M2_STATE)
    cm = cm.reshape(bsz, l, M2_GROUPS, M2_STATE)
    dt = jax.nn.softplus(dt_raw.astype(f32).reshape(bsz, l, 2, M2_HEADS) + dt_bias.astype(f32))
    a_coef = -jnp.exp(a_log.astype(f32))
    y_f, h_f = ssd_scan(xs, dt[:, :, 0], a_coef[0], bm, cm, h0_f)
    y_b, h_b = ssd_scan(xs[:, ::-1], dt[:, ::-1, 1], a_coef[1], bm[:, ::-1], cm[:, ::-1], h0_b)
    y = y_f + y_b[:, ::-1] + d_skip.astype(f32)[:, None] * xs
    y = y.reshape(bsz, l, M2_INNER) * jax.nn.silu(z.astype(f32))
    y = rmsnorm(y, norm_g)
    return y.astype(z.dtype), h_f, h_b


def even_mixer(h, in_w, out_w, lam_re, lam_im, log_dt, b_re, b_im, c_re, c_im, s5_d, glu_w, glu_b,
               conv_w, conv_b, dt_bias, a_log, m_d, m_norm_g, s5_h0_f, s5_h0_b, m_h0_f, m_h0_b):
    proj = h @ in_w
    u, z, xbc, dt_raw = jnp.split(proj, [S5_WIDTH, S5_WIDTH + M2_INNER, S5_WIDTH + M2_INNER + M2_XBC], axis=-1)
    y_s5, s5_f, s5_b = s5_mixer(u, lam_re, lam_im, log_dt, b_re, b_im, c_re, c_im, s5_d, glu_w, glu_b,
                                s5_h0_f, s5_h0_b)
    y_m2, m_f, m_b = mamba2_mixer(z, xbc, dt_raw, conv_w, conv_b, dt_bias, a_log, m_d, m_norm_g, m_h0_f, m_h0_b)
    out = jnp.concatenate([y_s5, y_m2], axis=-1) @ out_w
    return out, (s5_f, s5_b, m_f, m_b)


def hyena_filters(l, f_w1, f_b1, f_freq1, f_w2, f_b2, f_freq2, f_w3):
    f32 = jnp.float32
    t = jnp.linspace(0.0, 1.0, l, dtype=f32)[:, None]
    bands = jnp.linspace(1e-4, HY_BANDS - 1, HY_BANDS, dtype=f32)
    ang = (2.0 * math.pi / l) * jnp.arange(l, dtype=f32)[:, None] * bands
    feats = jnp.concatenate([t, jnp.cos(ang), -jnp.sin(ang)], axis=-1)
    hid = jnp.sin(f_freq1.astype(f32) * (feats @ f_w1.astype(f32) + f_b1.astype(f32)))
    hid = jnp.sin(f_freq2.astype(f32) * (hid @ f_w2.astype(f32) + f_b2.astype(f32)))
    hf = (hid @ f_w3.astype(f32)).reshape(l, HY_ORDER, 2, HY_WIDTH)
    deltas = jnp.abs(jnp.linspace(HY_MIN_DECAY, HY_MAX_DECAY, HY_WIDTH, dtype=f32))
    hf = hf * jnp.exp(-t[:, :, None, None] * deltas)
    fwd = hf[:, :, 0]
    bwd = hf[:, :, 1]
    filt = jnp.concatenate([fwd, jnp.zeros((1, HY_ORDER, HY_WIDTH), f32), bwd[:0:-1]], axis=0)
    filt = filt / (jnp.sum(jnp.abs(filt), axis=0, keepdims=True) + 1e-6)
    return jnp.fft.rfft(filt, axis=0)


def long_conv(u, filt_f, d_skip):
    l = u.shape[1]
    uf = jnp.fft.rfft(u, n=2 * l, axis=1)
    y = jnp.fft.irfft(uf * filt_f[None], n=2 * l, axis=1)[:, :l]
    return y + d_skip * u


def hyena_mixer(h, in_w, in_b, conv_w, conv_b, f_w1, f_b1, f_freq1, f_w2, f_b2, f_freq2, f_w3, d_skip,
                out_w, out_b):
    f32 = jnp.float32
    l = h.shape[1]
    zz = dwconv_centred(h @ in_w + in_b, conv_w, conv_b).astype(f32)
    v, x1, x2 = jnp.split(zz, 3, axis=-1)
    filt_f = hyena_filters(l, f_w1, f_b1, f_freq1, f_w2, f_b2, f_freq2, f_w3)
    d = d_skip.astype(f32)
    y = x1 * long_conv(v, filt_f[:, 0], d[0])
    y = x2 * long_conv(y, filt_f[:, 1], d[1])
    return y.astype(h.dtype) @ out_w + out_b


def setup_inputs(seed: int = 0) -> dict:
    key = jax.random.key(seed)
    ks = list(jax.random.split(key, 48))
    f32 = jnp.float32
    D = D_MODEL

    def nrm(shape, scale):
        return scale * jax.random.normal(ks.pop(), shape, f32)

    def unif(shape, lo, hi):
        return jax.random.uniform(ks.pop(), shape, f32, lo, hi)

    x = nrm((BATCH, SEQ, D), 1.0)
    c = nrm((BATCH, D), 1.0)
    ctx = nrm((BATCH, CTX_LEN, D), 1.0)
    c_ctx = nrm((D,), 1.0)
    mod_w = nrm((DEPTH, D, N_MOD * D), 0.5 * D ** -0.5)
    mod_b = nrm((DEPTH, N_MOD * D), 0.02)
    norm_mix_g = 1.0 + nrm((DEPTH, D), 0.02)
    norm_mlp_g = 1.0 + nrm((DEPTH, D), 0.02)
    mlp_w1 = nrm((DEPTH, D, MLP_HIDDEN), D ** -0.5)
    mlp_w2 = nrm((DEPTH, MLP_HIDDEN, D), MLP_HIDDEN ** -0.5)
    final_norm_g = 1.0 + nrm((D,), 0.02)
    ev_in_w = nrm((N_EVEN, D, EVEN_IN), D ** -0.5)
    ev_out_w = nrm((N_EVEN, EVEN_MIX, D), EVEN_MIX ** -0.5)
    s5_lam_re = -0.5 + nrm((N_EVEN, 2, S5_GROUPS, S5_STATE), 0.01)
    s5_lam_im = jnp.pi * jnp.arange(S5_STATE, dtype=f32) + nrm((N_EVEN, 2, S5_GROUPS, S5_STATE), 0.01)
    s5_log_dt = unif((N_EVEN, 2, S5_GROUPS), math.log(1e-3), math.log(1e-1))
    s5_b_re = nrm((N_EVEN, S5_GROUPS, S5_STATE, S5_GROUP_CH), (2 * S5_GROUP_CH) ** -0.5)
    s5_b_im = nrm((N_EVEN, S5_GROUPS, S5_STATE, S5_GROUP_CH), (2 * S5_GROUP_CH) ** -0.5)
    s5_c_re = nrm((N_EVEN, S5_GROUPS, S5_GROUP_CH, S5_STATE), (2 * S5_STATE) ** -0.5)
    s5_c_im = nrm((N_EVEN, S5_GROUPS, S5_GROUP_CH, S5_STATE), (2 * S5_STATE) ** -0.5)
    s5_d = nrm((N_EVEN, S5_WIDTH), 1.0)
    s5_glu_w = nrm((N_EVEN, S5_WIDTH, S5_WIDTH), S5_WIDTH ** -0.5)
    s5_glu_b = nrm((N_EVEN, S5_WIDTH), 0.02)
    m2_conv_w = nrm((N_EVEN, M2_CONV, M2_XBC), 0.5)
    m2_conv_b = nrm((N_EVEN, M2_XBC), 0.02)
    dt0 = jnp.exp(unif((N_EVEN, 2, M2_HEADS), math.log(1e-3), math.log(1e-1)))
    m2_dt_bias = dt0 + jnp.log(-jnp.expm1(-dt0))
    m2_a_log = jnp.log(unif((N_EVEN, 2, M2_HEADS), 1.0, 16.0))
    m2_d = 1.0 + nrm((N_EVEN, M2_HEADS), 0.1)
    m2_norm_g = 1.0 + nrm((N_EVEN, M2_INNER), 0.02)
    hy_in_w = nrm((N_ODD, D, 3 * HY_WIDTH), D ** -0.5)
    hy_in_b = nrm((N_ODD, 3 * HY_WIDTH), 0.02)
    hy_conv_w = nrm((N_ODD, HY_CONV, 3 * HY_WIDTH), 0.5)
    hy_conv_b = nrm((N_ODD, 3 * HY_WIDTH), 0.02)
    hy_f_w1 = nrm((N_ODD, HY_EMB, HY_HIDDEN), HY_EMB ** -0.5)
    hy_f_b1 = nrm((N_ODD, HY_HIDDEN), 0.1)
    hy_f_freq1 = 1.0 + nrm((N_ODD, HY_HIDDEN), 0.01)
    hy_f_w2 = nrm((N_ODD, HY_HIDDEN, HY_HIDDEN), HY_HIDDEN ** -0.5)
    hy_f_b2 = nrm((N_ODD, HY_HIDDEN), 0.1)
    hy_f_freq2 = 1.0 + nrm((N_ODD, HY_HIDDEN), 0.01)
    hy_f_w3 = nrm((N_ODD, HY_HIDDEN, HY_ORDER * 2 * HY_WIDTH), HY_HIDDEN ** -0.5)
    hy_d = nrm((N_ODD, HY_ORDER, HY_WIDTH), 1.0)
    hy_out_w = nrm((N_ODD, HY_WIDTH, D), HY_WIDTH ** -0.5)
    hy_out_b = nrm((N_ODD, D), 0.02)
    return {'x': x, 'c': c, 'ctx': ctx, 'c_ctx': c_ctx, 'mod_w': mod_w, 'mod_b': mod_b,
            'norm_mix_g': norm_mix_g, 'norm_mlp_g': norm_mlp_g, 'mlp_w1': mlp_w1, 'mlp_w2': mlp_w2,
            'final_norm_g': final_norm_g, 'ev_in_w': ev_in_w, 'ev_out_w': ev_out_w,
            's5_lam_re': s5_lam_re, 's5_lam_im': s5_lam_im, 's5_log_dt': s5_log_dt,
            's5_b_re': s5_b_re, 's5_b_im': s5_b_im, 's5_c_re': s5_c_re, 's5_c_im': s5_c_im,
            's5_d': s5_d, 's5_glu_w': s5_glu_w, 's5_glu_b': s5_glu_b,
            'm2_conv_w': m2_conv_w, 'm2_conv_b': m2_conv_b, 'm2_dt_bias': m2_dt_bias, 'm2_a_log': m2_a_log,
            'm2_d': m2_d, 'm2_norm_g': m2_norm_g,
            'hy_in_w': hy_in_w, 'hy_in_b': hy_in_b, 'hy_conv_w': hy_conv_w, 'hy_conv_b': hy_conv_b,
            'hy_f_w1': hy_f_w1, 'hy_f_b1': hy_f_b1, 'hy_f_freq1': hy_f_freq1, 'hy_f_w2': hy_f_w2,
            'hy_f_b2': hy_f_b2, 'hy_f_freq2': hy_f_freq2, 'hy_f_w3': hy_f_w3, 'hy_d': hy_d,
            'hy_out_w': hy_out_w, 'hy_out_b': hy_out_b}


def reference(x, c, ctx, c_ctx, mod_w, mod_b, norm_mix_g, norm_mlp_g, mlp_w1, mlp_w2, final_norm_g,
              ev_in_w, ev_out_w, s5_lam_re, s5_lam_im, s5_log_dt, s5_b_re, s5_b_im, s5_c_re, s5_c_im,
              s5_d, s5_glu_w, s5_glu_b, m2_conv_w, m2_conv_b, m2_dt_bias, m2_a_log, m2_d, m2_norm_g,
              hy_in_w, hy_in_b, hy_conv_w, hy_conv_b, hy_f_w1, hy_f_b1, hy_f_freq1, hy_f_w2, hy_f_b2,
              hy_f_freq2, hy_f_w3, hy_d, hy_out_w, hy_out_b):
    f32 = jnp.float32
    bsz, n, dm = x.shape
    x = x + grid_sincos(n, dm).astype(x.dtype)
    for i in range(DEPTH):
        j = i // 2
        m_x = jax.nn.silu(c) @ mod_w[i] + mod_b[i]
        sh_a, sc_a, g_a, sh_f, sc_f, g_f = jnp.split(m_x[:, None, :], N_MOD, axis=-1)
        m_c = jax.nn.silu(c_ctx) @ mod_w[i] + mod_b[i]
        ch_a, cs_a, cg_a, ch_f, cs_f, cg_f = jnp.split(m_c, N_MOD, axis=-1)
        ctx_later = any(k % 2 == 0 for k in range(i + 1, DEPTH))
        hx = adaln(x, norm_mix_g[i], sh_a, sc_a)
        if i % 2 == 0:
            ev = (ev_in_w[j], ev_out_w[j], s5_lam_re[j], s5_lam_im[j], s5_log_dt[j], s5_b_re[j], s5_b_im[j],
                  s5_c_re[j], s5_c_im[j], s5_d[j], s5_glu_w[j], s5_glu_b[j], m2_conv_w[j], m2_conv_b[j],
                  m2_dt_bias[j], m2_a_log[j], m2_d[j], m2_norm_g[j])
            bc = ctx.shape[0]
            z_s5 = (jnp.zeros((bc, S5_GROUPS, S5_STATE), f32), jnp.zeros((bc, S5_GROUPS, S5_STATE), f32))
            z_m2 = jnp.zeros((bc, M2_HEADS, M2_HEAD_DIM, M2_STATE), f32)
            hc = adaln(ctx, norm_mix_g[i], ch_a, cs_a)
            out_c, ctx_states = even_mixer(hc, *ev, z_s5, z_s5, z_m2, z_m2)
            out_x, _ = even_mixer(hx, *ev, *ctx_states)
        else:
            od = (hy_in_w[j], hy_in_b[j], hy_conv_w[j], hy_conv_b[j], hy_f_w1[j], hy_f_b1[j], hy_f_freq1[j],
                  hy_f_w2[j], hy_f_b2[j], hy_f_freq2[j], hy_f_w3[j], hy_d[j], hy_out_w[j], hy_out_b[j])
            out_x = hyena_mixer(hx, *od)
            if ctx_later:
                out_c = hyena_mixer(adaln(ctx, norm_mix_g[i], ch_a, cs_a), *od)
        x = x + g_a * out_x
        x = x + g_f * sq_relu_mlp(adaln(x, norm_mlp_g[i], sh_f, sc_f), mlp_w1[i], mlp_w2[i])
        if ctx_later:
            ctx = ctx + cg_a * out_c
            ctx = ctx + cg_f * sq_relu_mlp(adaln(ctx, norm_mlp_g[i], ch_f, cs_f), mlp_w1[i], mlp_w2[i])
    return rmsnorm(x, final_norm_g)
```

```python
import functools
import math

import numpy as np
import jax
import jax.numpy as jnp
from jax import lax
from jax.experimental import pallas as pl
from jax.experimental.pallas import tpu as pltpu

F32 = jnp.float32
BF16 = jnp.bfloat16

D_MODEL = 1024
DEPTH = 4
GRID_W = 64
N_MOD = 6
MLP_HIDDEN = 4 * D_MODEL
NORM_EPS = 1e-6

S5_WIDTH = 512
S5_GROUP_CH = 16
S5_GROUPS = 32
S5_STATE = 64
S5_SUPER = 4
S5_SLAB = 2 * 8 * S5_STATE
S5_LANES = S5_SUPER * S5_SLAB

M2_INNER = 1024
M2_HEAD_DIM = 64
M2_HEADS = 16
M2_GROUPS = 4
M2_STATE = 128
M2_BC = M2_GROUPS * M2_STATE
M2_XBC = M2_INNER + 2 * M2_BC
M2_GROUP_W = (M2_HEADS // M2_GROUPS) * M2_HEAD_DIM

HY_WIDTH = 1024
HY_ORDER = 2
HY_BANDS = 16
HY_EMB = 2 * HY_BANDS + 1
HY_HIDDEN = 64
HY_MIN_DECAY = math.log(1e-2) / 1.5
HY_MAX_DECAY = math.log(1e-2) / 0.3

V7X_VMEM_LIMIT_BYTES = 48 * 1024 * 1024


def _cparams(*sem):
    return pltpu.CompilerParams(dimension_semantics=sem, vmem_limit_bytes=V7X_VMEM_LIMIT_BYTES)


def _bdot(a, b):
    return jnp.dot(a.astype(BF16), b.astype(BF16), preferred_element_type=F32)


def _split3(a):
    a1 = a.astype(BF16)
    r1 = a - a1.astype(F32)
    a2 = r1.astype(BF16)
    a3 = (r1 - a2.astype(F32)).astype(BF16)
    return a1, a2, a3


def _dot_exact_rhs(a, b_bf16):
    a1, a2, a3 = _split3(a)
    d = functools.partial(jnp.dot, preferred_element_type=F32)
    return d(a1, b_bf16) + d(a2, b_bf16) + d(a3, b_bf16)


def _dot_exact_lhs(a_bf16, b):
    b1, b2, b3 = _split3(b)
    d = functools.partial(jnp.dot, preferred_element_type=F32)
    return d(a_bf16, b1) + d(a_bf16, b2) + d(a_bf16, b3)


def _dot3(a, b):
    a1 = a.astype(BF16)
    a2 = (a - a1.astype(F32)).astype(BF16)
    b1 = b.astype(BF16)
    b2 = (b - b1.astype(F32)).astype(BF16)
    d = functools.partial(jnp.dot, preferred_element_type=F32)
    return d(a1, b1) + d(a1, b2) + d(a2, b1)


def _silu(x):
    return x * (1.0 / (1.0 + jnp.exp(-x)))


def _sigmoid(x):
    return 1.0 / (1.0 + jnp.exp(-x))


def _adaln(x, g, shift, scale):
    ms = jnp.mean(x * x, axis=-1, keepdims=True)
    return (x * lax.rsqrt(ms + NORM_EPS) * g) * (1.0 + scale) + shift


def _mod_kernel(c_ref, w_ref, b_ref, o_ref):
    o_ref[0] = _bdot(_silu(c_ref[...]), w_ref[0]) + b_ref[0]


def _modulation(c8, mod_w, mod_b):
    n = mod_w.shape[-1]
    tn = 1536
    return pl.pallas_call(
        _mod_kernel,
        grid=(DEPTH, n // tn),
        in_specs=[pl.BlockSpec((8, D_MODEL), lambda i, j: (0, 0)),
                  pl.BlockSpec((1, D_MODEL, tn), lambda i, j: (i, 0, j)),
                  pl.BlockSpec((1, 1, tn), lambda i, j: (i, 0, j))],
        out_specs=pl.BlockSpec((1, 8, tn), lambda i, j: (i, 0, j)),
        out_shape=jax.ShapeDtypeStruct((DEPTH, 8, n), F32),
        compiler_params=_cparams("parallel", "parallel"),
        name="modulation",
    )(c8, mod_w, mod_b.reshape(DEPTH, 1, n))


def _add_pe_kernel(x_ref, pe_ref, o_ref):
    o_ref[0] = x_ref[0] + pe_ref[...]


def _add_pe(x, pe):
    b, l, d = x.shape
    tl = min(l, 1024)
    return pl.pallas_call(
        _add_pe_kernel,
        grid=(l // tl, b),
        in_specs=[pl.BlockSpec((1, tl, d), lambda i, bb: (bb, i, 0)),
                  pl.BlockSpec((tl, d), lambda i, bb: (i, 0))],
        out_specs=pl.BlockSpec((1, tl, d), lambda i, bb: (bb, i, 0)),
        out_shape=jax.ShapeDtypeStruct(x.shape, F32),
        compiler_params=_cparams("parallel", "parallel"),
        name="add_pe",
    )(x, pe)


def _adaln_mm_kernel(x_ref, g_ref, sh_ref, sc_ref, w_ref, b_ref, o_ref):
    h = _adaln(x_ref[0], g_ref[...], sh_ref[0], sc_ref[0])
    o_ref[0] = _bdot(h, w_ref[...]) + b_ref[...]


def _adaln_mm(x, g, shift, scale, w_bf16, bias):
    b, l, d = x.shape
    n = w_bf16.shape[1]
    tl = min(l, 512)
    tn = min(n, 1024)
    return pl.pallas_call(
        _adaln_mm_kernel,
        grid=(b, l // tl, n // tn),
        in_specs=[pl.BlockSpec((1, tl, d), lambda bb, i, j: (bb, i, 0)),
                  pl.BlockSpec((1, d), lambda bb, i, j: (0, 0)),
                  pl.BlockSpec((1, 1, d), lambda bb, i, j: (bb, 0, 0)),
                  pl.BlockSpec((1, 1, d), lambda bb, i, j: (bb, 0, 0)),
                  pl.BlockSpec((d, tn), lambda bb, i, j: (0, j)),
                  pl.BlockSpec((1, tn), lambda bb, i, j: (0, j))],
        out_specs=pl.BlockSpec((1, tl, tn), lambda bb, i, j: (bb, i, j)),
        out_shape=jax.ShapeDtypeStruct((b, l, n), F32),
        compiler_params=_cparams("parallel", "parallel", "parallel"),
        name="adaln_mm",
    )(x, g, shift, scale, w_bf16, bias)


def _out_proj_kernel(n_in, *refs):
    x_ref, gate_ref, bias_ref = refs[0], refs[1], refs[2]
    y_refs = refs[3:3 + n_in]
    w_refs = refs[3 + n_in:3 + 2 * n_in]
    o_ref = refs[3 + 2 * n_in]
    acc = bias_ref[...] + _bdot(y_refs[0][0], w_refs[0][...])
    for y_ref, w_ref in zip(y_refs[1:], w_refs[1:]):
        acc = acc + _bdot(y_ref[0], w_ref[...])
    o_ref[0] = x_ref[0] + gate_ref[0] * acc


def _out_proj(x, gate, bias, ys, ws_bf16):
    b, l, d = x.shape
    tl = min(l, 512)
    n_in = len(ys)
    row = lambda bb, i: (bb, i, 0)
    in_specs = [pl.BlockSpec((1, tl, d), row),
                pl.BlockSpec((1, 1, d), lambda bb, i: (bb, 0, 0)),
                pl.BlockSpec((1, d), lambda bb, i: (0, 0))]
    in_specs += [pl.BlockSpec((1, tl, y.shape[-1]), row) for y in ys]
    in_specs += [pl.BlockSpec(w.shape, lambda bb, i: (0, 0)) for w in ws_bf16]
    return pl.pallas_call(
        functools.partial(_out_proj_kernel, n_in),
        grid=(b, l // tl),
        in_specs=in_specs,
        out_specs=pl.BlockSpec((1, tl, d), row),
        out_shape=jax.ShapeDtypeStruct(x.shape, F32),
        compiler_params=_cparams("parallel", "parallel"),
        name="out_proj",
    )(x, gate, bias, *ys, *ws_bf16)


def _mlp_kernel(final_norm, x_ref, g_ref, sh_ref, sc_ref, gate_ref, w1_ref, w2_ref, fg_ref, o_ref,
                h_scr, acc_scr):
    k = pl.program_id(2)

    @pl.when(k == 0)
    def _():
        h_scr[...] = _adaln(x_ref[0], g_ref[...], sh_ref[0], sc_ref[0]).astype(BF16)
        acc_scr[...] = jnp.zeros_like(acc_scr)

    a = jnp.dot(h_scr[...], w1_ref[...], preferred_element_type=F32)
    a = jnp.square(jnp.maximum(a, 0.0))
    acc_scr[...] += jnp.dot(a.astype(BF16), w2_ref[...], preferred_element_type=F32)

    @pl.when(k == pl.num_programs(2) - 1)
    def _():
        y = x_ref[0] + gate_ref[0] * acc_scr[...]
        if final_norm:
            ms = jnp.mean(y * y, axis=-1, keepdims=True)
            y = y * lax.rsqrt(ms + NORM_EPS) * fg_ref[...]
        o_ref[0] = y


def _mlp(x, g, shift, scale, gate, w1_bf16, w2_bf16, final_g, final_norm):
    b, l, d = x.shape
    hdim = w1_bf16.shape[1]
    tl = min(l, 1024)
    th = 512
    row = lambda bb, i, k: (bb, i, 0)
    vec = lambda bb, i, k: (bb, 0, 0)
    return pl.pallas_call(
        functools.partial(_mlp_kernel, final_norm),
        grid=(b, l // tl, hdim // th),
        in_specs=[pl.BlockSpec((1, tl, d), row),
                  pl.BlockSpec((1, d), lambda bb, i, k: (0, 0)),
                  pl.BlockSpec((1, 1, d), vec),
                  pl.BlockSpec((1, 1, d), vec),
                  pl.BlockSpec((1, 1, d), vec),
                  pl.BlockSpec((d, th), lambda bb, i, k: (0, k)),
                  pl.BlockSpec((th, d), lambda bb, i, k: (k, 0)),
                  pl.BlockSpec((1, d), lambda bb, i, k: (0, 0))],
        out_specs=pl.BlockSpec((1, tl, d), row),
        out_shape=jax.ShapeDtypeStruct(x.shape, F32),
        scratch_shapes=[pltpu.VMEM((tl, d), BF16), pltpu.VMEM((tl, d), F32)],
        compiler_params=_cparams("parallel", "parallel", "arbitrary"),
        name="mlp",
    )(x, g, shift, scale, gate, w1_bf16, w2_bf16, final_g)


def _dwconv_kernel(act, x_ref, w_ref, b_ref, o_ref):
    x = x_ref[0]
    l = x.shape[0]
    t = lax.broadcasted_iota(jnp.int32, x.shape, 0)
    prev = jnp.where(t == 0, 0.0, pltpu.roll(x, 1, axis=0))
    nxt = jnp.where(t == l - 1, 0.0, pltpu.roll(x, l - 1, axis=0))
    w = w_ref[...]
    y = prev * w[0:1] + x * w[1:2] + nxt * w[2:3] + b_ref[...]
    if act:
        y = _silu(y)
    o_ref[0] = y


def _dwconv(x, w, bias, act, col0=0, ncols=None):
    b, l, c = x.shape
    ncols = c if ncols is None else ncols
    tc = 256 if l > 1024 else 512
    j0 = col0 // tc
    return pl.pallas_call(
        functools.partial(_dwconv_kernel, act),
        grid=(b, ncols // tc),
        in_specs=[pl.BlockSpec((1, l, tc), lambda bb, j: (bb, 0, j0 + j)),
                  pl.BlockSpec((3, tc), lambda bb, j: (0, j0 + j)),
                  pl.BlockSpec((1, tc), lambda bb, j: (0, j0 + j))],
        out_specs=pl.BlockSpec((1, l, tc), lambda bb, j: (bb, 0, j)),
        out_shape=jax.ShapeDtypeStruct((b, l, ncols), F32),
        compiler_params=_cparams("parallel", "parallel"),
        name="dwconv",
    )(x, w, bias)


def _s5_prepare(lam_re, lam_im, log_dt, b_re, b_im, c_re, c_im):
    step = jnp.exp(log_dt)[..., None]
    mag = jnp.exp(lam_re * step)
    ar = mag * jnp.cos(lam_im * step)
    ai = mag * jnp.sin(lam_im * step)
    den = lam_re * lam_re + lam_im * lam_im
    fr = ((ar - 1.0) * lam_re + ai * lam_im) / den
    fi = (ai * lam_re - (ar - 1.0) * lam_im) / den
    eye = jnp.eye(8, dtype=F32)

    def blockdiag_in(b):
        b4 = b.reshape(S5_SUPER, 8, S5_STATE, S5_GROUP_CH)
        return jnp.einsum('sgpk,gh->sgkhp', b4, eye).reshape(S5_SUPER, 128, 512)

    bd = jnp.concatenate([blockdiag_in(b_re), blockdiag_in(b_im)], axis=-1).astype(BF16)

    cpr = c_re[None] * fr[:, :, None, :] - c_im[None] * fi[:, :, None, :]
    cpi = c_re[None] * fi[:, :, None, :] + c_im[None] * fr[:, :, None, :]

    def blockdiag_out(c):
        c5 = c.reshape(2, S5_SUPER, 8, S5_GROUP_CH, S5_STATE)
        return jnp.einsum('dsgkp,gh->dsgphk', c5, eye).reshape(2, S5_SUPER, 512, 128)

    cd = jnp.concatenate([blockdiag_out(cpr), blockdiag_out(-cpi)], axis=2).astype(BF16)
    rows = lambda a: jnp.repeat(a.reshape(2, S5_SUPER * 512), 4, axis=0)
    return bd, cd, rows(ar), rows(ai)


def _s5_scan_kernel(tc, uf_ref, ub_ref, bd_ref, cd_ref, ar_ref, ai_ref, h0_ref, flip_ref,
                    yf_ref, yb_ref, hfin_ref, g_scr, h_scr):
    c = pl.program_id(0)

    @pl.when(c == 0)
    def _():
        h_scr[...] = h0_ref[...]

    flip = flip_ref[...]
    tiles = S5_SLAB // 128
    for b in range(4):
        uf = uf_ref[b].astype(BF16)
        ub = jnp.dot(flip, ub_ref[b].astype(BF16), preferred_element_type=F32).astype(BF16)
        for sg in range(S5_SUPER):
            gf = jnp.dot(uf[:, sg * 128:(sg + 1) * 128], bd_ref[sg], preferred_element_type=F32)
            gb = jnp.dot(ub[:, sg * 128:(sg + 1) * 128], bd_ref[sg], preferred_element_type=F32)
            for k in range(tiles):
                g_scr[sg * tiles + k, pl.ds(b, tc, stride=8), :] = gf[:, k * 128:(k + 1) * 128]
                g_scr[sg * tiles + k, pl.ds(4 + b, tc, stride=8), :] = gb[:, k * 128:(k + 1) * 128]

    half = tiles // 2
    for sg in range(S5_SUPER):
        ar = [ar_ref[:, (sg * half + k) * 128:(sg * half + k + 1) * 128] for k in range(half)]
        ai = [ai_ref[:, (sg * half + k) * 128:(sg * half + k + 1) * 128] for k in range(half)]
        t_re = [sg * tiles + k for k in range(half)]
        t_im = [sg * tiles + half + k for k in range(half)]

        def body(j, carry, ar=ar, ai=ai, t_re=t_re, t_im=t_im):
            r0 = pl.multiple_of(j * 8, 8)
            out = []
            for k in range(half):
                hr, hi = carry[2 * k], carry[2 * k + 1]
                nr = ar[k] * hr - ai[k] * hi + g_scr[t_re[k], pl.ds(r0, 8), :]
                ni = ar[k] * hi + ai[k] * hr + g_scr[t_im[k], pl.ds(r0, 8), :]
                g_scr[t_re[k], pl.ds(r0, 8), :] = nr
                g_scr[t_im[k], pl.ds(r0, 8), :] = ni
                out += [nr, ni]
            return tuple(out)

        init = []
        for k in range(half):
            init += [h_scr[:, t_re[k] * 128:(t_re[k] + 1) * 128], h_scr[:, t_im[k] * 128:(t_im[k] + 1) * 128]]
        fin = lax.fori_loop(0, tc, body, tuple(init), unroll=4)
        for k in range(half):
            h_scr[:, t_re[k] * 128:(t_re[k] + 1) * 128] = fin[2 * k]
            h_scr[:, t_im[k] * 128:(t_im[k] + 1) * 128] = fin[2 * k + 1]

    for d in range(2):
        for b in range(4):
            for sg in range(S5_SUPER):
                hs = jnp.concatenate(
                    [g_scr[sg * tiles + k, pl.ds(d * 4 + b, tc, stride=8), :].astype(BF16) for k in range(tiles)],
                    axis=1)
                y = jnp.dot(hs, cd_ref[d, sg], preferred_element_type=F32)
                if d == 0:
                    yf_ref[b, :, sg * 128:(sg + 1) * 128] = y
                else:
                    yb_ref[b, :, sg * 128:(sg + 1) * 128] = _dot_exact_lhs(flip, y)

    hfin_ref[...] = h_scr[...]


def _s5_scan(u, bd, cd, ar8, ai8, h0):
    b, l, w = u.shape
    assert b == 4
    tc = 128
    nc = l // tc
    flip = jnp.asarray(np.eye(tc, dtype=np.float32)[::-1], dtype=BF16)
    full = lambda shape: pl.BlockSpec(shape, lambda c: (0,) * len(shape))
    return pl.pallas_call(
        functools.partial(_s5_scan_kernel, tc),
        grid=(nc,),
        in_specs=[pl.BlockSpec((4, tc, w), lambda c: (0, c, 0)),
                  pl.BlockSpec((4, tc, w), lambda c: (0, nc - 1 - c, 0)),
                  full(bd.shape), full(cd.shape), full(ar8.shape), full(ai8.shape),
                  full(h0.shape), full(flip.shape)],
        out_specs=[pl.BlockSpec((4, tc, w), lambda c: (0, c, 0)),
                   pl.BlockSpec((4, tc, w), lambda c: (0, nc - 1 - c, 0)),
                   full((8, S5_LANES))],
        out_shape=[jax.ShapeDtypeStruct(u.shape, F32), jax.ShapeDtypeStruct(u.shape, F32),
                   jax.ShapeDtypeStruct((8, S5_LANES), F32)],
        scratch_shapes=[pltpu.VMEM((S5_LANES // 128, 8 * tc, 128), F32), pltpu.VMEM((8, S5_LANES), F32)],
        compiler_params=_cparams("arbitrary"),
        name="s5_scan",
    )(u, u, bd, cd, ar8, ai8, h0, flip)


def _gelu_tanh(x):
    return 0.5 * x * (1.0 + jnp.tanh(math.sqrt(2.0 / math.pi) * (x + 0.044715 * (x * x * x))))


def _s5_glu_kernel(yf_ref, yb_ref, u_ref, d_ref, w_ref, b_ref, o_ref):
    y = _gelu_tanh(yf_ref[0] + yb_ref[0] + d_ref[...] * u_ref[0])
    o_ref[0] = y * _sigmoid(_bdot(y, w_ref[...]) + b_ref[...])


def _s5_glu(yf, yb, u, d_skip, glu_w_bf16, glu_b):
    b, l, w = u.shape
    tl = min(l, 1024)
    row = lambda bb, i: (bb, i, 0)
    cst = lambda bb, i: (0, 0)
    return pl.pallas_call(
        _s5_glu_kernel,
        grid=(b, l // tl),
        in_specs=[pl.BlockSpec((1, tl, w), row)] * 3 + [
            pl.BlockSpec((1, w), cst), pl.BlockSpec((w, w), cst), pl.BlockSpec((1, w), cst)],
        out_specs=pl.BlockSpec((1, tl, w), row),
        out_shape=jax.ShapeDtypeStruct(u.shape, F32),
        compiler_params=_cparams("parallel", "parallel"),
        name="s5_glu",
    )(yf, yb, u, d_skip, glu_w_bf16, glu_b)


SSD_T = 128


def _softplus(x):
    return jnp.maximum(x, 0.0) + jnp.log(1.0 + jnp.exp(-jnp.abs(x)))


def _ssd_kernel(xs_ref, bm_ref, cm_ref, dtc_ref, dtr_ref, bias_c_ref, bias_r_ref, a_c_ref, a_r_ref,
                tri_ref, trit_ref, exp_ref, h0_ref, y_ref, hfin_ref, s_scr):
    c = pl.program_id(2)

    @pl.when(c == 0)
    def _():
        s_scr[...] = h0_ref[0, 0]

    tri = tri_ref[0]
    expand = exp_ref[...]
    dt_c = _softplus(dtc_ref[0, 0] + bias_c_ref[0])
    dt_r = _softplus(dtr_ref[0, 0] + bias_r_ref[0])
    cum_c = _dot_exact_lhs(tri, dt_c * a_c_ref[0])
    cum_r = _dot_exact_rhs(dt_r * a_r_ref[0], trit_ref[0])
    tot_c = jnp.min(cum_c, axis=0, keepdims=True)

    dt_x = _dot_exact_rhs(dt_c, expand)
    in_x = _dot_exact_rhs(jnp.exp(cum_c), expand)
    out_x = _dot_exact_rhs(jnp.exp(tot_c - cum_c), expand)
    tot_x = _dot_exact_rhs(jnp.broadcast_to(jnp.exp(tot_c), (8, M2_HEADS)), expand)[0:1]

    xdt = xs_ref[0] * dt_x
    xout = (xdt * out_x).astype(BF16)
    xdt = xdt.astype(BF16)
    mask = tri > 0
    for g in range(M2_GROUPS):
        bg = bm_ref[0, :, g * M2_STATE:(g + 1) * M2_STATE].astype(BF16)
        cg = cm_ref[0, :, g * M2_STATE:(g + 1) * M2_STATE].astype(BF16)
        cb = lax.dot_general(cg, bg, (((1,), (1,)), ((), ())), preferred_element_type=F32)
        gc = slice(g * M2_GROUP_W, (g + 1) * M2_GROUP_W)
        s_prev = s_scr[g]
        y_off = jnp.dot(cg, s_prev.astype(BF16), preferred_element_type=F32) * in_x[:, gc]
        for r in range(M2_HEADS // M2_GROUPS):
            h = g * (M2_HEADS // M2_GROUPS) + r
            seg = cum_c[:, h:h + 1] - cum_r[h:h + 1, :]
            m = (cb * jnp.exp(jnp.where(mask, seg, -jnp.inf))).astype(BF16)
            hc = slice(h * M2_HEAD_DIM, (h + 1) * M2_HEAD_DIM)
            y_ref[0, 0, :, hc] = (jnp.dot(m, xdt[:, hc], preferred_element_type=F32)
                                  + y_off[:, r * M2_HEAD_DIM:(r + 1) * M2_HEAD_DIM])
        s_scr[g] = s_prev * tot_x[:, gc] + lax.dot_general(
            bg, xout[:, gc], (((0,), (0,)), ((), ())), preferred_element_type=F32)

    @pl.when(c == pl.num_programs(2) - 1)
    def _():
        hfin_ref[0, 0] = s_scr[...]


def _ssd(xbc, dt_col, dt_row, dt_bias, a_coef, h0):
    b, l, _ = xbc.shape
    t = SSD_T
    nc = l // t
    tri_f = np.tril(np.ones((t, t), np.float32))
    tri = jnp.asarray(np.stack([tri_f, tri_f.T]), dtype=BF16)
    trit = jnp.asarray(np.stack([tri_f.T, tri_f]), dtype=BF16)
    expand = jnp.asarray(np.kron(np.eye(M2_HEADS, dtype=np.float32),
                                 np.ones((1, M2_HEAD_DIM), np.float32)), dtype=BF16)
    chunk = lambda d, c: c + d * (nc - 1 - 2 * c)
    return pl.pallas_call(
        _ssd_kernel,
        grid=(2, b, nc),
        in_specs=[pl.BlockSpec((1, t, M2_INNER), lambda d, bb, c: (bb, chunk(d, c), 0)),
                  pl.BlockSpec((1, t, M2_BC), lambda d, bb, c: (bb, chunk(d, c), 2)),
                  pl.BlockSpec((1, t, M2_BC), lambda d, bb, c: (bb, chunk(d, c), 3)),
                  pl.BlockSpec((1, 1, t, M2_HEADS), lambda d, bb, c: (d, bb, chunk(d, c), 0)),
                  pl.BlockSpec((1, 1, M2_HEADS, t), lambda d, bb, c: (d, bb, 0, chunk(d, c))),
                  pl.BlockSpec((1, 1, M2_HEADS), lambda d, bb, c: (d, 0, 0)),
                  pl.BlockSpec((1, M2_HEADS, 1), lambda d, bb, c: (d, 0, 0)),
                  pl.BlockSpec((1, 1, M2_HEADS), lambda d, bb, c: (d, 0, 0)),
                  pl.BlockSpec((1, M2_HEADS, 1), lambda d, bb, c: (d, 0, 0)),
                  pl.BlockSpec((1, t, t), lambda d, bb, c: (d, 0, 0)),
                  pl.BlockSpec((1, t, t), lambda d, bb, c: (d, 0, 0)),
                  pl.BlockSpec((M2_HEADS, M2_INNER), lambda d, bb, c: (0, 0)),
                  pl.BlockSpec((1, 1, M2_GROUPS, M2_STATE, M2_GROUP_W), lambda d, bb, c: (d, bb, 0, 0, 0))],
        out_specs=[pl.BlockSpec((1, 1, t, M2_INNER), lambda d, bb, c: (d, bb, chunk(d, c), 0)),
                   pl.BlockSpec((1, 1, M2_GROUPS, M2_STATE, M2_GROUP_W), lambda d, bb, c: (d, bb, 0, 0, 0))],
        out_shape=[jax.ShapeDtypeStruct((2, b, l, M2_INNER), F32),
                   jax.ShapeDtypeStruct((2, b, M2_GROUPS, M2_STATE, M2_GROUP_W), F32)],
        scratch_shapes=[pltpu.VMEM((M2_GROUPS, M2_STATE, M2_GROUP_W), F32)],
        compiler_params=_cparams("parallel", "parallel", "arbitrary"),
        name="ssd",
    )(xbc, xbc, xbc, dt_col, dt_row, dt_bias.reshape(2, 1, M2_HEADS), dt_bias.reshape(2, M2_HEADS, 1),
      a_coef.reshape(2, 1, M2_HEADS), a_coef.reshape(2, M2_HEADS, 1), tri, trit, expand, h0)


def _m2_gate_kernel(yf_ref, yb_ref, xs_ref, z_ref, d_ref, g_ref, o_ref):
    y = (yf_ref[0, 0] + yb_ref[0, 0] + d_ref[...] * xs_ref[0]) * _silu(z_ref[0])
    ms = jnp.mean(y * y, axis=-1, keepdims=True)
    o_ref[0] = y * lax.rsqrt(ms + NORM_EPS) * g_ref[...]


def _m2_gate(ydir, xbc, z, d_x, norm_g):
    _, b, l, w = ydir.shape
    tl = min(l, 512)
    row = lambda bb, i: (bb, i, 0)
    cst = lambda bb, i: (0, 0)
    return pl.pallas_call(
        _m2_gate_kernel,
        grid=(b, l // tl),
        in_specs=[pl.BlockSpec((1, 1, tl, w), lambda bb, i: (0, bb, i, 0)),
                  pl.BlockSpec((1, 1, tl, w), lambda bb, i: (1, bb, i, 0)),
                  pl.BlockSpec((1, tl, w), row), pl.BlockSpec((1, tl, w), row),
                  pl.BlockSpec((1, w), cst), pl.BlockSpec((1, w), cst)],
        out_specs=pl.BlockSpec((1, tl, w), row),
        out_shape=jax.ShapeDtypeStruct((b, l, w), F32),
        compiler_params=_cparams("parallel", "parallel"),
        name="m2_gate",
    )(ydir, ydir, xbc, z, d_x, norm_g)


HY_N2 = 128


def _round8(n):
    return (n + 7) // 8 * 8


def _dft2_tables(n1, n2):
    n = n1 * n2
    k1n = n1 // 2 + 1
    k1p = _round8(k1n)
    k1 = np.arange(k1p, dtype=np.float64)[:, None]
    valid = (k1 < k1n).astype(np.float64)
    i1 = np.arange(n1, dtype=np.float64)[None, :]
    th1 = 2.0 * np.pi * k1 * i1 / n1
    stage1 = np.concatenate([np.cos(th1) * valid, -np.sin(th1) * valid], axis=0)
    i2 = np.arange(n2, dtype=np.float64)[None, :]
    tht = 2.0 * np.pi * k1 * i2 / n
    twr, twi = np.cos(tht), -np.sin(tht)
    k2 = np.arange(n2, dtype=np.float64)[:, None]
    th2 = 2.0 * np.pi * k2 * i2 / n2
    fc, fs = np.cos(th2), np.sin(th2)
    fwd2 = np.block([[fc, fs], [-fs, fc]])
    inv2 = np.block([[fc, -fs], [fs, fc]])
    wk = np.where((k1 == 0) | (k1 == n1 // 2), 1.0, 2.0) * valid
    o1 = np.arange(n1 // 2, dtype=np.float64)[:, None]
    tho = 2.0 * np.pi * o1 * k1.T / n1
    last = np.concatenate([np.cos(tho) * wk.T, -np.sin(tho) * wk.T], axis=1) / n
    f = lambda a: jnp.asarray(a, dtype=F32)
    return dict(k1n=k1n, k1p=k1p, stage1=f(stage1), twr=f(twr)[..., None], twi=f(twi)[..., None],
                fwd2=f(fwd2), inv2=f(inv2), last=f(last))


def _dft1_tables(l):
    n = 2 * l
    kn = l + 1
    kp = _round8(kn)
    k = np.arange(kp, dtype=np.float64)[:, None]
    valid = (k < kn).astype(np.float64)
    t = np.arange(n, dtype=np.float64)[None, :]
    th = 2.0 * np.pi * k * t / n
    fwd = np.concatenate([np.cos(th) * valid, -np.sin(th) * valid], axis=0)
    wk = np.where((k == 0) | (k == l), 1.0, 2.0) * valid
    o = np.arange(l, dtype=np.float64)[:, None]
    tho = 2.0 * np.pi * o * k.T / n
    inv = np.concatenate([np.cos(tho) * wk.T, -np.sin(tho) * wk.T], axis=1) / n
    f = lambda a: jnp.asarray(a, dtype=F32)
    return dict(kp=kp, fwd=f(fwd), inv=f(inv))


def _rowdft_kernel(hi, f_ref, x_ref, o_ref):
    dot = _dot3 if hi else _bdot
    o_ref[0] = dot(f_ref[...], x_ref[0])


def _rowdft(fmat, x, hi):
    bx, r, n = x.shape
    m = fmat.shape[0]
    tn = min(n, 4096)
    return pl.pallas_call(
        functools.partial(_rowdft_kernel, hi),
        grid=(bx, n // tn),
        in_specs=[pl.BlockSpec((m, r), lambda b, j: (0, 0)),
                  pl.BlockSpec((1, r, tn), lambda b, j: (b, 0, j))],
        out_specs=pl.BlockSpec((1, m, tn), lambda b, j: (b, 0, j)),
        out_shape=jax.ShapeDtypeStruct((bx, m, n), F32),
        compiler_params=_cparams("parallel", "parallel"),
        name="hy_rowdft",
    )(fmat if hi else fmat.astype(BF16), x)


def _hy_mid_kernel(k1n, n2, a_ref, twr_ref, twi_ref, h_ref, fwd_ref, inv_ref, o_ref):
    k1 = pl.program_id(0)

    @pl.when(k1 < k1n)
    def _():
        ar, ai = a_ref[0, 0, 0], a_ref[0, 1, 0]
        twr, twi = twr_ref[0], twi_ref[0]
        p = jnp.concatenate([ar * twr - ai * twi, ar * twi + ai * twr], axis=0)
        x = _bdot(fwd_ref[...], p)
        xr, xi = x[:n2], x[n2:]
        hr, hi = h_ref[0, 0], h_ref[1, 0]
        q = jnp.concatenate([xr * hr - xi * hi, xr * hi + xi * hr], axis=0)
        z = _bdot(inv_ref[...], q)
        zr, zi = z[:n2], z[n2:]
        o_ref[0, 0, 0] = zr * twr + zi * twi
        o_ref[0, 1, 0] = zi * twr - zr * twi

    @pl.when(k1 >= k1n)
    def _():
        o_ref[...] = jnp.zeros_like(o_ref)


def _hy_mid(a5, tab, hspec, order):
    b, _, k1p, n2, c = a5.shape
    return pl.pallas_call(
        functools.partial(_hy_mid_kernel, tab['k1n'], n2),
        grid=(k1p, b),
        in_specs=[pl.BlockSpec((1, 2, 1, n2, c), lambda k, bb: (bb, 0, k, 0, 0)),
                  pl.BlockSpec((1, n2, 1), lambda k, bb: (k, 0, 0)),
                  pl.BlockSpec((1, n2, 1), lambda k, bb: (k, 0, 0)),
                  pl.BlockSpec((2, 1, n2, c), lambda k, bb: (0, k, 0, order)),
                  pl.BlockSpec((2 * n2, 2 * n2), lambda k, bb: (0, 0)),
                  pl.BlockSpec((2 * n2, 2 * n2), lambda k, bb: (0, 0))],
        out_specs=pl.BlockSpec((1, 2, 1, n2, c), lambda k, bb: (bb, 0, k, 0, 0)),
        out_shape=jax.ShapeDtypeStruct(a5.shape, F32),
        compiler_params=_cparams("parallel", "parallel"),
        name="hy_mid",
    )(a5, tab['twr'], tab['twi'], hspec, tab['fwd2'].astype(BF16), tab['inv2'].astype(BF16))


def _hy_filt_mid_kernel(n2, a_ref, twr_ref, twi_ref, sum_ref, fwd_ref, o_ref):
    ar, ai = a_ref[0, 0, 0], a_ref[0, 1, 0]
    twr, twi = twr_ref[0], twi_ref[0]
    p = jnp.concatenate([ar * twr - ai * twi, ar * twi + ai * twr], axis=0)
    x = _dot3(fwd_ref[...], p) * (1.0 / (sum_ref[...] + 1e-6))
    o_ref[0, 0] = x[:n2]
    o_ref[1, 0] = x[n2:]


def _hy_filt_mid(a5, tab, colsum):
    _, _, k1p, n2, c2 = a5.shape
    return pl.pallas_call(
        functools.partial(_hy_filt_mid_kernel, n2),
        grid=(k1p,),
        in_specs=[pl.BlockSpec((1, 2, 1, n2, c2), lambda k: (0, 0, k, 0, 0)),
                  pl.BlockSpec((1, n2, 1), lambda k: (k, 0, 0)),
                  pl.BlockSpec((1, n2, 1), lambda k: (k, 0, 0)),
                  pl.BlockSpec((1, c2), lambda k: (0, 0)),
                  pl.BlockSpec((2 * n2, 2 * n2), lambda k: (0, 0))],
        out_specs=pl.BlockSpec((2, 1, n2, c2), lambda k: (0, k, 0, 0)),
        out_shape=jax.ShapeDtypeStruct((2, k1p, n2, c2), F32),
        compiler_params=_cparams("parallel"),
        name="hy_filt_mid",
    )(a5, tab['twr'], tab['twi'], colsum, tab['fwd2'])


def _hy_last_kernel(g_ref, b_ref, v_ref, gate_ref, d_ref, o_ref):
    o_ref[0] = gate_ref[0] * (_bdot(g_ref[...], b_ref[0]) + d_ref[...] * v_ref[0])


def _hy_last(last, bp, v2, gate2, d_tiled):
    b, r, n = v2.shape
    m = bp.shape[1]
    tn = d_tiled.shape[1]
    return pl.pallas_call(
        _hy_last_kernel,
        grid=(b, n // tn),
        in_specs=[pl.BlockSpec((r, m), lambda bb, j: (0, 0)),
                  pl.BlockSpec((1, m, tn), lambda bb, j: (bb, 0, j)),
                  pl.BlockSpec((1, r, tn), lambda bb, j: (bb, 0, j)),
                  pl.BlockSpec((1, r, tn), lambda bb, j: (bb, 0, j)),
                  pl.BlockSpec((1, tn), lambda bb, j: (0, 0))],
        out_specs=pl.BlockSpec((1, r, tn), lambda bb, j: (bb, 0, j)),
        out_shape=jax.ShapeDtypeStruct(v2.shape, F32),
        compiler_params=_cparams("parallel", "parallel"),
        name="hy_last",
    )(last.astype(BF16), bp, v2, gate2, d_tiled)


def _hy_long_conv2(v, gate, hspec, order, d_vec, tab):
    b, l, c = v.shape
    n2 = HY_N2
    n1h = l // n2
    k1p = tab['k1p']
    v2 = v.reshape(b, n1h, n2 * c)
    a = _rowdft(tab['stage1'][:, :n1h], v2, False)
    bp = _hy_mid(a.reshape(b, 2, k1p, n2, c), tab, hspec, order)
    tn = 4 * c
    out = _hy_last(tab['last'], bp.reshape(b, 2 * k1p, n2 * c), v2, gate.reshape(b, n1h, n2 * c),
                   jnp.tile(d_vec.reshape(1, c), (1, tn // c)))
    return out.reshape(b, l, c)


def _hy_ctx_kernel(kp, fw_ref, inv_ref, h_ref, v_ref, gate_ref, d_ref, o_ref):
    v = v_ref[0]
    u = _bdot(fw_ref[...], v)
    ur, ui = u[:kp], u[kp:]
    hr, hi = h_ref[0], h_ref[1]
    q = jnp.concatenate([ur * hr - ui * hi, ur * hi + ui * hr], axis=0)
    o_ref[0] = gate_ref[0] * (_bdot(inv_ref[...], q) + d_ref[...] * v)


def _hy_long_conv1(v, gate, hspec, order, d_vec, tab):
    b, l, c = v.shape
    kp = tab['kp']
    row = lambda bb: (bb, 0, 0)
    return pl.pallas_call(
        functools.partial(_hy_ctx_kernel, kp),
        grid=(b,),
        in_specs=[pl.BlockSpec((2 * kp, l), lambda bb: (0, 0)),
                  pl.BlockSpec((l, 2 * kp), lambda bb: (0, 0)),
                  pl.BlockSpec((2, kp, c), lambda bb: (0, 0, order)),
                  pl.BlockSpec((1, l, c), row), pl.BlockSpec((1, l, c), row),
                  pl.BlockSpec((1, c), lambda bb: (0, 0))],
        out_specs=pl.BlockSpec((1, l, c), row),
        out_shape=jax.ShapeDtypeStruct(v.shape, F32),
        compiler_params=_cparams("parallel"),
        name="hy_ctx_conv",
    )(tab['fwd'][:, :l].astype(BF16), tab['inv'].astype(BF16), hspec, v, gate, d_vec.reshape(1, c))


def _hy_filter_kernel(feat_ref, t_ref, keep_ref, w1_ref, b1_ref, q1_ref, w2_ref, b2_ref, q2_ref,
                      w3_ref, nd_ref, f_ref, sum_ref):
    @pl.when(pl.program_id(0) == 0)
    def _():
        sum_ref[...] = jnp.zeros_like(sum_ref)

    hid = jnp.sin(q1_ref[...] * (_bdot(feat_ref[...], w1_ref[...]) + b1_ref[...]))
    hid = jnp.sin(q2_ref[...] * (_bdot(hid, w2_ref[...]) + b2_ref[...]))
    f = _bdot(hid, w3_ref[0]) * jnp.exp(t_ref[...] * nd_ref[...]) * keep_ref[...]
    f_ref[...] = f
    sum_ref[...] += jnp.sum(jnp.abs(f), axis=0, keepdims=True)


def _hy_filters(l, f_w1, f_b1, f_freq1, f_w2, f_b2, f_freq2, f_w3):
    c2 = HY_ORDER * HY_WIDTH
    t = jnp.linspace(0.0, 1.0, l, dtype=F32)[:, None]
    bands = jnp.linspace(1e-4, HY_BANDS - 1, HY_BANDS, dtype=F32)
    ang = (2.0 * math.pi / l) * jnp.arange(l, dtype=F32)[:, None] * bands
    feats = jnp.concatenate([t, jnp.cos(ang), -jnp.sin(ang)], axis=-1)
    rows = np.arange(2 * l)
    pos = np.where(rows < l, rows, np.where(rows == l, 0, 2 * l - rows))
    feats2 = jnp.pad(feats[pos], ((0, 0), (0, 128 - HY_EMB)))
    keep = jnp.asarray((rows != l).astype(np.float32))[:, None]
    w1p = jnp.pad(f_w1, ((0, 128 - HY_EMB), (0, 0)))
    w3d = jnp.transpose(f_w3.reshape(HY_HIDDEN, HY_ORDER, 2, HY_WIDTH), (2, 0, 1, 3)).reshape(2, HY_HIDDEN, c2)
    deltas = jnp.abs(jnp.linspace(HY_MIN_DECAY, HY_MAX_DECAY, HY_WIDTH, dtype=F32))
    negd = jnp.tile(-deltas, HY_ORDER)[None, :]
    tr = min(l, 512)
    per_dir = l // tr
    cst = lambda i: (0, 0)
    vec = lambda a: a.reshape(1, HY_HIDDEN)
    return pl.pallas_call(
        _hy_filter_kernel,
        grid=(2 * l // tr,),
        in_specs=[pl.BlockSpec((tr, 128), lambda i: (i, 0)),
                  pl.BlockSpec((tr, 1), lambda i: (i, 0)),
                  pl.BlockSpec((tr, 1), lambda i: (i, 0)),
                  pl.BlockSpec((128, HY_HIDDEN), cst), pl.BlockSpec((1, HY_HIDDEN), cst),
                  pl.BlockSpec((1, HY_HIDDEN), cst),
                  pl.BlockSpec((HY_HIDDEN, HY_HIDDEN), cst), pl.BlockSpec((1, HY_HIDDEN), cst),
                  pl.BlockSpec((1, HY_HIDDEN), cst),
                  pl.BlockSpec((1, HY_HIDDEN, c2), lambda i: (i // per_dir, 0, 0)),
                  pl.BlockSpec((1, c2), cst)],
        out_specs=[pl.BlockSpec((tr, c2), lambda i: (i, 0)), pl.BlockSpec((1, c2), cst)],
        out_shape=[jax.ShapeDtypeStruct((2 * l, c2), F32), jax.ShapeDtypeStruct((1, c2), F32)],
        compiler_params=_cparams("arbitrary"),
        name="hy_filter",
    )(feats2, t[pos], keep, w1p, vec(f_b1), vec(f_freq1), f_w2, vec(f_b2), vec(f_freq2), w3d, negd)


def _col_scale_kernel(x_ref, sum_ref, o_ref):
    o_ref[...] = x_ref[...] * (1.0 / (sum_ref[...] + 1e-6))


def _hy_spectrum(l, fparams):
    filt, colsum = _hy_filters(l, *fparams)
    c2 = filt.shape[1]
    if 2 * l >= 16 * HY_N2:
        n1 = 2 * l // HY_N2
        tab = _dft2_tables(n1, HY_N2)
        a = _rowdft(tab['stage1'], filt.reshape(1, n1, HY_N2 * c2), True)
        return _hy_filt_mid(a.reshape(1, 2, tab['k1p'], HY_N2, c2), tab, colsum), tab
    tab = _dft1_tables(l)
    spec = _rowdft(tab['fwd'], filt[None], True)[0]
    spec = pl.pallas_call(
        _col_scale_kernel,
        in_specs=[pl.BlockSpec(spec.shape, lambda: (0, 0)), pl.BlockSpec((1, c2), lambda: (0, 0))],
        out_specs=pl.BlockSpec(spec.shape, lambda: (0, 0)),
        out_shape=jax.ShapeDtypeStruct(spec.shape, F32),
        name="hy_col_scale",
    )(spec, colsum)
    return spec.reshape(2, tab['kp'], c2), tab


def _hyena_mixer(h_proj, conv_w, conv_b, hspec, tab, d_skip):
    c = HY_WIDTH
    parts = [_dwconv(h_proj, conv_w, conv_b, False, col0=i * c, ncols=c) for i in range(3)]
    v, x1, x2 = parts
    conv = _hy_long_conv2 if 'k1p' in tab else _hy_long_conv1
    y = conv(v, x1, hspec, 0, d_skip[0], tab)
    return conv(y, x2, hspec, 1, d_skip[1], tab)


def _grid_sincos(n, dm):
    rows = n // GRID_W
    quarter = dm // 4
    omega = 1.0 / (10000.0 ** (jnp.arange(quarter, dtype=F32) / quarter))
    ang_r = jnp.arange(rows, dtype=F32)[:, None] * omega
    ang_c = jnp.arange(GRID_W, dtype=F32)[:, None] * omega
    emb_r = jnp.concatenate([jnp.sin(ang_r), jnp.cos(ang_r)], axis=-1)
    emb_c = jnp.concatenate([jnp.sin(ang_c), jnp.cos(ang_c)], axis=-1)
    half = emb_r.shape[-1]
    pe = jnp.concatenate([jnp.broadcast_to(emb_r[:, None, :], (rows, GRID_W, half)),
                          jnp.broadcast_to(emb_c[None, :, :], (rows, GRID_W, half))], axis=-1)
    return pe.reshape(rows * GRID_W, 2 * half)


def _even_mixer(h_in, mod, p, states):
    x, g, shift, scale = h_in
    b, l, _ = x.shape
    zero = lambda n: jnp.zeros((1, n), F32)
    u = _adaln_mm(x, g, shift, scale, p['w_u'], zero(S5_WIDTH))
    z = _adaln_mm(x, g, shift, scale, p['w_z'], zero(M2_INNER))
    xbc = _adaln_mm(x, g, shift, scale, p['w_xbc'], zero(M2_XBC))
    dt = _adaln_mm(x, g, shift, scale, p['w_dt'], zero(128))[..., :2 * M2_HEADS]
    s5_h0, m2_h0 = states

    yf, yb, s5_fin = _s5_scan(u, p['s5_bd'], p['s5_cd'], p['s5_ar'], p['s5_ai'], s5_h0)
    y_s5 = _s5_glu(yf, yb, u, p['s5_d'], p['s5_glu_w'], p['s5_glu_b'])

    xbc = _dwconv(xbc, p['m2_conv_w'], p['m2_conv_b'], True)
    dt4 = dt.reshape(b, l, 2, M2_HEADS)
    dt_col = jnp.transpose(dt4, (2, 0, 1, 3))
    dt_row = jnp.transpose(dt4, (2, 0, 3, 1))
    ydir, m2_fin = _ssd(xbc, dt_col, dt_row, p['m2_dt_bias'], p['m2_a'], m2_h0)
    y_m2 = _m2_gate(ydir, xbc, z, p['m2_d'], p['m2_norm_g'])
    return (y_s5, y_m2), (s5_fin, m2_fin)


def kernel(x, c, ctx, c_ctx, mod_w, mod_b, norm_mix_g, norm_mlp_g, mlp_w1, mlp_w2, final_norm_g, ev_in_w, ev_out_w, s5_lam_re, s5_lam_im, s5_log_dt, s5_b_re, s5_b_im, s5_c_re, s5_c_im, s5_d, s5_glu_w, s5_glu_b, m2_conv_w, m2_conv_b, m2_dt_bias, m2_a_log, m2_d, m2_norm_g, hy_in_w, hy_in_b, hy_conv_w, hy_conv_b, hy_f_w1, hy_f_b1, hy_f_freq1, hy_f_w2, hy_f_b2, hy_f_freq2, hy_f_w3, hy_d, hy_out_w, hy_out_b):
    bsz, n, dm = x.shape
    lc = ctx.shape[1]
    x = _add_pe(x, _grid_sincos(n, dm))

    c8 = jnp.concatenate([c, c_ctx[None], jnp.zeros((3, dm), F32)], axis=0)
    mods = _modulation(c8, mod_w, mod_b).reshape(DEPTH, 8, N_MOD, dm)
    final_g = final_norm_g.reshape(1, dm)

    for i in range(DEPTH):
        j = i // 2
        mx = [mods[i, :bsz, k][:, None, :] for k in range(N_MOD)]
        mc = [jnp.broadcast_to(mods[i, bsz, k][None, None, :], (bsz, 1, dm)) for k in range(N_MOD)]
        g_mix = norm_mix_g[i].reshape(1, dm)
        g_mlp = norm_mlp_g[i].reshape(1, dm)
        ctx_later = any(k % 2 == 0 for k in range(i + 1, DEPTH))
        w1 = mlp_w1[i].astype(BF16)
        w2 = mlp_w2[i].astype(BF16)

        if i % 2 == 0:
            in_w = ev_in_w[j]
            o0, o1, o2 = S5_WIDTH, S5_WIDTH + M2_INNER, S5_WIDTH + M2_INNER + M2_XBC
            bd, cd, ar8, ai8 = _s5_prepare(s5_lam_re[j], s5_lam_im[j], s5_log_dt[j], s5_b_re[j], s5_b_im[j],
                                           s5_c_re[j], s5_c_im[j])
            p = dict(
                w_u=in_w[:, :o0].astype(BF16), w_z=in_w[:, o0:o1].astype(BF16),
                w_xbc=in_w[:, o1:o2].astype(BF16),
                w_dt=jnp.pad(in_w[:, o2:], ((0, 0), (0, 128 - 2 * M2_HEADS))).astype(BF16),
                s5_bd=bd, s5_cd=cd, s5_ar=ar8, s5_ai=ai8,
                s5_d=s5_d[j].reshape(1, S5_WIDTH), s5_glu_w=s5_glu_w[j].astype(BF16),
                s5_glu_b=s5_glu_b[j].reshape(1, S5_WIDTH),
                m2_conv_w=m2_conv_w[j], m2_conv_b=m2_conv_b[j].reshape(1, M2_XBC),
                m2_dt_bias=m2_dt_bias[j], m2_a=-jnp.exp(m2_a_log[j]),
                m2_d=jnp.repeat(m2_d[j], M2_HEAD_DIM).reshape(1, M2_INNER),
                m2_norm_g=m2_norm_g[j].reshape(1, M2_INNER))
            zero_states = (jnp.zeros((8, S5_LANES), F32),
                           jnp.zeros((2, bsz, M2_GROUPS, M2_STATE, M2_GROUP_W), F32))
            ys_c, ctx_states = _even_mixer((ctx, g_mix, mc[0], mc[1]), None, p, zero_states)
            ys_x, _ = _even_mixer((x, g_mix, mx[0], mx[1]), None, p, ctx_states)
            out_ws = [ev_out_w[j][:S5_WIDTH].astype(BF16), ev_out_w[j][S5_WIDTH:].astype(BF16)]
            out_b = jnp.zeros((1, dm), F32)
        else:
            fparams = (hy_f_w1[j], hy_f_b1[j], hy_f_freq1[j], hy_f_w2[j], hy_f_b2[j], hy_f_freq2[j], hy_f_w3[j])
            in_w = hy_in_w[j].astype(BF16)
            in_b = hy_in_b[j].reshape(1, -1)
            conv_b = hy_conv_b[j].reshape(1, -1)
            hspec, tab = _hy_spectrum(n, fparams)
            ys_x = [_hyena_mixer(_adaln_mm(x, g_mix, mx[0], mx[1], in_w, in_b), hy_conv_w[j], conv_b,
                                 hspec, tab, hy_d[j])]
            if ctx_later:
                hspec_c, tab_c = _hy_spectrum(lc, fparams)
                ys_c = [_hyena_mixer(_adaln_mm(ctx, g_mix, mc[0], mc[1], in_w, in_b), hy_conv_w[j], conv_b,
                                     hspec_c, tab_c, hy_d[j])]
            out_ws = [hy_out_w[j].astype(BF16)]
            out_b = hy_out_b[j].reshape(1, dm)

        x = _out_proj(x, mx[2], out_b, ys_x, out_ws)
        x = _mlp(x, g_mlp, mx[3], mx[4], mx[5], w1, w2, final_g, i == DEPTH - 1)
        if ctx_later:
            ctx = _out_proj(ctx, mc[2], out_b, ys_c, out_ws)
            ctx = _mlp(ctx, g_mlp, mc[3], mc[4], mc[5], w1, w2, final_g, False)
    return x
```

```python
import functools
import math

import numpy as np
import jax
import jax.numpy as jnp
from jax import lax
from jax.experimental import pallas as pl
from jax.experimental.pallas import tpu as pltpu

F32 = jnp.float32
BF16 = jnp.bfloat16

D_MODEL = 1024
DEPTH = 4
GRID_W = 64
N_MOD = 6
MLP_HIDDEN = 4 * D_MODEL
NORM_EPS = 1e-6

S5_WIDTH = 512
S5_GROUP_CH = 16
S5_GROUPS = 32
S5_STATE = 64
S5_SUPER = 4
S5_SLAB = 2 * 8 * S5_STATE
S5_LANES = S5_SUPER * S5_SLAB

M2_INNER = 1024
M2_HEAD_DIM = 64
M2_HEADS = 16
M2_GROUPS = 4
M2_STATE = 128
M2_BC = M2_GROUPS * M2_STATE
M2_XBC = M2_INNER + 2 * M2_BC
M2_GROUP_W = (M2_HEADS // M2_GROUPS) * M2_HEAD_DIM

HY_WIDTH = 1024
HY_ORDER = 2
HY_BANDS = 16
HY_EMB = 2 * HY_BANDS + 1
HY_HIDDEN = 64
HY_MIN_DECAY = math.log(1e-2) / 1.5
HY_MAX_DECAY = math.log(1e-2) / 0.3

V7X_VMEM_LIMIT_BYTES = 48 * 1024 * 1024


def _cparams(*sem):
    return pltpu.CompilerParams(dimension_semantics=sem, vmem_limit_bytes=V7X_VMEM_LIMIT_BYTES)


def _bdot(a, b):
    return jnp.dot(a.astype(BF16), b.astype(BF16), preferred_element_type=F32)


def _split3(a):
    a1 = a.astype(BF16)
    r1 = a - a1.astype(F32)
    a2 = r1.astype(BF16)
    a3 = (r1 - a2.astype(F32)).astype(BF16)
    return a1, a2, a3


def _dot_exact_rhs(a, b_bf16):
    a1, a2, a3 = _split3(a)
    d = functools.partial(jnp.dot, preferred_element_type=F32)
    return d(a1, b_bf16) + d(a2, b_bf16) + d(a3, b_bf16)


def _dot_exact_lhs(a_bf16, b):
    b1, b2, b3 = _split3(b)
    d = functools.partial(jnp.dot, preferred_element_type=F32)
    return d(a_bf16, b1) + d(a_bf16, b2) + d(a_bf16, b3)


def _dot3(a, b):
    a1 = a.astype(BF16)
    a2 = (a - a1.astype(F32)).astype(BF16)
    b1 = b.astype(BF16)
    b2 = (b - b1.astype(F32)).astype(BF16)
    d = functools.partial(jnp.dot, preferred_element_type=F32)
    return d(a1, b1) + d(a1, b2) + d(a2, b1)


def _silu(x):
    return x * (1.0 / (1.0 + jnp.exp(-x)))


def _sigmoid(x):
    return 1.0 / (1.0 + jnp.exp(-x))


def _adaln(x, g, shift, scale):
    ms = jnp.mean(x * x, axis=-1, keepdims=True)
    return (x * lax.rsqrt(ms + NORM_EPS) * g) * (1.0 + scale) + shift


def _mod_kernel(c_ref, w_ref, b_ref, o_ref):
    o_ref[0] = _bdot(_silu(c_ref[...]), w_ref[0]) + b_ref[0]


def _modulation(c8, mod_w, mod_b):
    n = mod_w.shape[-1]
    tn = 1536
    return pl.pallas_call(
        _mod_kernel,
        grid=(DEPTH, n // tn),
        in_specs=[pl.BlockSpec((8, D_MODEL), lambda i, j: (0, 0)),
                  pl.BlockSpec((1, D_MODEL, tn), lambda i, j: (i, 0, j)),
                  pl.BlockSpec((1, 1, tn), lambda i, j: (i, 0, j))],
        out_specs=pl.BlockSpec((1, 8, tn), lambda i, j: (i, 0, j)),
        out_shape=jax.ShapeDtypeStruct((DEPTH, 8, n), F32),
        compiler_params=_cparams("parallel", "parallel"),
        name="modulation",
    )(c8, mod_w, mod_b.reshape(DEPTH, 1, n))


def _add_pe_kernel(x_ref, pe_ref, o_ref):
    o_ref[0] = x_ref[0] + pe_ref[...]


def _add_pe(x, pe):
    b, l, d = x.shape
    tl = min(l, 1024)
    return pl.pallas_call(
        _add_pe_kernel,
        grid=(l // tl, b),
        in_specs=[pl.BlockSpec((1, tl, d), lambda i, bb: (bb, i, 0)),
                  pl.BlockSpec((tl, d), lambda i, bb: (i, 0))],
        out_specs=pl.BlockSpec((1, tl, d), lambda i, bb: (bb, i, 0)),
        out_shape=jax.ShapeDtypeStruct(x.shape, F32),
        compiler_params=_cparams("parallel", "parallel"),
        name="add_pe",
    )(x, pe)


def _adaln_mm_kernel(x_ref, g_ref, sh_ref, sc_ref, w_ref, b_ref, o_ref):
    h = _adaln(x_ref[0], g_ref[...], sh_ref[0], sc_ref[0])
    o_ref[0] = _bdot(h, w_ref[...]) + b_ref[...]


def _adaln_mm(x, g, shift, scale, w_bf16, bias):
    b, l, d = x.shape
    n = w_bf16.shape[1]
    tl = min(l, 256)
    return pl.pallas_call(
        _adaln_mm_kernel,
        grid=(b, l // tl),
        in_specs=[pl.BlockSpec((1, tl, d), lambda bb, i: (bb, i, 0)),
                  pl.BlockSpec((1, d), lambda bb, i: (0, 0)),
                  pl.BlockSpec((1, 1, d), lambda bb, i: (bb, 0, 0)),
                  pl.BlockSpec((1, 1, d), lambda bb, i: (bb, 0, 0)),
                  pl.BlockSpec((d, n), lambda bb, i: (0, 0)),
                  pl.BlockSpec((1, n), lambda bb, i: (0, 0))],
        out_specs=pl.BlockSpec((1, tl, n), lambda bb, i: (bb, i, 0)),
        out_shape=jax.ShapeDtypeStruct((b, l, n), F32),
        compiler_params=_cparams("parallel", "parallel"),
        name="adaln_mm",
    )(x, g, shift, scale, w_bf16, bias)


def _out_proj_kernel(n_in, *refs):
    x_ref, gate_ref, bias_ref = refs[0], refs[1], refs[2]
    y_refs = refs[3:3 + n_in]
    w_refs = refs[3 + n_in:3 + 2 * n_in]
    o_ref = refs[3 + 2 * n_in]
    acc = bias_ref[...] + _bdot(y_refs[0][0], w_refs[0][...])
    for y_ref, w_ref in zip(y_refs[1:], w_refs[1:]):
        acc = acc + _bdot(y_ref[0], w_ref[...])
    o_ref[0] = x_ref[0] + gate_ref[0] * acc


def _out_proj(x, gate, bias, ys, ws_bf16):
    b, l, d = x.shape
    tl = min(l, 512)
    n_in = len(ys)
    row = lambda bb, i: (bb, i, 0)
    in_specs = [pl.BlockSpec((1, tl, d), row),
                pl.BlockSpec((1, 1, d), lambda bb, i: (bb, 0, 0)),
                pl.BlockSpec((1, d), lambda bb, i: (0, 0))]
    in_specs += [pl.BlockSpec((1, tl, y.shape[-1]), row) for y in ys]
    in_specs += [pl.BlockSpec(w.shape, lambda bb, i: (0, 0)) for w in ws_bf16]
    return pl.pallas_call(
        functools.partial(_out_proj_kernel, n_in),
        grid=(b, l // tl),
        in_specs=in_specs,
        out_specs=pl.BlockSpec((1, tl, d), row),
        out_shape=jax.ShapeDtypeStruct(x.shape, F32),
        compiler_params=_cparams("parallel", "parallel"),
        name="out_proj",
    )(x, gate, bias, *ys, *ws_bf16)


def _mlp_kernel(final_norm, x_ref, g_ref, sh_ref, sc_ref, gate_ref, w1_ref, w2_ref, fg_ref, o_ref,
                h_scr, acc_scr):
    k = pl.program_id(2)

    @pl.when(k == 0)
    def _():
        h_scr[...] = _adaln(x_ref[0], g_ref[...], sh_ref[0], sc_ref[0]).astype(BF16)
        acc_scr[...] = jnp.zeros_like(acc_scr)

    a = jnp.dot(h_scr[...], w1_ref[...], preferred_element_type=F32)
    a = jnp.square(jnp.maximum(a, 0.0))
    acc_scr[...] += jnp.dot(a.astype(BF16), w2_ref[...], preferred_element_type=F32)

    @pl.when(k == pl.num_programs(2) - 1)
    def _():
        y = x_ref[0] + gate_ref[0] * acc_scr[...]
        if final_norm:
            ms = jnp.mean(y * y, axis=-1, keepdims=True)
            y = y * lax.rsqrt(ms + NORM_EPS) * fg_ref[...]
        o_ref[0] = y


def _mlp(x, g, shift, scale, gate, w1_bf16, w2_bf16, final_g, final_norm):
    b, l, d = x.shape
    hdim = w1_bf16.shape[1]
    tl = min(l, 1024)
    th = 512
    row = lambda bb, i, k: (bb, i, 0)
    vec = lambda bb, i, k: (bb, 0, 0)
    return pl.pallas_call(
        functools.partial(_mlp_kernel, final_norm),
        grid=(b, l // tl, hdim // th),
        in_specs=[pl.BlockSpec((1, tl, d), row),
                  pl.BlockSpec((1, d), lambda bb, i, k: (0, 0)),
                  pl.BlockSpec((1, 1, d), vec),
                  pl.BlockSpec((1, 1, d), vec),
                  pl.BlockSpec((1, 1, d), vec),
                  pl.BlockSpec((d, th), lambda bb, i, k: (0, k)),
                  pl.BlockSpec((th, d), lambda bb, i, k: (k, 0)),
                  pl.BlockSpec((1, d), lambda bb, i, k: (0, 0))],
        out_specs=pl.BlockSpec((1, tl, d), row),
        out_shape=jax.ShapeDtypeStruct(x.shape, F32),
        scratch_shapes=[pltpu.VMEM((tl, d), BF16), pltpu.VMEM((tl, d), F32)],
        compiler_params=_cparams("parallel", "parallel", "arbitrary"),
        name="mlp",
    )(x, g, shift, scale, gate, w1_bf16, w2_bf16, final_g)


def _dwconv_kernel(act, x_ref, w_ref, b_ref, o_ref):
    x = x_ref[0]
    l = x.shape[0]
    t = lax.broadcasted_iota(jnp.int32, x.shape, 0)
    prev = jnp.where(t == 0, 0.0, pltpu.roll(x, 1, axis=0))
    nxt = jnp.where(t == l - 1, 0.0, pltpu.roll(x, l - 1, axis=0))
    w = w_ref[...]
    y = prev * w[0:1] + x * w[1:2] + nxt * w[2:3] + b_ref[...]
    if act:
        y = _silu(y)
    o_ref[0] = y


def _dwconv(x, w, bias, act, col0=0, ncols=None):
    b, l, c = x.shape
    ncols = c if ncols is None else ncols
    tc = 256 if l > 1024 else 512
    j0 = col0 // tc
    return pl.pallas_call(
        functools.partial(_dwconv_kernel, act),
        grid=(b, ncols // tc),
        in_specs=[pl.BlockSpec((1, l, tc), lambda bb, j: (bb, 0, j0 + j)),
                  pl.BlockSpec((3, tc), lambda bb, j: (0, j0 + j)),
                  pl.BlockSpec((1, tc), lambda bb, j: (0, j0 + j))],
        out_specs=pl.BlockSpec((1, l, tc), lambda bb, j: (bb, 0, j)),
        out_shape=jax.ShapeDtypeStruct((b, l, ncols), F32),
        compiler_params=_cparams("parallel", "parallel"),
        name="dwconv",
    )(x, w, bias)


def _s5_prepare(lam_re, lam_im, log_dt, b_re, b_im, c_re, c_im):
    step = jnp.exp(log_dt)[..., None]
    mag = jnp.exp(lam_re * step)
    ar = mag * jnp.cos(lam_im * step)
    ai = mag * jnp.sin(lam_im * step)
    den = lam_re * lam_re + lam_im * lam_im
    fr = ((ar - 1.0) * lam_re + ai * lam_im) / den
    fi = (ai * lam_re - (ar - 1.0) * lam_im) / den
    eye = jnp.eye(8, dtype=F32)

    def blockdiag_in(b):
        b4 = b.reshape(S5_SUPER, 8, S5_STATE, S5_GROUP_CH)
        return jnp.einsum('sgpk,gh->sgkhp', b4, eye).reshape(S5_SUPER, 128, 512)

    bd = jnp.concatenate([blockdiag_in(b_re), blockdiag_in(b_im)], axis=-1).astype(BF16)

    cpr = c_re[None] * fr[:, :, None, :] - c_im[None] * fi[:, :, None, :]
    cpi = c_re[None] * fi[:, :, None, :] + c_im[None] * fr[:, :, None, :]

    def blockdiag_out(c):
        c5 = c.reshape(2, S5_SUPER, 8, S5_GROUP_CH, S5_STATE)
        return jnp.einsum('dsgkp,gh->dsgphk', c5, eye).reshape(2, S5_SUPER, 512, 128)

    cd = jnp.concatenate([blockdiag_out(cpr), blockdiag_out(-cpi)], axis=2).astype(BF16)
    rows = lambda a: jnp.repeat(a.reshape(2, S5_SUPER * 512), 4, axis=0)
    return bd, cd, rows(ar), rows(ai)


def _s5_scan_kernel(tc, uf_ref, ub_ref, bd_ref, cd_ref, ar_ref, ai_ref, h0_ref, flip_ref,
                    yf_ref, yb_ref, hfin_ref, g_scr, h_scr):
    c = pl.program_id(0)

    @pl.when(c == 0)
    def _():
        h_scr[...] = h0_ref[...]

    flip = flip_ref[...]
    tiles = S5_SLAB // 128
    for b in range(4):
        uf = uf_ref[b].astype(BF16)
        ub = jnp.dot(flip, ub_ref[b].astype(BF16), preferred_element_type=F32).astype(BF16)
        for sg in range(S5_SUPER):
            gf = jnp.dot(uf[:, sg * 128:(sg + 1) * 128], bd_ref[sg], preferred_element_type=F32)
            gb = jnp.dot(ub[:, sg * 128:(sg + 1) * 128], bd_ref[sg], preferred_element_type=F32)
            for k in range(tiles):
                g_scr[sg * tiles + k, pl.ds(b, tc, stride=8), :] = gf[:, k * 128:(k + 1) * 128]
                g_scr[sg * tiles + k, pl.ds(4 + b, tc, stride=8), :] = gb[:, k * 128:(k + 1) * 128]

    half = tiles // 2
    for sg in range(S5_SUPER):
        ar = [ar_ref[:, (sg * half + k) * 128:(sg * half + k + 1) * 128] for k in range(half)]
        ai = [ai_ref[:, (sg * half + k) * 128:(sg * half + k + 1) * 128] for k in range(half)]
        t_re = [sg * tiles + k for k in range(half)]
        t_im = [sg * tiles + half + k for k in range(half)]

        def body(j, carry, ar=ar, ai=ai, t_re=t_re, t_im=t_im):
            r0 = pl.multiple_of(j * 8, 8)
            out = []
            for k in range(half):
                hr, hi = carry[2 * k], carry[2 * k + 1]
                nr = ar[k] * hr - ai[k] * hi + g_scr[t_re[k], pl.ds(r0, 8), :]
                ni = ar[k] * hi + ai[k] * hr + g_scr[t_im[k], pl.ds(r0, 8), :]
                g_scr[t_re[k], pl.ds(r0, 8), :] = nr
                g_scr[t_im[k], pl.ds(r0, 8), :] = ni
                out += [nr, ni]
            return tuple(out)

        init = []
        for k in range(half):
            init += [h_scr[:, t_re[k] * 128:(t_re[k] + 1) * 128], h_scr[:, t_im[k] * 128:(t_im[k] + 1) * 128]]
        fin = lax.fori_loop(0, tc, body, tuple(init), unroll=4)
        for k in range(half):
            h_scr[:, t_re[k] * 128:(t_re[k] + 1) * 128] = fin[2 * k]
            h_scr[:, t_im[k] * 128:(t_im[k] + 1) * 128] = fin[2 * k + 1]

    for d in range(2):
        for b in range(4):
            for sg in range(S5_SUPER):
                hs = jnp.concatenate(
                    [g_scr[sg * tiles + k, pl.ds(d * 4 + b, tc, stride=8), :].astype(BF16) for k in range(tiles)],
                    axis=1)
                y = jnp.dot(hs, cd_ref[d, sg], preferred_element_type=F32)
                if d == 0:
                    yf_ref[b, :, sg * 128:(sg + 1) * 128] = y
                else:
                    yb_ref[b, :, sg * 128:(sg + 1) * 128] = _dot_exact_lhs(flip, y)

    hfin_ref[...] = h_scr[...]


def _s5_scan(u, ucol, bd, cd, ar8, ai8, h0):
    b, l, _ = u.shape
    w = S5_WIDTH
    assert b == 4
    tc = 128
    nc = l // tc
    flip = jnp.asarray(np.eye(tc, dtype=np.float32)[::-1], dtype=BF16)
    full = lambda shape: pl.BlockSpec(shape, lambda c: (0,) * len(shape))
    y_shape = jax.ShapeDtypeStruct((b, l, w), F32)
    return pl.pallas_call(
        functools.partial(_s5_scan_kernel, tc),
        grid=(nc,),
        in_specs=[pl.BlockSpec((4, tc, w), lambda c: (0, c, ucol)),
                  pl.BlockSpec((4, tc, w), lambda c: (0, nc - 1 - c, ucol)),
                  full(bd.shape), full(cd.shape), full(ar8.shape), full(ai8.shape),
                  full(h0.shape), full(flip.shape)],
        out_specs=[pl.BlockSpec((4, tc, w), lambda c: (0, c, 0)),
                   pl.BlockSpec((4, tc, w), lambda c: (0, nc - 1 - c, 0)),
                   full((8, S5_LANES))],
        out_shape=[y_shape, y_shape, jax.ShapeDtypeStruct((8, S5_LANES), F32)],
        scratch_shapes=[pltpu.VMEM((S5_LANES // 128, 8 * tc, 128), F32), pltpu.VMEM((8, S5_LANES), F32)],
        compiler_params=_cparams("arbitrary"),
        name="s5_scan",
    )(u, u, bd, cd, ar8, ai8, h0, flip)


def _gelu_tanh(x):
    return 0.5 * x * (1.0 + jnp.tanh(math.sqrt(2.0 / math.pi) * (x + 0.044715 * (x * x * x))))


def _s5_glu_kernel(yf_ref, yb_ref, u_ref, d_ref, w_ref, b_ref, o_ref):
    y = _gelu_tanh(yf_ref[0] + yb_ref[0] + d_ref[...] * u_ref[0])
    o_ref[0] = y * _sigmoid(_bdot(y, w_ref[...]) + b_ref[...])


def _s5_glu(yf, yb, u, ucol, d_skip, glu_w_bf16, glu_b):
    b, l, w = yf.shape
    tl = min(l, 1024)
    row = lambda bb, i: (bb, i, 0)
    cst = lambda bb, i: (0, 0)
    return pl.pallas_call(
        _s5_glu_kernel,
        grid=(b, l // tl),
        in_specs=[pl.BlockSpec((1, tl, w), row)] * 2 + [
            pl.BlockSpec((1, tl, w), lambda bb, i: (bb, i, ucol)),
            pl.BlockSpec((1, w), cst), pl.BlockSpec((w, w), cst), pl.BlockSpec((1, w), cst)],
        out_specs=pl.BlockSpec((1, tl, w), row),
        out_shape=jax.ShapeDtypeStruct(yf.shape, F32),
        compiler_params=_cparams("parallel", "parallel"),
        name="s5_glu",
    )(yf, yb, u, d_skip, glu_w_bf16, glu_b)


SSD_T = 128


def _softplus(x):
    return jnp.maximum(x, 0.0) + jnp.log(1.0 + jnp.exp(-jnp.abs(x)))


def _ssd_kernel(xs_ref, bm_ref, cm_ref, dtc_ref, dtr_ref, bias_c_ref, bias_r_ref, a_c_ref, a_r_ref,
                tri_ref, trit_ref, exp_ref, h0_ref, y_ref, hfin_ref, s_scr):
    c = pl.program_id(2)

    @pl.when(c == 0)
    def _():
        s_scr[...] = h0_ref[0, 0]

    tri = tri_ref[0]
    expand = exp_ref[...]
    dt_c = _softplus(dtc_ref[0, 0] + bias_c_ref[0])
    dt_r = _softplus(dtr_ref[0, 0] + bias_r_ref[0])
    cum_c = _dot_exact_lhs(tri, dt_c * a_c_ref[0])
    cum_r = _dot_exact_rhs(dt_r * a_r_ref[0], trit_ref[0])
    tot_c = jnp.min(cum_c, axis=0, keepdims=True)

    dt_x = _dot_exact_rhs(dt_c, expand)
    in_x = _dot_exact_rhs(jnp.exp(cum_c), expand)
    out_x = _dot_exact_rhs(jnp.exp(tot_c - cum_c), expand)
    tot_x = _dot_exact_rhs(jnp.broadcast_to(jnp.exp(tot_c), (8, M2_HEADS)), expand)[0:1]

    xdt = xs_ref[0] * dt_x
    xout = (xdt * out_x).astype(BF16)
    xdt = xdt.astype(BF16)
    mask = tri > 0
    for g in range(M2_GROUPS):
        bg = bm_ref[0, :, g * M2_STATE:(g + 1) * M2_STATE].astype(BF16)
        cg = cm_ref[0, :, g * M2_STATE:(g + 1) * M2_STATE].astype(BF16)
        cb = lax.dot_general(cg, bg, (((1,), (1,)), ((), ())), preferred_element_type=F32)
        gc = slice(g * M2_GROUP_W, (g + 1) * M2_GROUP_W)
        s_prev = s_scr[g]
        y_off = jnp.dot(cg, s_prev.astype(BF16), preferred_element_type=F32) * in_x[:, gc]
        for r in range(M2_HEADS // M2_GROUPS):
            h = g * (M2_HEADS // M2_GROUPS) + r
            seg = cum_c[:, h:h + 1] - cum_r[h:h + 1, :]
            m = (cb * jnp.exp(jnp.where(mask, seg, -jnp.inf))).astype(BF16)
            hc = slice(h * M2_HEAD_DIM, (h + 1) * M2_HEAD_DIM)
            y_ref[0, 0, :, hc] = (jnp.dot(m, xdt[:, hc], preferred_element_type=F32)
                                  + y_off[:, r * M2_HEAD_DIM:(r + 1) * M2_HEAD_DIM])
        s_scr[g] = s_prev * tot_x[:, gc] + lax.dot_general(
            bg, xout[:, gc], (((0,), (0,)), ((), ())), preferred_element_type=F32)

    @pl.when(c == pl.num_programs(2) - 1)
    def _():
        hfin_ref[0, 0] = s_scr[...]


def _ssd(xbc, dt_col, dt_row, dt_bias, a_coef, h0):
    b, l, _ = xbc.shape
    t = SSD_T
    nc = l // t
    tri_f = np.tril(np.ones((t, t), np.float32))
    tri = jnp.asarray(np.stack([tri_f, tri_f.T]), dtype=BF16)
    trit = jnp.asarray(np.stack([tri_f.T, tri_f]), dtype=BF16)
    expand = jnp.asarray(np.kron(np.eye(M2_HEADS, dtype=np.float32),
                                 np.ones((1, M2_HEAD_DIM), np.float32)), dtype=BF16)
    chunk = lambda d, c: c + d * (nc - 1 - 2 * c)
    return pl.pallas_call(
        _ssd_kernel,
        grid=(2, b, nc),
        in_specs=[pl.BlockSpec((1, t, M2_INNER), lambda d, bb, c: (bb, chunk(d, c), 0)),
                  pl.BlockSpec((1, t, M2_BC), lambda d, bb, c: (bb, chunk(d, c), 2)),
                  pl.BlockSpec((1, t, M2_BC), lambda d, bb, c: (bb, chunk(d, c), 3)),
                  pl.BlockSpec((1, 1, t, M2_HEADS), lambda d, bb, c: (d, bb, chunk(d, c), 0)),
                  pl.BlockSpec((1, 1, M2_HEADS, t), lambda d, bb, c: (d, bb, 0, chunk(d, c))),
                  pl.BlockSpec((1, 1, M2_HEADS), lambda d, bb, c: (d, 0, 0)),
                  pl.BlockSpec((1, M2_HEADS, 1), lambda d, bb, c: (d, 0, 0)),
                  pl.BlockSpec((1, 1, M2_HEADS), lambda d, bb, c: (d, 0, 0)),
                  pl.BlockSpec((1, M2_HEADS, 1), lambda d, bb, c: (d, 0, 0)),
                  pl.BlockSpec((1, t, t), lambda d, bb, c: (d, 0, 0)),
                  pl.BlockSpec((1, t, t), lambda d, bb, c: (d, 0, 0)),
                  pl.BlockSpec((M2_HEADS, M2_INNER), lambda d, bb, c: (0, 0)),
                  pl.BlockSpec((1, 1, M2_GROUPS, M2_STATE, M2_GROUP_W), lambda d, bb, c: (d, bb, 0, 0, 0))],
        out_specs=[pl.BlockSpec((1, 1, t, M2_INNER), lambda d, bb, c: (d, bb, chunk(d, c), 0)),
                   pl.BlockSpec((1, 1, M2_GROUPS, M2_STATE, M2_GROUP_W), lambda d, bb, c: (d, bb, 0, 0, 0))],
        out_shape=[jax.ShapeDtypeStruct((2, b, l, M2_INNER), F32),
                   jax.ShapeDtypeStruct((2, b, M2_GROUPS, M2_STATE, M2_GROUP_W), F32)],
        scratch_shapes=[pltpu.VMEM((M2_GROUPS, M2_STATE, M2_GROUP_W), F32)],
        compiler_params=_cparams("parallel", "parallel", "arbitrary"),
        name="ssd",
    )(xbc, xbc, xbc, dt_col, dt_row, dt_bias.reshape(2, 1, M2_HEADS), dt_bias.reshape(2, M2_HEADS, 1),
      a_coef.reshape(2, 1, M2_HEADS), a_coef.reshape(2, M2_HEADS, 1), tri, trit, expand, h0)


def _m2_gate_kernel(yf_ref, yb_ref, xs_ref, z_ref, d_ref, g_ref, o_ref):
    y = (yf_ref[0, 0] + yb_ref[0, 0] + d_ref[...] * xs_ref[0]) * _silu(z_ref[0])
    ms = jnp.mean(y * y, axis=-1, keepdims=True)
    o_ref[0] = y * lax.rsqrt(ms + NORM_EPS) * g_ref[...]


def _m2_gate(ydir, xbc, z, zcol, d_x, norm_g):
    _, b, l, w = ydir.shape
    tl = min(l, 512)
    row = lambda bb, i: (bb, i, 0)
    cst = lambda bb, i: (0, 0)
    return pl.pallas_call(
        _m2_gate_kernel,
        grid=(b, l // tl),
        in_specs=[pl.BlockSpec((1, 1, tl, w), lambda bb, i: (0, bb, i, 0)),
                  pl.BlockSpec((1, 1, tl, w), lambda bb, i: (1, bb, i, 0)),
                  pl.BlockSpec((1, tl, w), row), pl.BlockSpec((1, tl, w), lambda bb, i: (bb, i, zcol)),
                  pl.BlockSpec((1, w), cst), pl.BlockSpec((1, w), cst)],
        out_specs=pl.BlockSpec((1, tl, w), row),
        out_shape=jax.ShapeDtypeStruct((b, l, w), F32),
        compiler_params=_cparams("parallel", "parallel"),
        name="m2_gate",
    )(ydir, ydir, xbc, z, d_x, norm_g)


HY_N2 = 128


def _round8(n):
    return (n + 7) // 8 * 8


def _dft2_tables(n1, n2):
    n = n1 * n2
    k1n = n1 // 2 + 1
    k1p = _round8(k1n)
    k1 = np.arange(k1p, dtype=np.float64)[:, None]
    valid = (k1 < k1n).astype(np.float64)
    i1 = np.arange(n1, dtype=np.float64)[None, :]
    th1 = 2.0 * np.pi * k1 * i1 / n1
    stage1 = np.concatenate([np.cos(th1) * valid, -np.sin(th1) * valid], axis=0)
    i2 = np.arange(n2, dtype=np.float64)[None, :]
    tht = 2.0 * np.pi * k1 * i2 / n
    twr, twi = np.cos(tht), -np.sin(tht)
    k2 = np.arange(n2, dtype=np.float64)[:, None]
    th2 = 2.0 * np.pi * k2 * i2 / n2
    fc, fs = np.cos(th2), np.sin(th2)
    fwd2 = np.block([[fc, fs], [-fs, fc]])
    inv2 = np.block([[fc, -fs], [fs, fc]])
    wk = np.where((k1 == 0) | (k1 == n1 // 2), 1.0, 2.0) * valid
    o1 = np.arange(n1 // 2, dtype=np.float64)[:, None]
    tho = 2.0 * np.pi * o1 * k1.T / n1
    last = np.concatenate([np.cos(tho) * wk.T, -np.sin(tho) * wk.T], axis=1) / n
    f = lambda a: jnp.asarray(a, dtype=F32)
    return dict(k1n=k1n, k1p=k1p, stage1=f(stage1), twr=f(twr)[..., None], twi=f(twi)[..., None],
                fwd2=f(fwd2), inv2=f(inv2), last=f(last))


def _dft1_tables(l):
    n = 2 * l
    kn = l + 1
    kp = _round8(kn)
    k = np.arange(kp, dtype=np.float64)[:, None]
    valid = (k < kn).astype(np.float64)
    t = np.arange(n, dtype=np.float64)[None, :]
    th = 2.0 * np.pi * k * t / n
    fwd = np.concatenate([np.cos(th) * valid, -np.sin(th) * valid], axis=0)
    wk = np.where((k == 0) | (k == l), 1.0, 2.0) * valid
    o = np.arange(l, dtype=np.float64)[:, None]
    tho = 2.0 * np.pi * o * k.T / n
    inv = np.concatenate([np.cos(tho) * wk.T, -np.sin(tho) * wk.T], axis=1) / n
    f = lambda a: jnp.asarray(a, dtype=F32)
    return dict(kp=kp, fwd=f(fwd), inv=f(inv))


def _rowdft_kernel(hi, f_ref, x_ref, o_ref):
    dot = _dot3 if hi else _bdot
    o_ref[0] = dot(f_ref[...], x_ref[0])


def _rowdft(fmat, x, hi):
    bx, r, n = x.shape
    m = fmat.shape[0]
    tn = min(n, 4096)
    return pl.pallas_call(
        functools.partial(_rowdft_kernel, hi),
        grid=(bx, n // tn),
        in_specs=[pl.BlockSpec((m, r), lambda b, j: (0, 0)),
                  pl.BlockSpec((1, r, tn), lambda b, j: (b, 0, j))],
        out_specs=pl.BlockSpec((1, m, tn), lambda b, j: (b, 0, j)),
        out_shape=jax.ShapeDtypeStruct((bx, m, n), F32),
        compiler_params=_cparams("parallel", "parallel"),
        name="hy_rowdft",
    )(fmat if hi else fmat.astype(BF16), x)


HY_TS = 8


def _hy_first_kernel(hi, k1p, f_ref, x_ref, o_ref):
    dot = _dot3 if hi else _bdot
    f = f_ref[...]
    for s in range(HY_TS):
        y = dot(f, x_ref[:, s, :])
        o_ref[0, :, s, :] = y[:k1p]
        o_ref[1, :, s, :] = y[k1p:]


def _hy_first(stage1, x4, hi):
    bx, r, n2, c = x4.shape
    k1p = stage1.shape[0] // 2
    return pl.pallas_call(
        functools.partial(_hy_first_kernel, hi, k1p),
        grid=(bx, n2 // HY_TS),
        in_specs=[pl.BlockSpec((2 * k1p, r), lambda b, s: (0, 0)),
                  pl.BlockSpec((None, r, HY_TS, c), lambda b, s: (b, 0, s, 0))],
        out_specs=pl.BlockSpec((None, 2, k1p, HY_TS, c), lambda b, s: (b, 0, 0, s, 0)),
        out_shape=jax.ShapeDtypeStruct((bx, 2, k1p, n2, c), F32),
        compiler_params=_cparams("parallel", "parallel"),
        name="hy_first",
    )(stage1 if hi else stage1.astype(BF16), x4)


def _hy_mid_kernel(k1n, n2, a_ref, twr_ref, twi_ref, h_ref, fwd_ref, inv_ref, o_ref):
    k1 = pl.program_id(0)

    @pl.when(k1 < k1n)
    def _():
        ar, ai = a_ref[0, 0, 0], a_ref[0, 1, 0]
        twr, twi = twr_ref[0], twi_ref[0]
        p = jnp.concatenate([ar * twr - ai * twi, ar * twi + ai * twr], axis=0)
        x = _bdot(fwd_ref[...], p)
        xr, xi = x[:n2], x[n2:]
        hr, hi = h_ref[0, 0], h_ref[1, 0]
        q = jnp.concatenate([xr * hr - xi * hi, xr * hi + xi * hr], axis=0)
        z = _bdot(inv_ref[...], q)
        zr, zi = z[:n2], z[n2:]
        o_ref[0, 0, 0] = zr * twr + zi * twi
        o_ref[0, 1, 0] = zi * twr - zr * twi

    @pl.when(k1 >= k1n)
    def _():
        o_ref[...] = jnp.zeros_like(o_ref)


def _hy_mid(a5, tab, hspec, order):
    b, _, k1p, n2, c = a5.shape
    return pl.pallas_call(
        functools.partial(_hy_mid_kernel, tab['k1n'], n2),
        grid=(k1p, b),
        in_specs=[pl.BlockSpec((1, 2, 1, n2, c), lambda k, bb: (bb, 0, k, 0, 0)),
                  pl.BlockSpec((1, n2, 1), lambda k, bb: (k, 0, 0)),
                  pl.BlockSpec((1, n2, 1), lambda k, bb: (k, 0, 0)),
                  pl.BlockSpec((2, 1, n2, c), lambda k, bb: (0, k, 0, order)),
                  pl.BlockSpec((2 * n2, 2 * n2), lambda k, bb: (0, 0)),
                  pl.BlockSpec((2 * n2, 2 * n2), lambda k, bb: (0, 0))],
        out_specs=pl.BlockSpec((1, 2, 1, n2, c), lambda k, bb: (bb, 0, k, 0, 0)),
        out_shape=jax.ShapeDtypeStruct(a5.shape, F32),
        compiler_params=_cparams("parallel", "parallel"),
        name="hy_mid",
    )(a5, tab['twr'], tab['twi'], hspec, tab['fwd2'].astype(BF16), tab['inv2'].astype(BF16))


def _hy_filt_mid_kernel(n2, a_ref, twr_ref, twi_ref, sum_ref, fwd_ref, o_ref):
    ar, ai = a_ref[0, 0, 0], a_ref[0, 1, 0]
    twr, twi = twr_ref[0], twi_ref[0]
    p = jnp.concatenate([ar * twr - ai * twi, ar * twi + ai * twr], axis=0)
    x = _dot3(fwd_ref[...], p) * (1.0 / (sum_ref[...] + 1e-6))
    o_ref[0, 0] = x[:n2]
    o_ref[1, 0] = x[n2:]


def _hy_filt_mid(a5, tab, colsum):
    _, _, k1p, n2, c2 = a5.shape
    return pl.pallas_call(
        functools.partial(_hy_filt_mid_kernel, n2),
        grid=(k1p,),
        in_specs=[pl.BlockSpec((1, 2, 1, n2, c2), lambda k: (0, 0, k, 0, 0)),
                  pl.BlockSpec((1, n2, 1), lambda k: (k, 0, 0)),
                  pl.BlockSpec((1, n2, 1), lambda k: (k, 0, 0)),
                  pl.BlockSpec((1, c2), lambda k: (0, 0)),
                  pl.BlockSpec((2 * n2, 2 * n2), lambda k: (0, 0))],
        out_specs=pl.BlockSpec((2, 1, n2, c2), lambda k: (0, k, 0, 0)),
        out_shape=jax.ShapeDtypeStruct((2, k1p, n2, c2), F32),
        compiler_params=_cparams("parallel"),
        name="hy_filt_mid",
    )(a5, tab['twr'], tab['twi'], colsum, tab['fwd2'])


def _hy_last_kernel(k1p, g_ref, b_ref, v_ref, gate_ref, d_ref, o_ref):
    g = g_ref[...]
    for s in range(HY_TS):
        z = jnp.concatenate([b_ref[0, :, s, :], b_ref[1, :, s, :]], axis=0)
        o_ref[:, s, :] = gate_ref[:, s, :] * (_bdot(g, z) + d_ref[...] * v_ref[:, s, :])


def _hy_last(last, bp, v4, gate4, d_vec):
    b, r, n2, c = v4.shape
    k1p = bp.shape[2]
    rows = pl.BlockSpec((None, r, HY_TS, c), lambda bb, s: (bb, 0, s, 0))
    return pl.pallas_call(
        functools.partial(_hy_last_kernel, k1p),
        grid=(b, n2 // HY_TS),
        in_specs=[pl.BlockSpec((r, 2 * k1p), lambda bb, s: (0, 0)),
                  pl.BlockSpec((None, 2, k1p, HY_TS, c), lambda bb, s: (bb, 0, 0, s, 0)),
                  rows, rows,
                  pl.BlockSpec((1, c), lambda bb, s: (0, 0))],
        out_specs=rows,
        out_shape=jax.ShapeDtypeStruct(v4.shape, F32),
        compiler_params=_cparams("parallel", "parallel"),
        name="hy_last",
    )(last.astype(BF16), bp, v4, gate4, d_vec)


def _hy_long_conv2(v, gate, hspec, order, d_vec, tab):
    b, l, c = v.shape
    n2 = HY_N2
    n1h = l // n2
    v4 = v.reshape(b, n1h, n2, c)
    a = _hy_first(tab['stage1'][:, :n1h], v4, False)
    bp = _hy_mid(a, tab, hspec, order)
    out = _hy_last(tab['last'], bp, v4, gate.reshape(b, n1h, n2, c), d_vec.reshape(1, c))
    return out.reshape(b, l, c)


def _hy_ctx_kernel(kp, fw_ref, inv_ref, h_ref, v_ref, gate_ref, d_ref, o_ref):
    v = v_ref[0]
    u = _bdot(fw_ref[...], v)
    ur, ui = u[:kp], u[kp:]
    hr, hi = h_ref[0], h_ref[1]
    q = jnp.concatenate([ur * hr - ui * hi, ur * hi + ui * hr], axis=0)
    o_ref[0] = gate_ref[0] * (_bdot(inv_ref[...], q) + d_ref[...] * v)


def _hy_long_conv1(v, gate, hspec, order, d_vec, tab):
    b, l, c = v.shape
    kp = tab['kp']
    row = lambda bb: (bb, 0, 0)
    return pl.pallas_call(
        functools.partial(_hy_ctx_kernel, kp),
        grid=(b,),
        in_specs=[pl.BlockSpec((2 * kp, l), lambda bb: (0, 0)),
                  pl.BlockSpec((l, 2 * kp), lambda bb: (0, 0)),
                  pl.BlockSpec((2, kp, c), lambda bb: (0, 0, order)),
                  pl.BlockSpec((1, l, c), row), pl.BlockSpec((1, l, c), row),
                  pl.BlockSpec((1, c), lambda bb: (0, 0))],
        out_specs=pl.BlockSpec((1, l, c), row),
        out_shape=jax.ShapeDtypeStruct(v.shape, F32),
        compiler_params=_cparams("parallel"),
        name="hy_ctx_conv",
    )(tab['fwd'][:, :l].astype(BF16), tab['inv'].astype(BF16), hspec, v, gate, d_vec.reshape(1, c))


def _hy_filter_kernel(feat_ref, t_ref, keep_ref, w1_ref, b1_ref, q1_ref, w2_ref, b2_ref, q2_ref,
                      w3_ref, nd_ref, f_ref, sum_ref):
    @pl.when(pl.program_id(0) == 0)
    def _():
        sum_ref[...] = jnp.zeros_like(sum_ref)

    hid = jnp.sin(q1_ref[...] * (_bdot(feat_ref[...], w1_ref[...]) + b1_ref[...]))
    hid = jnp.sin(q2_ref[...] * (_bdot(hid, w2_ref[...]) + b2_ref[...]))
    f = _bdot(hid, w3_ref[0]) * jnp.exp(t_ref[...] * nd_ref[...]) * keep_ref[...]
    f_ref[...] = f
    sum_ref[...] += jnp.sum(jnp.abs(f), axis=0, keepdims=True)


def _hy_filters(l, f_w1, f_b1, f_freq1, f_w2, f_b2, f_freq2, f_w3):
    c2 = HY_ORDER * HY_WIDTH
    t = np.linspace(0.0, 1.0, l, dtype=np.float32)[:, None]
    bands = np.linspace(1e-4, HY_BANDS - 1, HY_BANDS, dtype=np.float32)
    ang = np.float32(2.0 * math.pi / l) * np.arange(l, dtype=np.float32)[:, None] * bands
    feats = np.concatenate([t, np.cos(ang), -np.sin(ang)], axis=-1)
    rows = np.arange(2 * l)
    pos = np.where(rows < l, rows, np.where(rows == l, 0, 2 * l - rows))
    feats2 = jnp.asarray(np.pad(feats[pos], ((0, 0), (0, 128 - HY_EMB))))
    keep = jnp.asarray((rows != l).astype(np.float32))[:, None]
    t2 = jnp.asarray(t[pos])
    w1p = jnp.pad(f_w1, ((0, 128 - HY_EMB), (0, 0)))
    w3d = jnp.transpose(f_w3.reshape(HY_HIDDEN, HY_ORDER, 2, HY_WIDTH), (2, 0, 1, 3)).reshape(2, HY_HIDDEN, c2)
    deltas = jnp.abs(jnp.linspace(HY_MIN_DECAY, HY_MAX_DECAY, HY_WIDTH, dtype=F32))
    negd = jnp.tile(-deltas, HY_ORDER)[None, :]
    tr = min(l, 512)
    per_dir = l // tr
    cst = lambda i: (0, 0)
    vec = lambda a: a.reshape(1, HY_HIDDEN)
    return pl.pallas_call(
        _hy_filter_kernel,
        grid=(2 * l // tr,),
        in_specs=[pl.BlockSpec((tr, 128), lambda i: (i, 0)),
                  pl.BlockSpec((tr, 1), lambda i: (i, 0)),
                  pl.BlockSpec((tr, 1), lambda i: (i, 0)),
                  pl.BlockSpec((128, HY_HIDDEN), cst), pl.BlockSpec((1, HY_HIDDEN), cst),
                  pl.BlockSpec((1, HY_HIDDEN), cst),
                  pl.BlockSpec((HY_HIDDEN, HY_HIDDEN), cst), pl.BlockSpec((1, HY_HIDDEN), cst),
                  pl.BlockSpec((1, HY_HIDDEN), cst),
                  pl.BlockSpec((1, HY_HIDDEN, c2), lambda i: (i // per_dir, 0, 0)),
                  pl.BlockSpec((1, c2), cst)],
        out_specs=[pl.BlockSpec((tr, c2), lambda i: (i, 0)), pl.BlockSpec((1, c2), cst)],
        out_shape=[jax.ShapeDtypeStruct((2 * l, c2), F32), jax.ShapeDtypeStruct((1, c2), F32)],
        compiler_params=_cparams("arbitrary"),
        name="hy_filter",
    )(feats2, t2, keep, w1p, vec(f_b1), vec(f_freq1), f_w2, vec(f_b2), vec(f_freq2), w3d, negd)


def _col_scale_kernel(x_ref, sum_ref, o_ref):
    o_ref[...] = x_ref[...] * (1.0 / (sum_ref[...] + 1e-6))


def _hy_spectrum(l, fparams):
    filt, colsum = _hy_filters(l, *fparams)
    c2 = filt.shape[1]
    if 2 * l >= 16 * HY_N2:
        n1 = 2 * l // HY_N2
        tab = _dft2_tables(n1, HY_N2)
        a = _hy_first(tab['stage1'], filt.reshape(1, n1, HY_N2, c2), True)
        return _hy_filt_mid(a, tab, colsum), tab
    tab = _dft1_tables(l)
    spec = _rowdft(tab['fwd'], filt[None], True)[0]
    spec = pl.pallas_call(
        _col_scale_kernel,
        in_specs=[pl.BlockSpec(spec.shape, lambda: (0, 0)), pl.BlockSpec((1, c2), lambda: (0, 0))],
        out_specs=pl.BlockSpec(spec.shape, lambda: (0, 0)),
        out_shape=jax.ShapeDtypeStruct(spec.shape, F32),
        name="hy_col_scale",
    )(spec, colsum)
    return spec.reshape(2, tab['kp'], c2), tab


def _hyena_mixer(h_proj, conv_w, conv_b, hspec, tab, d_skip):
    c = HY_WIDTH
    parts = [_dwconv(h_proj, conv_w, conv_b, False, col0=i * c, ncols=c) for i in range(3)]
    v, x1, x2 = parts
    conv = _hy_long_conv2 if 'k1p' in tab else _hy_long_conv1
    y = conv(v, x1, hspec, 0, d_skip[0], tab)
    return conv(y, x2, hspec, 1, d_skip[1], tab)


def _grid_sincos(n, dm):
    rows = n // GRID_W
    quarter = dm // 4
    omega = 1.0 / (10000.0 ** (jnp.arange(quarter, dtype=F32) / quarter))
    ang_r = jnp.arange(rows, dtype=F32)[:, None] * omega
    ang_c = jnp.arange(GRID_W, dtype=F32)[:, None] * omega
    emb_r = jnp.concatenate([jnp.sin(ang_r), jnp.cos(ang_r)], axis=-1)
    emb_c = jnp.concatenate([jnp.sin(ang_c), jnp.cos(ang_c)], axis=-1)
    half = emb_r.shape[-1]
    pe = jnp.concatenate([jnp.broadcast_to(emb_r[:, None, :], (rows, GRID_W, half)),
                          jnp.broadcast_to(emb_c[None, :, :], (rows, GRID_W, half))], axis=-1)
    return pe.reshape(rows * GRID_W, 2 * half)


EV_Z_OFF = M2_XBC
EV_U_OFF = EV_Z_OFF + M2_INNER
EV_DT_OFF = EV_U_OFF + S5_WIDTH
EV_PROJ_W = EV_DT_OFF + 128


def _even_mixer(h_in, p, states):
    x, g, shift, scale = h_in
    b, l, _ = x.shape
    proj = _adaln_mm(x, g, shift, scale, p['w_in'], jnp.zeros((1, EV_PROJ_W), F32))
    s5_h0, m2_h0 = states

    ucol = EV_U_OFF // S5_WIDTH
    yf, yb, s5_fin = _s5_scan(proj, ucol, p['s5_bd'], p['s5_cd'], p['s5_ar'], p['s5_ai'], s5_h0)
    y_s5 = _s5_glu(yf, yb, proj, ucol, p['s5_d'], p['s5_glu_w'], p['s5_glu_b'])

    xbc = _dwconv(proj, p['m2_conv_w'], p['m2_conv_b'], True, col0=0, ncols=M2_XBC)
    dt4 = proj[..., EV_DT_OFF:EV_DT_OFF + 2 * M2_HEADS].reshape(b, l, 2, M2_HEADS)
    dt_col = jnp.transpose(dt4, (2, 0, 1, 3))
    dt_row = jnp.transpose(dt4, (2, 0, 3, 1))
    ydir, m2_fin = _ssd(xbc, dt_col, dt_row, p['m2_dt_bias'], p['m2_a'], m2_h0)
    y_m2 = _m2_gate(ydir, xbc, proj, EV_Z_OFF // M2_INNER, p['m2_d'], p['m2_norm_g'])
    return (y_s5, y_m2), (s5_fin, m2_fin)


def kernel(x, c, ctx, c_ctx, mod_w, mod_b, norm_mix_g, norm_mlp_g, mlp_w1, mlp_w2, final_norm_g, ev_in_w, ev_out_w, s5_lam_re, s5_lam_im, s5_log_dt, s5_b_re, s5_b_im, s5_c_re, s5_c_im, s5_d, s5_glu_w, s5_glu_b, m2_conv_w, m2_conv_b, m2_dt_bias, m2_a_log, m2_d, m2_norm_g, hy_in_w, hy_in_b, hy_conv_w, hy_conv_b, hy_f_w1, hy_f_b1, hy_f_freq1, hy_f_w2, hy_f_b2, hy_f_freq2, hy_f_w3, hy_d, hy_out_w, hy_out_b):
    bsz, n, dm = x.shape
    lc = ctx.shape[1]
    x = _add_pe(x, _grid_sincos(n, dm))

    c8 = jnp.concatenate([c, c_ctx[None], jnp.zeros((3, dm), F32)], axis=0)
    mods = _modulation(c8, mod_w, mod_b).reshape(DEPTH, 8, N_MOD, dm)
    final_g = final_norm_g.reshape(1, dm)

    for i in range(DEPTH):
        j = i // 2
        mx = [mods[i, :bsz, k][:, None, :] for k in range(N_MOD)]
        mc = [jnp.broadcast_to(mods[i, bsz, k][None, None, :], (bsz, 1, dm)) for k in range(N_MOD)]
        g_mix = norm_mix_g[i].reshape(1, dm)
        g_mlp = norm_mlp_g[i].reshape(1, dm)
        ctx_later = any(k % 2 == 0 for k in range(i + 1, DEPTH))
        w1 = mlp_w1[i].astype(BF16)
        w2 = mlp_w2[i].astype(BF16)

        if i % 2 == 0:
            in_w = ev_in_w[j]
            o0, o1, o2 = S5_WIDTH, S5_WIDTH + M2_INNER, S5_WIDTH + M2_INNER + M2_XBC
            bd, cd, ar8, ai8 = _s5_prepare(s5_lam_re[j], s5_lam_im[j], s5_log_dt[j], s5_b_re[j], s5_b_im[j],
                                           s5_c_re[j], s5_c_im[j])
            p = dict(
                w_in=jnp.concatenate([in_w[:, o1:o2], in_w[:, o0:o1], in_w[:, :o0],
                                      jnp.pad(in_w[:, o2:], ((0, 0), (0, 128 - 2 * M2_HEADS)))],
                                     axis=1).astype(BF16),
                s5_bd=bd, s5_cd=cd, s5_ar=ar8, s5_ai=ai8,
                s5_d=s5_d[j].reshape(1, S5_WIDTH), s5_glu_w=s5_glu_w[j].astype(BF16),
                s5_glu_b=s5_glu_b[j].reshape(1, S5_WIDTH),
                m2_conv_w=m2_conv_w[j], m2_conv_b=m2_conv_b[j].reshape(1, M2_XBC),
                m2_dt_bias=m2_dt_bias[j], m2_a=-jnp.exp(m2_a_log[j]),
                m2_d=jnp.repeat(m2_d[j], M2_HEAD_DIM).reshape(1, M2_INNER),
                m2_norm_g=m2_norm_g[j].reshape(1, M2_INNER))
            zero_states = (jnp.zeros((8, S5_LANES), F32),
                           jnp.zeros((2, bsz, M2_GROUPS, M2_STATE, M2_GROUP_W), F32))
            ys_c, ctx_states = _even_mixer((ctx, g_mix, mc[0], mc[1]), p, zero_states)
            ys_x, _ = _even_mixer((x, g_mix, mx[0], mx[1]), p, ctx_states)
            out_ws = [ev_out_w[j][:S5_WIDTH].astype(BF16), ev_out_w[j][S5_WIDTH:].astype(BF16)]
            out_b = jnp.zeros((1, dm), F32)
        else:
            fparams = (hy_f_w1[j], hy_f_b1[j], hy_f_freq1[j], hy_f_w2[j], hy_f_b2[j], hy_f_freq2[j], hy_f_w3[j])
            in_w = hy_in_w[j].astype(BF16)
            in_b = hy_in_b[j].reshape(1, -1)
            conv_b = hy_conv_b[j].reshape(1, -1)
            hspec, tab = _hy_spectrum(n, fparams)
            ys_x = [_hyena_mixer(_adaln_mm(x, g_mix, mx[0], mx[1], in_w, in_b), hy_conv_w[j], conv_b,
                                 hspec, tab, hy_d[j])]
            if ctx_later:
                hspec_c, tab_c = _hy_spectrum(lc, fparams)
                ys_c = [_hyena_mixer(_adaln_mm(ctx, g_mix, mc[0], mc[1], in_w, in_b), hy_conv_w[j], conv_b,
                                     hspec_c, tab_c, hy_d[j])]
            out_ws = [hy_out_w[j].astype(BF16)]
            out_b = hy_out_b[j].reshape(1, dm)

        x = _out_proj(x, mx[2], out_b, ys_x, out_ws)
        x = _mlp(x, g_mlp, mx[3], mx[4], mx[5], w1, w2, final_g, i == DEPTH - 1)
        if ctx_later:
            ctx = _out_proj(ctx, mc[2], out_b, ys_c, out_ws)
            ctx = _mlp(ctx, g_mlp, mc[3], mc[4], mc[5], w1, w2, final_g, False)
    return x
```

```python
import functools
import math

import numpy as np
import jax
import jax.numpy as jnp
from jax import lax
from jax.experimental import pallas as pl
from jax.experimental.pallas import tpu as pltpu

F32 = jnp.float32
BF16 = jnp.bfloat16

D_MODEL = 1024
DEPTH = 4
GRID_W = 64
N_MOD = 6
MLP_HIDDEN = 4 * D_MODEL
NORM_EPS = 1e-6

S5_WIDTH = 512
S5_GROUP_CH = 16
S5_GROUPS = 32
S5_STATE = 64
S5_SUPER = 4
S5_SLAB = 2 * 8 * S5_STATE
S5_LANES = S5_SUPER * S5_SLAB

M2_INNER = 1024
M2_HEAD_DIM = 64
M2_HEADS = 16
M2_GROUPS = 4
M2_STATE = 128
M2_BC = M2_GROUPS * M2_STATE
M2_XBC = M2_INNER + 2 * M2_BC
M2_GROUP_W = (M2_HEADS // M2_GROUPS) * M2_HEAD_DIM

HY_WIDTH = 1024
HY_ORDER = 2
HY_BANDS = 16
HY_EMB = 2 * HY_BANDS + 1
HY_HIDDEN = 64
HY_MIN_DECAY = math.log(1e-2) / 1.5
HY_MAX_DECAY = math.log(1e-2) / 0.3

V7X_VMEM_LIMIT_BYTES = 48 * 1024 * 1024


def _cparams(*sem):
    return pltpu.CompilerParams(dimension_semantics=sem, vmem_limit_bytes=V7X_VMEM_LIMIT_BYTES)


def _bdot(a, b):
    return jnp.dot(a.astype(BF16), b.astype(BF16), preferred_element_type=F32)


def _split3(a):
    a1 = a.astype(BF16)
    r1 = a - a1.astype(F32)
    a2 = r1.astype(BF16)
    a3 = (r1 - a2.astype(F32)).astype(BF16)
    return a1, a2, a3


def _dot_exact_rhs(a, b_bf16):
    a1, a2, a3 = _split3(a)
    d = functools.partial(jnp.dot, preferred_element_type=F32)
    return d(a1, b_bf16) + d(a2, b_bf16) + d(a3, b_bf16)


def _dot_exact_lhs(a_bf16, b):
    b1, b2, b3 = _split3(b)
    d = functools.partial(jnp.dot, preferred_element_type=F32)
    return d(a_bf16, b1) + d(a_bf16, b2) + d(a_bf16, b3)


def _dot3(a, b):
    a1 = a.astype(BF16)
    a2 = (a - a1.astype(F32)).astype(BF16)
    b1 = b.astype(BF16)
    b2 = (b - b1.astype(F32)).astype(BF16)
    d = functools.partial(jnp.dot, preferred_element_type=F32)
    return d(a1, b1) + d(a1, b2) + d(a2, b1)


def _silu(x):
    return x * (1.0 / (1.0 + jnp.exp(-x)))


def _sigmoid(x):
    return 1.0 / (1.0 + jnp.exp(-x))


def _adaln(x, g, shift, scale):
    ms = jnp.mean(x * x, axis=-1, keepdims=True)
    return (x * lax.rsqrt(ms + NORM_EPS) * g) * (1.0 + scale) + shift


def _mod_kernel(c_ref, w_ref, b_ref, o_ref):
    o_ref[0] = _bdot(_silu(c_ref[...]), w_ref[0]) + b_ref[0]


def _modulation(c8, mod_w, mod_b):
    n = mod_w.shape[-1]
    tn = 1536
    return pl.pallas_call(
        _mod_kernel,
        grid=(DEPTH, n // tn),
        in_specs=[pl.BlockSpec((8, D_MODEL), lambda i, j: (0, 0)),
                  pl.BlockSpec((1, D_MODEL, tn), lambda i, j: (i, 0, j)),
                  pl.BlockSpec((1, 1, tn), lambda i, j: (i, 0, j))],
        out_specs=pl.BlockSpec((1, 8, tn), lambda i, j: (i, 0, j)),
        out_shape=jax.ShapeDtypeStruct((DEPTH, 8, n), F32),
        compiler_params=_cparams("parallel", "parallel"),
        name="modulation",
    )(c8, mod_w, mod_b.reshape(DEPTH, 1, n))


def _add_pe_kernel(x_ref, pe_ref, o_ref):
    o_ref[0] = x_ref[0] + pe_ref[...]


def _add_pe(x, pe):
    b, l, d = x.shape
    tl = min(l, 1024)
    return pl.pallas_call(
        _add_pe_kernel,
        grid=(l // tl, b),
        in_specs=[pl.BlockSpec((1, tl, d), lambda i, bb: (bb, i, 0)),
                  pl.BlockSpec((tl, d), lambda i, bb: (i, 0))],
        out_specs=pl.BlockSpec((1, tl, d), lambda i, bb: (bb, i, 0)),
        out_shape=jax.ShapeDtypeStruct(x.shape, F32),
        compiler_params=_cparams("parallel", "parallel"),
        name="add_pe",
    )(x, pe)


def _adaln_mm_kernel(x_ref, g_ref, sh_ref, sc_ref, w_ref, b_ref, o_ref):
    h = _adaln(x_ref[0], g_ref[...], sh_ref[0], sc_ref[0])
    o_ref[0] = _bdot(h, w_ref[...]) + b_ref[...]


def _adaln_mm(x, g, shift, scale, w_bf16, bias):
    b, l, d = x.shape
    n = w_bf16.shape[1]
    tl = min(l, 256)
    return pl.pallas_call(
        _adaln_mm_kernel,
        grid=(b, l // tl),
        in_specs=[pl.BlockSpec((1, tl, d), lambda bb, i: (bb, i, 0)),
                  pl.BlockSpec((1, d), lambda bb, i: (0, 0)),
                  pl.BlockSpec((1, 1, d), lambda bb, i: (bb, 0, 0)),
                  pl.BlockSpec((1, 1, d), lambda bb, i: (bb, 0, 0)),
                  pl.BlockSpec((d, n), lambda bb, i: (0, 0)),
                  pl.BlockSpec((1, n), lambda bb, i: (0, 0))],
        out_specs=pl.BlockSpec((1, tl, n), lambda bb, i: (bb, i, 0)),
        out_shape=jax.ShapeDtypeStruct((b, l, n), F32),
        compiler_params=_cparams("parallel", "parallel"),
        name="adaln_mm",
    )(x, g, shift, scale, w_bf16, bias)


MLP_CHUNK = 512


def _mix_mlp_kernel(n_in, final_norm, *refs):
    x_ref, ga_ref, bo_ref = refs[0:3]
    y_refs = refs[3:3 + n_in]
    wo_refs = refs[3 + n_in:3 + 2 * n_in]
    g_ref, sh_ref, sc_ref, gf_ref, w1_ref, w2_ref, fg_ref, o_ref, a_scr = refs[3 + 2 * n_in:]
    mix = bo_ref[...] + _bdot(y_refs[0][0], wo_refs[0][...])
    for y_ref, wo_ref in zip(y_refs[1:], wo_refs[1:]):
        mix = mix + _bdot(y_ref[0], wo_ref[...])
    x1 = x_ref[0] + ga_ref[0] * mix
    h = _adaln(x1, g_ref[...], sh_ref[0], sc_ref[0]).astype(BF16)
    for c in range(w1_ref.shape[1] // MLP_CHUNK):
        cols = slice(c * MLP_CHUNK, (c + 1) * MLP_CHUNK)
        a = jnp.dot(h, w1_ref[:, cols], preferred_element_type=F32)
        a_scr[:, cols] = jnp.square(jnp.maximum(a, 0.0)).astype(BF16)
    y = x1 + gf_ref[0] * jnp.dot(a_scr[...], w2_ref[...], preferred_element_type=F32)
    if final_norm:
        ms = jnp.mean(y * y, axis=-1, keepdims=True)
        y = y * lax.rsqrt(ms + NORM_EPS) * fg_ref[...]
    o_ref[0] = y


def _mix_mlp(x, gate_a, out_b, ys, out_ws, g, shift, scale, gate_f, w1_bf16, w2_bf16, final_g, final_norm):
    b, l, d = x.shape
    hdim = w1_bf16.shape[1]
    tl = min(l, 512)
    n_in = len(ys)
    row = lambda bb, i: (bb, i, 0)
    vec = pl.BlockSpec((1, 1, d), lambda bb, i: (bb, 0, 0))
    cvec = pl.BlockSpec((1, d), lambda bb, i: (0, 0))
    resident = lambda a: pl.BlockSpec(a.shape, lambda bb, i: (0, 0), pipeline_mode=pl.Buffered(1))
    in_specs = [pl.BlockSpec((1, tl, d), row), vec, cvec]
    in_specs += [pl.BlockSpec((1, tl, y.shape[-1]), row) for y in ys]
    in_specs += [resident(w) for w in out_ws]
    in_specs += [cvec, vec, vec, vec, resident(w1_bf16), resident(w2_bf16), cvec]
    return pl.pallas_call(
        functools.partial(_mix_mlp_kernel, n_in, final_norm),
        grid=(b, l // tl),
        in_specs=in_specs,
        out_specs=pl.BlockSpec((1, tl, d), row),
        out_shape=jax.ShapeDtypeStruct(x.shape, F32),
        scratch_shapes=[pltpu.VMEM((tl, hdim), BF16)],
        compiler_params=_cparams("parallel", "parallel"),
        name="mix_mlp",
    )(x, gate_a, out_b, *ys, *out_ws, g, shift, scale, gate_f, w1_bf16, w2_bf16, final_g)


def _dwconv_kernel(act, x_ref, w_ref, b_ref, o_ref):
    x = x_ref[0]
    l = x.shape[0]
    t = lax.broadcasted_iota(jnp.int32, x.shape, 0)
    prev = jnp.where(t == 0, 0.0, pltpu.roll(x, 1, axis=0))
    nxt = jnp.where(t == l - 1, 0.0, pltpu.roll(x, l - 1, axis=0))
    w = w_ref[...]
    y = prev * w[0:1] + x * w[1:2] + nxt * w[2:3] + b_ref[...]
    if act:
        y = _silu(y)
    o_ref[0] = y


def _dwconv(x, w, bias, act, col0=0, ncols=None):
    b, l, c = x.shape
    ncols = c if ncols is None else ncols
    tc = 256 if l > 1024 else 512
    j0 = col0 // tc
    return pl.pallas_call(
        functools.partial(_dwconv_kernel, act),
        grid=(b, ncols // tc),
        in_specs=[pl.BlockSpec((1, l, tc), lambda bb, j: (bb, 0, j0 + j)),
                  pl.BlockSpec((3, tc), lambda bb, j: (0, j0 + j)),
                  pl.BlockSpec((1, tc), lambda bb, j: (0, j0 + j))],
        out_specs=pl.BlockSpec((1, l, tc), lambda bb, j: (bb, 0, j)),
        out_shape=jax.ShapeDtypeStruct((b, l, ncols), F32),
        compiler_params=_cparams("parallel", "parallel"),
        name="dwconv",
    )(x, w, bias)


def _s5_prepare(lam_re, lam_im, log_dt, b_re, b_im, c_re, c_im):
    step = jnp.exp(log_dt)[..., None]
    mag = jnp.exp(lam_re * step)
    ar = mag * jnp.cos(lam_im * step)
    ai = mag * jnp.sin(lam_im * step)
    den = lam_re * lam_re + lam_im * lam_im
    fr = ((ar - 1.0) * lam_re + ai * lam_im) / den
    fi = (ai * lam_re - (ar - 1.0) * lam_im) / den
    eye = jnp.eye(8, dtype=F32)

    def blockdiag_in(b):
        b4 = b.reshape(S5_SUPER, 8, S5_STATE, S5_GROUP_CH)
        return jnp.einsum('sgpk,gh->sgkhp', b4, eye).reshape(S5_SUPER, 128, 512)

    bd = jnp.concatenate([blockdiag_in(b_re), blockdiag_in(b_im)], axis=-1).astype(BF16)

    cpr = c_re[None] * fr[:, :, None, :] - c_im[None] * fi[:, :, None, :]
    cpi = c_re[None] * fi[:, :, None, :] + c_im[None] * fr[:, :, None, :]

    def blockdiag_out(c):
        c5 = c.reshape(2, S5_SUPER, 8, S5_GROUP_CH, S5_STATE)
        return jnp.einsum('dsgkp,gh->dsgphk', c5, eye).reshape(2, S5_SUPER, 512, 128)

    cd = jnp.concatenate([blockdiag_out(cpr), blockdiag_out(-cpi)], axis=2)
    cd = jnp.concatenate([cd[0], cd[1]], axis=-1).astype(BF16)
    rows = lambda a: jnp.repeat(a.reshape(2, S5_SUPER * 512), 4, axis=0)
    return bd, cd, rows(ar), rows(ai)


S5_ROWS = 256


def _s5_scan_kernel(tc, uf_ref, ub_ref, bd_ref, cd_ref, ar_ref, ai_ref, h0_ref, flip_ref,
                    yf_ref, yb_ref, hfin_ref, u_scr, g_scr, y_scr, h_scr):
    c = pl.program_id(0)

    @pl.when(c == 0)
    def _():
        h_scr[...] = h0_ref[...]

    flip = flip_ref[...]
    rows = 8 * tc
    for b in range(4):
        uf = uf_ref[b]
        ub = jnp.dot(flip, ub_ref[b].astype(BF16), preferred_element_type=F32)
        for sg in range(S5_SUPER):
            u_scr[sg, pl.ds(b, tc, stride=8), :] = uf[:, sg * 128:(sg + 1) * 128]
            u_scr[sg, pl.ds(4 + b, tc, stride=8), :] = ub[:, sg * 128:(sg + 1) * 128]

    for sg in range(S5_SUPER):
        for r0 in range(0, rows, S5_ROWS):
            g_scr[r0:r0 + S5_ROWS, sg * S5_SLAB:(sg + 1) * S5_SLAB] = jnp.dot(
                u_scr[sg, r0:r0 + S5_ROWS, :].astype(BF16), bd_ref[sg], preferred_element_type=F32)

    half = S5_SLAB // 256
    for sg in range(S5_SUPER):
        ar = [ar_ref[:, (sg * half + k) * 128:(sg * half + k + 1) * 128] for k in range(half)]
        ai = [ai_ref[:, (sg * half + k) * 128:(sg * half + k + 1) * 128] for k in range(half)]
        l_re = [sg * S5_SLAB + k * 128 for k in range(half)]
        l_im = [sg * S5_SLAB + (half + k) * 128 for k in range(half)]

        def body(j, carry, ar=ar, ai=ai, l_re=l_re, l_im=l_im):
            r0 = pl.multiple_of(j * 8, 8)
            out = []
            for k in range(half):
                hr, hi = carry[2 * k], carry[2 * k + 1]
                nr = ar[k] * hr - ai[k] * hi + g_scr[pl.ds(r0, 8), l_re[k]:l_re[k] + 128]
                ni = ar[k] * hi + ai[k] * hr + g_scr[pl.ds(r0, 8), l_im[k]:l_im[k] + 128]
                g_scr[pl.ds(r0, 8), l_re[k]:l_re[k] + 128] = nr
                g_scr[pl.ds(r0, 8), l_im[k]:l_im[k] + 128] = ni
                out += [nr, ni]
            return tuple(out)

        init = []
        for k in range(half):
            init += [h_scr[:, l_re[k]:l_re[k] + 128], h_scr[:, l_im[k]:l_im[k] + 128]]
        fin = lax.fori_loop(0, tc, body, tuple(init), unroll=4)
        for k in range(half):
            h_scr[:, l_re[k]:l_re[k] + 128] = fin[2 * k]
            h_scr[:, l_im[k]:l_im[k] + 128] = fin[2 * k + 1]

    for sg in range(S5_SUPER):
        for r0 in range(0, rows, S5_ROWS):
            y = jnp.dot(g_scr[r0:r0 + S5_ROWS, sg * S5_SLAB:(sg + 1) * S5_SLAB].astype(BF16), cd_ref[sg],
                        preferred_element_type=F32)
            y_scr[2 * sg, r0:r0 + S5_ROWS, :] = y[:, :128]
            y_scr[2 * sg + 1, r0:r0 + S5_ROWS, :] = y[:, 128:]

    for b in range(4):
        yf_ref[b] = jnp.concatenate(
            [y_scr[2 * sg, pl.ds(b, tc, stride=8), :] for sg in range(S5_SUPER)], axis=1)
        yb = jnp.concatenate(
            [y_scr[2 * sg + 1, pl.ds(4 + b, tc, stride=8), :] for sg in range(S5_SUPER)], axis=1)
        yb_ref[b] = _dot_exact_lhs(flip, yb)

    hfin_ref[...] = h_scr[...]


def _s5_scan(u, ucol, bd, cd, ar8, ai8, h0):
    b, l, _ = u.shape
    w = S5_WIDTH
    assert b == 4
    tc = 128
    nc = l // tc
    flip = jnp.asarray(np.eye(tc, dtype=np.float32)[::-1], dtype=BF16)
    full = lambda shape: pl.BlockSpec(shape, lambda c: (0,) * len(shape))
    y_shape = jax.ShapeDtypeStruct((b, l, w), F32)
    return pl.pallas_call(
        functools.partial(_s5_scan_kernel, tc),
        grid=(nc,),
        in_specs=[pl.BlockSpec((4, tc, w), lambda c: (0, c, ucol)),
                  pl.BlockSpec((4, tc, w), lambda c: (0, nc - 1 - c, ucol)),
                  full(bd.shape), full(cd.shape), full(ar8.shape), full(ai8.shape),
                  full(h0.shape), full(flip.shape)],
        out_specs=[pl.BlockSpec((4, tc, w), lambda c: (0, c, 0)),
                   pl.BlockSpec((4, tc, w), lambda c: (0, nc - 1 - c, 0)),
                   full((8, S5_LANES))],
        out_shape=[y_shape, y_shape, jax.ShapeDtypeStruct((8, S5_LANES), F32)],
        scratch_shapes=[pltpu.VMEM((S5_SUPER, 8 * tc, 128), F32), pltpu.VMEM((8 * tc, S5_LANES), F32),
                        pltpu.VMEM((2 * S5_SUPER, 8 * tc, 128), F32), pltpu.VMEM((8, S5_LANES), F32)],
        compiler_params=_cparams("arbitrary"),
        name="s5_scan",
    )(u, u, bd, cd, ar8, ai8, h0, flip)


def _gelu_tanh(x):
    return 0.5 * x * (1.0 + jnp.tanh(math.sqrt(2.0 / math.pi) * (x + 0.044715 * (x * x * x))))


def _s5_glu_kernel(yf_ref, yb_ref, u_ref, d_ref, w_ref, b_ref, o_ref):
    y = _gelu_tanh(yf_ref[0] + yb_ref[0] + d_ref[...] * u_ref[0])
    o_ref[0] = y * _sigmoid(_bdot(y, w_ref[...]) + b_ref[...])


def _s5_glu(yf, yb, u, ucol, d_skip, glu_w_bf16, glu_b):
    b, l, w = yf.shape
    tl = min(l, 1024)
    row = lambda bb, i: (bb, i, 0)
    cst = lambda bb, i: (0, 0)
    return pl.pallas_call(
        _s5_glu_kernel,
        grid=(b, l // tl),
        in_specs=[pl.BlockSpec((1, tl, w), row)] * 2 + [
            pl.BlockSpec((1, tl, w), lambda bb, i: (bb, i, ucol)),
            pl.BlockSpec((1, w), cst), pl.BlockSpec((w, w), cst), pl.BlockSpec((1, w), cst)],
        out_specs=pl.BlockSpec((1, tl, w), row),
        out_shape=jax.ShapeDtypeStruct(yf.shape, F32),
        compiler_params=_cparams("parallel", "parallel"),
        name="s5_glu",
    )(yf, yb, u, d_skip, glu_w_bf16, glu_b)


SSD_T = 128


def _softplus(x):
    return jnp.maximum(x, 0.0) + jnp.log(1.0 + jnp.exp(-jnp.abs(x)))


def _ssd_kernel(xs_ref, bm_ref, cm_ref, dtc_ref, dtr_ref, bias_c_ref, bias_r_ref, a_c_ref, a_r_ref,
                tri_ref, trit_ref, exp_ref, h0_ref, y_ref, hfin_ref, s_scr):
    c = pl.program_id(2)

    @pl.when(c == 0)
    def _():
        s_scr[...] = h0_ref[0, 0]

    tri = tri_ref[0]
    expand = exp_ref[...]
    dt_c = _softplus(dtc_ref[0, 0] + bias_c_ref[0])
    dt_r = _softplus(dtr_ref[0, 0] + bias_r_ref[0])
    cum_c = _dot_exact_lhs(tri, dt_c * a_c_ref[0])
    cum_r = _dot_exact_rhs(dt_r * a_r_ref[0], trit_ref[0])
    tot_c = jnp.min(cum_c, axis=0, keepdims=True)

    dt_x = _dot_exact_rhs(dt_c, expand)
    in_x = _dot_exact_rhs(jnp.exp(cum_c), expand)
    out_x = _dot_exact_rhs(jnp.exp(tot_c - cum_c), expand)
    tot_x = _dot_exact_rhs(jnp.broadcast_to(jnp.exp(tot_c), (8, M2_HEADS)), expand)[0:1]

    xdt = xs_ref[0] * dt_x
    xout = (xdt * out_x).astype(BF16)
    xdt = xdt.astype(BF16)
    mask = tri > 0
    for g in range(M2_GROUPS):
        bg = bm_ref[0, :, g * M2_STATE:(g + 1) * M2_STATE].astype(BF16)
        cg = cm_ref[0, :, g * M2_STATE:(g + 1) * M2_STATE].astype(BF16)
        cb = lax.dot_general(cg, bg, (((1,), (1,)), ((), ())), preferred_element_type=F32)
        gc = slice(g * M2_GROUP_W, (g + 1) * M2_GROUP_W)
        s_prev = s_scr[g]
        y_off = jnp.dot(cg, s_prev.astype(BF16), preferred_element_type=F32) * in_x[:, gc]
        for r in range(M2_HEADS // M2_GROUPS):
            h = g * (M2_HEADS // M2_GROUPS) + r
            seg = cum_c[:, h:h + 1] - cum_r[h:h + 1, :]
            m = (cb * jnp.exp(jnp.where(mask, seg, -jnp.inf))).astype(BF16)
            hc = slice(h * M2_HEAD_DIM, (h + 1) * M2_HEAD_DIM)
            y_ref[0, 0, :, hc] = (jnp.dot(m, xdt[:, hc], preferred_element_type=F32)
                                  + y_off[:, r * M2_HEAD_DIM:(r + 1) * M2_HEAD_DIM])
        s_scr[g] = s_prev * tot_x[:, gc] + lax.dot_general(
            bg, xout[:, gc], (((0,), (0,)), ((), ())), preferred_element_type=F32)

    @pl.when(c == pl.num_programs(2) - 1)
    def _():
        hfin_ref[0, 0] = s_scr[...]


def _ssd(xbc, dt_col, dt_row, dt_bias, a_coef, h0):
    b, l, _ = xbc.shape
    t = SSD_T
    nc = l // t
    tri_f = np.tril(np.ones((t, t), np.float32))
    tri = jnp.asarray(np.stack([tri_f, tri_f.T]), dtype=BF16)
    trit = jnp.asarray(np.stack([tri_f.T, tri_f]), dtype=BF16)
    expand = jnp.asarray(np.kron(np.eye(M2_HEADS, dtype=np.float32),
                                 np.ones((1, M2_HEAD_DIM), np.float32)), dtype=BF16)
    chunk = lambda d, c: c + d * (nc - 1 - 2 * c)
    return pl.pallas_call(
        _ssd_kernel,
        grid=(2, b, nc),
        in_specs=[pl.BlockSpec((1, t, M2_INNER), lambda d, bb, c: (bb, chunk(d, c), 0)),
                  pl.BlockSpec((1, t, M2_BC), lambda d, bb, c: (bb, chunk(d, c), 2)),
                  pl.BlockSpec((1, t, M2_BC), lambda d, bb, c: (bb, chunk(d, c), 3)),
                  pl.BlockSpec((1, 1, t, M2_HEADS), lambda d, bb, c: (d, bb, chunk(d, c), 0)),
                  pl.BlockSpec((1, 1, M2_HEADS, t), lambda d, bb, c: (d, bb, 0, chunk(d, c))),
                  pl.BlockSpec((1, 1, M2_HEADS), lambda d, bb, c: (d, 0, 0)),
                  pl.BlockSpec((1, M2_HEADS, 1), lambda d, bb, c: (d, 0, 0)),
                  pl.BlockSpec((1, 1, M2_HEADS), lambda d, bb, c: (d, 0, 0)),
                  pl.BlockSpec((1, M2_HEADS, 1), lambda d, bb, c: (d, 0, 0)),
                  pl.BlockSpec((1, t, t), lambda d, bb, c: (d, 0, 0)),
                  pl.BlockSpec((1, t, t), lambda d, bb, c: (d, 0, 0)),
                  pl.BlockSpec((M2_HEADS, M2_INNER), lambda d, bb, c: (0, 0)),
                  pl.BlockSpec((1, 1, M2_GROUPS, M2_STATE, M2_GROUP_W), lambda d, bb, c: (d, bb, 0, 0, 0))],
        out_specs=[pl.BlockSpec((1, 1, t, M2_INNER), lambda d, bb, c: (d, bb, chunk(d, c), 0)),
                   pl.BlockSpec((1, 1, M2_GROUPS, M2_STATE, M2_GROUP_W), lambda d, bb, c: (d, bb, 0, 0, 0))],
        out_shape=[jax.ShapeDtypeStruct((2, b, l, M2_INNER), F32),
                   jax.ShapeDtypeStruct((2, b, M2_GROUPS, M2_STATE, M2_GROUP_W), F32)],
        scratch_shapes=[pltpu.VMEM((M2_GROUPS, M2_STATE, M2_GROUP_W), F32)],
        compiler_params=_cparams("parallel", "parallel", "arbitrary"),
        name="ssd",
    )(xbc, xbc, xbc, dt_col, dt_row, dt_bias.reshape(2, 1, M2_HEADS), dt_bias.reshape(2, M2_HEADS, 1),
      a_coef.reshape(2, 1, M2_HEADS), a_coef.reshape(2, M2_HEADS, 1), tri, trit, expand, h0)


def _m2_gate_kernel(yf_ref, yb_ref, xs_ref, z_ref, d_ref, g_ref, o_ref):
    y = (yf_ref[0, 0] + yb_ref[0, 0] + d_ref[...] * xs_ref[0]) * _silu(z_ref[0])
    ms = jnp.mean(y * y, axis=-1, keepdims=True)
    o_ref[0] = y * lax.rsqrt(ms + NORM_EPS) * g_ref[...]


def _m2_gate(ydir, xbc, z, zcol, d_x, norm_g):
    _, b, l, w = ydir.shape
    tl = min(l, 512)
    row = lambda bb, i: (bb, i, 0)
    cst = lambda bb, i: (0, 0)
    return pl.pallas_call(
        _m2_gate_kernel,
        grid=(b, l // tl),
        in_specs=[pl.BlockSpec((1, 1, tl, w), lambda bb, i: (0, bb, i, 0)),
                  pl.BlockSpec((1, 1, tl, w), lambda bb, i: (1, bb, i, 0)),
                  pl.BlockSpec((1, tl, w), row), pl.BlockSpec((1, tl, w), lambda bb, i: (bb, i, zcol)),
                  pl.BlockSpec((1, w), cst), pl.BlockSpec((1, w), cst)],
        out_specs=pl.BlockSpec((1, tl, w), row),
        out_shape=jax.ShapeDtypeStruct((b, l, w), F32),
        compiler_params=_cparams("parallel", "parallel"),
        name="m2_gate",
    )(ydir, ydir, xbc, z, d_x, norm_g)


HY_N2 = 128


def _round8(n):
    return (n + 7) // 8 * 8


def _dft2_tables(n1, n2):
    n = n1 * n2
    k1n = n1 // 2 + 1
    k1p = _round8(k1n)
    k1 = np.arange(k1p, dtype=np.float64)[None, :, None]
    valid = (k1 < k1n).astype(np.float64)
    t = (n2 * np.arange(n1, dtype=np.float64)[None, None, :]
         + np.arange(n2, dtype=np.float64)[:, None, None])
    th = 2.0 * np.pi * k1 * t / n
    stage1 = np.concatenate([np.cos(th) * valid, -np.sin(th) * valid], axis=1)
    i2 = np.arange(n2, dtype=np.float64)[None, :]
    k2 = np.arange(n2, dtype=np.float64)[:, None]
    th2 = 2.0 * np.pi * k2 * i2 / n2
    fc, fs = np.cos(th2), np.sin(th2)
    fwd2 = np.block([[fc, fs], [-fs, fc]])
    inv2 = np.block([[fc, -fs], [fs, fc]])
    wk = np.where((k1 == 0) | (k1 == n1 // 2), 1.0, 2.0) * valid
    tho = np.transpose(th[:, :, :n1 // 2], (0, 2, 1))
    wko = np.transpose(wk, (0, 2, 1))
    last = np.concatenate([np.cos(tho) * wko, -np.sin(tho) * wko], axis=2) / n
    f = lambda a: jnp.asarray(a, dtype=F32)
    return dict(k1n=k1n, k1p=k1p, stage1=f(stage1), fwd2=f(fwd2), inv2=f(inv2), last=f(last))


def _dft1_tables(l):
    n = 2 * l
    kn = l + 1
    kp = _round8(kn)
    k = np.arange(kp, dtype=np.float64)[:, None]
    valid = (k < kn).astype(np.float64)
    t = np.arange(n, dtype=np.float64)[None, :]
    th = 2.0 * np.pi * k * t / n
    fwd = np.concatenate([np.cos(th) * valid, -np.sin(th) * valid], axis=0)
    wk = np.where((k == 0) | (k == l), 1.0, 2.0) * valid
    o = np.arange(l, dtype=np.float64)[:, None]
    tho = 2.0 * np.pi * o * k.T / n
    inv = np.concatenate([np.cos(tho) * wk.T, -np.sin(tho) * wk.T], axis=1) / n
    f = lambda a: jnp.asarray(a, dtype=F32)
    return dict(kp=kp, fwd=f(fwd), inv=f(inv))


def _rowdft_kernel(hi, f_ref, x_ref, o_ref):
    dot = _dot3 if hi else _bdot
    o_ref[0] = dot(f_ref[...], x_ref[0])


def _rowdft(fmat, x, hi):
    bx, r, n = x.shape
    m = fmat.shape[0]
    tn = min(n, 4096)
    return pl.pallas_call(
        functools.partial(_rowdft_kernel, hi),
        grid=(bx, n // tn),
        in_specs=[pl.BlockSpec((m, r), lambda b, j: (0, 0)),
                  pl.BlockSpec((1, r, tn), lambda b, j: (b, 0, j))],
        out_specs=pl.BlockSpec((1, m, tn), lambda b, j: (b, 0, j)),
        out_shape=jax.ShapeDtypeStruct((bx, m, n), F32),
        compiler_params=_cparams("parallel", "parallel"),
        name="hy_rowdft",
    )(fmat if hi else fmat.astype(BF16), x)


HY_TS = 8


def _hy_first_kernel(k1p, f_ref, x_ref, o_ref):
    for s in range(HY_TS):
        y = _bdot(f_ref[s], x_ref[:, s, :])
        o_ref[0, :, s, :] = y[:k1p]
        o_ref[1, :, s, :] = y[k1p:]


def _hy_first(stage1, x4):
    bx, r, n2, c = x4.shape
    k1p = stage1.shape[1] // 2
    return pl.pallas_call(
        functools.partial(_hy_first_kernel, k1p),
        grid=(bx, n2 // HY_TS),
        in_specs=[pl.BlockSpec((HY_TS, 2 * k1p, r), lambda b, s: (s, 0, 0)),
                  pl.BlockSpec((None, r, HY_TS, c), lambda b, s: (b, 0, s, 0))],
        out_specs=pl.BlockSpec((None, 2, k1p, HY_TS, c), lambda b, s: (b, 0, 0, s, 0)),
        out_shape=jax.ShapeDtypeStruct((bx, 2, k1p, n2, c), F32),
        compiler_params=_cparams("parallel", "parallel"),
        name="hy_first",
    )(stage1.astype(BF16), x4)


HY_KB = 4


def _hy_mid_kernel(k1n, n2, a_ref, h_ref, fwd_ref, inv_ref, o_ref):
    @pl.when(pl.program_id(0) * HY_KB < k1n)
    def _():
        for j in range(HY_KB):
            x = _bdot(fwd_ref[...], jnp.concatenate([a_ref[0, 0, j], a_ref[0, 1, j]], axis=0))
            xr, xi = x[:n2], x[n2:]
            hr, hi = h_ref[0, j], h_ref[1, j]
            q = jnp.concatenate([xr * hr - xi * hi, xr * hi + xi * hr], axis=0)
            z = _bdot(inv_ref[...], q)
            o_ref[0, 0, j] = z[:n2]
            o_ref[0, 1, j] = z[n2:]

    @pl.when(pl.program_id(0) * HY_KB >= k1n)
    def _():
        o_ref[...] = jnp.zeros_like(o_ref)


def _hy_mid(a5, tab, hspec, order):
    b, _, k1p, n2, c = a5.shape
    return pl.pallas_call(
        functools.partial(_hy_mid_kernel, tab['k1n'], n2),
        grid=(k1p // HY_KB, b),
        in_specs=[pl.BlockSpec((1, 2, HY_KB, n2, c), lambda k, bb: (bb, 0, k, 0, 0)),
                  pl.BlockSpec((2, HY_KB, n2, c), lambda k, bb: (0, k, 0, order)),
                  pl.BlockSpec((2 * n2, 2 * n2), lambda k, bb: (0, 0)),
                  pl.BlockSpec((2 * n2, 2 * n2), lambda k, bb: (0, 0))],
        out_specs=pl.BlockSpec((1, 2, HY_KB, n2, c), lambda k, bb: (bb, 0, k, 0, 0)),
        out_shape=jax.ShapeDtypeStruct(a5.shape, F32),
        compiler_params=_cparams("parallel", "parallel"),
        name="hy_mid",
    )(a5, hspec, tab['fwd2'].astype(BF16), tab['inv2'].astype(BF16))


def _hy_filt_first_kernel(k1p, feat_ref, t_ref, keep_ref, w1_ref, b1_ref, q1_ref, w2_ref, b2_ref, q2_ref,
                          w3_ref, nd_ref, f_ref, o_ref, sum_ref):
    @pl.when(pl.program_id(0) == 0)
    def _():
        sum_ref[...] = jnp.zeros_like(sum_ref)

    n1 = feat_ref.shape[1]
    half = n1 // 2
    feats = feat_ref[...].reshape(HY_TS * n1, feat_ref.shape[2])
    hid = jnp.sin(q1_ref[...] * (_bdot(feats, w1_ref[...]) + b1_ref[...]))
    hid = jnp.sin(q2_ref[...] * (_bdot(hid, w2_ref[...]) + b2_ref[...]))
    acc = jnp.zeros(sum_ref.shape, F32)
    for s in range(HY_TS):
        hs = hid[s * n1:(s + 1) * n1]
        f = jnp.concatenate([_bdot(hs[:half], w3_ref[0]), _bdot(hs[half:], w3_ref[1])], axis=0)
        f = f * jnp.exp(t_ref[s] * nd_ref[...]) * keep_ref[s]
        acc = acc + jnp.sum(jnp.abs(f), axis=0, keepdims=True)
        y = _dot3(f_ref[s], f)
        o_ref[0, :, s, :] = y[:k1p]
        o_ref[1, :, s, :] = y[k1p:]
    sum_ref[...] += acc


def _hy_filt_first(l, tab, f_w1, f_b1, f_freq1, f_w2, f_b2, f_freq2, f_w3):
    c2 = HY_ORDER * HY_WIDTH
    n2 = HY_N2
    n1 = 2 * l // n2
    k1p = tab['k1p']
    t = np.linspace(0.0, 1.0, l, dtype=np.float32)[:, None]
    bands = np.linspace(1e-4, HY_BANDS - 1, HY_BANDS, dtype=np.float32)
    ang = np.float32(2.0 * math.pi / l) * np.arange(l, dtype=np.float32)[:, None] * bands
    feats = np.concatenate([t, np.cos(ang), -np.sin(ang)], axis=-1)
    rows = (n2 * np.arange(n1)[None, :] + np.arange(n2)[:, None]).reshape(-1)
    pos = np.where(rows < l, rows, np.where(rows == l, 0, 2 * l - rows))
    feats_t = jnp.asarray(np.pad(feats[pos], ((0, 0), (0, 128 - HY_EMB))).reshape(n2, n1, 128))
    t_t = jnp.asarray(t[pos].reshape(n2, n1, 1))
    keep_t = jnp.asarray((rows != l).astype(np.float32).reshape(n2, n1, 1))
    w1p = jnp.pad(f_w1, ((0, 128 - HY_EMB), (0, 0)))
    w3d = jnp.transpose(f_w3.reshape(HY_HIDDEN, HY_ORDER, 2, HY_WIDTH), (2, 0, 1, 3)).reshape(2, HY_HIDDEN, c2)
    deltas = jnp.abs(jnp.linspace(HY_MIN_DECAY, HY_MAX_DECAY, HY_WIDTH, dtype=F32))
    negd = jnp.tile(-deltas, HY_ORDER)[None, :]
    cst = lambda i: (0, 0)
    vec = lambda a: a.reshape(1, HY_HIDDEN)
    tile3 = lambda w: pl.BlockSpec((HY_TS, n1, w), lambda i: (i, 0, 0))
    return pl.pallas_call(
        functools.partial(_hy_filt_first_kernel, k1p),
        grid=(n2 // HY_TS,),
        in_specs=[tile3(128), tile3(1), tile3(1),
                  pl.BlockSpec((128, HY_HIDDEN), cst), pl.BlockSpec((1, HY_HIDDEN), cst),
                  pl.BlockSpec((1, HY_HIDDEN), cst),
                  pl.BlockSpec((HY_HIDDEN, HY_HIDDEN), cst), pl.BlockSpec((1, HY_HIDDEN), cst),
                  pl.BlockSpec((1, HY_HIDDEN), cst),
                  pl.BlockSpec((2, HY_HIDDEN, c2), lambda i: (0, 0, 0)),
                  pl.BlockSpec((1, c2), cst),
                  pl.BlockSpec((HY_TS, 2 * k1p, n1), lambda i: (i, 0, 0))],
        out_specs=[pl.BlockSpec((None, 2, k1p, HY_TS, c2), lambda i: (0, 0, 0, i, 0)),
                   pl.BlockSpec((1, c2), cst)],
        out_shape=[jax.ShapeDtypeStruct((1, 2, k1p, n2, c2), F32), jax.ShapeDtypeStruct((1, c2), F32)],
        compiler_params=_cparams("arbitrary"),
        name="hy_filt_first",
    )(feats_t, t_t, keep_t, w1p, vec(f_b1), vec(f_freq1), f_w2, vec(f_b2), vec(f_freq2), w3d, negd,
      tab['stage1'])


def _hy_filt_mid_kernel(n2, a_ref, sum_ref, fwd_ref, o_ref):
    c2 = a_ref.shape[-1]
    x = _dot3(fwd_ref[...], a_ref[0, :, 0].reshape(2 * n2, c2)) * (1.0 / (sum_ref[...] + 1e-6))
    o_ref[0, 0] = x[:n2]
    o_ref[1, 0] = x[n2:]


def _hy_filt_mid(a5, tab, colsum):
    _, _, k1p, n2, c2 = a5.shape
    return pl.pallas_call(
        functools.partial(_hy_filt_mid_kernel, n2),
        grid=(k1p,),
        in_specs=[pl.BlockSpec((1, 2, 1, n2, c2), lambda k: (0, 0, k, 0, 0)),
                  pl.BlockSpec((1, c2), lambda k: (0, 0)),
                  pl.BlockSpec((2 * n2, 2 * n2), lambda k: (0, 0))],
        out_specs=pl.BlockSpec((2, 1, n2, c2), lambda k: (0, k, 0, 0)),
        out_shape=jax.ShapeDtypeStruct((2, k1p, n2, c2), F32),
        compiler_params=_cparams("parallel"),
        name="hy_filt_mid",
    )(a5, colsum, tab['fwd2'])


def _hy_last_kernel(g_ref, b_ref, v_ref, gate_ref, d_ref, o_ref):
    for s in range(HY_TS):
        z = jnp.concatenate([b_ref[0, :, s, :], b_ref[1, :, s, :]], axis=0)
        o_ref[:, s, :] = gate_ref[:, s, :] * (_bdot(g_ref[s], z) + d_ref[...] * v_ref[:, s, :])


def _hy_last(last, bp, v4, gate4, d_vec):
    b, r, n2, c = v4.shape
    k1p = bp.shape[2]
    rows = pl.BlockSpec((None, r, HY_TS, c), lambda bb, s: (bb, 0, s, 0))
    return pl.pallas_call(
        _hy_last_kernel,
        grid=(b, n2 // HY_TS),
        in_specs=[pl.BlockSpec((HY_TS, r, 2 * k1p), lambda bb, s: (s, 0, 0)),
                  pl.BlockSpec((None, 2, k1p, HY_TS, c), lambda bb, s: (bb, 0, 0, s, 0)),
                  rows, rows,
                  pl.BlockSpec((1, c), lambda bb, s: (0, 0))],
        out_specs=rows,
        out_shape=jax.ShapeDtypeStruct(v4.shape, F32),
        compiler_params=_cparams("parallel", "parallel"),
        name="hy_last",
    )(last.astype(BF16), bp, v4, gate4, d_vec)


def _hy_long_conv2(v, gate, hspec, order, d_vec, tab):
    b, l, c = v.shape
    n2 = HY_N2
    n1h = l // n2
    v4 = v.reshape(b, n1h, n2, c)
    a = _hy_first(tab['stage1'][:, :, :n1h], v4)
    bp = _hy_mid(a, tab, hspec, order)
    out = _hy_last(tab['last'], bp, v4, gate.reshape(b, n1h, n2, c), d_vec.reshape(1, c))
    return out.reshape(b, l, c)


def _hy_ctx_kernel(kp, fw_ref, inv_ref, h_ref, v_ref, gate_ref, d_ref, o_ref):
    v = v_ref[0]
    u = _bdot(fw_ref[...], v)
    ur, ui = u[:kp], u[kp:]
    hr, hi = h_ref[0], h_ref[1]
    q = jnp.concatenate([ur * hr - ui * hi, ur * hi + ui * hr], axis=0)
    o_ref[0] = gate_ref[0] * (_bdot(inv_ref[...], q) + d_ref[...] * v)


def _hy_long_conv1(v, gate, hspec, order, d_vec, tab):
    b, l, c = v.shape
    kp = tab['kp']
    row = lambda bb: (bb, 0, 0)
    return pl.pallas_call(
        functools.partial(_hy_ctx_kernel, kp),
        grid=(b,),
        in_specs=[pl.BlockSpec((2 * kp, l), lambda bb: (0, 0)),
                  pl.BlockSpec((l, 2 * kp), lambda bb: (0, 0)),
                  pl.BlockSpec((2, kp, c), lambda bb: (0, 0, order)),
                  pl.BlockSpec((1, l, c), row), pl.BlockSpec((1, l, c), row),
                  pl.BlockSpec((1, c), lambda bb: (0, 0))],
        out_specs=pl.BlockSpec((1, l, c), row),
        out_shape=jax.ShapeDtypeStruct(v.shape, F32),
        compiler_params=_cparams("parallel"),
        name="hy_ctx_conv",
    )(tab['fwd'][:, :l].astype(BF16), tab['inv'].astype(BF16), hspec, v, gate, d_vec.reshape(1, c))


def _hy_filter_kernel(feat_ref, t_ref, keep_ref, w1_ref, b1_ref, q1_ref, w2_ref, b2_ref, q2_ref,
                      w3_ref, nd_ref, f_ref, sum_ref):
    @pl.when(pl.program_id(0) == 0)
    def _():
        sum_ref[...] = jnp.zeros_like(sum_ref)

    hid = jnp.sin(q1_ref[...] * (_bdot(feat_ref[...], w1_ref[...]) + b1_ref[...]))
    hid = jnp.sin(q2_ref[...] * (_bdot(hid, w2_ref[...]) + b2_ref[...]))
    f = _bdot(hid, w3_ref[0]) * jnp.exp(t_ref[...] * nd_ref[...]) * keep_ref[...]
    f_ref[...] = f
    sum_ref[...] += jnp.sum(jnp.abs(f), axis=0, keepdims=True)


def _hy_filters(l, f_w1, f_b1, f_freq1, f_w2, f_b2, f_freq2, f_w3):
    c2 = HY_ORDER * HY_WIDTH
    t = np.linspace(0.0, 1.0, l, dtype=np.float32)[:, None]
    bands = np.linspace(1e-4, HY_BANDS - 1, HY_BANDS, dtype=np.float32)
    ang = np.float32(2.0 * math.pi / l) * np.arange(l, dtype=np.float32)[:, None] * bands
    feats = np.concatenate([t, np.cos(ang), -np.sin(ang)], axis=-1)
    rows = np.arange(2 * l)
    pos = np.where(rows < l, rows, np.where(rows == l, 0, 2 * l - rows))
    feats2 = jnp.asarray(np.pad(feats[pos], ((0, 0), (0, 128 - HY_EMB))))
    keep = jnp.asarray((rows != l).astype(np.float32))[:, None]
    t2 = jnp.asarray(t[pos])
    w1p = jnp.pad(f_w1, ((0, 128 - HY_EMB), (0, 0)))
    w3d = jnp.transpose(f_w3.reshape(HY_HIDDEN, HY_ORDER, 2, HY_WIDTH), (2, 0, 1, 3)).reshape(2, HY_HIDDEN, c2)
    deltas = jnp.abs(jnp.linspace(HY_MIN_DECAY, HY_MAX_DECAY, HY_WIDTH, dtype=F32))
    negd = jnp.tile(-deltas, HY_ORDER)[None, :]
    tr = min(l, 512)
    per_dir = l // tr
    cst = lambda i: (0, 0)
    vec = lambda a: a.reshape(1, HY_HIDDEN)
    return pl.pallas_call(
        _hy_filter_kernel,
        grid=(2 * l // tr,),
        in_specs=[pl.BlockSpec((tr, 128), lambda i: (i, 0)),
                  pl.BlockSpec((tr, 1), lambda i: (i, 0)),
                  pl.BlockSpec((tr, 1), lambda i: (i, 0)),
                  pl.BlockSpec((128, HY_HIDDEN), cst), pl.BlockSpec((1, HY_HIDDEN), cst),
                  pl.BlockSpec((1, HY_HIDDEN), cst),
                  pl.BlockSpec((HY_HIDDEN, HY_HIDDEN), cst), pl.BlockSpec((1, HY_HIDDEN), cst),
                  pl.BlockSpec((1, HY_HIDDEN), cst),
                  pl.BlockSpec((1, HY_HIDDEN, c2), lambda i: (i // per_dir, 0, 0)),
                  pl.BlockSpec((1, c2), cst)],
        out_specs=[pl.BlockSpec((tr, c2), lambda i: (i, 0)), pl.BlockSpec((1, c2), cst)],
        out_shape=[jax.ShapeDtypeStruct((2 * l, c2), F32), jax.ShapeDtypeStruct((1, c2), F32)],
        compiler_params=_cparams("arbitrary"),
        name="hy_filter",
    )(feats2, t2, keep, w1p, vec(f_b1), vec(f_freq1), f_w2, vec(f_b2), vec(f_freq2), w3d, negd)


def _col_scale_kernel(x_ref, sum_ref, o_ref):
    o_ref[...] = x_ref[...] * (1.0 / (sum_ref[...] + 1e-6))


def _hy_spectrum(l, fparams):
    if 2 * l >= 16 * HY_N2:
        tab = _dft2_tables(2 * l // HY_N2, HY_N2)
        a, colsum = _hy_filt_first(l, tab, *fparams)
        return _hy_filt_mid(a, tab, colsum), tab
    filt, colsum = _hy_filters(l, *fparams)
    c2 = filt.shape[1]
    tab = _dft1_tables(l)
    spec = _rowdft(tab['fwd'], filt[None], True)[0]
    spec = pl.pallas_call(
        _col_scale_kernel,
        in_specs=[pl.BlockSpec(spec.shape, lambda: (0, 0)), pl.BlockSpec((1, c2), lambda: (0, 0))],
        out_specs=pl.BlockSpec(spec.shape, lambda: (0, 0)),
        out_shape=jax.ShapeDtypeStruct(spec.shape, F32),
        name="hy_col_scale",
    )(spec, colsum)
    return spec.reshape(2, tab['kp'], c2), tab


def _hyena_mixer(h_proj, conv_w, conv_b, hspec, tab, d_skip):
    c = HY_WIDTH
    parts = [_dwconv(h_proj, conv_w, conv_b, False, col0=i * c, ncols=c) for i in range(3)]
    v, x1, x2 = parts
    conv = _hy_long_conv2 if 'k1p' in tab else _hy_long_conv1
    y = conv(v, x1, hspec, 0, d_skip[0], tab)
    return conv(y, x2, hspec, 1, d_skip[1], tab)


def _grid_sincos(n, dm):
    rows = n // GRID_W
    quarter = dm // 4
    omega = 1.0 / (10000.0 ** (jnp.arange(quarter, dtype=F32) / quarter))
    ang_r = jnp.arange(rows, dtype=F32)[:, None] * omega
    ang_c = jnp.arange(GRID_W, dtype=F32)[:, None] * omega
    emb_r = jnp.concatenate([jnp.sin(ang_r), jnp.cos(ang_r)], axis=-1)
    emb_c = jnp.concatenate([jnp.sin(ang_c), jnp.cos(ang_c)], axis=-1)
    half = emb_r.shape[-1]
    pe = jnp.concatenate([jnp.broadcast_to(emb_r[:, None, :], (rows, GRID_W, half)),
                          jnp.broadcast_to(emb_c[None, :, :], (rows, GRID_W, half))], axis=-1)
    return pe.reshape(rows * GRID_W, 2 * half)


EV_Z_OFF = M2_XBC
EV_U_OFF = EV_Z_OFF + M2_INNER
EV_DT_OFF = EV_U_OFF + S5_WIDTH
EV_PROJ_W = EV_DT_OFF + 128


def _even_mixer(h_in, p, states):
    x, g, shift, scale = h_in
    b, l, _ = x.shape
    proj = _adaln_mm(x, g, shift, scale, p['w_in'], jnp.zeros((1, EV_PROJ_W), F32))
    s5_h0, m2_h0 = states

    ucol = EV_U_OFF // S5_WIDTH
    yf, yb, s5_fin = _s5_scan(proj, ucol, p['s5_bd'], p['s5_cd'], p['s5_ar'], p['s5_ai'], s5_h0)
    y_s5 = _s5_glu(yf, yb, proj, ucol, p['s5_d'], p['s5_glu_w'], p['s5_glu_b'])

    xbc = _dwconv(proj, p['m2_conv_w'], p['m2_conv_b'], True, col0=0, ncols=M2_XBC)
    dt4 = proj[..., EV_DT_OFF:EV_DT_OFF + 2 * M2_HEADS].reshape(b, l, 2, M2_HEADS)
    dt_col = jnp.transpose(dt4, (2, 0, 1, 3))
    dt_row = jnp.transpose(dt4, (2, 0, 3, 1))
    ydir, m2_fin = _ssd(xbc, dt_col, dt_row, p['m2_dt_bias'], p['m2_a'], m2_h0)
    y_m2 = _m2_gate(ydir, xbc, proj, EV_Z_OFF // M2_INNER, p['m2_d'], p['m2_norm_g'])
    return (y_s5, y_m2), (s5_fin, m2_fin)


def kernel(x, c, ctx, c_ctx, mod_w, mod_b, norm_mix_g, norm_mlp_g, mlp_w1, mlp_w2, final_norm_g, ev_in_w, ev_out_w, s5_lam_re, s5_lam_im, s5_log_dt, s5_b_re, s5_b_im, s5_c_re, s5_c_im, s5_d, s5_glu_w, s5_glu_b, m2_conv_w, m2_conv_b, m2_dt_bias, m2_a_log, m2_d, m2_norm_g, hy_in_w, hy_in_b, hy_conv_w, hy_conv_b, hy_f_w1, hy_f_b1, hy_f_freq1, hy_f_w2, hy_f_b2, hy_f_freq2, hy_f_w3, hy_d, hy_out_w, hy_out_b):
    bsz, n, dm = x.shape
    lc = ctx.shape[1]
    x = _add_pe(x, _grid_sincos(n, dm))

    c8 = jnp.concatenate([c, c_ctx[None], jnp.zeros((3, dm), F32)], axis=0)
    mods = _modulation(c8, mod_w, mod_b).reshape(DEPTH, 8, N_MOD, dm)
    final_g = final_norm_g.reshape(1, dm)

    for i in range(DEPTH):
        j = i // 2
        mx = [mods[i, :bsz, k][:, None, :] for k in range(N_MOD)]
        mc = [jnp.broadcast_to(mods[i, bsz, k][None, None, :], (bsz, 1, dm)) for k in range(N_MOD)]
        g_mix = norm_mix_g[i].reshape(1, dm)
        g_mlp = norm_mlp_g[i].reshape(1, dm)
        ctx_later = any(k % 2 == 0 for k in range(i + 1, DEPTH))
        w1 = mlp_w1[i].astype(BF16)
        w2 = mlp_w2[i].astype(BF16)

        if i % 2 == 0:
            in_w = ev_in_w[j]
            o0, o1, o2 = S5_WIDTH, S5_WIDTH + M2_INNER, S5_WIDTH + M2_INNER + M2_XBC
            bd, cd, ar8, ai8 = _s5_prepare(s5_lam_re[j], s5_lam_im[j], s5_log_dt[j], s5_b_re[j], s5_b_im[j],
                                           s5_c_re[j], s5_c_im[j])
            p = dict(
                w_in=jnp.concatenate([in_w[:, o1:o2], in_w[:, o0:o1], in_w[:, :o0],
                                      jnp.pad(in_w[:, o2:], ((0, 0), (0, 128 - 2 * M2_HEADS)))],
                                     axis=1).astype(BF16),
                s5_bd=bd, s5_cd=cd, s5_ar=ar8, s5_ai=ai8,
                s5_d=s5_d[j].reshape(1, S5_WIDTH), s5_glu_w=s5_glu_w[j].astype(BF16),
                s5_glu_b=s5_glu_b[j].reshape(1, S5_WIDTH),
                m2_conv_w=m2_conv_w[j], m2_conv_b=m2_conv_b[j].reshape(1, M2_XBC),
                m2_dt_bias=m2_dt_bias[j], m2_a=-jnp.exp(m2_a_log[j]),
                m2_d=jnp.repeat(m2_d[j], M2_HEAD_DIM).reshape(1, M2_INNER),
                m2_norm_g=m2_norm_g[j].reshape(1, M2_INNER))
            zero_states = (jnp.zeros((8, S5_LANES), F32),
                           jnp.zeros((2, bsz, M2_GROUPS, M2_STATE, M2_GROUP_W), F32))
            ys_c, ctx_states = _even_mixer((ctx, g_mix, mc[0], mc[1]), p, zero_states)
            ys_x, _ = _even_mixer((x, g_mix, mx[0], mx[1]), p, ctx_states)
            out_ws = [ev_out_w[j][:S5_WIDTH].astype(BF16), ev_out_w[j][S5_WIDTH:].astype(BF16)]
            out_b = jnp.zeros((1, dm), F32)
        else:
            fparams = (hy_f_w1[j], hy_f_b1[j], hy_f_freq1[j], hy_f_w2[j], hy_f_b2[j], hy_f_freq2[j], hy_f_w3[j])
            in_w = hy_in_w[j].astype(BF16)
            in_b = hy_in_b[j].reshape(1, -1)
            conv_b = hy_conv_b[j].reshape(1, -1)
            hspec, tab = _hy_spectrum(n, fparams)
            ys_x = [_hyena_mixer(_adaln_mm(x, g_mix, mx[0], mx[1], in_w, in_b), hy_conv_w[j], conv_b,
                                 hspec, tab, hy_d[j])]
            if ctx_later:
                hspec_c, tab_c = _hy_spectrum(lc, fparams)
                ys_c = [_hyena_mixer(_adaln_mm(ctx, g_mix, mc[0], mc[1], in_w, in_b), hy_conv_w[j], conv_b,
                                     hspec_c, tab_c, hy_d[j])]
            out_ws = [hy_out_w[j].astype(BF16)]
            out_b = hy_out_b[j].reshape(1, dm)

        x = _mix_mlp(x, mx[2], out_b, ys_x, out_ws, g_mlp, mx[3], mx[4], mx[5], w1, w2, final_g, i == DEPTH - 1)
        if ctx_later:
            ctx = _mix_mlp(ctx, mc[2], out_b, ys_c, out_ws, g_mlp, mc[3], mc[4], mc[5], w1, w2, final_g, False)
    return x
```

```python
import functools
import math

import numpy as np
import jax
import jax.numpy as jnp
from jax import lax
from jax.experimental import pallas as pl
from jax.experimental.pallas import tpu as pltpu

F32 = jnp.float32
BF16 = jnp.bfloat16

D_MODEL = 1024
DEPTH = 4
GRID_W = 64
N_MOD = 6
MLP_HIDDEN = 4 * D_MODEL
NORM_EPS = 1e-6

S5_WIDTH = 512
S5_GROUP_CH = 16
S5_GROUPS = 32
S5_STATE = 64
S5_SUPER = 4
S5_SLAB = 2 * 8 * S5_STATE
S5_LANES = S5_SUPER * S5_SLAB

M2_INNER = 1024
M2_HEAD_DIM = 64
M2_HEADS = 16
M2_GROUPS = 4
M2_STATE = 128
M2_BC = M2_GROUPS * M2_STATE
M2_XBC = M2_INNER + 2 * M2_BC
M2_GROUP_W = (M2_HEADS // M2_GROUPS) * M2_HEAD_DIM

HY_WIDTH = 1024
HY_ORDER = 2
HY_BANDS = 16
HY_EMB = 2 * HY_BANDS + 1
HY_HIDDEN = 64
HY_MIN_DECAY = math.log(1e-2) / 1.5
HY_MAX_DECAY = math.log(1e-2) / 0.3

V7X_VMEM_LIMIT_BYTES = 48 * 1024 * 1024


def _cparams(*sem):
    return pltpu.CompilerParams(dimension_semantics=sem, vmem_limit_bytes=V7X_VMEM_LIMIT_BYTES)


def _bdot(a, b):
    return jnp.dot(a.astype(BF16), b.astype(BF16), preferred_element_type=F32)


def _split3(a):
    a1 = a.astype(BF16)
    r1 = a - a1.astype(F32)
    a2 = r1.astype(BF16)
    a3 = (r1 - a2.astype(F32)).astype(BF16)
    return a1, a2, a3


def _dot_exact_rhs(a, b_bf16):
    a1, a2, a3 = _split3(a)
    d = functools.partial(jnp.dot, preferred_element_type=F32)
    return d(a1, b_bf16) + d(a2, b_bf16) + d(a3, b_bf16)


def _dot_split2_rhs(a, b_bf16):
    a1 = a.astype(BF16)
    a2 = (a - a1.astype(F32)).astype(BF16)
    d = functools.partial(jnp.dot, preferred_element_type=F32)
    return d(a1, b_bf16) + d(a2, b_bf16)


def _dot_exact_lhs(a_bf16, b):
    b1, b2, b3 = _split3(b)
    d = functools.partial(jnp.dot, preferred_element_type=F32)
    return d(a_bf16, b1) + d(a_bf16, b2) + d(a_bf16, b3)


def _dot3(a, b):
    a1 = a.astype(BF16)
    a2 = (a - a1.astype(F32)).astype(BF16)
    b1 = b.astype(BF16)
    b2 = (b - b1.astype(F32)).astype(BF16)
    d = functools.partial(jnp.dot, preferred_element_type=F32)
    return d(a1, b1) + d(a1, b2) + d(a2, b1)


def _silu(x):
    return x * (1.0 / (1.0 + jnp.exp(-x)))


def _sigmoid(x):
    return 1.0 / (1.0 + jnp.exp(-x))


def _adaln(x, g, shift, scale):
    ms = jnp.mean(x * x, axis=-1, keepdims=True)
    return (x * lax.rsqrt(ms + NORM_EPS) * g) * (1.0 + scale) + shift


def _mod_kernel(c_ref, w_ref, b_ref, o_ref):
    o_ref[0] = _bdot(_silu(c_ref[...]), w_ref[0]) + b_ref[0]


def _modulation(c8, mod_w, mod_b):
    n = mod_w.shape[-1]
    tn = 1536
    return pl.pallas_call(
        _mod_kernel,
        grid=(DEPTH, n // tn),
        in_specs=[pl.BlockSpec((8, D_MODEL), lambda i, j: (0, 0)),
                  pl.BlockSpec((1, D_MODEL, tn), lambda i, j: (i, 0, j)),
                  pl.BlockSpec((1, 1, tn), lambda i, j: (i, 0, j))],
        out_specs=pl.BlockSpec((1, 8, tn), lambda i, j: (i, 0, j)),
        out_shape=jax.ShapeDtypeStruct((DEPTH, 8, n), F32),
        compiler_params=_cparams("parallel", "parallel"),
        name="modulation",
    )(c8, mod_w, mod_b.reshape(DEPTH, 1, n))


def _add_pe_kernel(x_ref, pe_ref, o_ref):
    o_ref[0] = x_ref[0] + pe_ref[...]


def _add_pe(x, pe):
    b, l, d = x.shape
    tl = min(l, 1024)
    return pl.pallas_call(
        _add_pe_kernel,
        grid=(l // tl, b),
        in_specs=[pl.BlockSpec((1, tl, d), lambda i, bb: (bb, i, 0)),
                  pl.BlockSpec((tl, d), lambda i, bb: (i, 0))],
        out_specs=pl.BlockSpec((1, tl, d), lambda i, bb: (bb, i, 0)),
        out_shape=jax.ShapeDtypeStruct(x.shape, F32),
        compiler_params=_cparams("parallel", "parallel"),
        name="add_pe",
    )(x, pe)


def _adaln_mm_kernel(x_ref, g_ref, sh_ref, sc_ref, w_ref, b_ref, o_ref):
    h = _adaln(x_ref[0], g_ref[...], sh_ref[0], sc_ref[0])
    o_ref[0] = (_bdot(h, w_ref[...]) + b_ref[...]).astype(o_ref.dtype)


def _adaln_mm(x, g, shift, scale, w_bf16, bias):
    b, l, d = x.shape
    n = w_bf16.shape[1]
    tl = min(l, 512)
    return pl.pallas_call(
        _adaln_mm_kernel,
        grid=(b, l // tl),
        in_specs=[pl.BlockSpec((1, tl, d), lambda bb, i: (bb, i, 0)),
                  pl.BlockSpec((1, d), lambda bb, i: (0, 0)),
                  pl.BlockSpec((1, 1, d), lambda bb, i: (bb, 0, 0)),
                  pl.BlockSpec((1, 1, d), lambda bb, i: (bb, 0, 0)),
                  pl.BlockSpec((d, n), lambda bb, i: (0, 0)),
                  pl.BlockSpec((1, n), lambda bb, i: (0, 0))],
        out_specs=pl.BlockSpec((1, tl, n), lambda bb, i: (bb, i, 0)),
        out_shape=jax.ShapeDtypeStruct((b, l, n), BF16),
        compiler_params=_cparams("parallel", "parallel"),
        name="adaln_mm",
    )(x, g, shift, scale, w_bf16, bias)


MLP_CHUNK = 512


def _mix_mlp_kernel(n_in, final_norm, *refs):
    x_ref, ga_ref, bo_ref = refs[0:3]
    y_refs = refs[3:3 + n_in]
    wo_refs = refs[3 + n_in:3 + 2 * n_in]
    g_ref, sh_ref, sc_ref, gf_ref, w1_ref, w2_ref, fg_ref, o_ref, a_scr = refs[3 + 2 * n_in:]
    mix = bo_ref[...] + _bdot(y_refs[0][0], wo_refs[0][...])
    for y_ref, wo_ref in zip(y_refs[1:], wo_refs[1:]):
        mix = mix + _bdot(y_ref[0], wo_ref[...])
    x1 = x_ref[0] + ga_ref[0] * mix
    h = _adaln(x1, g_ref[...], sh_ref[0], sc_ref[0]).astype(BF16)
    for c in range(w1_ref.shape[1] // MLP_CHUNK):
        cols = slice(c * MLP_CHUNK, (c + 1) * MLP_CHUNK)
        a = jnp.dot(h, w1_ref[:, cols], preferred_element_type=F32)
        a_scr[:, cols] = jnp.square(jnp.maximum(a, 0.0)).astype(BF16)
    y = x1 + gf_ref[0] * jnp.dot(a_scr[...], w2_ref[...], preferred_element_type=F32)
    if final_norm:
        ms = jnp.mean(y * y, axis=-1, keepdims=True)
        y = y * lax.rsqrt(ms + NORM_EPS) * fg_ref[...]
    o_ref[0] = y


def _mix_mlp(x, gate_a, out_b, ys, out_ws, g, shift, scale, gate_f, w1_bf16, w2_bf16, final_g, final_norm):
    b, l, d = x.shape
    hdim = w1_bf16.shape[1]
    tl = min(l, 512)
    n_in = len(ys)
    row = lambda bb, i: (bb, i, 0)
    vec = pl.BlockSpec((1, 1, d), lambda bb, i: (bb, 0, 0))
    cvec = pl.BlockSpec((1, d), lambda bb, i: (0, 0))
    resident = lambda a: pl.BlockSpec(a.shape, lambda bb, i: (0, 0), pipeline_mode=pl.Buffered(1))
    in_specs = [pl.BlockSpec((1, tl, d), row), vec, cvec]
    in_specs += [pl.BlockSpec((1, tl, y.shape[-1]), row) for y in ys]
    in_specs += [resident(w) for w in out_ws]
    in_specs += [cvec, vec, vec, vec, resident(w1_bf16), resident(w2_bf16), cvec]
    return pl.pallas_call(
        functools.partial(_mix_mlp_kernel, n_in, final_norm),
        grid=(b, l // tl),
        in_specs=in_specs,
        out_specs=pl.BlockSpec((1, tl, d), row),
        out_shape=jax.ShapeDtypeStruct(x.shape, F32),
        scratch_shapes=[pltpu.VMEM((tl, hdim), BF16)],
        compiler_params=_cparams("parallel", "parallel"),
        name="mix_mlp",
    )(x, gate_a, out_b, *ys, *out_ws, g, shift, scale, gate_f, w1_bf16, w2_bf16, final_g)


def _dwconv_kernel(act, x_ref, w_ref, b_ref, o_ref):
    x = x_ref[0].astype(F32)
    l = x.shape[0]
    t = lax.broadcasted_iota(jnp.int32, x.shape, 0)
    prev = jnp.where(t == 0, 0.0, pltpu.roll(x, 1, axis=0))
    nxt = jnp.where(t == l - 1, 0.0, pltpu.roll(x, l - 1, axis=0))
    w = w_ref[...]
    y = prev * w[0:1] + x * w[1:2] + nxt * w[2:3] + b_ref[...]
    if act:
        y = _silu(y)
    o_ref[0] = y


def _dwconv(x, w, bias, act, col0=0, ncols=None):
    b, l, c = x.shape
    ncols = c if ncols is None else ncols
    tc = 256 if l > 1024 else 512
    j0 = col0 // tc
    return pl.pallas_call(
        functools.partial(_dwconv_kernel, act),
        grid=(b, ncols // tc),
        in_specs=[pl.BlockSpec((1, l, tc), lambda bb, j: (bb, 0, j0 + j)),
                  pl.BlockSpec((3, tc), lambda bb, j: (0, j0 + j)),
                  pl.BlockSpec((1, tc), lambda bb, j: (0, j0 + j))],
        out_specs=pl.BlockSpec((1, l, tc), lambda bb, j: (bb, 0, j)),
        out_shape=jax.ShapeDtypeStruct((b, l, ncols), F32),
        compiler_params=_cparams("parallel", "parallel"),
        name="dwconv",
    )(x, w, bias)


def _s5_prepare(lam_re, lam_im, log_dt, b_re, b_im, c_re, c_im):
    step = jnp.exp(log_dt)[..., None]
    mag = jnp.exp(lam_re * step)
    ar = mag * jnp.cos(lam_im * step)
    ai = mag * jnp.sin(lam_im * step)
    den = lam_re * lam_re + lam_im * lam_im
    fr = ((ar - 1.0) * lam_re + ai * lam_im) / den
    fi = (ai * lam_re - (ar - 1.0) * lam_im) / den
    eye = jnp.eye(8, dtype=F32)

    def blockdiag_in(b):
        b4 = b.reshape(S5_SUPER, 8, S5_STATE, S5_GROUP_CH)
        return jnp.einsum('sgpk,gh->sgkhp', b4, eye).reshape(S5_SUPER, 128, 512)

    bd = jnp.concatenate([blockdiag_in(b_re), blockdiag_in(b_im)], axis=-1).astype(BF16)

    cpr = c_re[None] * fr[:, :, None, :] - c_im[None] * fi[:, :, None, :]
    cpi = c_re[None] * fi[:, :, None, :] + c_im[None] * fr[:, :, None, :]

    def blockdiag_out(c):
        c5 = c.reshape(2, S5_SUPER, 8, S5_GROUP_CH, S5_STATE)
        return jnp.einsum('dsgkp,gh->dsgphk', c5, eye).reshape(2, S5_SUPER, 512, 128)

    cd = jnp.concatenate([blockdiag_out(cpr), blockdiag_out(-cpi)], axis=2)
    cd = jnp.concatenate([cd[0], cd[1]], axis=-1).astype(BF16)
    rows = lambda a: jnp.repeat(a.reshape(2, S5_SUPER * 512), 4, axis=0)
    return bd, cd, rows(ar), rows(ai)


S5_ROWS = 256


def _s5_scan_kernel(tc, uf_ref, ub_ref, bd_ref, cd_ref, ar_ref, ai_ref, h0_ref, flip_ref,
                    yf_ref, yb_ref, hfin_ref, u_scr, g_scr, y_scr, h_scr):
    c = pl.program_id(0)

    @pl.when(c == 0)
    def _():
        h_scr[...] = h0_ref[...]

    flip = flip_ref[...]
    rows = 8 * tc
    for b in range(4):
        uf = uf_ref[b].astype(F32)
        ub = jnp.dot(flip, ub_ref[b].astype(BF16), preferred_element_type=F32)
        for sg in range(S5_SUPER):
            u_scr[sg, pl.ds(b, tc, stride=8), :] = uf[:, sg * 128:(sg + 1) * 128]
            u_scr[sg, pl.ds(4 + b, tc, stride=8), :] = ub[:, sg * 128:(sg + 1) * 128]

    for sg in range(S5_SUPER):
        for r0 in range(0, rows, S5_ROWS):
            g_scr[r0:r0 + S5_ROWS, sg * S5_SLAB:(sg + 1) * S5_SLAB] = jnp.dot(
                u_scr[sg, r0:r0 + S5_ROWS, :].astype(BF16), bd_ref[sg], preferred_element_type=F32)

    half = S5_SLAB // 256
    for sg in range(S5_SUPER):
        ar = [ar_ref[:, (sg * half + k) * 128:(sg * half + k + 1) * 128] for k in range(half)]
        ai = [ai_ref[:, (sg * half + k) * 128:(sg * half + k + 1) * 128] for k in range(half)]
        l_re = [sg * S5_SLAB + k * 128 for k in range(half)]
        l_im = [sg * S5_SLAB + (half + k) * 128 for k in range(half)]

        def body(j, carry, ar=ar, ai=ai, l_re=l_re, l_im=l_im):
            r0 = pl.multiple_of(j * 8, 8)
            out = []
            for k in range(half):
                hr, hi = carry[2 * k], carry[2 * k + 1]
                nr = ar[k] * hr - ai[k] * hi + g_scr[pl.ds(r0, 8), l_re[k]:l_re[k] + 128]
                ni = ar[k] * hi + ai[k] * hr + g_scr[pl.ds(r0, 8), l_im[k]:l_im[k] + 128]
                g_scr[pl.ds(r0, 8), l_re[k]:l_re[k] + 128] = nr
                g_scr[pl.ds(r0, 8), l_im[k]:l_im[k] + 128] = ni
                out += [nr, ni]
            return tuple(out)

        init = []
        for k in range(half):
            init += [h_scr[:, l_re[k]:l_re[k] + 128], h_scr[:, l_im[k]:l_im[k] + 128]]
        fin = lax.fori_loop(0, tc, body, tuple(init), unroll=4)
        for k in range(half):
            h_scr[:, l_re[k]:l_re[k] + 128] = fin[2 * k]
            h_scr[:, l_im[k]:l_im[k] + 128] = fin[2 * k + 1]

    for sg in range(S5_SUPER):
        for r0 in range(0, rows, S5_ROWS):
            y = jnp.dot(g_scr[r0:r0 + S5_ROWS, sg * S5_SLAB:(sg + 1) * S5_SLAB].astype(BF16), cd_ref[sg],
                        preferred_element_type=F32)
            y_scr[2 * sg, r0:r0 + S5_ROWS, :] = y[:, :128]
            y_scr[2 * sg + 1, r0:r0 + S5_ROWS, :] = y[:, 128:]

    for b in range(4):
        yf_ref[b] = jnp.concatenate(
            [y_scr[2 * sg, pl.ds(b, tc, stride=8), :] for sg in range(S5_SUPER)], axis=1)
        yb = jnp.concatenate(
            [y_scr[2 * sg + 1, pl.ds(4 + b, tc, stride=8), :] for sg in range(S5_SUPER)], axis=1)
        yb_ref[b] = _dot_exact_lhs(flip, yb)

    hfin_ref[...] = h_scr[...]


def _s5_scan(u, ucol, bd, cd, ar8, ai8, h0):
    b, l, _ = u.shape
    w = S5_WIDTH
    assert b == 4
    tc = 128
    nc = l // tc
    flip = jnp.asarray(np.eye(tc, dtype=np.float32)[::-1], dtype=BF16)
    full = lambda shape: pl.BlockSpec(shape, lambda c: (0,) * len(shape))
    y_shape = jax.ShapeDtypeStruct((b, l, w), F32)
    return pl.pallas_call(
        functools.partial(_s5_scan_kernel, tc),
        grid=(nc,),
        in_specs=[pl.BlockSpec((4, tc, w), lambda c: (0, c, ucol)),
                  pl.BlockSpec((4, tc, w), lambda c: (0, nc - 1 - c, ucol)),
                  full(bd.shape), full(cd.shape), full(ar8.shape), full(ai8.shape),
                  full(h0.shape), full(flip.shape)],
        out_specs=[pl.BlockSpec((4, tc, w), lambda c: (0, c, 0)),
                   pl.BlockSpec((4, tc, w), lambda c: (0, nc - 1 - c, 0)),
                   full((8, S5_LANES))],
        out_shape=[y_shape, y_shape, jax.ShapeDtypeStruct((8, S5_LANES), F32)],
        scratch_shapes=[pltpu.VMEM((S5_SUPER, 8 * tc, 128), F32), pltpu.VMEM((8 * tc, S5_LANES), F32),
                        pltpu.VMEM((2 * S5_SUPER, 8 * tc, 128), F32), pltpu.VMEM((8, S5_LANES), F32)],
        compiler_params=_cparams("arbitrary"),
        name="s5_scan",
    )(u, u, bd, cd, ar8, ai8, h0, flip)


def _gelu_tanh(x):
    return 0.5 * x * (1.0 + jnp.tanh(math.sqrt(2.0 / math.pi) * (x + 0.044715 * (x * x * x))))


def _s5_glu_kernel(yf_ref, yb_ref, u_ref, d_ref, w_ref, b_ref, o_ref):
    y = _gelu_tanh(yf_ref[0] + yb_ref[0] + d_ref[...] * u_ref[0].astype(F32))
    o_ref[0] = y * _sigmoid(_bdot(y, w_ref[...]) + b_ref[...])


def _s5_glu(yf, yb, u, ucol, d_skip, glu_w_bf16, glu_b):
    b, l, w = yf.shape
    tl = min(l, 1024)
    row = lambda bb, i: (bb, i, 0)
    cst = lambda bb, i: (0, 0)
    return pl.pallas_call(
        _s5_glu_kernel,
        grid=(b, l // tl),
        in_specs=[pl.BlockSpec((1, tl, w), row)] * 2 + [
            pl.BlockSpec((1, tl, w), lambda bb, i: (bb, i, ucol)),
            pl.BlockSpec((1, w), cst), pl.BlockSpec((w, w), cst), pl.BlockSpec((1, w), cst)],
        out_specs=pl.BlockSpec((1, tl, w), row),
        out_shape=jax.ShapeDtypeStruct(yf.shape, F32),
        compiler_params=_cparams("parallel", "parallel"),
        name="s5_glu",
    )(yf, yb, u, d_skip, glu_w_bf16, glu_b)


SSD_T = 128
SSD_NB = 2


def _softplus(x):
    return jnp.maximum(x, 0.0) + jnp.log(1.0 + jnp.exp(-jnp.abs(x)))


def _ssd_kernel(xs_ref, bm_ref, cm_ref, dtc_ref, dtr_ref, bias_c_ref, bias_r_ref, a_c_ref, a_r_ref,
                tri_ref, trit_ref, exp_ref, h0_ref, y_ref, hfin_ref, s_scr):
    c = pl.program_id(2)

    @pl.when(c == 0)
    def _():
        s_scr[...] = h0_ref[0]

    tri = tri_ref[0]
    expand = exp_ref[...]
    mask = tri > 0
    for bi in range(SSD_NB):
        dt_c = _softplus(dtc_ref[0, bi] + bias_c_ref[0])
        dt_r = _softplus(dtr_ref[0, bi] + bias_r_ref[0])
        cum_c = _dot_exact_lhs(tri, dt_c * a_c_ref[0])
        cum_r = _dot_exact_rhs(dt_r * a_r_ref[0], trit_ref[0])
        tot_c = jnp.min(cum_c, axis=0, keepdims=True)

        t = dt_c.shape[0]
        ex = _dot_split2_rhs(jnp.concatenate([dt_c, jnp.exp(cum_c), jnp.exp(tot_c - cum_c)], axis=0), expand)
        dt_x = ex[:t]
        in_x = ex[t:2 * t]
        out_x = ex[2 * t:]
        tot_x = jnp.where(pl.program_id(0) == 0, in_x[t - 1:t], in_x[0:1])

        xdt = xs_ref[bi] * dt_x
        xout = (xdt * out_x).astype(BF16)
        xdt = xdt.astype(BF16)
        for g in range(M2_GROUPS):
            bg = bm_ref[bi, :, g * M2_STATE:(g + 1) * M2_STATE].astype(BF16)
            cg = cm_ref[bi, :, g * M2_STATE:(g + 1) * M2_STATE].astype(BF16)
            cb = lax.dot_general(cg, bg, (((1,), (1,)), ((), ())), preferred_element_type=F32)
            gc = slice(g * M2_GROUP_W, (g + 1) * M2_GROUP_W)
            s_prev = s_scr[bi, g]
            y_off = jnp.dot(cg, s_prev.astype(BF16), preferred_element_type=F32) * in_x[:, gc]
            for r in range(M2_HEADS // M2_GROUPS):
                h = g * (M2_HEADS // M2_GROUPS) + r
                seg = cum_c[:, h:h + 1] - cum_r[h:h + 1, :]
                m = (cb * jnp.exp(jnp.where(mask, seg, -jnp.inf))).astype(BF16)
                hc = slice(h * M2_HEAD_DIM, (h + 1) * M2_HEAD_DIM)
                y_ref[0, bi, :, hc] = (jnp.dot(m, xdt[:, hc], preferred_element_type=F32)
                                       + y_off[:, r * M2_HEAD_DIM:(r + 1) * M2_HEAD_DIM])
            s_scr[bi, g] = s_prev * tot_x[:, gc] + lax.dot_general(
                bg, xout[:, gc], (((0,), (0,)), ((), ())), preferred_element_type=F32)

    @pl.when(c == pl.num_programs(2) - 1)
    def _():
        hfin_ref[0] = s_scr[...]


def _ssd(xbc, dt_col, dt_row, dt_bias, a_coef, h0):
    b, l, _ = xbc.shape
    t = SSD_T
    nc = l // t
    tri_f = np.tril(np.ones((t, t), np.float32))
    tri = jnp.asarray(np.stack([tri_f, tri_f.T]), dtype=BF16)
    trit = jnp.asarray(np.stack([tri_f.T, tri_f]), dtype=BF16)
    expand = jnp.asarray(np.kron(np.eye(M2_HEADS, dtype=np.float32),
                                 np.ones((1, M2_HEAD_DIM), np.float32)), dtype=BF16)
    chunk = lambda d, c: c + d * (nc - 1 - 2 * c)
    nb = SSD_NB
    state = pl.BlockSpec((1, nb, M2_GROUPS, M2_STATE, M2_GROUP_W), lambda d, bb, c: (d, bb, 0, 0, 0))
    return pl.pallas_call(
        _ssd_kernel,
        grid=(2, b // nb, nc),
        in_specs=[pl.BlockSpec((nb, t, M2_INNER), lambda d, bb, c: (bb, chunk(d, c), 0)),
                  pl.BlockSpec((nb, t, M2_BC), lambda d, bb, c: (bb, chunk(d, c), 2)),
                  pl.BlockSpec((nb, t, M2_BC), lambda d, bb, c: (bb, chunk(d, c), 3)),
                  pl.BlockSpec((1, nb, t, M2_HEADS), lambda d, bb, c: (d, bb, chunk(d, c), 0)),
                  pl.BlockSpec((1, nb, M2_HEADS, t), lambda d, bb, c: (d, bb, 0, chunk(d, c))),
                  pl.BlockSpec((1, 1, M2_HEADS), lambda d, bb, c: (d, 0, 0)),
                  pl.BlockSpec((1, M2_HEADS, 1), lambda d, bb, c: (d, 0, 0)),
                  pl.BlockSpec((1, 1, M2_HEADS), lambda d, bb, c: (d, 0, 0)),
                  pl.BlockSpec((1, M2_HEADS, 1), lambda d, bb, c: (d, 0, 0)),
                  pl.BlockSpec((1, t, t), lambda d, bb, c: (d, 0, 0)),
                  pl.BlockSpec((1, t, t), lambda d, bb, c: (d, 0, 0)),
                  pl.BlockSpec((M2_HEADS, M2_INNER), lambda d, bb, c: (0, 0)),
                  state],
        out_specs=[pl.BlockSpec((1, nb, t, M2_INNER), lambda d, bb, c: (d, bb, chunk(d, c), 0)), state],
        out_shape=[jax.ShapeDtypeStruct((2, b, l, M2_INNER), F32),
                   jax.ShapeDtypeStruct((2, b, M2_GROUPS, M2_STATE, M2_GROUP_W), F32)],
        scratch_shapes=[pltpu.VMEM((nb, M2_GROUPS, M2_STATE, M2_GROUP_W), F32)],
        compiler_params=_cparams("parallel", "parallel", "arbitrary"),
        name="ssd",
    )(xbc, xbc, xbc, dt_col, dt_row, dt_bias.reshape(2, 1, M2_HEADS), dt_bias.reshape(2, M2_HEADS, 1),
      a_coef.reshape(2, 1, M2_HEADS), a_coef.reshape(2, M2_HEADS, 1), tri, trit, expand, h0)


def _m2_gate_kernel(yf_ref, yb_ref, xs_ref, z_ref, d_ref, g_ref, o_ref):
    y = (yf_ref[0, 0] + yb_ref[0, 0] + d_ref[...] * xs_ref[0]) * _silu(z_ref[0].astype(F32))
    ms = jnp.mean(y * y, axis=-1, keepdims=True)
    o_ref[0] = y * lax.rsqrt(ms + NORM_EPS) * g_ref[...]


def _m2_gate(ydir, xbc, z, zcol, d_x, norm_g):
    _, b, l, w = ydir.shape
    tl = min(l, 512)
    row = lambda bb, i: (bb, i, 0)
    cst = lambda bb, i: (0, 0)
    return pl.pallas_call(
        _m2_gate_kernel,
        grid=(b, l // tl),
        in_specs=[pl.BlockSpec((1, 1, tl, w), lambda bb, i: (0, bb, i, 0)),
                  pl.BlockSpec((1, 1, tl, w), lambda bb, i: (1, bb, i, 0)),
                  pl.BlockSpec((1, tl, w), row), pl.BlockSpec((1, tl, w), lambda bb, i: (bb, i, zcol)),
                  pl.BlockSpec((1, w), cst), pl.BlockSpec((1, w), cst)],
        out_specs=pl.BlockSpec((1, tl, w), row),
        out_shape=jax.ShapeDtypeStruct((b, l, w), F32),
        compiler_params=_cparams("parallel", "parallel"),
        name="m2_gate",
    )(ydir, ydir, xbc, z, d_x, norm_g)


HY_N2 = 128


def _round8(n):
    return (n + 7) // 8 * 8


def _dft2_tables(n1, n2):
    n = n1 * n2
    k1n = n1 // 2 + 1
    k1p = _round8(k1n)
    k1 = np.arange(k1p, dtype=np.float64)[:, None]
    valid = (k1 < k1n).astype(np.float64)
    i1 = np.arange(n1, dtype=np.float64)[None, :]
    th1 = 2.0 * np.pi * k1 * i1 / n1
    stage1 = np.concatenate([np.cos(th1) * valid, -np.sin(th1) * valid], axis=0)
    t = n2 * i1[None] + np.arange(n2, dtype=np.float64)[:, None, None]
    th = 2.0 * np.pi * k1[None] * t / n
    stage1_tw = np.concatenate([np.cos(th) * valid[None], -np.sin(th) * valid[None]], axis=1)
    i2 = np.arange(n2, dtype=np.float64)
    tht = 2.0 * np.pi * i2 / n
    tw1 = np.stack([np.broadcast_to(np.cos(tht)[:, None], (n2, 128)),
                    np.broadcast_to(-np.sin(tht)[:, None], (n2, 128))])
    k2 = np.arange(n2, dtype=np.float64)[:, None]
    th2 = 2.0 * np.pi * k2 * i2[None, :] / n2
    fc, fs = np.cos(th2), np.sin(th2)
    fwd2 = np.block([[fc, fs], [-fs, fc]])
    inv2 = np.block([[fc, -fs], [fs, fc]])
    wk = np.where((k1 == 0) | (k1 == n1 // 2), 1.0, 2.0) * valid
    o1 = np.arange(n1 // 2, dtype=np.float64)[:, None]
    tho = 2.0 * np.pi * o1 * k1.T / n1
    last = np.concatenate([np.cos(tho) * wk.T, -np.sin(tho) * wk.T], axis=1) / n
    f = lambda a: jnp.asarray(a, dtype=F32)
    return dict(k1n=k1n, k1p=k1p, stage1=f(stage1), stage1_tw=f(stage1_tw), tw1=f(tw1),
                fwd2=f(fwd2), inv2=f(inv2), last=f(last))


def _dft1_tables(l):
    n = 2 * l
    kn = l + 1
    kp = _round8(kn)
    k = np.arange(kp, dtype=np.float64)[:, None]
    valid = (k < kn).astype(np.float64)
    t = np.arange(n, dtype=np.float64)[None, :]
    th = 2.0 * np.pi * k * t / n
    fwd = np.concatenate([np.cos(th) * valid, -np.sin(th) * valid], axis=0)
    wk = np.where((k == 0) | (k == l), 1.0, 2.0) * valid
    o = np.arange(l, dtype=np.float64)[:, None]
    tho = 2.0 * np.pi * o * k.T / n
    inv = np.concatenate([np.cos(tho) * wk.T, -np.sin(tho) * wk.T], axis=1) / n
    f = lambda a: jnp.asarray(a, dtype=F32)
    return dict(kp=kp, fwd=f(fwd), inv=f(inv))


def _rowdft_kernel(hi, f_ref, x_ref, o_ref):
    dot = _dot3 if hi else _bdot
    o_ref[0] = dot(f_ref[...], x_ref[0])


def _rowdft(fmat, x, hi):
    bx, r, n = x.shape
    m = fmat.shape[0]
    tn = min(n, 4096)
    return pl.pallas_call(
        functools.partial(_rowdft_kernel, hi),
        grid=(bx, n // tn),
        in_specs=[pl.BlockSpec((m, r), lambda b, j: (0, 0)),
                  pl.BlockSpec((1, r, tn), lambda b, j: (b, 0, j))],
        out_specs=pl.BlockSpec((1, m, tn), lambda b, j: (b, 0, j)),
        out_shape=jax.ShapeDtypeStruct((bx, m, n), F32),
        compiler_params=_cparams("parallel", "parallel"),
        name="hy_rowdft",
    )(fmat if hi else fmat.astype(BF16), x)


HY_TS = 8


HY_PITCH = HY_N2 + 8
HY_CONV_VMEM_BYTES = 58 * 1024 * 1024


def _hy_conv_kernel(k1n, k1p, n2, v_ref, gate_ref, h_ref, st1_ref, tw1_ref, fwd_ref, inv_ref, last_ref,
                    d_ref, o_ref, a_scr, t_scr):
    n1h = v_ref.shape[0] // n2
    st1 = st1_ref[...]
    last = last_ref[...]

    def copy_in(i, _):
        t_scr[pl.ds(pl.multiple_of(i * HY_PITCH, 8), n2), :] = v_ref[pl.ds(pl.multiple_of(i * n2, n2), n2), :]
        return 0
    lax.fori_loop(0, n1h, copy_in, 0)

    def stage1(j, _):
        s = 2 * j
        xs = jnp.concatenate([t_scr[pl.ds(s, n1h, stride=HY_PITCH), :],
                              t_scr[pl.ds(s + 1, n1h, stride=HY_PITCH), :]], axis=1).astype(BF16)
        y = jnp.dot(st1, xs, preferred_element_type=F32)
        a_scr[pl.ds(s, 2 * k1p, stride=HY_PITCH), :] = y[:, :128]
        a_scr[pl.ds(s + 1, 2 * k1p, stride=HY_PITCH), :] = y[:, 128:]
        return 0
    lax.fori_loop(0, n2 // 2, stage1, 0, unroll=4)

    def cmul(ar, ai, br, bi):
        return ar * br - ai * bi, ar * bi + ai * br

    def stage2(j, tw):
        t1r, t1i = tw1_ref[0], tw1_ref[1]
        tw_a = tw
        tw_b = cmul(*tw_a, t1r, t1i)
        ks = (2 * j, 2 * j + 1)
        rows_r = [pl.ds(pl.multiple_of(k * HY_PITCH, 8), n2) for k in ks]
        rows_i = [pl.ds(pl.multiple_of((k1p + k) * HY_PITCH, 8), n2) for k in ks]
        pr, pi = [], []
        for k, twk, rr, ri in zip(ks, (tw_a, tw_b), rows_r, rows_i):
            a, b = cmul(a_scr[rr, :], a_scr[ri, :], *twk)
            pr.append(a)
            pi.append(b)
        p = jnp.concatenate([jnp.concatenate(pr, axis=1), jnp.concatenate(pi, axis=1)], axis=0)
        x = jnp.dot(fwd_ref[...], p.astype(BF16), preferred_element_type=F32)
        xr, xi = x[:n2], x[n2:]
        hr = jnp.concatenate([h_ref[0, ks[0]], h_ref[0, ks[1]]], axis=1).astype(F32)
        hi = jnp.concatenate([h_ref[1, ks[0]], h_ref[1, ks[1]]], axis=1).astype(F32)
        qr, qi = cmul(xr, xi, hr, hi)
        z = jnp.dot(inv_ref[...], jnp.concatenate([qr, qi], axis=0).astype(BF16), preferred_element_type=F32)
        for n, (twk, rr, ri) in enumerate(zip((tw_a, tw_b), rows_r, rows_i)):
            zr, zi = z[:n2, n * 128:(n + 1) * 128], z[n2:, n * 128:(n + 1) * 128]
            a_scr[rr, :] = zr * twk[0] + zi * twk[1]
            a_scr[ri, :] = zi * twk[0] - zr * twk[1]
        return cmul(*tw_b, t1r, t1i)

    pairs = 2 * ((k1n + 3) // 4)
    assert 2 * pairs <= k1p
    lax.fori_loop(0, pairs, stage2, (jnp.ones((n2, 128), F32), jnp.zeros((n2, 128), F32)), unroll=2)

    def stage3(j, _):
        s = 2 * j
        zs = jnp.concatenate([a_scr[pl.ds(s, 2 * k1p, stride=HY_PITCH), :],
                              a_scr[pl.ds(s + 1, 2 * k1p, stride=HY_PITCH), :]], axis=1).astype(BF16)
        y = jnp.dot(last, zs, preferred_element_type=F32)
        t_scr[pl.ds(s, n1h, stride=HY_PITCH), :] = y[:, :128]
        t_scr[pl.ds(s + 1, n1h, stride=HY_PITCH), :] = y[:, 128:]
        return 0
    lax.fori_loop(0, n2 // 2, stage3, 0, unroll=4)

    def finish(i, _):
        rows = pl.ds(pl.multiple_of(i * n2, n2), n2)
        y = t_scr[pl.ds(pl.multiple_of(i * HY_PITCH, 8), n2), :]
        o_ref[rows, :] = gate_ref[rows, :] * (y + d_ref[...] * v_ref[rows, :])
        return 0
    lax.fori_loop(0, n1h, finish, 0)


def _hy_long_conv2(v, gate, hspec, order, d_vec, tab):
    b, l, c = v.shape
    n2 = HY_N2
    n1h = l // n2
    k1p = tab['k1p']
    tiles = c // 128
    seq = pl.BlockSpec((None, l, 128), lambda j, bb: (bb, 0, j))
    cst2 = lambda a: pl.BlockSpec(a.shape, lambda j, bb: (0, 0))
    st1 = tab['stage1'][:, :n1h].astype(BF16)
    fwd2, inv2, last = tab['fwd2'].astype(BF16), tab['inv2'].astype(BF16), tab['last'].astype(BF16)
    return pl.pallas_call(
        functools.partial(_hy_conv_kernel, tab['k1n'], k1p, n2),
        grid=(tiles, b),
        in_specs=[seq, seq,
                  pl.BlockSpec((2, k1p, n2, 128), lambda j, bb: (0, 0, 0, order * tiles + j)),
                  cst2(st1),
                  pl.BlockSpec((2, n2, 128), lambda j, bb: (0, 0, 0)),
                  cst2(fwd2), cst2(inv2), cst2(last),
                  pl.BlockSpec((1, 128), lambda j, bb: (0, j))],
        out_specs=seq,
        out_shape=jax.ShapeDtypeStruct(v.shape, F32),
        scratch_shapes=[pltpu.VMEM((2 * k1p * HY_PITCH, 128), F32), pltpu.VMEM((n1h * HY_PITCH, 128), F32)],
        compiler_params=pltpu.CompilerParams(dimension_semantics=("parallel", "parallel"),
                                             vmem_limit_bytes=HY_CONV_VMEM_BYTES),
        name="hy_conv",
    )(v, gate, hspec, st1, tab['tw1'], fwd2, inv2, last, d_vec.reshape(1, c))


def _hy_filt_first_kernel(k1p, feat_ref, t_ref, keep_ref, w1_ref, b1_ref, q1_ref, w2_ref, b2_ref, q2_ref,
                          w3_ref, nd_ref, f_ref, o_ref, sum_ref):
    @pl.when(pl.program_id(0) == 0)
    def _():
        sum_ref[...] = jnp.zeros_like(sum_ref)

    n1 = feat_ref.shape[1]
    half = n1 // 2
    feats = feat_ref[...].reshape(HY_TS * n1, feat_ref.shape[2])
    hid = jnp.sin(q1_ref[...] * (_bdot(feats, w1_ref[...]) + b1_ref[...]))
    hid = jnp.sin(q2_ref[...] * (_bdot(hid, w2_ref[...]) + b2_ref[...]))
    acc = jnp.zeros(sum_ref.shape, F32)
    for s in range(HY_TS):
        hs = hid[s * n1:(s + 1) * n1]
        f = jnp.concatenate([_bdot(hs[:half], w3_ref[0]), _bdot(hs[half:], w3_ref[1])], axis=0)
        f = f * jnp.exp(t_ref[s] * nd_ref[...]) * keep_ref[s]
        acc = acc + jnp.sum(jnp.abs(f), axis=0, keepdims=True)
        y = _dot3(f_ref[s], f)
        o_ref[0, :, s, :] = y[:k1p]
        o_ref[1, :, s, :] = y[k1p:]
    sum_ref[...] += acc


def _hy_filt_first(l, tab, f_w1, f_b1, f_freq1, f_w2, f_b2, f_freq2, f_w3):
    c2 = HY_ORDER * HY_WIDTH
    n2 = HY_N2
    n1 = 2 * l // n2
    k1p = tab['k1p']
    t = np.linspace(0.0, 1.0, l, dtype=np.float32)[:, None]
    bands = np.linspace(1e-4, HY_BANDS - 1, HY_BANDS, dtype=np.float32)
    ang = np.float32(2.0 * math.pi / l) * np.arange(l, dtype=np.float32)[:, None] * bands
    feats = np.concatenate([t, np.cos(ang), -np.sin(ang)], axis=-1)
    rows = (n2 * np.arange(n1)[None, :] + np.arange(n2)[:, None]).reshape(-1)
    pos = np.where(rows < l, rows, np.where(rows == l, 0, 2 * l - rows))
    feats_t = jnp.asarray(np.pad(feats[pos], ((0, 0), (0, 128 - HY_EMB))).reshape(n2, n1, 128))
    t_t = jnp.asarray(t[pos].reshape(n2, n1, 1))
    keep_t = jnp.asarray((rows != l).astype(np.float32).reshape(n2, n1, 1))
    w1p = jnp.pad(f_w1, ((0, 128 - HY_EMB), (0, 0)))
    w3d = jnp.transpose(f_w3.reshape(HY_HIDDEN, HY_ORDER, 2, HY_WIDTH), (2, 0, 1, 3)).reshape(2, HY_HIDDEN, c2)
    deltas = jnp.abs(jnp.linspace(HY_MIN_DECAY, HY_MAX_DECAY, HY_WIDTH, dtype=F32))
    negd = jnp.tile(-deltas, HY_ORDER)[None, :]
    cst = lambda i: (0, 0)
    vec = lambda a: a.reshape(1, HY_HIDDEN)
    tile3 = lambda w: pl.BlockSpec((HY_TS, n1, w), lambda i: (i, 0, 0))
    return pl.pallas_call(
        functools.partial(_hy_filt_first_kernel, k1p),
        grid=(n2 // HY_TS,),
        in_specs=[tile3(128), tile3(1), tile3(1),
                  pl.BlockSpec((128, HY_HIDDEN), cst), pl.BlockSpec((1, HY_HIDDEN), cst),
                  pl.BlockSpec((1, HY_HIDDEN), cst),
                  pl.BlockSpec((HY_HIDDEN, HY_HIDDEN), cst), pl.BlockSpec((1, HY_HIDDEN), cst),
                  pl.BlockSpec((1, HY_HIDDEN), cst),
                  pl.BlockSpec((2, HY_HIDDEN, c2), lambda i: (0, 0, 0)),
                  pl.BlockSpec((1, c2), cst),
                  pl.BlockSpec((HY_TS, 2 * k1p, n1), lambda i: (i, 0, 0))],
        out_specs=[pl.BlockSpec((None, 2, k1p, HY_TS, c2), lambda i: (0, 0, 0, i, 0)),
                   pl.BlockSpec((1, c2), cst)],
        out_shape=[jax.ShapeDtypeStruct((1, 2, k1p, n2, c2), F32), jax.ShapeDtypeStruct((1, c2), F32)],
        compiler_params=_cparams("arbitrary"),
        name="hy_filt_first",
    )(feats_t, t_t, keep_t, w1p, vec(f_b1), vec(f_freq1), f_w2, vec(f_b2), vec(f_freq2), w3d, negd,
      tab['stage1_tw'])


def _hy_filt_mid_kernel(n2, a_ref, sum_ref, fwd_ref, o_ref):
    c2 = a_ref.shape[-1]
    x = _dot3(fwd_ref[...], a_ref[0, :, 0].reshape(2 * n2, c2)) * (1.0 / (sum_ref[...] + 1e-6))
    o_ref[0, 0] = x[:n2].astype(BF16)
    o_ref[1, 0] = x[n2:].astype(BF16)


def _hy_filt_mid(a5, tab, colsum):
    _, _, k1p, n2, c2 = a5.shape
    return pl.pallas_call(
        functools.partial(_hy_filt_mid_kernel, n2),
        grid=(k1p,),
        in_specs=[pl.BlockSpec((1, 2, 1, n2, c2), lambda k: (0, 0, k, 0, 0)),
                  pl.BlockSpec((1, c2), lambda k: (0, 0)),
                  pl.BlockSpec((2 * n2, 2 * n2), lambda k: (0, 0))],
        out_specs=pl.BlockSpec((2, 1, n2, c2), lambda k: (0, k, 0, 0)),
        out_shape=jax.ShapeDtypeStruct((2, k1p, n2, c2), BF16),
        compiler_params=_cparams("parallel"),
        name="hy_filt_mid",
    )(a5, colsum, tab['fwd2'])


def _hy_ctx_kernel(kp, fw_ref, inv_ref, h_ref, v_ref, gate_ref, d_ref, o_ref):
    v = v_ref[0]
    u = _bdot(fw_ref[...], v)
    ur, ui = u[:kp], u[kp:]
    hr, hi = h_ref[0], h_ref[1]
    q = jnp.concatenate([ur * hr - ui * hi, ur * hi + ui * hr], axis=0)
    o_ref[0] = gate_ref[0] * (_bdot(inv_ref[...], q) + d_ref[...] * v)


def _hy_long_conv1(v, gate, hspec, order, d_vec, tab):
    b, l, c = v.shape
    kp = tab['kp']
    row = lambda bb: (bb, 0, 0)
    return pl.pallas_call(
        functools.partial(_hy_ctx_kernel, kp),
        grid=(b,),
        in_specs=[pl.BlockSpec((2 * kp, l), lambda bb: (0, 0)),
                  pl.BlockSpec((l, 2 * kp), lambda bb: (0, 0)),
                  pl.BlockSpec((2, kp, c), lambda bb: (0, 0, order)),
                  pl.BlockSpec((1, l, c), row), pl.BlockSpec((1, l, c), row),
                  pl.BlockSpec((1, c), lambda bb: (0, 0))],
        out_specs=pl.BlockSpec((1, l, c), row),
        out_shape=jax.ShapeDtypeStruct(v.shape, F32),
        compiler_params=_cparams("parallel"),
        name="hy_ctx_conv",
    )(tab['fwd'][:, :l].astype(BF16), tab['inv'].astype(BF16), hspec, v, gate, d_vec.reshape(1, c))


def _hy_filter_kernel(feat_ref, t_ref, keep_ref, w1_ref, b1_ref, q1_ref, w2_ref, b2_ref, q2_ref,
                      w3_ref, nd_ref, f_ref, sum_ref):
    @pl.when(pl.program_id(0) == 0)
    def _():
        sum_ref[...] = jnp.zeros_like(sum_ref)

    hid = jnp.sin(q1_ref[...] * (_bdot(feat_ref[...], w1_ref[...]) + b1_ref[...]))
    hid = jnp.sin(q2_ref[...] * (_bdot(hid, w2_ref[...]) + b2_ref[...]))
    f = _bdot(hid, w3_ref[0]) * jnp.exp(t_ref[...] * nd_ref[...]) * keep_ref[...]
    f_ref[...] = f
    sum_ref[...] += jnp.sum(jnp.abs(f), axis=0, keepdims=True)


def _hy_filters(l, f_w1, f_b1, f_freq1, f_w2, f_b2, f_freq2, f_w3):
    c2 = HY_ORDER * HY_WIDTH
    t = np.linspace(0.0, 1.0, l, dtype=np.float32)[:, None]
    bands = np.linspace(1e-4, HY_BANDS - 1, HY_BANDS, dtype=np.float32)
    ang = np.float32(2.0 * math.pi / l) * np.arange(l, dtype=np.float32)[:, None] * bands
    feats = np.concatenate([t, np.cos(ang), -np.sin(ang)], axis=-1)
    rows = np.arange(2 * l)
    pos = np.where(rows < l, rows, np.where(rows == l, 0, 2 * l - rows))
    feats2 = jnp.asarray(np.pad(feats[pos], ((0, 0), (0, 128 - HY_EMB))))
    keep = jnp.asarray((rows != l).astype(np.float32))[:, None]
    t2 = jnp.asarray(t[pos])
    w1p = jnp.pad(f_w1, ((0, 128 - HY_EMB), (0, 0)))
    w3d = jnp.transpose(f_w3.reshape(HY_HIDDEN, HY_ORDER, 2, HY_WIDTH), (2, 0, 1, 3)).reshape(2, HY_HIDDEN, c2)
    deltas = jnp.abs(jnp.linspace(HY_MIN_DECAY, HY_MAX_DECAY, HY_WIDTH, dtype=F32))
    negd = jnp.tile(-deltas, HY_ORDER)[None, :]
    tr = min(l, 512)
    per_dir = l // tr
    cst = lambda i: (0, 0)
    vec = lambda a: a.reshape(1, HY_HIDDEN)
    return pl.pallas_call(
        _hy_filter_kernel,
        grid=(2 * l // tr,),
        in_specs=[pl.BlockSpec((tr, 128), lambda i: (i, 0)),
                  pl.BlockSpec((tr, 1), lambda i: (i, 0)),
                  pl.BlockSpec((tr, 1), lambda i: (i, 0)),
                  pl.BlockSpec((128, HY_HIDDEN), cst), pl.BlockSpec((1, HY_HIDDEN), cst),
                  pl.BlockSpec((1, HY_HIDDEN), cst),
                  pl.BlockSpec((HY_HIDDEN, HY_HIDDEN), cst), pl.BlockSpec((1, HY_HIDDEN), cst),
                  pl.BlockSpec((1, HY_HIDDEN), cst),
                  pl.BlockSpec((1, HY_HIDDEN, c2), lambda i: (i // per_dir, 0, 0)),
                  pl.BlockSpec((1, c2), cst)],
        out_specs=[pl.BlockSpec((tr, c2), lambda i: (i, 0)), pl.BlockSpec((1, c2), cst)],
        out_shape=[jax.ShapeDtypeStruct((2 * l, c2), F32), jax.ShapeDtypeStruct((1, c2), F32)],
        compiler_params=_cparams("arbitrary"),
        name="hy_filter",
    )(feats2, t2, keep, w1p, vec(f_b1), vec(f_freq1), f_w2, vec(f_b2), vec(f_freq2), w3d, negd)


def _col_scale_kernel(x_ref, sum_ref, o_ref):
    o_ref[...] = x_ref[...] * (1.0 / (sum_ref[...] + 1e-6))


def _hy_spectrum(l, fparams):
    if 2 * l >= 16 * HY_N2:
        tab = _dft2_tables(2 * l // HY_N2, HY_N2)
        a, colsum = _hy_filt_first(l, tab, *fparams)
        return _hy_filt_mid(a, tab, colsum), tab
    filt, colsum = _hy_filters(l, *fparams)
    c2 = filt.shape[1]
    tab = _dft1_tables(l)
    spec = _rowdft(tab['fwd'], filt[None], True)[0]
    spec = pl.pallas_call(
        _col_scale_kernel,
        in_specs=[pl.BlockSpec(spec.shape, lambda: (0, 0)), pl.BlockSpec((1, c2), lambda: (0, 0))],
        out_specs=pl.BlockSpec(spec.shape, lambda: (0, 0)),
        out_shape=jax.ShapeDtypeStruct(spec.shape, F32),
        name="hy_col_scale",
    )(spec, colsum)
    return spec.reshape(2, tab['kp'], c2), tab


def _hyena_mixer(h_proj, conv_w, conv_b, hspec, tab, d_skip):
    c = HY_WIDTH
    parts = [_dwconv(h_proj, conv_w, conv_b, False, col0=i * c, ncols=c) for i in range(3)]
    v, x1, x2 = parts
    conv = _hy_long_conv2 if 'k1p' in tab else _hy_long_conv1
    y = conv(v, x1, hspec, 0, d_skip[0], tab)
    return conv(y, x2, hspec, 1, d_skip[1], tab)


def _grid_sincos(n, dm):
    rows = n // GRID_W
    quarter = dm // 4
    omega = 1.0 / (10000.0 ** (jnp.arange(quarter, dtype=F32) / quarter))
    ang_r = jnp.arange(rows, dtype=F32)[:, None] * omega
    ang_c = jnp.arange(GRID_W, dtype=F32)[:, None] * omega
    emb_r = jnp.concatenate([jnp.sin(ang_r), jnp.cos(ang_r)], axis=-1)
    emb_c = jnp.concatenate([jnp.sin(ang_c), jnp.cos(ang_c)], axis=-1)
    half = emb_r.shape[-1]
    pe = jnp.concatenate([jnp.broadcast_to(emb_r[:, None, :], (rows, GRID_W, half)),
                          jnp.broadcast_to(emb_c[None, :, :], (rows, GRID_W, half))], axis=-1)
    return pe.reshape(rows * GRID_W, 2 * half)


EV_Z_OFF = M2_XBC
EV_U_OFF = EV_Z_OFF + M2_INNER
EV_DT_OFF = EV_U_OFF + S5_WIDTH
EV_PROJ_W = EV_DT_OFF + 128


def _even_mixer(h_in, p, states):
    x, g, shift, scale = h_in
    b, l, _ = x.shape
    proj = _adaln_mm(x, g, shift, scale, p['w_in'], jnp.zeros((1, EV_PROJ_W), F32))
    s5_h0, m2_h0 = states

    ucol = EV_U_OFF // S5_WIDTH
    yf, yb, s5_fin = _s5_scan(proj, ucol, p['s5_bd'], p['s5_cd'], p['s5_ar'], p['s5_ai'], s5_h0)
    y_s5 = _s5_glu(yf, yb, proj, ucol, p['s5_d'], p['s5_glu_w'], p['s5_glu_b'])

    xbc = _dwconv(proj, p['m2_conv_w'], p['m2_conv_b'], True, col0=0, ncols=M2_XBC)
    dt4 = proj[..., EV_DT_OFF:EV_DT_OFF + 2 * M2_HEADS].astype(F32).reshape(b, l, 2, M2_HEADS)
    dt_col = jnp.transpose(dt4, (2, 0, 1, 3))
    dt_row = jnp.transpose(dt4, (2, 0, 3, 1))
    ydir, m2_fin = _ssd(xbc, dt_col, dt_row, p['m2_dt_bias'], p['m2_a'], m2_h0)
    y_m2 = _m2_gate(ydir, xbc, proj, EV_Z_OFF // M2_INNER, p['m2_d'], p['m2_norm_g'])
    return (y_s5, y_m2), (s5_fin, m2_fin)


def kernel(x, c, ctx, c_ctx, mod_w, mod_b, norm_mix_g, norm_mlp_g, mlp_w1, mlp_w2, final_norm_g, ev_in_w, ev_out_w, s5_lam_re, s5_lam_im, s5_log_dt, s5_b_re, s5_b_im, s5_c_re, s5_c_im, s5_d, s5_glu_w, s5_glu_b, m2_conv_w, m2_conv_b, m2_dt_bias, m2_a_log, m2_d, m2_norm_g, hy_in_w, hy_in_b, hy_conv_w, hy_conv_b, hy_f_w1, hy_f_b1, hy_f_freq1, hy_f_w2, hy_f_b2, hy_f_freq2, hy_f_w3, hy_d, hy_out_w, hy_out_b):
    bsz, n, dm = x.shape
    lc = ctx.shape[1]
    x = _add_pe(x, _grid_sincos(n, dm))

    c8 = jnp.concatenate([c, c_ctx[None], jnp.zeros((3, dm), F32)], axis=0)
    mods = _modulation(c8, mod_w, mod_b).reshape(DEPTH, 8, N_MOD, dm)
    final_g = final_norm_g.reshape(1, dm)

    for i in range(DEPTH):
        j = i // 2
        mx = [mods[i, :bsz, k][:, None, :] for k in range(N_MOD)]
        mc = [jnp.broadcast_to(mods[i, bsz, k][None, None, :], (bsz, 1, dm)) for k in range(N_MOD)]
        g_mix = norm_mix_g[i].reshape(1, dm)
        g_mlp = norm_mlp_g[i].reshape(1, dm)
        ctx_later = any(k % 2 == 0 for k in range(i + 1, DEPTH))
        w1 = mlp_w1[i].astype(BF16)
        w2 = mlp_w2[i].astype(BF16)

        if i % 2 == 0:
            in_w = ev_in_w[j]
            o0, o1, o2 = S5_WIDTH, S5_WIDTH + M2_INNER, S5_WIDTH + M2_INNER + M2_XBC
            bd, cd, ar8, ai8 = _s5_prepare(s5_lam_re[j], s5_lam_im[j], s5_log_dt[j], s5_b_re[j], s5_b_im[j],
                                           s5_c_re[j], s5_c_im[j])
            p = dict(
                w_in=jnp.concatenate([in_w[:, o1:o2], in_w[:, o0:o1], in_w[:, :o0],
                                      jnp.pad(in_w[:, o2:], ((0, 0), (0, 128 - 2 * M2_HEADS)))],
                                     axis=1).astype(BF16),
                s5_bd=bd, s5_cd=cd, s5_ar=ar8, s5_ai=ai8,
                s5_d=s5_d[j].reshape(1, S5_WIDTH), s5_glu_w=s5_glu_w[j].astype(BF16),
                s5_glu_b=s5_glu_b[j].reshape(1, S5_WIDTH),
                m2_conv_w=m2_conv_w[j], m2_conv_b=m2_conv_b[j].reshape(1, M2_XBC),
                m2_dt_bias=m2_dt_bias[j], m2_a=-jnp.exp(m2_a_log[j]),
                m2_d=jnp.repeat(m2_d[j], M2_HEAD_DIM).reshape(1, M2_INNER),
                m2_norm_g=m2_norm_g[j].reshape(1, M2_INNER))
            zero_states = (jnp.zeros((8, S5_LANES), F32),
                           jnp.zeros((2, bsz, M2_GROUPS, M2_STATE, M2_GROUP_W), F32))
            ys_c, ctx_states = _even_mixer((ctx, g_mix, mc[0], mc[1]), p, zero_states)
            ys_x, _ = _even_mixer((x, g_mix, mx[0], mx[1]), p, ctx_states)
            out_ws = [ev_out_w[j][:S5_WIDTH].astype(BF16), ev_out_w[j][S5_WIDTH:].astype(BF16)]
            out_b = jnp.zeros((1, dm), F32)
        else:
            fparams = (hy_f_w1[j], hy_f_b1[j], hy_f_freq1[j], hy_f_w2[j], hy_f_b2[j], hy_f_freq2[j], hy_f_w3[j])
            in_w = hy_in_w[j].astype(BF16)
            in_b = hy_in_b[j].reshape(1, -1)
            conv_b = hy_conv_b[j].reshape(1, -1)
            hspec, tab = _hy_spectrum(n, fparams)
            ys_x = [_hyena_mixer(_adaln_mm(x, g_mix, mx[0], mx[1], in_w, in_b), hy_conv_w[j], conv_b,
                                 hspec, tab, hy_d[j])]
            if ctx_later:
                hspec_c, tab_c = _hy_spectrum(lc, fparams)
                ys_c = [_hyena_mixer(_adaln_mm(ctx, g_mix, mc[0], mc[1], in_w, in_b), hy_conv_w[j], conv_b,
                                     hspec_c, tab_c, hy_d[j])]
            out_ws = [hy_out_w[j].astype(BF16)]
            out_b = hy_out_b[j].reshape(1, dm)

        x = _mix_mlp(x, mx[2], out_b, ys_x, out_ws, g_mlp, mx[3], mx[4], mx[5], w1, w2, final_g, i == DEPTH - 1)
        if ctx_later:
            ctx = _mix_mlp(ctx, mc[2], out_b, ys_c, out_ws, g_mlp, mc[3], mc[4], mc[5], w1, w2, final_g, False)
    return x
```

```python
import functools
import math

import numpy as np
import jax
import jax.numpy as jnp
from jax import lax
from jax.experimental import pallas as pl
from jax.experimental.pallas import tpu as pltpu

F32 = jnp.float32
BF16 = jnp.bfloat16

D_MODEL = 1024
DEPTH = 4
GRID_W = 64
N_MOD = 6
MLP_HIDDEN = 4 * D_MODEL
NORM_EPS = 1e-6

S5_WIDTH = 512
S5_GROUP_CH = 16
S5_GROUPS = 32
S5_STATE = 64
S5_SUPER = 4
S5_SLAB = 2 * 8 * S5_STATE
S5_LANES = S5_SUPER * S5_SLAB

M2_INNER = 1024
M2_HEAD_DIM = 64
M2_HEADS = 16
M2_GROUPS = 4
M2_STATE = 128
M2_BC = M2_GROUPS * M2_STATE
M2_XBC = M2_INNER + 2 * M2_BC
M2_GROUP_W = (M2_HEADS // M2_GROUPS) * M2_HEAD_DIM

HY_WIDTH = 1024
HY_ORDER = 2
HY_BANDS = 16
HY_EMB = 2 * HY_BANDS + 1
HY_HIDDEN = 64
HY_MIN_DECAY = math.log(1e-2) / 1.5
HY_MAX_DECAY = math.log(1e-2) / 0.3

V7X_VMEM_LIMIT_BYTES = 48 * 1024 * 1024


def _cparams(*sem):
    return pltpu.CompilerParams(dimension_semantics=sem, vmem_limit_bytes=V7X_VMEM_LIMIT_BYTES)


def _bdot(a, b):
    return jnp.dot(a.astype(BF16), b.astype(BF16), preferred_element_type=F32)


def _split3(a):
    a1 = a.astype(BF16)
    r1 = a - a1.astype(F32)
    a2 = r1.astype(BF16)
    a3 = (r1 - a2.astype(F32)).astype(BF16)
    return a1, a2, a3


def _dot_exact_rhs(a, b_bf16):
    a1, a2, a3 = _split3(a)
    d = functools.partial(jnp.dot, preferred_element_type=F32)
    return d(a1, b_bf16) + d(a2, b_bf16) + d(a3, b_bf16)


def _dot_split2_rhs(a, b_bf16):
    a1 = a.astype(BF16)
    a2 = (a - a1.astype(F32)).astype(BF16)
    d = functools.partial(jnp.dot, preferred_element_type=F32)
    return d(a1, b_bf16) + d(a2, b_bf16)


def _dot_exact_lhs(a_bf16, b):
    b1, b2, b3 = _split3(b)
    d = functools.partial(jnp.dot, preferred_element_type=F32)
    return d(a_bf16, b1) + d(a_bf16, b2) + d(a_bf16, b3)


def _dot3(a, b):
    a1 = a.astype(BF16)
    a2 = (a - a1.astype(F32)).astype(BF16)
    b1 = b.astype(BF16)
    b2 = (b - b1.astype(F32)).astype(BF16)
    d = functools.partial(jnp.dot, preferred_element_type=F32)
    return d(a1, b1) + d(a1, b2) + d(a2, b1)


def _silu(x):
    return x * (1.0 / (1.0 + jnp.exp(-x)))


def _sigmoid(x):
    return 1.0 / (1.0 + jnp.exp(-x))


def _adaln(x, g, shift, scale):
    ms = jnp.mean(x * x, axis=-1, keepdims=True)
    return (x * lax.rsqrt(ms + NORM_EPS) * g) * (1.0 + scale) + shift


def _mod_kernel(c_ref, w_ref, b_ref, o_ref):
    o_ref[0] = _bdot(_silu(c_ref[...]), w_ref[0]) + b_ref[0]


def _modulation(c8, mod_w, mod_b):
    n = mod_w.shape[-1]
    tn = 1536
    return pl.pallas_call(
        _mod_kernel,
        grid=(DEPTH, n // tn),
        in_specs=[pl.BlockSpec((8, D_MODEL), lambda i, j: (0, 0)),
                  pl.BlockSpec((1, D_MODEL, tn), lambda i, j: (i, 0, j)),
                  pl.BlockSpec((1, 1, tn), lambda i, j: (i, 0, j))],
        out_specs=pl.BlockSpec((1, 8, tn), lambda i, j: (i, 0, j)),
        out_shape=jax.ShapeDtypeStruct((DEPTH, 8, n), F32),
        compiler_params=_cparams("parallel", "parallel"),
        name="modulation",
    )(c8, mod_w, mod_b.reshape(DEPTH, 1, n))


def _add_pe_kernel(x_ref, pe_ref, o_ref):
    o_ref[0] = x_ref[0] + pe_ref[...]


def _add_pe(x, pe):
    b, l, d = x.shape
    tl = min(l, 1024)
    return pl.pallas_call(
        _add_pe_kernel,
        grid=(l // tl, b),
        in_specs=[pl.BlockSpec((1, tl, d), lambda i, bb: (bb, i, 0)),
                  pl.BlockSpec((tl, d), lambda i, bb: (i, 0))],
        out_specs=pl.BlockSpec((1, tl, d), lambda i, bb: (bb, i, 0)),
        out_shape=jax.ShapeDtypeStruct(x.shape, F32),
        compiler_params=_cparams("parallel", "parallel"),
        name="add_pe",
    )(x, pe)


def _adaln_mm_kernel(x_ref, g_ref, sh_ref, sc_ref, w_ref, b_ref, o_ref):
    h = _adaln(x_ref[0], g_ref[...], sh_ref[0], sc_ref[0])
    o_ref[0] = (_bdot(h, w_ref[...]) + b_ref[...]).astype(o_ref.dtype)


def _adaln_mm(x, g, shift, scale, w_bf16, bias):
    b, l, d = x.shape
    n = w_bf16.shape[1]
    tl = min(l, 512)
    return pl.pallas_call(
        _adaln_mm_kernel,
        grid=(b, l // tl),
        in_specs=[pl.BlockSpec((1, tl, d), lambda bb, i: (bb, i, 0)),
                  pl.BlockSpec((1, d), lambda bb, i: (0, 0)),
                  pl.BlockSpec((1, 1, d), lambda bb, i: (bb, 0, 0)),
                  pl.BlockSpec((1, 1, d), lambda bb, i: (bb, 0, 0)),
                  pl.BlockSpec((d, n), lambda bb, i: (0, 0)),
                  pl.BlockSpec((1, n), lambda bb, i: (0, 0))],
        out_specs=pl.BlockSpec((1, tl, n), lambda bb, i: (bb, i, 0)),
        out_shape=jax.ShapeDtypeStruct((b, l, n), BF16),
        compiler_params=_cparams("parallel", "parallel"),
        name="adaln_mm",
    )(x, g, shift, scale, w_bf16, bias)


MLP_CHUNK = 512


def _mix_mlp_kernel(n_in, final_norm, *refs):
    x_ref, ga_ref, bo_ref = refs[0:3]
    y_refs = refs[3:3 + n_in]
    wo_refs = refs[3 + n_in:3 + 2 * n_in]
    g_ref, sh_ref, sc_ref, gf_ref, w1_ref, w2_ref, fg_ref, o_ref, a_scr = refs[3 + 2 * n_in:]
    mix = bo_ref[...] + _bdot(y_refs[0][0], wo_refs[0][...])
    for y_ref, wo_ref in zip(y_refs[1:], wo_refs[1:]):
        mix = mix + _bdot(y_ref[0], wo_ref[...])
    x1 = x_ref[0] + ga_ref[0] * mix
    h = _adaln(x1, g_ref[...], sh_ref[0], sc_ref[0]).astype(BF16)
    for c in range(w1_ref.shape[1] // MLP_CHUNK):
        cols = slice(c * MLP_CHUNK, (c + 1) * MLP_CHUNK)
        a = jnp.dot(h, w1_ref[:, cols], preferred_element_type=F32)
        a_scr[:, cols] = jnp.square(jnp.maximum(a, 0.0)).astype(BF16)
    y = x1 + gf_ref[0] * jnp.dot(a_scr[...], w2_ref[...], preferred_element_type=F32)
    if final_norm:
        ms = jnp.mean(y * y, axis=-1, keepdims=True)
        y = y * lax.rsqrt(ms + NORM_EPS) * fg_ref[...]
    o_ref[0] = y


def _mix_mlp(x, gate_a, out_b, ys, out_ws, g, shift, scale, gate_f, w1_bf16, w2_bf16, final_g, final_norm):
    b, l, d = x.shape
    hdim = w1_bf16.shape[1]
    tl = min(l, 512)
    n_in = len(ys)
    row = lambda bb, i: (bb, i, 0)
    vec = pl.BlockSpec((1, 1, d), lambda bb, i: (bb, 0, 0))
    cvec = pl.BlockSpec((1, d), lambda bb, i: (0, 0))
    resident = lambda a: pl.BlockSpec(a.shape, lambda bb, i: (0, 0), pipeline_mode=pl.Buffered(1))
    in_specs = [pl.BlockSpec((1, tl, d), row), vec, cvec]
    in_specs += [pl.BlockSpec((1, tl, y.shape[-1]), row) for y in ys]
    in_specs += [resident(w) for w in out_ws]
    in_specs += [cvec, vec, vec, vec, resident(w1_bf16), resident(w2_bf16), cvec]
    return pl.pallas_call(
        functools.partial(_mix_mlp_kernel, n_in, final_norm),
        grid=(b, l // tl),
        in_specs=in_specs,
        out_specs=pl.BlockSpec((1, tl, d), row),
        out_shape=jax.ShapeDtypeStruct(x.shape, F32),
        scratch_shapes=[pltpu.VMEM((tl, hdim), BF16)],
        compiler_params=_cparams("parallel", "parallel"),
        name="mix_mlp",
    )(x, gate_a, out_b, *ys, *out_ws, g, shift, scale, gate_f, w1_bf16, w2_bf16, final_g)


def _dwconv_kernel(act, x_ref, w_ref, b_ref, o_ref):
    x = x_ref[0].astype(F32)
    l = x.shape[0]
    t = lax.broadcasted_iota(jnp.int32, x.shape, 0)
    prev = jnp.where(t == 0, 0.0, pltpu.roll(x, 1, axis=0))
    nxt = jnp.where(t == l - 1, 0.0, pltpu.roll(x, l - 1, axis=0))
    w = w_ref[...]
    y = prev * w[0:1] + x * w[1:2] + nxt * w[2:3] + b_ref[...]
    if act:
        y = _silu(y)
    o_ref[0] = y.astype(o_ref.dtype)


def _dwconv(x, w, bias, act, col0=0, ncols=None):
    b, l, c = x.shape
    ncols = c if ncols is None else ncols
    tc = 256 if l > 1024 else 512
    j0 = col0 // tc
    return pl.pallas_call(
        functools.partial(_dwconv_kernel, act),
        grid=(b, ncols // tc),
        in_specs=[pl.BlockSpec((1, l, tc), lambda bb, j: (bb, 0, j0 + j)),
                  pl.BlockSpec((3, tc), lambda bb, j: (0, j0 + j)),
                  pl.BlockSpec((1, tc), lambda bb, j: (0, j0 + j))],
        out_specs=pl.BlockSpec((1, l, tc), lambda bb, j: (bb, 0, j)),
        out_shape=jax.ShapeDtypeStruct((b, l, ncols), BF16),
        compiler_params=_cparams("parallel", "parallel"),
        name="dwconv",
    )(x, w, bias)


def _s5_prepare(lam_re, lam_im, log_dt, b_re, b_im, c_re, c_im):
    step = jnp.exp(log_dt)[..., None]
    mag = jnp.exp(lam_re * step)
    ar = mag * jnp.cos(lam_im * step)
    ai = mag * jnp.sin(lam_im * step)
    den = lam_re * lam_re + lam_im * lam_im
    fr = ((ar - 1.0) * lam_re + ai * lam_im) / den
    fi = (ai * lam_re - (ar - 1.0) * lam_im) / den
    eye = jnp.eye(8, dtype=F32)

    def blockdiag_in(b):
        b4 = b.reshape(S5_SUPER, 8, S5_STATE, S5_GROUP_CH)
        return jnp.einsum('sgpk,gh->sgkhp', b4, eye).reshape(S5_SUPER, 128, 512)

    bd = jnp.concatenate([blockdiag_in(b_re), blockdiag_in(b_im)], axis=-1).astype(BF16)

    cpr = c_re[None] * fr[:, :, None, :] - c_im[None] * fi[:, :, None, :]
    cpi = c_re[None] * fi[:, :, None, :] + c_im[None] * fr[:, :, None, :]

    def blockdiag_out(c):
        c5 = c.reshape(2, S5_SUPER, 8, S5_GROUP_CH, S5_STATE)
        return jnp.einsum('dsgkp,gh->dsgphk', c5, eye).reshape(2, S5_SUPER, 512, 128)

    cd = jnp.concatenate([blockdiag_out(cpr), blockdiag_out(-cpi)], axis=2)
    cd = jnp.concatenate([cd[0], cd[1]], axis=-1).astype(BF16)
    rows = lambda a: jnp.repeat(a.reshape(2, S5_SUPER * 512), 4, axis=0)
    return bd, cd, rows(ar), rows(ai)


S5_ROWS = 256


def _s5_scan_kernel(tc, uf_ref, ub_ref, bd_ref, cd_ref, ar_ref, ai_ref, h0_ref, flip_ref,
                    yf_ref, yb_ref, hfin_ref, u_scr, g_scr, y_scr, h_scr):
    c = pl.program_id(0)

    @pl.when(c == 0)
    def _():
        h_scr[...] = h0_ref[...]

    flip = flip_ref[...]
    rows = 8 * tc
    for b in range(4):
        uf = uf_ref[b].astype(F32)
        ub = jnp.dot(flip, ub_ref[b].astype(BF16), preferred_element_type=F32)
        for sg in range(S5_SUPER):
            u_scr[sg, pl.ds(b, tc, stride=8), :] = uf[:, sg * 128:(sg + 1) * 128]
            u_scr[sg, pl.ds(4 + b, tc, stride=8), :] = ub[:, sg * 128:(sg + 1) * 128]

    for sg in range(S5_SUPER):
        for r0 in range(0, rows, S5_ROWS):
            g_scr[r0:r0 + S5_ROWS, sg * S5_SLAB:(sg + 1) * S5_SLAB] = jnp.dot(
                u_scr[sg, r0:r0 + S5_ROWS, :].astype(BF16), bd_ref[sg], preferred_element_type=F32)

    half = S5_SLAB // 256
    for sg in range(S5_SUPER):
        ar = [ar_ref[:, (sg * half + k) * 128:(sg * half + k + 1) * 128] for k in range(half)]
        ai = [ai_ref[:, (sg * half + k) * 128:(sg * half + k + 1) * 128] for k in range(half)]
        l_re = [sg * S5_SLAB + k * 128 for k in range(half)]
        l_im = [sg * S5_SLAB + (half + k) * 128 for k in range(half)]

        def body(j, carry, ar=ar, ai=ai, l_re=l_re, l_im=l_im):
            r0 = pl.multiple_of(j * 8, 8)
            out = []
            for k in range(half):
                hr, hi = carry[2 * k], carry[2 * k + 1]
                nr = ar[k] * hr - ai[k] * hi + g_scr[pl.ds(r0, 8), l_re[k]:l_re[k] + 128]
                ni = ar[k] * hi + ai[k] * hr + g_scr[pl.ds(r0, 8), l_im[k]:l_im[k] + 128]
                g_scr[pl.ds(r0, 8), l_re[k]:l_re[k] + 128] = nr
                g_scr[pl.ds(r0, 8), l_im[k]:l_im[k] + 128] = ni
                out += [nr, ni]
            return tuple(out)

        init = []
        for k in range(half):
            init += [h_scr[:, l_re[k]:l_re[k] + 128], h_scr[:, l_im[k]:l_im[k] + 128]]
        fin = lax.fori_loop(0, tc, body, tuple(init), unroll=4)
        for k in range(half):
            h_scr[:, l_re[k]:l_re[k] + 128] = fin[2 * k]
            h_scr[:, l_im[k]:l_im[k] + 128] = fin[2 * k + 1]

    for sg in range(S5_SUPER):
        for r0 in range(0, rows, S5_ROWS):
            y = jnp.dot(g_scr[r0:r0 + S5_ROWS, sg * S5_SLAB:(sg + 1) * S5_SLAB].astype(BF16), cd_ref[sg],
                        preferred_element_type=F32)
            y_scr[2 * sg, r0:r0 + S5_ROWS, :] = y[:, :128]
            y_scr[2 * sg + 1, r0:r0 + S5_ROWS, :] = y[:, 128:]

    for b in range(4):
        yf_ref[b] = jnp.concatenate(
            [y_scr[2 * sg, pl.ds(b, tc, stride=8), :] for sg in range(S5_SUPER)], axis=1)
        yb = jnp.concatenate(
            [y_scr[2 * sg + 1, pl.ds(4 + b, tc, stride=8), :] for sg in range(S5_SUPER)], axis=1)
        yb_ref[b] = _dot_exact_lhs(flip, yb)

    hfin_ref[...] = h_scr[...]


def _s5_scan(u, ucol, bd, cd, ar8, ai8, h0):
    b, l, _ = u.shape
    w = S5_WIDTH
    assert b == 4
    tc = 128
    nc = l // tc
    flip = jnp.asarray(np.eye(tc, dtype=np.float32)[::-1], dtype=BF16)
    full = lambda shape: pl.BlockSpec(shape, lambda c: (0,) * len(shape))
    y_shape = jax.ShapeDtypeStruct((b, l, w), F32)
    return pl.pallas_call(
        functools.partial(_s5_scan_kernel, tc),
        grid=(nc,),
        in_specs=[pl.BlockSpec((4, tc, w), lambda c: (0, c, ucol)),
                  pl.BlockSpec((4, tc, w), lambda c: (0, nc - 1 - c, ucol)),
                  full(bd.shape), full(cd.shape), full(ar8.shape), full(ai8.shape),
                  full(h0.shape), full(flip.shape)],
        out_specs=[pl.BlockSpec((4, tc, w), lambda c: (0, c, 0)),
                   pl.BlockSpec((4, tc, w), lambda c: (0, nc - 1 - c, 0)),
                   full((8, S5_LANES))],
        out_shape=[y_shape, y_shape, jax.ShapeDtypeStruct((8, S5_LANES), F32)],
        scratch_shapes=[pltpu.VMEM((S5_SUPER, 8 * tc, 128), F32), pltpu.VMEM((8 * tc, S5_LANES), F32),
                        pltpu.VMEM((2 * S5_SUPER, 8 * tc, 128), F32), pltpu.VMEM((8, S5_LANES), F32)],
        compiler_params=_cparams("arbitrary"),
        name="s5_scan",
    )(u, u, bd, cd, ar8, ai8, h0, flip)


def _gelu_tanh(x):
    return 0.5 * x * (1.0 + jnp.tanh(math.sqrt(2.0 / math.pi) * (x + 0.044715 * (x * x * x))))


def _s5_glu_kernel(yf_ref, yb_ref, u_ref, d_ref, w_ref, b_ref, o_ref):
    y = _gelu_tanh(yf_ref[0] + yb_ref[0] + d_ref[...] * u_ref[0].astype(F32))
    o_ref[0] = (y * _sigmoid(_bdot(y, w_ref[...]) + b_ref[...])).astype(o_ref.dtype)


def _s5_glu(yf, yb, u, ucol, d_skip, glu_w_bf16, glu_b):
    b, l, w = yf.shape
    tl = min(l, 1024)
    row = lambda bb, i: (bb, i, 0)
    cst = lambda bb, i: (0, 0)
    return pl.pallas_call(
        _s5_glu_kernel,
        grid=(b, l // tl),
        in_specs=[pl.BlockSpec((1, tl, w), row)] * 2 + [
            pl.BlockSpec((1, tl, w), lambda bb, i: (bb, i, ucol)),
            pl.BlockSpec((1, w), cst), pl.BlockSpec((w, w), cst), pl.BlockSpec((1, w), cst)],
        out_specs=pl.BlockSpec((1, tl, w), row),
        out_shape=jax.ShapeDtypeStruct(yf.shape, BF16),
        compiler_params=_cparams("parallel", "parallel"),
        name="s5_glu",
    )(yf, yb, u, d_skip, glu_w_bf16, glu_b)


SSD_T = 128
SSD_NB = 2


def _softplus(x):
    return jnp.maximum(x, 0.0) + jnp.log(1.0 + jnp.exp(-jnp.abs(x)))


def _ssd_kernel(xs_ref, bm_ref, cm_ref, dtc_ref, dtr_ref, bias_c_ref, bias_r_ref, a_c_ref, a_r_ref,
                tri_ref, trit_ref, exp_ref, h0_ref, y_ref, hfin_ref, s_scr):
    c = pl.program_id(2)

    @pl.when(c == 0)
    def _():
        s_scr[...] = h0_ref[0]

    tri = tri_ref[0]
    expand = exp_ref[...]
    mask = tri > 0
    for bi in range(SSD_NB):
        dt_c = _softplus(dtc_ref[0, bi] + bias_c_ref[0])
        dt_r = _softplus(dtr_ref[0, bi] + bias_r_ref[0])
        cum_c = _dot_exact_lhs(tri, dt_c * a_c_ref[0])
        cum_r = _dot_exact_rhs(dt_r * a_r_ref[0], trit_ref[0])
        tot_c = jnp.min(cum_c, axis=0, keepdims=True)

        t = dt_c.shape[0]
        ex = _dot_split2_rhs(jnp.concatenate([dt_c, jnp.exp(cum_c), jnp.exp(tot_c - cum_c)], axis=0), expand)
        dt_x = ex[:t]
        in_x = ex[t:2 * t]
        out_x = ex[2 * t:]
        tot_x = jnp.where(pl.program_id(0) == 0, in_x[t - 1:t], in_x[0:1])

        xdt = xs_ref[bi].astype(F32) * dt_x
        xout = (xdt * out_x).astype(BF16)
        xdt = xdt.astype(BF16)
        for g in range(M2_GROUPS):
            bg = bm_ref[bi, :, g * M2_STATE:(g + 1) * M2_STATE].astype(BF16)
            cg = cm_ref[bi, :, g * M2_STATE:(g + 1) * M2_STATE].astype(BF16)
            cb = lax.dot_general(cg, bg, (((1,), (1,)), ((), ())), preferred_element_type=F32)
            gc = slice(g * M2_GROUP_W, (g + 1) * M2_GROUP_W)
            s_prev = s_scr[bi, g]
            y_off = jnp.dot(cg, s_prev.astype(BF16), preferred_element_type=F32) * in_x[:, gc]
            y_heads = []
            for r in range(M2_HEADS // M2_GROUPS):
                h = g * (M2_HEADS // M2_GROUPS) + r
                seg = cum_c[:, h:h + 1] - cum_r[h:h + 1, :]
                m = (cb * jnp.exp(jnp.where(mask, seg, -jnp.inf))).astype(BF16)
                hc = slice(h * M2_HEAD_DIM, (h + 1) * M2_HEAD_DIM)
                y_heads.append(jnp.dot(m, xdt[:, hc], preferred_element_type=F32))
            y_ref[0, bi, :, gc] = (jnp.concatenate(y_heads, axis=1) + y_off).astype(y_ref.dtype)
            s_scr[bi, g] = s_prev * tot_x[:, gc] + lax.dot_general(
                bg, xout[:, gc], (((0,), (0,)), ((), ())), preferred_element_type=F32)

    @pl.when(c == pl.num_programs(2) - 1)
    def _():
        hfin_ref[0] = s_scr[...]


def _ssd(xbc, dt_col, dt_row, dt_bias, a_coef, h0):
    b, l, _ = xbc.shape
    t = SSD_T
    nc = l // t
    tri_f = np.tril(np.ones((t, t), np.float32))
    tri = jnp.asarray(np.stack([tri_f, tri_f.T]), dtype=BF16)
    trit = jnp.asarray(np.stack([tri_f.T, tri_f]), dtype=BF16)
    expand = jnp.asarray(np.kron(np.eye(M2_HEADS, dtype=np.float32),
                                 np.ones((1, M2_HEAD_DIM), np.float32)), dtype=BF16)
    chunk = lambda d, c: c + d * (nc - 1 - 2 * c)
    nb = SSD_NB
    state = pl.BlockSpec((1, nb, M2_GROUPS, M2_STATE, M2_GROUP_W), lambda d, bb, c: (d, bb, 0, 0, 0))
    return pl.pallas_call(
        _ssd_kernel,
        grid=(2, b // nb, nc),
        in_specs=[pl.BlockSpec((nb, t, M2_INNER), lambda d, bb, c: (bb, chunk(d, c), 0)),
                  pl.BlockSpec((nb, t, M2_BC), lambda d, bb, c: (bb, chunk(d, c), 2)),
                  pl.BlockSpec((nb, t, M2_BC), lambda d, bb, c: (bb, chunk(d, c), 3)),
                  pl.BlockSpec((1, nb, t, M2_HEADS), lambda d, bb, c: (d, bb, chunk(d, c), 0)),
                  pl.BlockSpec((1, nb, M2_HEADS, t), lambda d, bb, c: (d, bb, 0, chunk(d, c))),
                  pl.BlockSpec((1, 1, M2_HEADS), lambda d, bb, c: (d, 0, 0)),
                  pl.BlockSpec((1, M2_HEADS, 1), lambda d, bb, c: (d, 0, 0)),
                  pl.BlockSpec((1, 1, M2_HEADS), lambda d, bb, c: (d, 0, 0)),
                  pl.BlockSpec((1, M2_HEADS, 1), lambda d, bb, c: (d, 0, 0)),
                  pl.BlockSpec((1, t, t), lambda d, bb, c: (d, 0, 0)),
                  pl.BlockSpec((1, t, t), lambda d, bb, c: (d, 0, 0)),
                  pl.BlockSpec((M2_HEADS, M2_INNER), lambda d, bb, c: (0, 0)),
                  state],
        out_specs=[pl.BlockSpec((1, nb, t, M2_INNER), lambda d, bb, c: (d, bb, chunk(d, c), 0)), state],
        out_shape=[jax.ShapeDtypeStruct((2, b, l, M2_INNER), BF16),
                   jax.ShapeDtypeStruct((2, b, M2_GROUPS, M2_STATE, M2_GROUP_W), F32)],
        scratch_shapes=[pltpu.VMEM((nb, M2_GROUPS, M2_STATE, M2_GROUP_W), F32)],
        compiler_params=_cparams("parallel", "parallel", "arbitrary"),
        name="ssd",
    )(xbc, xbc, xbc, dt_col, dt_row, dt_bias.reshape(2, 1, M2_HEADS), dt_bias.reshape(2, M2_HEADS, 1),
      a_coef.reshape(2, 1, M2_HEADS), a_coef.reshape(2, M2_HEADS, 1), tri, trit, expand, h0)


def _m2_gate_kernel(yf_ref, yb_ref, xs_ref, z_ref, d_ref, g_ref, o_ref):
    y = (yf_ref[0, 0].astype(F32) + yb_ref[0, 0].astype(F32)
         + d_ref[...] * xs_ref[0].astype(F32)) * _silu(z_ref[0].astype(F32))
    ms = jnp.mean(y * y, axis=-1, keepdims=True)
    o_ref[0] = (y * lax.rsqrt(ms + NORM_EPS) * g_ref[...]).astype(o_ref.dtype)


def _m2_gate(ydir, xbc, z, zcol, d_x, norm_g):
    _, b, l, w = ydir.shape
    tl = min(l, 512)
    row = lambda bb, i: (bb, i, 0)
    cst = lambda bb, i: (0, 0)
    return pl.pallas_call(
        _m2_gate_kernel,
        grid=(b, l // tl),
        in_specs=[pl.BlockSpec((1, 1, tl, w), lambda bb, i: (0, bb, i, 0)),
                  pl.BlockSpec((1, 1, tl, w), lambda bb, i: (1, bb, i, 0)),
                  pl.BlockSpec((1, tl, w), row), pl.BlockSpec((1, tl, w), lambda bb, i: (bb, i, zcol)),
                  pl.BlockSpec((1, w), cst), pl.BlockSpec((1, w), cst)],
        out_specs=pl.BlockSpec((1, tl, w), row),
        out_shape=jax.ShapeDtypeStruct((b, l, w), BF16),
        compiler_params=_cparams("parallel", "parallel"),
        name="m2_gate",
    )(ydir, ydir, xbc, z, d_x, norm_g)


HY_N2 = 128


def _round8(n):
    return (n + 7) // 8 * 8


def _dft2_tables(n1, n2):
    n = n1 * n2
    k1n = n1 // 2 + 1
    k1p = _round8(k1n)
    k1 = np.arange(k1p, dtype=np.float64)[:, None]
    valid = (k1 < k1n).astype(np.float64)
    i1 = np.arange(n1, dtype=np.float64)[None, :]
    th1 = 2.0 * np.pi * k1 * i1 / n1
    stage1 = np.concatenate([np.cos(th1) * valid, -np.sin(th1) * valid], axis=0)
    t = n2 * i1[None] + np.arange(n2, dtype=np.float64)[:, None, None]
    th = 2.0 * np.pi * k1[None] * t / n
    stage1_tw = np.concatenate([np.cos(th) * valid[None], -np.sin(th) * valid[None]], axis=1)
    i2 = np.arange(n2, dtype=np.float64)
    tht = 2.0 * np.pi * i2 / n
    tw1 = np.stack([np.broadcast_to(np.cos(tht)[:, None], (n2, 128)),
                    np.broadcast_to(-np.sin(tht)[:, None], (n2, 128))])
    k2 = np.arange(n2, dtype=np.float64)[:, None]
    th2 = 2.0 * np.pi * k2 * i2[None, :] / n2
    fc, fs = np.cos(th2), np.sin(th2)
    fwd2 = np.block([[fc, fs], [-fs, fc]])
    inv2 = np.block([[fc, -fs], [fs, fc]])
    wk = np.where((k1 == 0) | (k1 == n1 // 2), 1.0, 2.0) * valid
    o1 = np.arange(n1 // 2, dtype=np.float64)[:, None]
    tho = 2.0 * np.pi * o1 * k1.T / n1
    last = np.concatenate([np.cos(tho) * wk.T, -np.sin(tho) * wk.T], axis=1) / n
    f = lambda a: jnp.asarray(a, dtype=F32)
    return dict(k1n=k1n, k1p=k1p, stage1=f(stage1), stage1_tw=f(stage1_tw), tw1=f(tw1),
                fwd2=f(fwd2), inv2=f(inv2), last=f(last))


def _dft1_tables(l):
    n = 2 * l
    kn = l + 1
    kp = _round8(kn)
    k = np.arange(kp, dtype=np.float64)[:, None]
    valid = (k < kn).astype(np.float64)
    t = np.arange(n, dtype=np.float64)[None, :]
    th = 2.0 * np.pi * k * t / n
    fwd = np.concatenate([np.cos(th) * valid, -np.sin(th) * valid], axis=0)
    wk = np.where((k == 0) | (k == l), 1.0, 2.0) * valid
    o = np.arange(l, dtype=np.float64)[:, None]
    tho = 2.0 * np.pi * o * k.T / n
    inv = np.concatenate([np.cos(tho) * wk.T, -np.sin(tho) * wk.T], axis=1) / n
    f = lambda a: jnp.asarray(a, dtype=F32)
    return dict(kp=kp, fwd=f(fwd), inv=f(inv))


def _rowdft_kernel(hi, f_ref, x_ref, o_ref):
    dot = _dot3 if hi else _bdot
    o_ref[0] = dot(f_ref[...], x_ref[0])


def _rowdft(fmat, x, hi):
    bx, r, n = x.shape
    m = fmat.shape[0]
    tn = min(n, 4096)
    return pl.pallas_call(
        functools.partial(_rowdft_kernel, hi),
        grid=(bx, n // tn),
        in_specs=[pl.BlockSpec((m, r), lambda b, j: (0, 0)),
                  pl.BlockSpec((1, r, tn), lambda b, j: (b, 0, j))],
        out_specs=pl.BlockSpec((1, m, tn), lambda b, j: (b, 0, j)),
        out_shape=jax.ShapeDtypeStruct((bx, m, n), F32),
        compiler_params=_cparams("parallel", "parallel"),
        name="hy_rowdft",
    )(fmat if hi else fmat.astype(BF16), x)


HY_TS = 8


HY_PITCH = HY_N2 + 8
HY_CONV_VMEM_BYTES = 58 * 1024 * 1024


def _hy_conv_kernel(k1n, k1p, n2, v_ref, gate_ref, h_ref, st1_ref, tw1_ref, fwd_ref, inv_ref, last_ref,
                    d_ref, o_ref, a_scr, t_scr):
    n1h = v_ref.shape[0] // n2
    st1 = st1_ref[...]
    last = last_ref[...]

    def copy_in(i, _):
        t_scr[pl.ds(pl.multiple_of(i * HY_PITCH, 8), n2), :] = (
            v_ref[pl.ds(pl.multiple_of(i * n2, n2), n2), :].astype(F32))
        return 0
    lax.fori_loop(0, n1h, copy_in, 0)

    def stage1(j, _):
        s = 2 * j
        xs = jnp.concatenate([t_scr[pl.ds(s, n1h, stride=HY_PITCH), :],
                              t_scr[pl.ds(s + 1, n1h, stride=HY_PITCH), :]], axis=1).astype(BF16)
        y = jnp.dot(st1, xs, preferred_element_type=F32)
        a_scr[pl.ds(s, 2 * k1p, stride=HY_PITCH), :] = y[:, :128]
        a_scr[pl.ds(s + 1, 2 * k1p, stride=HY_PITCH), :] = y[:, 128:]
        return 0
    lax.fori_loop(0, n2 // 2, stage1, 0, unroll=4)

    def cmul(ar, ai, br, bi):
        return ar * br - ai * bi, ar * bi + ai * br

    def stage2(j, tw):
        t1r, t1i = tw1_ref[0], tw1_ref[1]
        tw_a = tw
        tw_b = cmul(*tw_a, t1r, t1i)
        ks = (2 * j, 2 * j + 1)
        rows_r = [pl.ds(pl.multiple_of(k * HY_PITCH, 8), n2) for k in ks]
        rows_i = [pl.ds(pl.multiple_of((k1p + k) * HY_PITCH, 8), n2) for k in ks]
        pr, pi = [], []
        for k, twk, rr, ri in zip(ks, (tw_a, tw_b), rows_r, rows_i):
            a, b = cmul(a_scr[rr, :], a_scr[ri, :], *twk)
            pr.append(a)
            pi.append(b)
        p = jnp.concatenate([jnp.concatenate(pr, axis=1), jnp.concatenate(pi, axis=1)], axis=0)
        x = jnp.dot(fwd_ref[...], p.astype(BF16), preferred_element_type=F32)
        xr, xi = x[:n2], x[n2:]
        hr = jnp.concatenate([h_ref[0, ks[0]], h_ref[0, ks[1]]], axis=1).astype(F32)
        hi = jnp.concatenate([h_ref[1, ks[0]], h_ref[1, ks[1]]], axis=1).astype(F32)
        qr, qi = cmul(xr, xi, hr, hi)
        z = jnp.dot(inv_ref[...], jnp.concatenate([qr, qi], axis=0).astype(BF16), preferred_element_type=F32)
        for n, (twk, rr, ri) in enumerate(zip((tw_a, tw_b), rows_r, rows_i)):
            zr, zi = z[:n2, n * 128:(n + 1) * 128], z[n2:, n * 128:(n + 1) * 128]
            a_scr[rr, :] = zr * twk[0] + zi * twk[1]
            a_scr[ri, :] = zi * twk[0] - zr * twk[1]
        return cmul(*tw_b, t1r, t1i)

    pairs = 2 * ((k1n + 3) // 4)
    assert 2 * pairs <= k1p
    lax.fori_loop(0, pairs, stage2, (jnp.ones((n2, 128), F32), jnp.zeros((n2, 128), F32)), unroll=2)

    def stage3(j, _):
        s = 2 * j
        zs = jnp.concatenate([a_scr[pl.ds(s, 2 * k1p, stride=HY_PITCH), :],
                              a_scr[pl.ds(s + 1, 2 * k1p, stride=HY_PITCH), :]], axis=1).astype(BF16)
        y = jnp.dot(last, zs, preferred_element_type=F32)
        t_scr[pl.ds(s, n1h, stride=HY_PITCH), :] = y[:, :128]
        t_scr[pl.ds(s + 1, n1h, stride=HY_PITCH), :] = y[:, 128:]
        return 0
    lax.fori_loop(0, n2 // 2, stage3, 0, unroll=4)

    def finish(i, _):
        rows = pl.ds(pl.multiple_of(i * n2, n2), n2)
        y = t_scr[pl.ds(pl.multiple_of(i * HY_PITCH, 8), n2), :]
        o_ref[rows, :] = (gate_ref[rows, :].astype(F32)
                          * (y + d_ref[...] * v_ref[rows, :].astype(F32))).astype(o_ref.dtype)
        return 0
    lax.fori_loop(0, n1h, finish, 0)


def _hy_long_conv2(v, gate, hspec, order, d_vec, tab):
    b, l, c = v.shape
    n2 = HY_N2
    n1h = l // n2
    k1p = tab['k1p']
    tiles = c // 128
    seq = pl.BlockSpec((None, l, 128), lambda j, bb: (bb, 0, j))
    cst2 = lambda a: pl.BlockSpec(a.shape, lambda j, bb: (0, 0))
    st1 = tab['stage1'][:, :n1h].astype(BF16)
    fwd2, inv2, last = tab['fwd2'].astype(BF16), tab['inv2'].astype(BF16), tab['last'].astype(BF16)
    return pl.pallas_call(
        functools.partial(_hy_conv_kernel, tab['k1n'], k1p, n2),
        grid=(tiles, b),
        in_specs=[seq, seq,
                  pl.BlockSpec((2, k1p, n2, 128), lambda j, bb: (0, 0, 0, order * tiles + j)),
                  cst2(st1),
                  pl.BlockSpec((2, n2, 128), lambda j, bb: (0, 0, 0)),
                  cst2(fwd2), cst2(inv2), cst2(last),
                  pl.BlockSpec((1, 128), lambda j, bb: (0, j))],
        out_specs=seq,
        out_shape=jax.ShapeDtypeStruct(v.shape, BF16),
        scratch_shapes=[pltpu.VMEM((2 * k1p * HY_PITCH, 128), F32), pltpu.VMEM((n1h * HY_PITCH, 128), F32)],
        compiler_params=pltpu.CompilerParams(dimension_semantics=("parallel", "parallel"),
                                             vmem_limit_bytes=HY_CONV_VMEM_BYTES),
        name="hy_conv",
    )(v, gate, hspec, st1, tab['tw1'], fwd2, inv2, last, d_vec.reshape(1, c))


def _hy_filt_first_kernel(k1p, feat_ref, t_ref, keep_ref, w1_ref, b1_ref, q1_ref, w2_ref, b2_ref, q2_ref,
                          w3_ref, nd_ref, f_ref, o_ref, sum_ref):
    @pl.when(pl.program_id(0) == 0)
    def _():
        sum_ref[...] = jnp.zeros_like(sum_ref)

    n1 = feat_ref.shape[1]
    half = n1 // 2
    feats = feat_ref[...].reshape(HY_TS * n1, feat_ref.shape[2])
    hid = jnp.sin(q1_ref[...] * (_bdot(feats, w1_ref[...]) + b1_ref[...]))
    hid = jnp.sin(q2_ref[...] * (_bdot(hid, w2_ref[...]) + b2_ref[...]))
    acc = jnp.zeros(sum_ref.shape, F32)
    for s in range(HY_TS):
        hs = hid[s * n1:(s + 1) * n1]
        f = jnp.concatenate([_bdot(hs[:half], w3_ref[0]), _bdot(hs[half:], w3_ref[1])], axis=0)
        f = f * jnp.exp(t_ref[s] * nd_ref[...]) * keep_ref[s]
        acc = acc + jnp.sum(jnp.abs(f), axis=0, keepdims=True)
        y = _dot3(f_ref[s], f)
        o_ref[0, :, s, :] = y[:k1p]
        o_ref[1, :, s, :] = y[k1p:]
    sum_ref[...] += acc


def _hy_filt_first(l, tab, f_w1, f_b1, f_freq1, f_w2, f_b2, f_freq2, f_w3):
    c2 = HY_ORDER * HY_WIDTH
    n2 = HY_N2
    n1 = 2 * l // n2
    k1p = tab['k1p']
    t = np.linspace(0.0, 1.0, l, dtype=np.float32)[:, None]
    bands = np.linspace(1e-4, HY_BANDS - 1, HY_BANDS, dtype=np.float32)
    ang = np.float32(2.0 * math.pi / l) * np.arange(l, dtype=np.float32)[:, None] * bands
    feats = np.concatenate([t, np.cos(ang), -np.sin(ang)], axis=-1)
    rows = (n2 * np.arange(n1)[None, :] + np.arange(n2)[:, None]).reshape(-1)
    pos = np.where(rows < l, rows, np.where(rows == l, 0, 2 * l - rows))
    feats_t = jnp.asarray(np.pad(feats[pos], ((0, 0), (0, 128 - HY_EMB))).reshape(n2, n1, 128))
    t_t = jnp.asarray(t[pos].reshape(n2, n1, 1))
    keep_t = jnp.asarray((rows != l).astype(np.float32).reshape(n2, n1, 1))
    w1p = jnp.pad(f_w1, ((0, 128 - HY_EMB), (0, 0)))
    w3d = jnp.transpose(f_w3.reshape(HY_HIDDEN, HY_ORDER, 2, HY_WIDTH), (2, 0, 1, 3)).reshape(2, HY_HIDDEN, c2)
    deltas = jnp.abs(jnp.linspace(HY_MIN_DECAY, HY_MAX_DECAY, HY_WIDTH, dtype=F32))
    negd = jnp.tile(-deltas, HY_ORDER)[None, :]
    cst = lambda i: (0, 0)
    vec = lambda a: a.reshape(1, HY_HIDDEN)
    tile3 = lambda w: pl.BlockSpec((HY_TS, n1, w), lambda i: (i, 0, 0))
    return pl.pallas_call(
        functools.partial(_hy_filt_first_kernel, k1p),
        grid=(n2 // HY_TS,),
        in_specs=[tile3(128), tile3(1), tile3(1),
                  pl.BlockSpec((128, HY_HIDDEN), cst), pl.BlockSpec((1, HY_HIDDEN), cst),
                  pl.BlockSpec((1, HY_HIDDEN), cst),
                  pl.BlockSpec((HY_HIDDEN, HY_HIDDEN), cst), pl.BlockSpec((1, HY_HIDDEN), cst),
                  pl.BlockSpec((1, HY_HIDDEN), cst),
                  pl.BlockSpec((2, HY_HIDDEN, c2), lambda i: (0, 0, 0)),
                  pl.BlockSpec((1, c2), cst),
                  pl.BlockSpec((HY_TS, 2 * k1p, n1), lambda i: (i, 0, 0))],
        out_specs=[pl.BlockSpec((None, 2, k1p, HY_TS, c2), lambda i: (0, 0, 0, i, 0)),
                   pl.BlockSpec((1, c2), cst)],
        out_shape=[jax.ShapeDtypeStruct((1, 2, k1p, n2, c2), F32), jax.ShapeDtypeStruct((1, c2), F32)],
        compiler_params=_cparams("arbitrary"),
        name="hy_filt_first",
    )(feats_t, t_t, keep_t, w1p, vec(f_b1), vec(f_freq1), f_w2, vec(f_b2), vec(f_freq2), w3d, negd,
      tab['stage1_tw'])


def _hy_filt_mid_kernel(n2, a_ref, sum_ref, fwd_ref, o_ref):
    c2 = a_ref.shape[-1]
    x = _dot3(fwd_ref[...], a_ref[0, :, 0].reshape(2 * n2, c2)) * (1.0 / (sum_ref[...] + 1e-6))
    o_ref[0, 0] = x[:n2].astype(BF16)
    o_ref[1, 0] = x[n2:].astype(BF16)


def _hy_filt_mid(a5, tab, colsum):
    _, _, k1p, n2, c2 = a5.shape
    return pl.pallas_call(
        functools.partial(_hy_filt_mid_kernel, n2),
        grid=(k1p,),
        in_specs=[pl.BlockSpec((1, 2, 1, n2, c2), lambda k: (0, 0, k, 0, 0)),
                  pl.BlockSpec((1, c2), lambda k: (0, 0)),
                  pl.BlockSpec((2 * n2, 2 * n2), lambda k: (0, 0))],
        out_specs=pl.BlockSpec((2, 1, n2, c2), lambda k: (0, k, 0, 0)),
        out_shape=jax.ShapeDtypeStruct((2, k1p, n2, c2), BF16),
        compiler_params=_cparams("parallel"),
        name="hy_filt_mid",
    )(a5, colsum, tab['fwd2'])


def _hy_ctx_kernel(kp, fw_ref, inv_ref, h_ref, v_ref, gate_ref, d_ref, o_ref):
    v = v_ref[0].astype(F32)
    u = _bdot(fw_ref[...], v)
    ur, ui = u[:kp], u[kp:]
    hr, hi = h_ref[0], h_ref[1]
    q = jnp.concatenate([ur * hr - ui * hi, ur * hi + ui * hr], axis=0)
    o_ref[0] = (gate_ref[0].astype(F32) * (_bdot(inv_ref[...], q) + d_ref[...] * v)).astype(o_ref.dtype)


def _hy_long_conv1(v, gate, hspec, order, d_vec, tab):
    b, l, c = v.shape
    kp = tab['kp']
    row = lambda bb: (bb, 0, 0)
    return pl.pallas_call(
        functools.partial(_hy_ctx_kernel, kp),
        grid=(b,),
        in_specs=[pl.BlockSpec((2 * kp, l), lambda bb: (0, 0)),
                  pl.BlockSpec((l, 2 * kp), lambda bb: (0, 0)),
                  pl.BlockSpec((2, kp, c), lambda bb: (0, 0, order)),
                  pl.BlockSpec((1, l, c), row), pl.BlockSpec((1, l, c), row),
                  pl.BlockSpec((1, c), lambda bb: (0, 0))],
        out_specs=pl.BlockSpec((1, l, c), row),
        out_shape=jax.ShapeDtypeStruct(v.shape, BF16),
        compiler_params=_cparams("parallel"),
        name="hy_ctx_conv",
    )(tab['fwd'][:, :l].astype(BF16), tab['inv'].astype(BF16), hspec, v, gate, d_vec.reshape(1, c))


def _hy_filter_kernel(feat_ref, t_ref, keep_ref, w1_ref, b1_ref, q1_ref, w2_ref, b2_ref, q2_ref,
                      w3_ref, nd_ref, f_ref, sum_ref):
    @pl.when(pl.program_id(0) == 0)
    def _():
        sum_ref[...] = jnp.zeros_like(sum_ref)

    hid = jnp.sin(q1_ref[...] * (_bdot(feat_ref[...], w1_ref[...]) + b1_ref[...]))
    hid = jnp.sin(q2_ref[...] * (_bdot(hid, w2_ref[...]) + b2_ref[...]))
    f = _bdot(hid, w3_ref[0]) * jnp.exp(t_ref[...] * nd_ref[...]) * keep_ref[...]
    f_ref[...] = f
    sum_ref[...] += jnp.sum(jnp.abs(f), axis=0, keepdims=True)


def _hy_filters(l, f_w1, f_b1, f_freq1, f_w2, f_b2, f_freq2, f_w3):
    c2 = HY_ORDER * HY_WIDTH
    t = np.linspace(0.0, 1.0, l, dtype=np.float32)[:, None]
    bands = np.linspace(1e-4, HY_BANDS - 1, HY_BANDS, dtype=np.float32)
    ang = np.float32(2.0 * math.pi / l) * np.arange(l, dtype=np.float32)[:, None] * bands
    feats = np.concatenate([t, np.cos(ang), -np.sin(ang)], axis=-1)
    rows = np.arange(2 * l)
    pos = np.where(rows < l, rows, np.where(rows == l, 0, 2 * l - rows))
    feats2 = jnp.asarray(np.pad(feats[pos], ((0, 0), (0, 128 - HY_EMB))))
    keep = jnp.asarray((rows != l).astype(np.float32))[:, None]
    t2 = jnp.asarray(t[pos])
    w1p = jnp.pad(f_w1, ((0, 128 - HY_EMB), (0, 0)))
    w3d = jnp.transpose(f_w3.reshape(HY_HIDDEN, HY_ORDER, 2, HY_WIDTH), (2, 0, 1, 3)).reshape(2, HY_HIDDEN, c2)
    deltas = jnp.abs(jnp.linspace(HY_MIN_DECAY, HY_MAX_DECAY, HY_WIDTH, dtype=F32))
    negd = jnp.tile(-deltas, HY_ORDER)[None, :]
    tr = min(l, 512)
    per_dir = l // tr
    cst = lambda i: (0, 0)
    vec = lambda a: a.reshape(1, HY_HIDDEN)
    return pl.pallas_call(
        _hy_filter_kernel,
        grid=(2 * l // tr,),
        in_specs=[pl.BlockSpec((tr, 128), lambda i: (i, 0)),
                  pl.BlockSpec((tr, 1), lambda i: (i, 0)),
                  pl.BlockSpec((tr, 1), lambda i: (i, 0)),
                  pl.BlockSpec((128, HY_HIDDEN), cst), pl.BlockSpec((1, HY_HIDDEN), cst),
                  pl.BlockSpec((1, HY_HIDDEN), cst),
                  pl.BlockSpec((HY_HIDDEN, HY_HIDDEN), cst), pl.BlockSpec((1, HY_HIDDEN), cst),
                  pl.BlockSpec((1, HY_HIDDEN), cst),
                  pl.BlockSpec((1, HY_HIDDEN, c2), lambda i: (i // per_dir, 0, 0)),
                  pl.BlockSpec((1, c2), cst)],
        out_specs=[pl.BlockSpec((tr, c2), lambda i: (i, 0)), pl.BlockSpec((1, c2), cst)],
        out_shape=[jax.ShapeDtypeStruct((2 * l, c2), F32), jax.ShapeDtypeStruct((1, c2), F32)],
        compiler_params=_cparams("arbitrary"),
        name="hy_filter",
    )(feats2, t2, keep, w1p, vec(f_b1), vec(f_freq1), f_w2, vec(f_b2), vec(f_freq2), w3d, negd)


def _col_scale_kernel(x_ref, sum_ref, o_ref):
    o_ref[...] = x_ref[...] * (1.0 / (sum_ref[...] + 1e-6))


def _hy_spectrum(l, fparams):
    if 2 * l >= 16 * HY_N2:
        tab = _dft2_tables(2 * l // HY_N2, HY_N2)
        a, colsum = _hy_filt_first(l, tab, *fparams)
        return _hy_filt_mid(a, tab, colsum), tab
    filt, colsum = _hy_filters(l, *fparams)
    c2 = filt.shape[1]
    tab = _dft1_tables(l)
    spec = _rowdft(tab['fwd'], filt[None], True)[0]
    spec = pl.pallas_call(
        _col_scale_kernel,
        in_specs=[pl.BlockSpec(spec.shape, lambda: (0, 0)), pl.BlockSpec((1, c2), lambda: (0, 0))],
        out_specs=pl.BlockSpec(spec.shape, lambda: (0, 0)),
        out_shape=jax.ShapeDtypeStruct(spec.shape, F32),
        name="hy_col_scale",
    )(spec, colsum)
    return spec.reshape(2, tab['kp'], c2), tab


def _hyena_mixer(h_proj, conv_w, conv_b, hspec, tab, d_skip):
    c = HY_WIDTH
    parts = [_dwconv(h_proj, conv_w, conv_b, False, col0=i * c, ncols=c) for i in range(3)]
    v, x1, x2 = parts
    conv = _hy_long_conv2 if 'k1p' in tab else _hy_long_conv1
    y = conv(v, x1, hspec, 0, d_skip[0], tab)
    return conv(y, x2, hspec, 1, d_skip[1], tab)


def _grid_sincos(n, dm):
    rows = n // GRID_W
    quarter = dm // 4
    omega = 1.0 / (10000.0 ** (jnp.arange(quarter, dtype=F32) / quarter))
    ang_r = jnp.arange(rows, dtype=F32)[:, None] * omega
    ang_c = jnp.arange(GRID_W, dtype=F32)[:, None] * omega
    emb_r = jnp.concatenate([jnp.sin(ang_r), jnp.cos(ang_r)], axis=-1)
    emb_c = jnp.concatenate([jnp.sin(ang_c), jnp.cos(ang_c)], axis=-1)
    half = emb_r.shape[-1]
    pe = jnp.concatenate([jnp.broadcast_to(emb_r[:, None, :], (rows, GRID_W, half)),
                          jnp.broadcast_to(emb_c[None, :, :], (rows, GRID_W, half))], axis=-1)
    return pe.reshape(rows * GRID_W, 2 * half)


EV_Z_OFF = M2_XBC
EV_U_OFF = EV_Z_OFF + M2_INNER
EV_DT_OFF = EV_U_OFF + S5_WIDTH
EV_PROJ_W = EV_DT_OFF + 128


def _even_mixer(h_in, p, states):
    x, g, shift, scale = h_in
    b, l, _ = x.shape
    proj = _adaln_mm(x, g, shift, scale, p['w_in'], jnp.zeros((1, EV_PROJ_W), F32))
    s5_h0, m2_h0 = states

    ucol = EV_U_OFF // S5_WIDTH
    yf, yb, s5_fin = _s5_scan(proj, ucol, p['s5_bd'], p['s5_cd'], p['s5_ar'], p['s5_ai'], s5_h0)
    y_s5 = _s5_glu(yf, yb, proj, ucol, p['s5_d'], p['s5_glu_w'], p['s5_glu_b'])

    xbc = _dwconv(proj, p['m2_conv_w'], p['m2_conv_b'], True, col0=0, ncols=M2_XBC)
    dt4 = proj[..., EV_DT_OFF:EV_DT_OFF + 2 * M2_HEADS].astype(F32).reshape(b, l, 2, M2_HEADS)
    dt_col = jnp.transpose(dt4, (2, 0, 1, 3))
    dt_row = jnp.transpose(dt4, (2, 0, 3, 1))
    ydir, m2_fin = _ssd(xbc, dt_col, dt_row, p['m2_dt_bias'], p['m2_a'], m2_h0)
    y_m2 = _m2_gate(ydir, xbc, proj, EV_Z_OFF // M2_INNER, p['m2_d'], p['m2_norm_g'])
    return (y_s5, y_m2), (s5_fin, m2_fin)


def kernel(x, c, ctx, c_ctx, mod_w, mod_b, norm_mix_g, norm_mlp_g, mlp_w1, mlp_w2, final_norm_g, ev_in_w, ev_out_w, s5_lam_re, s5_lam_im, s5_log_dt, s5_b_re, s5_b_im, s5_c_re, s5_c_im, s5_d, s5_glu_w, s5_glu_b, m2_conv_w, m2_conv_b, m2_dt_bias, m2_a_log, m2_d, m2_norm_g, hy_in_w, hy_in_b, hy_conv_w, hy_conv_b, hy_f_w1, hy_f_b1, hy_f_freq1, hy_f_w2, hy_f_b2, hy_f_freq2, hy_f_w3, hy_d, hy_out_w, hy_out_b):
    bsz, n, dm = x.shape
    lc = ctx.shape[1]
    x = _add_pe(x, _grid_sincos(n, dm))

    c8 = jnp.concatenate([c, c_ctx[None], jnp.zeros((3, dm), F32)], axis=0)
    mods = _modulation(c8, mod_w, mod_b).reshape(DEPTH, 8, N_MOD, dm)
    final_g = final_norm_g.reshape(1, dm)

    for i in range(DEPTH):
        j = i // 2
        mx = [mods[i, :bsz, k][:, None, :] for k in range(N_MOD)]
        mc = [jnp.broadcast_to(mods[i, bsz, k][None, None, :], (bsz, 1, dm)) for k in range(N_MOD)]
        g_mix = norm_mix_g[i].reshape(1, dm)
        g_mlp = norm_mlp_g[i].reshape(1, dm)
        ctx_later = any(k % 2 == 0 for k in range(i + 1, DEPTH))
        w1 = mlp_w1[i].astype(BF16)
        w2 = mlp_w2[i].astype(BF16)

        if i % 2 == 0:
            in_w = ev_in_w[j]
            o0, o1, o2 = S5_WIDTH, S5_WIDTH + M2_INNER, S5_WIDTH + M2_INNER + M2_XBC
            bd, cd, ar8, ai8 = _s5_prepare(s5_lam_re[j], s5_lam_im[j], s5_log_dt[j], s5_b_re[j], s5_b_im[j],
                                           s5_c_re[j], s5_c_im[j])
            p = dict(
                w_in=jnp.concatenate([in_w[:, o1:o2], in_w[:, o0:o1], in_w[:, :o0],
                                      jnp.pad(in_w[:, o2:], ((0, 0), (0, 128 - 2 * M2_HEADS)))],
                                     axis=1).astype(BF16),
                s5_bd=bd, s5_cd=cd, s5_ar=ar8, s5_ai=ai8,
                s5_d=s5_d[j].reshape(1, S5_WIDTH), s5_glu_w=s5_glu_w[j].astype(BF16),
                s5_glu_b=s5_glu_b[j].reshape(1, S5_WIDTH),
                m2_conv_w=m2_conv_w[j], m2_conv_b=m2_conv_b[j].reshape(1, M2_XBC),
                m2_dt_bias=m2_dt_bias[j], m2_a=-jnp.exp(m2_a_log[j]),
                m2_d=jnp.repeat(m2_d[j], M2_HEAD_DIM).reshape(1, M2_INNER),
                m2_norm_g=m2_norm_g[j].reshape(1, M2_INNER))
            zero_states = (jnp.zeros((8, S5_LANES), F32),
                           jnp.zeros((2, bsz, M2_GROUPS, M2_STATE, M2_GROUP_W), F32))
            ys_c, ctx_states = _even_mixer((ctx, g_mix, mc[0], mc[1]), p, zero_states)
            ys_x, _ = _even_mixer((x, g_mix, mx[0], mx[1]), p, ctx_states)
            out_ws = [ev_out_w[j][:S5_WIDTH].astype(BF16), ev_out_w[j][S5_WIDTH:].astype(BF16)]
            out_b = jnp.zeros((1, dm), F32)
        else:
            fparams = (hy_f_w1[j], hy_f_b1[j], hy_f_freq1[j], hy_f_w2[j], hy_f_b2[j], hy_f_freq2[j], hy_f_w3[j])
            in_w = hy_in_w[j].astype(BF16)
            in_b = hy_in_b[j].reshape(1, -1)
            conv_b = hy_conv_b[j].reshape(1, -1)
            hspec, tab = _hy_spectrum(n, fparams)
            ys_x = [_hyena_mixer(_adaln_mm(x, g_mix, mx[0], mx[1], in_w, in_b), hy_conv_w[j], conv_b,
                                 hspec, tab, hy_d[j])]
            if ctx_later:
                hspec_c, tab_c = _hy_spectrum(lc, fparams)
                ys_c = [_hyena_mixer(_adaln_mm(ctx, g_mix, mc[0], mc[1], in_w, in_b), hy_conv_w[j], conv_b,
                                     hspec_c, tab_c, hy_d[j])]
            out_ws = [hy_out_w[j].astype(BF16)]
            out_b = hy_out_b[j].reshape(1, dm)

        x = _mix_mlp(x, mx[2], out_b, ys_x, out_ws, g_mlp, mx[3], mx[4], mx[5], w1, w2, final_g, i == DEPTH - 1)
        if ctx_later:
            ctx = _mix_mlp(ctx, mc[2], out_b, ys_c, out_ws, g_mlp, mc[3], mc[4], mc[5], w1, w2, final_g, False)
    return x
```

```python
import functools
import math

import numpy as np
import jax
import jax.numpy as jnp
from jax import lax
from jax.experimental import pallas as pl
from jax.experimental.pallas import tpu as pltpu

F32 = jnp.float32
BF16 = jnp.bfloat16

D_MODEL = 1024
DEPTH = 4
GRID_W = 64
N_MOD = 6
MLP_HIDDEN = 4 * D_MODEL
NORM_EPS = 1e-6

S5_WIDTH = 512
S5_GROUP_CH = 16
S5_GROUPS = 32
S5_STATE = 64
S5_SUPER = 4
S5_SLAB = 2 * 8 * S5_STATE
S5_LANES = S5_SUPER * S5_SLAB

M2_INNER = 1024
M2_HEAD_DIM = 64
M2_HEADS = 16
M2_GROUPS = 4
M2_STATE = 128
M2_BC = M2_GROUPS * M2_STATE
M2_XBC = M2_INNER + 2 * M2_BC
M2_GROUP_W = (M2_HEADS // M2_GROUPS) * M2_HEAD_DIM

HY_WIDTH = 1024
HY_ORDER = 2
HY_BANDS = 16
HY_EMB = 2 * HY_BANDS + 1
HY_HIDDEN = 64
HY_MIN_DECAY = math.log(1e-2) / 1.5
HY_MAX_DECAY = math.log(1e-2) / 0.3

V7X_VMEM_LIMIT_BYTES = 48 * 1024 * 1024


def _cparams(*sem):
    return pltpu.CompilerParams(dimension_semantics=sem, vmem_limit_bytes=V7X_VMEM_LIMIT_BYTES)


def _bdot(a, b):
    return jnp.dot(a.astype(BF16), b.astype(BF16), preferred_element_type=F32)


def _split3(a):
    a1 = a.astype(BF16)
    r1 = a - a1.astype(F32)
    a2 = r1.astype(BF16)
    a3 = (r1 - a2.astype(F32)).astype(BF16)
    return a1, a2, a3


def _dot_exact_rhs(a, b_bf16):
    a1, a2, a3 = _split3(a)
    d = functools.partial(jnp.dot, preferred_element_type=F32)
    return d(a1, b_bf16) + d(a2, b_bf16) + d(a3, b_bf16)


def _dot_split2_rhs(a, b_bf16):
    a1 = a.astype(BF16)
    a2 = (a - a1.astype(F32)).astype(BF16)
    d = functools.partial(jnp.dot, preferred_element_type=F32)
    return d(a1, b_bf16) + d(a2, b_bf16)


def _dot_exact_lhs(a_bf16, b):
    b1, b2, b3 = _split3(b)
    d = functools.partial(jnp.dot, preferred_element_type=F32)
    return d(a_bf16, b1) + d(a_bf16, b2) + d(a_bf16, b3)


def _dot3(a, b):
    a1 = a.astype(BF16)
    a2 = (a - a1.astype(F32)).astype(BF16)
    b1 = b.astype(BF16)
    b2 = (b - b1.astype(F32)).astype(BF16)
    d = functools.partial(jnp.dot, preferred_element_type=F32)
    return d(a1, b1) + d(a1, b2) + d(a2, b1)


def _silu(x):
    return x * (1.0 / (1.0 + jnp.exp(-x)))


def _sigmoid(x):
    return 1.0 / (1.0 + jnp.exp(-x))


def _adaln(x, g, shift, scale):
    ms = jnp.mean(x * x, axis=-1, keepdims=True)
    return (x * lax.rsqrt(ms + NORM_EPS) * g) * (1.0 + scale) + shift


def _mod_kernel(c_ref, w_ref, b_ref, o_ref):
    o_ref[0] = _bdot(_silu(c_ref[...]), w_ref[0]) + b_ref[0]


def _modulation(c8, mod_w, mod_b):
    n = mod_w.shape[-1]
    tn = 1536
    return pl.pallas_call(
        _mod_kernel,
        grid=(DEPTH, n // tn),
        in_specs=[pl.BlockSpec((8, D_MODEL), lambda i, j: (0, 0)),
                  pl.BlockSpec((1, D_MODEL, tn), lambda i, j: (i, 0, j)),
                  pl.BlockSpec((1, 1, tn), lambda i, j: (i, 0, j))],
        out_specs=pl.BlockSpec((1, 8, tn), lambda i, j: (i, 0, j)),
        out_shape=jax.ShapeDtypeStruct((DEPTH, 8, n), F32),
        compiler_params=_cparams("parallel", "parallel"),
        name="modulation",
    )(c8, mod_w, mod_b.reshape(DEPTH, 1, n))


def _add_pe_kernel(x_ref, pe_ref, o_ref):
    o_ref[0] = x_ref[0] + pe_ref[...]


def _add_pe(x, pe):
    b, l, d = x.shape
    tl = min(l, 1024)
    return pl.pallas_call(
        _add_pe_kernel,
        grid=(l // tl, b),
        in_specs=[pl.BlockSpec((1, tl, d), lambda i, bb: (bb, i, 0)),
                  pl.BlockSpec((tl, d), lambda i, bb: (i, 0))],
        out_specs=pl.BlockSpec((1, tl, d), lambda i, bb: (bb, i, 0)),
        out_shape=jax.ShapeDtypeStruct(x.shape, F32),
        compiler_params=_cparams("parallel", "parallel"),
        name="add_pe",
    )(x, pe)


CONV_COLS = 512


def _adaln_mm_conv_kernel(ncv, act, x_ref, xp_ref, xn_ref, g_ref, sh_ref, sc_ref, w_ref, b_ref, cw_ref, cb_ref,
                          o_ref):
    i = pl.program_id(1)
    tl = x_ref.shape[1]
    rows = tl + 16
    xe = jnp.concatenate([xp_ref[0], x_ref[0], xn_ref[0]], axis=0)
    h = _adaln(xe, g_ref[...], sh_ref[0], sc_ref[0]).astype(BF16)
    n = w_ref.shape[1]
    r = lax.broadcasted_iota(jnp.int32, (tl, 1), 0)
    first = (r == 0) & (i == 0)
    last = (r == tl - 1) & (i == pl.num_programs(1) - 1)
    for c0 in range(0, n, CONV_COLS):
        cols = slice(c0, min(c0 + CONV_COLS, n))
        p = jnp.dot(h, w_ref[:, cols], preferred_element_type=F32) + b_ref[:, cols]
        mid = p[8:8 + tl]
        if c0 < ncv:
            prev = jnp.where(first, 0.0, pltpu.roll(p, 1, axis=0)[8:8 + tl])
            nxt = jnp.where(last, 0.0, pltpu.roll(p, rows - 1, axis=0)[8:8 + tl])
            cw = cw_ref[:, cols]
            mid = prev * cw[0:1] + mid * cw[1:2] + nxt * cw[2:3] + cb_ref[:, cols]
            if act:
                mid = _silu(mid)
        o_ref[0, :, cols] = mid.astype(o_ref.dtype)


def _adaln_mm_conv(x, g, shift, scale, w_bf16, bias, conv_w, conv_b, act):
    b, l, d = x.shape
    n = w_bf16.shape[1]
    ncv = conv_w.shape[1]
    assert ncv % CONV_COLS == 0
    tl = min(l, 256)
    t8 = tl // 8
    nblk8 = l // 8
    cst = lambda bb, i: (0, 0)
    vec = pl.BlockSpec((1, 1, d), lambda bb, i: (bb, 0, 0))
    return pl.pallas_call(
        functools.partial(_adaln_mm_conv_kernel, ncv, act),
        grid=(b, l // tl),
        in_specs=[pl.BlockSpec((1, tl, d), lambda bb, i: (bb, i, 0)),
                  pl.BlockSpec((1, 8, d), lambda bb, i: (bb, jnp.maximum(i * t8 - 1, 0), 0)),
                  pl.BlockSpec((1, 8, d), lambda bb, i: (bb, jnp.minimum((i + 1) * t8, nblk8 - 1), 0)),
                  pl.BlockSpec((1, d), cst), vec, vec,
                  pl.BlockSpec((d, n), cst), pl.BlockSpec((1, n), cst),
                  pl.BlockSpec((3, ncv), cst), pl.BlockSpec((1, ncv), cst)],
        out_specs=pl.BlockSpec((1, tl, n), lambda bb, i: (bb, i, 0)),
        out_shape=jax.ShapeDtypeStruct((b, l, n), BF16),
        compiler_params=_cparams("parallel", "parallel"),
        name="adaln_mm_conv",
    )(x, x, x, g, shift, scale, w_bf16, bias, conv_w, conv_b)


MLP_CHUNK = 512


def _mix_mlp_kernel(n_in, final_norm, *refs):
    x_ref, ga_ref, bo_ref = refs[0:3]
    y_refs = refs[3:3 + n_in]
    wo_refs = refs[3 + n_in:3 + 2 * n_in]
    g_ref, sh_ref, sc_ref, gf_ref, w1_ref, w2_ref, fg_ref, o_ref, a_scr = refs[3 + 2 * n_in:]
    mix = bo_ref[...] + _bdot(y_refs[0][0], wo_refs[0][...])
    for y_ref, wo_ref in zip(y_refs[1:], wo_refs[1:]):
        mix = mix + _bdot(y_ref[0], wo_ref[...])
    x1 = x_ref[0] + ga_ref[0] * mix
    h = _adaln(x1, g_ref[...], sh_ref[0], sc_ref[0]).astype(BF16)
    for c in range(w1_ref.shape[1] // MLP_CHUNK):
        cols = slice(c * MLP_CHUNK, (c + 1) * MLP_CHUNK)
        a = jnp.dot(h, w1_ref[:, cols], preferred_element_type=F32)
        a_scr[:, cols] = jnp.square(jnp.maximum(a, 0.0)).astype(BF16)
    y = x1 + gf_ref[0] * jnp.dot(a_scr[...], w2_ref[...], preferred_element_type=F32)
    if final_norm:
        ms = jnp.mean(y * y, axis=-1, keepdims=True)
        y = y * lax.rsqrt(ms + NORM_EPS) * fg_ref[...]
    o_ref[0] = y


def _mix_mlp(x, gate_a, out_b, ys, out_ws, g, shift, scale, gate_f, w1_bf16, w2_bf16, final_g, final_norm):
    b, l, d = x.shape
    hdim = w1_bf16.shape[1]
    tl = min(l, 512)
    n_in = len(ys)
    row = lambda bb, i: (bb, i, 0)
    vec = pl.BlockSpec((1, 1, d), lambda bb, i: (bb, 0, 0))
    cvec = pl.BlockSpec((1, d), lambda bb, i: (0, 0))
    resident = lambda a: pl.BlockSpec(a.shape, lambda bb, i: (0, 0), pipeline_mode=pl.Buffered(1))
    in_specs = [pl.BlockSpec((1, tl, d), row), vec, cvec]
    in_specs += [pl.BlockSpec((1, tl, y.shape[-1]), row) for y in ys]
    in_specs += [resident(w) for w in out_ws]
    in_specs += [cvec, vec, vec, vec, resident(w1_bf16), resident(w2_bf16), cvec]
    return pl.pallas_call(
        functools.partial(_mix_mlp_kernel, n_in, final_norm),
        grid=(b, l // tl),
        in_specs=in_specs,
        out_specs=pl.BlockSpec((1, tl, d), row),
        out_shape=jax.ShapeDtypeStruct(x.shape, F32),
        scratch_shapes=[pltpu.VMEM((tl, hdim), BF16)],
        compiler_params=_cparams("parallel", "parallel"),
        name="mix_mlp",
    )(x, gate_a, out_b, *ys, *out_ws, g, shift, scale, gate_f, w1_bf16, w2_bf16, final_g)


def _dwconv_kernel(act, x_ref, w_ref, b_ref, o_ref):
    x = x_ref[0].astype(F32)
    l = x.shape[0]
    t = lax.broadcasted_iota(jnp.int32, x.shape, 0)
    prev = jnp.where(t == 0, 0.0, pltpu.roll(x, 1, axis=0))
    nxt = jnp.where(t == l - 1, 0.0, pltpu.roll(x, l - 1, axis=0))
    w = w_ref[...]
    y = prev * w[0:1] + x * w[1:2] + nxt * w[2:3] + b_ref[...]
    if act:
        y = _silu(y)
    o_ref[0] = y.astype(o_ref.dtype)


def _dwconv(x, w, bias, act, col0=0, ncols=None):
    b, l, c = x.shape
    ncols = c if ncols is None else ncols
    tc = 256 if l > 1024 else 512
    j0 = col0 // tc
    return pl.pallas_call(
        functools.partial(_dwconv_kernel, act),
        grid=(b, ncols // tc),
        in_specs=[pl.BlockSpec((1, l, tc), lambda bb, j: (bb, 0, j0 + j)),
                  pl.BlockSpec((3, tc), lambda bb, j: (0, j0 + j)),
                  pl.BlockSpec((1, tc), lambda bb, j: (0, j0 + j))],
        out_specs=pl.BlockSpec((1, l, tc), lambda bb, j: (bb, 0, j)),
        out_shape=jax.ShapeDtypeStruct((b, l, ncols), BF16),
        compiler_params=_cparams("parallel", "parallel"),
        name="dwconv",
    )(x, w, bias)


def _s5_prepare(lam_re, lam_im, log_dt, b_re, b_im, c_re, c_im):
    step = jnp.exp(log_dt)[..., None]
    mag = jnp.exp(lam_re * step)
    ar = mag * jnp.cos(lam_im * step)
    ai = mag * jnp.sin(lam_im * step)
    den = lam_re * lam_re + lam_im * lam_im
    fr = ((ar - 1.0) * lam_re + ai * lam_im) / den
    fi = (ai * lam_re - (ar - 1.0) * lam_im) / den
    eye = jnp.eye(8, dtype=F32)

    def blockdiag_in(b):
        b4 = b.reshape(S5_SUPER, 8, S5_STATE, S5_GROUP_CH)
        return jnp.einsum('sgpk,gh->sgkhp', b4, eye).reshape(S5_SUPER, 128, 512)

    bd = jnp.concatenate([blockdiag_in(b_re), blockdiag_in(b_im)], axis=-1).astype(BF16)

    cpr = c_re[None] * fr[:, :, None, :] - c_im[None] * fi[:, :, None, :]
    cpi = c_re[None] * fi[:, :, None, :] + c_im[None] * fr[:, :, None, :]

    def blockdiag_out(c):
        c5 = c.reshape(2, S5_SUPER, 8, S5_GROUP_CH, S5_STATE)
        return jnp.einsum('dsgkp,gh->dsgphk', c5, eye).reshape(2, S5_SUPER, 512, 128)

    cd = jnp.concatenate([blockdiag_out(cpr), blockdiag_out(-cpi)], axis=2)
    cd = jnp.concatenate([cd[0], cd[1]], axis=-1).astype(BF16)
    rows = lambda a: jnp.repeat(a.reshape(2, S5_SUPER * 512), 4, axis=0)
    return bd, cd, rows(ar), rows(ai)


S5_ROWS = 256


def _s5_scan_kernel(tc, uf_ref, ub_ref, bd_ref, cd_ref, ar_ref, ai_ref, h0_ref, flip_ref,
                    yf_ref, yb_ref, hfin_ref, u_scr, g_scr, y_scr, h_scr):
    c = pl.program_id(0)

    @pl.when(c == 0)
    def _():
        h_scr[...] = h0_ref[...]

    flip = flip_ref[...]
    rows = 8 * tc
    for b in range(4):
        uf = uf_ref[b].astype(F32)
        ub = jnp.dot(flip, ub_ref[b].astype(BF16), preferred_element_type=F32)
        for sg in range(S5_SUPER):
            u_scr[sg, pl.ds(b, tc, stride=8), :] = uf[:, sg * 128:(sg + 1) * 128]
            u_scr[sg, pl.ds(4 + b, tc, stride=8), :] = ub[:, sg * 128:(sg + 1) * 128]

    for sg in range(S5_SUPER):
        for r0 in range(0, rows, S5_ROWS):
            g_scr[r0:r0 + S5_ROWS, sg * S5_SLAB:(sg + 1) * S5_SLAB] = jnp.dot(
                u_scr[sg, r0:r0 + S5_ROWS, :].astype(BF16), bd_ref[sg], preferred_element_type=F32)

    half = S5_SLAB // 256
    for sg in range(S5_SUPER):
        ar = [ar_ref[:, (sg * half + k) * 128:(sg * half + k + 1) * 128] for k in range(half)]
        ai = [ai_ref[:, (sg * half + k) * 128:(sg * half + k + 1) * 128] for k in range(half)]
        l_re = [sg * S5_SLAB + k * 128 for k in range(half)]
        l_im = [sg * S5_SLAB + (half + k) * 128 for k in range(half)]

        def body(j, carry, ar=ar, ai=ai, l_re=l_re, l_im=l_im):
            r0 = pl.multiple_of(j * 8, 8)
            out = []
            for k in range(half):
                hr, hi = carry[2 * k], carry[2 * k + 1]
                nr = ar[k] * hr - ai[k] * hi + g_scr[pl.ds(r0, 8), l_re[k]:l_re[k] + 128]
                ni = ar[k] * hi + ai[k] * hr + g_scr[pl.ds(r0, 8), l_im[k]:l_im[k] + 128]
                g_scr[pl.ds(r0, 8), l_re[k]:l_re[k] + 128] = nr
                g_scr[pl.ds(r0, 8), l_im[k]:l_im[k] + 128] = ni
                out += [nr, ni]
            return tuple(out)

        init = []
        for k in range(half):
            init += [h_scr[:, l_re[k]:l_re[k] + 128], h_scr[:, l_im[k]:l_im[k] + 128]]
        fin = lax.fori_loop(0, tc, body, tuple(init), unroll=4)
        for k in range(half):
            h_scr[:, l_re[k]:l_re[k] + 128] = fin[2 * k]
            h_scr[:, l_im[k]:l_im[k] + 128] = fin[2 * k + 1]

    for sg in range(S5_SUPER):
        for r0 in range(0, rows, S5_ROWS):
            y = jnp.dot(g_scr[r0:r0 + S5_ROWS, sg * S5_SLAB:(sg + 1) * S5_SLAB].astype(BF16), cd_ref[sg],
                        preferred_element_type=F32)
            y_scr[2 * sg, r0:r0 + S5_ROWS, :] = y[:, :128]
            y_scr[2 * sg + 1, r0:r0 + S5_ROWS, :] = y[:, 128:]

    for b in range(4):
        yf_ref[b] = jnp.concatenate(
            [y_scr[2 * sg, pl.ds(b, tc, stride=8), :] for sg in range(S5_SUPER)], axis=1)
        yb = jnp.concatenate(
            [y_scr[2 * sg + 1, pl.ds(4 + b, tc, stride=8), :] for sg in range(S5_SUPER)], axis=1)
        yb_ref[b] = _dot_exact_lhs(flip, yb)

    hfin_ref[...] = h_scr[...]


def _s5_scan(u, ucol, bd, cd, ar8, ai8, h0):
    b, l, _ = u.shape
    w = S5_WIDTH
    assert b == 4
    tc = 128
    nc = l // tc
    flip = jnp.asarray(np.eye(tc, dtype=np.float32)[::-1], dtype=BF16)
    full = lambda shape: pl.BlockSpec(shape, lambda c: (0,) * len(shape))
    y_shape = jax.ShapeDtypeStruct((b, l, w), F32)
    return pl.pallas_call(
        functools.partial(_s5_scan_kernel, tc),
        grid=(nc,),
        in_specs=[pl.BlockSpec((4, tc, w), lambda c: (0, c, ucol)),
                  pl.BlockSpec((4, tc, w), lambda c: (0, nc - 1 - c, ucol)),
                  full(bd.shape), full(cd.shape), full(ar8.shape), full(ai8.shape),
                  full(h0.shape), full(flip.shape)],
        out_specs=[pl.BlockSpec((4, tc, w), lambda c: (0, c, 0)),
                   pl.BlockSpec((4, tc, w), lambda c: (0, nc - 1 - c, 0)),
                   full((8, S5_LANES))],
        out_shape=[y_shape, y_shape, jax.ShapeDtypeStruct((8, S5_LANES), F32)],
        scratch_shapes=[pltpu.VMEM((S5_SUPER, 8 * tc, 128), F32), pltpu.VMEM((8 * tc, S5_LANES), F32),
                        pltpu.VMEM((2 * S5_SUPER, 8 * tc, 128), F32), pltpu.VMEM((8, S5_LANES), F32)],
        compiler_params=_cparams("arbitrary"),
        name="s5_scan",
    )(u, u, bd, cd, ar8, ai8, h0, flip)


def _gelu_tanh(x):
    return 0.5 * x * (1.0 + jnp.tanh(math.sqrt(2.0 / math.pi) * (x + 0.044715 * (x * x * x))))


def _s5_glu_kernel(yf_ref, yb_ref, u_ref, d_ref, w_ref, b_ref, o_ref):
    y = _gelu_tanh(yf_ref[0] + yb_ref[0] + d_ref[...] * u_ref[0].astype(F32))
    o_ref[0] = (y * _sigmoid(_bdot(y, w_ref[...]) + b_ref[...])).astype(o_ref.dtype)


def _s5_glu(yf, yb, u, ucol, d_skip, glu_w_bf16, glu_b):
    b, l, w = yf.shape
    tl = min(l, 1024)
    row = lambda bb, i: (bb, i, 0)
    cst = lambda bb, i: (0, 0)
    return pl.pallas_call(
        _s5_glu_kernel,
        grid=(b, l // tl),
        in_specs=[pl.BlockSpec((1, tl, w), row)] * 2 + [
            pl.BlockSpec((1, tl, w), lambda bb, i: (bb, i, ucol)),
            pl.BlockSpec((1, w), cst), pl.BlockSpec((w, w), cst), pl.BlockSpec((1, w), cst)],
        out_specs=pl.BlockSpec((1, tl, w), row),
        out_shape=jax.ShapeDtypeStruct(yf.shape, BF16),
        compiler_params=_cparams("parallel", "parallel"),
        name="s5_glu",
    )(yf, yb, u, d_skip, glu_w_bf16, glu_b)


SSD_T = 128
SSD_NB = 2


def _softplus(x):
    return jnp.maximum(x, 0.0) + jnp.log(1.0 + jnp.exp(-jnp.abs(x)))


def _ssd_kernel(xs_ref, bm_ref, cm_ref, dtc_ref, dtr_ref, bias_c_ref, bias_r_ref, a_c_ref, a_r_ref,
                tri_ref, trit_ref, exp_ref, h0_ref, y_ref, hfin_ref, s_scr):
    c = pl.program_id(2)

    @pl.when(c == 0)
    def _():
        s_scr[...] = h0_ref[0]

    tri = tri_ref[0]
    expand = exp_ref[...]
    mask = tri > 0
    for bi in range(SSD_NB):
        dt_c = _softplus(dtc_ref[0, bi] + bias_c_ref[0])
        dt_r = _softplus(dtr_ref[0, bi] + bias_r_ref[0])
        cum_c = _dot_exact_lhs(tri, dt_c * a_c_ref[0])
        cum_r = _dot_exact_rhs(dt_r * a_r_ref[0], trit_ref[0])
        tot_c = jnp.min(cum_c, axis=0, keepdims=True)

        t = dt_c.shape[0]
        ex = _dot_split2_rhs(jnp.concatenate([dt_c, jnp.exp(cum_c), jnp.exp(tot_c - cum_c)], axis=0), expand)
        dt_x = ex[:t]
        in_x = ex[t:2 * t]
        out_x = ex[2 * t:]
        tot_x = jnp.where(pl.program_id(0) == 0, in_x[t - 1:t], in_x[0:1])

        xdt = xs_ref[bi].astype(F32) * dt_x
        xout = (xdt * out_x).astype(BF16)
        xdt = xdt.astype(BF16)
        for g in range(M2_GROUPS):
            bg = bm_ref[bi, :, g * M2_STATE:(g + 1) * M2_STATE].astype(BF16)
            cg = cm_ref[bi, :, g * M2_STATE:(g + 1) * M2_STATE].astype(BF16)
            cb = lax.dot_general(cg, bg, (((1,), (1,)), ((), ())), preferred_element_type=F32)
            gc = slice(g * M2_GROUP_W, (g + 1) * M2_GROUP_W)
            s_prev = s_scr[bi, g]
            y_off = jnp.dot(cg, s_prev.astype(BF16), preferred_element_type=F32) * in_x[:, gc]
            y_heads = []
            for r in range(M2_HEADS // M2_GROUPS):
                h = g * (M2_HEADS // M2_GROUPS) + r
                seg = cum_c[:, h:h + 1] - cum_r[h:h + 1, :]
                m = (cb * jnp.exp(jnp.where(mask, seg, -jnp.inf))).astype(BF16)
                hc = slice(h * M2_HEAD_DIM, (h + 1) * M2_HEAD_DIM)
                y_heads.append(jnp.dot(m, xdt[:, hc], preferred_element_type=F32))
            y_ref[0, bi, :, gc] = (jnp.concatenate(y_heads, axis=1) + y_off).astype(y_ref.dtype)
            s_scr[bi, g] = s_prev * tot_x[:, gc] + lax.dot_general(
                bg, xout[:, gc], (((0,), (0,)), ((), ())), preferred_element_type=F32)

    @pl.when(c == pl.num_programs(2) - 1)
    def _():
        hfin_ref[0] = s_scr[...]


def _ssd(xbc, dt_col, dt_row, dt_bias, a_coef, h0):
    b, l, _ = xbc.shape
    t = SSD_T
    nc = l // t
    tri_f = np.tril(np.ones((t, t), np.float32))
    tri = jnp.asarray(np.stack([tri_f, tri_f.T]), dtype=BF16)
    trit = jnp.asarray(np.stack([tri_f.T, tri_f]), dtype=BF16)
    expand = jnp.asarray(np.kron(np.eye(M2_HEADS, dtype=np.float32),
                                 np.ones((1, M2_HEAD_DIM), np.float32)), dtype=BF16)
    chunk = lambda d, c: c + d * (nc - 1 - 2 * c)
    nb = SSD_NB
    state = pl.BlockSpec((1, nb, M2_GROUPS, M2_STATE, M2_GROUP_W), lambda d, bb, c: (d, bb, 0, 0, 0))
    return pl.pallas_call(
        _ssd_kernel,
        grid=(2, b // nb, nc),
        in_specs=[pl.BlockSpec((nb, t, M2_INNER), lambda d, bb, c: (bb, chunk(d, c), 0)),
                  pl.BlockSpec((nb, t, M2_BC), lambda d, bb, c: (bb, chunk(d, c), 2)),
                  pl.BlockSpec((nb, t, M2_BC), lambda d, bb, c: (bb, chunk(d, c), 3)),
                  pl.BlockSpec((1, nb, t, M2_HEADS), lambda d, bb, c: (d, bb, chunk(d, c), 0)),
                  pl.BlockSpec((1, nb, M2_HEADS, t), lambda d, bb, c: (d, bb, 0, chunk(d, c))),
                  pl.BlockSpec((1, 1, M2_HEADS), lambda d, bb, c: (d, 0, 0)),
                  pl.BlockSpec((1, M2_HEADS, 1), lambda d, bb, c: (d, 0, 0)),
                  pl.BlockSpec((1, 1, M2_HEADS), lambda d, bb, c: (d, 0, 0)),
                  pl.BlockSpec((1, M2_HEADS, 1), lambda d, bb, c: (d, 0, 0)),
                  pl.BlockSpec((1, t, t), lambda d, bb, c: (d, 0, 0)),
                  pl.BlockSpec((1, t, t), lambda d, bb, c: (d, 0, 0)),
                  pl.BlockSpec((M2_HEADS, M2_INNER), lambda d, bb, c: (0, 0)),
                  state],
        out_specs=[pl.BlockSpec((1, nb, t, M2_INNER), lambda d, bb, c: (d, bb, chunk(d, c), 0)), state],
        out_shape=[jax.ShapeDtypeStruct((2, b, l, M2_INNER), BF16),
                   jax.ShapeDtypeStruct((2, b, M2_GROUPS, M2_STATE, M2_GROUP_W), F32)],
        scratch_shapes=[pltpu.VMEM((nb, M2_GROUPS, M2_STATE, M2_GROUP_W), F32)],
        compiler_params=_cparams("parallel", "parallel", "arbitrary"),
        name="ssd",
    )(xbc, xbc, xbc, dt_col, dt_row, dt_bias.reshape(2, 1, M2_HEADS), dt_bias.reshape(2, M2_HEADS, 1),
      a_coef.reshape(2, 1, M2_HEADS), a_coef.reshape(2, M2_HEADS, 1), tri, trit, expand, h0)


def _m2_gate_kernel(yf_ref, yb_ref, xs_ref, z_ref, d_ref, g_ref, o_ref):
    y = (yf_ref[0, 0].astype(F32) + yb_ref[0, 0].astype(F32)
         + d_ref[...] * xs_ref[0].astype(F32)) * _silu(z_ref[0].astype(F32))
    ms = jnp.mean(y * y, axis=-1, keepdims=True)
    o_ref[0] = (y * lax.rsqrt(ms + NORM_EPS) * g_ref[...]).astype(o_ref.dtype)


def _m2_gate(ydir, xbc, z, zcol, d_x, norm_g):
    _, b, l, w = ydir.shape
    tl = min(l, 512)
    row = lambda bb, i: (bb, i, 0)
    cst = lambda bb, i: (0, 0)
    return pl.pallas_call(
        _m2_gate_kernel,
        grid=(b, l // tl),
        in_specs=[pl.BlockSpec((1, 1, tl, w), lambda bb, i: (0, bb, i, 0)),
                  pl.BlockSpec((1, 1, tl, w), lambda bb, i: (1, bb, i, 0)),
                  pl.BlockSpec((1, tl, w), row), pl.BlockSpec((1, tl, w), lambda bb, i: (bb, i, zcol)),
                  pl.BlockSpec((1, w), cst), pl.BlockSpec((1, w), cst)],
        out_specs=pl.BlockSpec((1, tl, w), row),
        out_shape=jax.ShapeDtypeStruct((b, l, w), BF16),
        compiler_params=_cparams("parallel", "parallel"),
        name="m2_gate",
    )(ydir, ydir, xbc, z, d_x, norm_g)


HY_N2 = 128


def _round8(n):
    return (n + 7) // 8 * 8


def _dft2_tables(n1, n2):
    n = n1 * n2
    k1n = n1 // 2 + 1
    k1p = _round8(k1n)
    k1 = np.arange(k1p, dtype=np.float64)[:, None]
    valid = (k1 < k1n).astype(np.float64)
    i1 = np.arange(n1, dtype=np.float64)[None, :]
    th1 = 2.0 * np.pi * k1 * i1 / n1
    stage1 = np.concatenate([np.cos(th1) * valid, -np.sin(th1) * valid], axis=0)
    t = n2 * i1[None] + np.arange(n2, dtype=np.float64)[:, None, None]
    th = 2.0 * np.pi * k1[None] * t / n
    stage1_tw = np.concatenate([np.cos(th) * valid[None], -np.sin(th) * valid[None]], axis=1)
    i2 = np.arange(n2, dtype=np.float64)
    tht = 2.0 * np.pi * i2 / n
    tw1 = np.stack([np.broadcast_to(np.cos(tht)[:, None], (n2, 128)),
                    np.broadcast_to(-np.sin(tht)[:, None], (n2, 128))])
    k2 = np.arange(n2, dtype=np.float64)[:, None]
    th2 = 2.0 * np.pi * k2 * i2[None, :] / n2
    fc, fs = np.cos(th2), np.sin(th2)
    fwd2 = np.block([[fc, fs], [-fs, fc]])
    inv2 = np.block([[fc, -fs], [fs, fc]])
    wk = np.where((k1 == 0) | (k1 == n1 // 2), 1.0, 2.0) * valid
    o1 = np.arange(n1 // 2, dtype=np.float64)[:, None]
    tho = 2.0 * np.pi * o1 * k1.T / n1
    last = np.concatenate([np.cos(tho) * wk.T, -np.sin(tho) * wk.T], axis=1) / n
    f = lambda a: jnp.asarray(a, dtype=F32)
    return dict(k1n=k1n, k1p=k1p, stage1=f(stage1), stage1_tw=f(stage1_tw), tw1=f(tw1),
                fwd2=f(fwd2), inv2=f(inv2), last=f(last))


def _dft1_tables(l):
    n = 2 * l
    kn = l + 1
    kp = _round8(kn)
    k = np.arange(kp, dtype=np.float64)[:, None]
    valid = (k < kn).astype(np.float64)
    t = np.arange(n, dtype=np.float64)[None, :]
    th = 2.0 * np.pi * k * t / n
    fwd = np.concatenate([np.cos(th) * valid, -np.sin(th) * valid], axis=0)
    wk = np.where((k == 0) | (k == l), 1.0, 2.0) * valid
    o = np.arange(l, dtype=np.float64)[:, None]
    tho = 2.0 * np.pi * o * k.T / n
    inv = np.concatenate([np.cos(tho) * wk.T, -np.sin(tho) * wk.T], axis=1) / n
    f = lambda a: jnp.asarray(a, dtype=F32)
    return dict(kp=kp, fwd=f(fwd), inv=f(inv))


def _rowdft_kernel(hi, f_ref, x_ref, o_ref):
    dot = _dot3 if hi else _bdot
    o_ref[0] = dot(f_ref[...], x_ref[0])


def _rowdft(fmat, x, hi):
    bx, r, n = x.shape
    m = fmat.shape[0]
    tn = min(n, 4096)
    return pl.pallas_call(
        functools.partial(_rowdft_kernel, hi),
        grid=(bx, n // tn),
        in_specs=[pl.BlockSpec((m, r), lambda b, j: (0, 0)),
                  pl.BlockSpec((1, r, tn), lambda b, j: (b, 0, j))],
        out_specs=pl.BlockSpec((1, m, tn), lambda b, j: (b, 0, j)),
        out_shape=jax.ShapeDtypeStruct((bx, m, n), F32),
        compiler_params=_cparams("parallel", "parallel"),
        name="hy_rowdft",
    )(fmat if hi else fmat.astype(BF16), x)


HY_TS = 8


HY_PITCH = HY_N2 + 8
HY_CONV_VMEM_BYTES = 58 * 1024 * 1024
HY_STAGE2_UNROLL = 4


def _hy_conv_kernel(k1n, k1p, n2, v_ref, gate_ref, h_ref, st1_ref, tw1_ref, fwd_ref, inv_ref, last_ref,
                    d_ref, o_ref, a_scr, t_scr):
    n1h = v_ref.shape[0] // n2
    st1 = st1_ref[...]
    last = last_ref[...]

    def copy_in(i, _):
        t_scr[pl.ds(pl.multiple_of(i * HY_PITCH, 8), n2), :] = (
            v_ref[pl.ds(pl.multiple_of(i * n2, n2), n2), :].astype(F32))
        return 0
    lax.fori_loop(0, n1h, copy_in, 0)

    def stage1(j, _):
        s = 2 * j
        xs = jnp.concatenate([t_scr[pl.ds(s, n1h, stride=HY_PITCH), :],
                              t_scr[pl.ds(s + 1, n1h, stride=HY_PITCH), :]], axis=1).astype(BF16)
        y = jnp.dot(st1, xs, preferred_element_type=F32)
        a_scr[pl.ds(s, 2 * k1p, stride=HY_PITCH), :] = y[:, :128]
        a_scr[pl.ds(s + 1, 2 * k1p, stride=HY_PITCH), :] = y[:, 128:]
        return 0
    lax.fori_loop(0, n2 // 2, stage1, 0, unroll=8)

    def cmul(ar, ai, br, bi):
        return ar * br - ai * bi, ar * bi + ai * br

    def stage2(j, tw):
        t1r, t1i = tw1_ref[0], tw1_ref[1]
        tw_a = tw
        tw_b = cmul(*tw_a, t1r, t1i)
        ks = (2 * j, 2 * j + 1)
        rows_r = [pl.ds(pl.multiple_of(k * HY_PITCH, 8), n2) for k in ks]
        rows_i = [pl.ds(pl.multiple_of((k1p + k) * HY_PITCH, 8), n2) for k in ks]
        pr, pi = [], []
        for k, twk, rr, ri in zip(ks, (tw_a, tw_b), rows_r, rows_i):
            a, b = cmul(a_scr[rr, :], a_scr[ri, :], *twk)
            pr.append(a)
            pi.append(b)
        p = jnp.concatenate([jnp.concatenate(pr, axis=1), jnp.concatenate(pi, axis=1)], axis=0)
        x = jnp.dot(fwd_ref[...], p.astype(BF16), preferred_element_type=F32)
        xr, xi = x[:n2], x[n2:]
        hr = jnp.concatenate([h_ref[0, ks[0]], h_ref[0, ks[1]]], axis=1).astype(F32)
        hi = jnp.concatenate([h_ref[1, ks[0]], h_ref[1, ks[1]]], axis=1).astype(F32)
        qr, qi = cmul(xr, xi, hr, hi)
        z = jnp.dot(inv_ref[...], jnp.concatenate([qr, qi], axis=0).astype(BF16), preferred_element_type=F32)
        for n, (twk, rr, ri) in enumerate(zip((tw_a, tw_b), rows_r, rows_i)):
            zr, zi = z[:n2, n * 128:(n + 1) * 128], z[n2:, n * 128:(n + 1) * 128]
            a_scr[rr, :] = zr * twk[0] + zi * twk[1]
            a_scr[ri, :] = zi * twk[0] - zr * twk[1]
        return cmul(*tw_b, t1r, t1i)

    pairs = k1p // 2
    lax.fori_loop(0, pairs, stage2, (jnp.ones((n2, 128), F32), jnp.zeros((n2, 128), F32)),
                  unroll=HY_STAGE2_UNROLL)

    def stage3(j, _):
        s = 2 * j
        zs = jnp.concatenate([a_scr[pl.ds(s, 2 * k1p, stride=HY_PITCH), :],
                              a_scr[pl.ds(s + 1, 2 * k1p, stride=HY_PITCH), :]], axis=1).astype(BF16)
        y = jnp.dot(last, zs, preferred_element_type=F32)
        t_scr[pl.ds(s, n1h, stride=HY_PITCH), :] = y[:, :128]
        t_scr[pl.ds(s + 1, n1h, stride=HY_PITCH), :] = y[:, 128:]
        return 0
    lax.fori_loop(0, n2 // 2, stage3, 0, unroll=8)

    def finish(i, _):
        rows = pl.ds(pl.multiple_of(i * n2, n2), n2)
        y = t_scr[pl.ds(pl.multiple_of(i * HY_PITCH, 8), n2), :]
        o_ref[rows, :] = (gate_ref[rows, :].astype(F32)
                          * (y + d_ref[...] * v_ref[rows, :].astype(F32))).astype(o_ref.dtype)
        return 0
    lax.fori_loop(0, n1h, finish, 0)


def _hy_long_conv2(v, vpart, gate, gpart, hspec, order, d_vec, tab):
    b, l, _ = v.shape
    c = HY_WIDTH
    n2 = HY_N2
    n1h = l // n2
    k1p = tab['k1p']
    tiles = c // 128
    seq = pl.BlockSpec((None, l, 128), lambda j, bb: (bb, 0, j))
    vseq = pl.BlockSpec((None, l, 128), lambda j, bb: (bb, 0, vpart * tiles + j))
    gseq = pl.BlockSpec((None, l, 128), lambda j, bb: (bb, 0, gpart * tiles + j))
    cst2 = lambda a: pl.BlockSpec(a.shape, lambda j, bb: (0, 0))
    st1 = tab['stage1'][:, :n1h].astype(BF16)
    fwd2, inv2, last = tab['fwd2'].astype(BF16), tab['inv2'].astype(BF16), tab['last'].astype(BF16)
    return pl.pallas_call(
        functools.partial(_hy_conv_kernel, tab['k1n'], k1p, n2),
        grid=(tiles, b),
        in_specs=[vseq, gseq,
                  pl.BlockSpec((2, k1p, n2, 128), lambda j, bb: (0, 0, 0, order * tiles + j)),
                  cst2(st1),
                  pl.BlockSpec((2, n2, 128), lambda j, bb: (0, 0, 0)),
                  cst2(fwd2), cst2(inv2), cst2(last),
                  pl.BlockSpec((1, 128), lambda j, bb: (0, j))],
        out_specs=seq,
        out_shape=jax.ShapeDtypeStruct((b, l, c), BF16),
        scratch_shapes=[pltpu.VMEM((2 * k1p * HY_PITCH, 128), F32), pltpu.VMEM((n1h * HY_PITCH, 128), F32)],
        compiler_params=pltpu.CompilerParams(dimension_semantics=("parallel", "parallel"),
                                             vmem_limit_bytes=HY_CONV_VMEM_BYTES),
        name="hy_conv",
    )(v, gate, hspec, st1, tab['tw1'], fwd2, inv2, last, d_vec.reshape(1, c))


def _hy_filt_first_kernel(k1p, feat_ref, t_ref, keep_ref, w1_ref, b1_ref, q1_ref, w2_ref, b2_ref, q2_ref,
                          w3_ref, nd_ref, f_ref, o_ref, sum_ref):
    @pl.when(pl.program_id(0) == 0)
    def _():
        sum_ref[...] = jnp.zeros_like(sum_ref)

    n1 = feat_ref.shape[1]
    half = n1 // 2
    feats = feat_ref[...].reshape(HY_TS * n1, feat_ref.shape[2])
    hid = jnp.sin(q1_ref[...] * (_bdot(feats, w1_ref[...]) + b1_ref[...]))
    hid = jnp.sin(q2_ref[...] * (_bdot(hid, w2_ref[...]) + b2_ref[...]))
    acc = jnp.zeros(sum_ref.shape, F32)
    for s in range(HY_TS):
        hs = hid[s * n1:(s + 1) * n1]
        f = jnp.concatenate([_bdot(hs[:half], w3_ref[0]), _bdot(hs[half:], w3_ref[1])], axis=0)
        f = f * jnp.exp(t_ref[s] * nd_ref[...]) * keep_ref[s]
        acc = acc + jnp.sum(jnp.abs(f), axis=0, keepdims=True)
        y = _dot3(f_ref[s], f)
        o_ref[0, :, s, :] = y[:k1p]
        o_ref[1, :, s, :] = y[k1p:]
    sum_ref[...] += acc


def _hy_filt_first(l, tab, f_w1, f_b1, f_freq1, f_w2, f_b2, f_freq2, f_w3):
    c2 = HY_ORDER * HY_WIDTH
    n2 = HY_N2
    n1 = 2 * l // n2
    k1p = tab['k1p']
    t = np.linspace(0.0, 1.0, l, dtype=np.float32)[:, None]
    bands = np.linspace(1e-4, HY_BANDS - 1, HY_BANDS, dtype=np.float32)
    ang = np.float32(2.0 * math.pi / l) * np.arange(l, dtype=np.float32)[:, None] * bands
    feats = np.concatenate([t, np.cos(ang), -np.sin(ang)], axis=-1)
    rows = (n2 * np.arange(n1)[None, :] + np.arange(n2)[:, None]).reshape(-1)
    pos = np.where(rows < l, rows, np.where(rows == l, 0, 2 * l - rows))
    feats_t = jnp.asarray(np.pad(feats[pos], ((0, 0), (0, 128 - HY_EMB))).reshape(n2, n1, 128))
    t_t = jnp.asarray(t[pos].reshape(n2, n1, 1))
    keep_t = jnp.asarray((rows != l).astype(np.float32).reshape(n2, n1, 1))
    w1p = jnp.pad(f_w1, ((0, 128 - HY_EMB), (0, 0)))
    w3d = jnp.transpose(f_w3.reshape(HY_HIDDEN, HY_ORDER, 2, HY_WIDTH), (2, 0, 1, 3)).reshape(2, HY_HIDDEN, c2)
    deltas = jnp.abs(jnp.linspace(HY_MIN_DECAY, HY_MAX_DECAY, HY_WIDTH, dtype=F32))
    negd = jnp.tile(-deltas, HY_ORDER)[None, :]
    cst = lambda i: (0, 0)
    vec = lambda a: a.reshape(1, HY_HIDDEN)
    tile3 = lambda w: pl.BlockSpec((HY_TS, n1, w), lambda i: (i, 0, 0))
    return pl.pallas_call(
        functools.partial(_hy_filt_first_kernel, k1p),
        grid=(n2 // HY_TS,),
        in_specs=[tile3(128), tile3(1), tile3(1),
                  pl.BlockSpec((128, HY_HIDDEN), cst), pl.BlockSpec((1, HY_HIDDEN), cst),
                  pl.BlockSpec((1, HY_HIDDEN), cst),
                  pl.BlockSpec((HY_HIDDEN, HY_HIDDEN), cst), pl.BlockSpec((1, HY_HIDDEN), cst),
                  pl.BlockSpec((1, HY_HIDDEN), cst),
                  pl.BlockSpec((2, HY_HIDDEN, c2), lambda i: (0, 0, 0)),
                  pl.BlockSpec((1, c2), cst),
                  pl.BlockSpec((HY_TS, 2 * k1p, n1), lambda i: (i, 0, 0))],
        out_specs=[pl.BlockSpec((None, 2, k1p, HY_TS, c2), lambda i: (0, 0, 0, i, 0)),
                   pl.BlockSpec((1, c2), cst)],
        out_shape=[jax.ShapeDtypeStruct((1, 2, k1p, n2, c2), F32), jax.ShapeDtypeStruct((1, c2), F32)],
        compiler_params=_cparams("arbitrary"),
        name="hy_filt_first",
    )(feats_t, t_t, keep_t, w1p, vec(f_b1), vec(f_freq1), f_w2, vec(f_b2), vec(f_freq2), w3d, negd,
      tab['stage1_tw'])


def _hy_filt_mid_kernel(n2, a_ref, sum_ref, fwd_ref, o_ref):
    c2 = a_ref.shape[-1]
    x = _dot3(fwd_ref[...], a_ref[0, :, 0].reshape(2 * n2, c2)) * (1.0 / (sum_ref[...] + 1e-6))
    o_ref[0, 0] = x[:n2].astype(BF16)
    o_ref[1, 0] = x[n2:].astype(BF16)


def _hy_filt_mid(a5, tab, colsum):
    _, _, k1p, n2, c2 = a5.shape
    return pl.pallas_call(
        functools.partial(_hy_filt_mid_kernel, n2),
        grid=(k1p,),
        in_specs=[pl.BlockSpec((1, 2, 1, n2, c2), lambda k: (0, 0, k, 0, 0)),
                  pl.BlockSpec((1, c2), lambda k: (0, 0)),
                  pl.BlockSpec((2 * n2, 2 * n2), lambda k: (0, 0))],
        out_specs=pl.BlockSpec((2, 1, n2, c2), lambda k: (0, k, 0, 0)),
        out_shape=jax.ShapeDtypeStruct((2, k1p, n2, c2), BF16),
        compiler_params=_cparams("parallel"),
        name="hy_filt_mid",
    )(a5, colsum, tab['fwd2'])


def _hy_ctx_kernel(kp, fw_ref, inv_ref, h_ref, v_ref, gate_ref, d_ref, o_ref):
    v = v_ref[0].astype(F32)
    u = _bdot(fw_ref[...], v)
    ur, ui = u[:kp], u[kp:]
    hr, hi = h_ref[0], h_ref[1]
    q = jnp.concatenate([ur * hr - ui * hi, ur * hi + ui * hr], axis=0)
    o_ref[0] = (gate_ref[0].astype(F32) * (_bdot(inv_ref[...], q) + d_ref[...] * v)).astype(o_ref.dtype)


def _hy_long_conv1(v, vpart, gate, gpart, hspec, order, d_vec, tab):
    b, l, _ = v.shape
    c = HY_WIDTH
    kp = tab['kp']
    row = lambda bb: (bb, 0, 0)
    return pl.pallas_call(
        functools.partial(_hy_ctx_kernel, kp),
        grid=(b,),
        in_specs=[pl.BlockSpec((2 * kp, l), lambda bb: (0, 0)),
                  pl.BlockSpec((l, 2 * kp), lambda bb: (0, 0)),
                  pl.BlockSpec((2, kp, c), lambda bb: (0, 0, order)),
                  pl.BlockSpec((1, l, c), lambda bb: (bb, 0, vpart)),
                  pl.BlockSpec((1, l, c), lambda bb: (bb, 0, gpart)),
                  pl.BlockSpec((1, c), lambda bb: (0, 0))],
        out_specs=pl.BlockSpec((1, l, c), row),
        out_shape=jax.ShapeDtypeStruct((b, l, c), BF16),
        compiler_params=_cparams("parallel"),
        name="hy_ctx_conv",
    )(tab['fwd'][:, :l].astype(BF16), tab['inv'].astype(BF16), hspec, v, gate, d_vec.reshape(1, c))


def _hy_filter_kernel(feat_ref, t_ref, keep_ref, w1_ref, b1_ref, q1_ref, w2_ref, b2_ref, q2_ref,
                      w3_ref, nd_ref, f_ref, sum_ref):
    @pl.when(pl.program_id(0) == 0)
    def _():
        sum_ref[...] = jnp.zeros_like(sum_ref)

    hid = jnp.sin(q1_ref[...] * (_bdot(feat_ref[...], w1_ref[...]) + b1_ref[...]))
    hid = jnp.sin(q2_ref[...] * (_bdot(hid, w2_ref[...]) + b2_ref[...]))
    f = _bdot(hid, w3_ref[0]) * jnp.exp(t_ref[...] * nd_ref[...]) * keep_ref[...]
    f_ref[...] = f
    sum_ref[...] += jnp.sum(jnp.abs(f), axis=0, keepdims=True)


def _hy_filters(l, f_w1, f_b1, f_freq1, f_w2, f_b2, f_freq2, f_w3):
    c2 = HY_ORDER * HY_WIDTH
    t = np.linspace(0.0, 1.0, l, dtype=np.float32)[:, None]
    bands = np.linspace(1e-4, HY_BANDS - 1, HY_BANDS, dtype=np.float32)
    ang = np.float32(2.0 * math.pi / l) * np.arange(l, dtype=np.float32)[:, None] * bands
    feats = np.concatenate([t, np.cos(ang), -np.sin(ang)], axis=-1)
    rows = np.arange(2 * l)
    pos = np.where(rows < l, rows, np.where(rows == l, 0, 2 * l - rows))
    feats2 = jnp.asarray(np.pad(feats[pos], ((0, 0), (0, 128 - HY_EMB))))
    keep = jnp.asarray((rows != l).astype(np.float32))[:, None]
    t2 = jnp.asarray(t[pos])
    w1p = jnp.pad(f_w1, ((0, 128 - HY_EMB), (0, 0)))
    w3d = jnp.transpose(f_w3.reshape(HY_HIDDEN, HY_ORDER, 2, HY_WIDTH), (2, 0, 1, 3)).reshape(2, HY_HIDDEN, c2)
    deltas = jnp.abs(jnp.linspace(HY_MIN_DECAY, HY_MAX_DECAY, HY_WIDTH, dtype=F32))
    negd = jnp.tile(-deltas, HY_ORDER)[None, :]
    tr = min(l, 512)
    per_dir = l // tr
    cst = lambda i: (0, 0)
    vec = lambda a: a.reshape(1, HY_HIDDEN)
    return pl.pallas_call(
        _hy_filter_kernel,
        grid=(2 * l // tr,),
        in_specs=[pl.BlockSpec((tr, 128), lambda i: (i, 0)),
                  pl.BlockSpec((tr, 1), lambda i: (i, 0)),
                  pl.BlockSpec((tr, 1), lambda i: (i, 0)),
                  pl.BlockSpec((128, HY_HIDDEN), cst), pl.BlockSpec((1, HY_HIDDEN), cst),
                  pl.BlockSpec((1, HY_HIDDEN), cst),
                  pl.BlockSpec((HY_HIDDEN, HY_HIDDEN), cst), pl.BlockSpec((1, HY_HIDDEN), cst),
                  pl.BlockSpec((1, HY_HIDDEN), cst),
                  pl.BlockSpec((1, HY_HIDDEN, c2), lambda i: (i // per_dir, 0, 0)),
                  pl.BlockSpec((1, c2), cst)],
        out_specs=[pl.BlockSpec((tr, c2), lambda i: (i, 0)), pl.BlockSpec((1, c2), cst)],
        out_shape=[jax.ShapeDtypeStruct((2 * l, c2), F32), jax.ShapeDtypeStruct((1, c2), F32)],
        compiler_params=_cparams("arbitrary"),
        name="hy_filter",
    )(feats2, t2, keep, w1p, vec(f_b1), vec(f_freq1), f_w2, vec(f_b2), vec(f_freq2), w3d, negd)


def _col_scale_kernel(x_ref, sum_ref, o_ref):
    o_ref[...] = x_ref[...] * (1.0 / (sum_ref[...] + 1e-6))


def _hy_spectrum(l, fparams):
    if 2 * l >= 16 * HY_N2:
        tab = _dft2_tables(2 * l // HY_N2, HY_N2)
        a, colsum = _hy_filt_first(l, tab, *fparams)
        return _hy_filt_mid(a, tab, colsum), tab
    filt, colsum = _hy_filters(l, *fparams)
    c2 = filt.shape[1]
    tab = _dft1_tables(l)
    spec = _rowdft(tab['fwd'], filt[None], True)[0]
    spec = pl.pallas_call(
        _col_scale_kernel,
        in_specs=[pl.BlockSpec(spec.shape, lambda: (0, 0)), pl.BlockSpec((1, c2), lambda: (0, 0))],
        out_specs=pl.BlockSpec(spec.shape, lambda: (0, 0)),
        out_shape=jax.ShapeDtypeStruct(spec.shape, F32),
        name="hy_col_scale",
    )(spec, colsum)
    return spec.reshape(2, tab['kp'], c2), tab


def _hyena_mixer(zz, hspec, tab, d_skip):
    conv = _hy_long_conv2 if 'k1p' in tab else _hy_long_conv1
    y = conv(zz, 0, zz, 1, hspec, 0, d_skip[0], tab)
    return conv(y, 0, zz, 2, hspec, 1, d_skip[1], tab)


def _grid_sincos(n, dm):
    rows = n // GRID_W
    quarter = dm // 4
    omega = 1.0 / (10000.0 ** (jnp.arange(quarter, dtype=F32) / quarter))
    ang_r = jnp.arange(rows, dtype=F32)[:, None] * omega
    ang_c = jnp.arange(GRID_W, dtype=F32)[:, None] * omega
    emb_r = jnp.concatenate([jnp.sin(ang_r), jnp.cos(ang_r)], axis=-1)
    emb_c = jnp.concatenate([jnp.sin(ang_c), jnp.cos(ang_c)], axis=-1)
    half = emb_r.shape[-1]
    pe = jnp.concatenate([jnp.broadcast_to(emb_r[:, None, :], (rows, GRID_W, half)),
                          jnp.broadcast_to(emb_c[None, :, :], (rows, GRID_W, half))], axis=-1)
    return pe.reshape(rows * GRID_W, 2 * half)


EV_Z_OFF = M2_XBC
EV_U_OFF = EV_Z_OFF + M2_INNER
EV_DT_OFF = EV_U_OFF + S5_WIDTH
EV_PROJ_W = EV_DT_OFF + 128


def _even_mixer(h_in, p, states):
    x, g, shift, scale = h_in
    b, l, _ = x.shape
    proj = _adaln_mm_conv(x, g, shift, scale, p['w_in'], jnp.zeros((1, EV_PROJ_W), F32),
                          p['m2_conv_w'], p['m2_conv_b'], True)
    s5_h0, m2_h0 = states

    ucol = EV_U_OFF // S5_WIDTH
    yf, yb, s5_fin = _s5_scan(proj, ucol, p['s5_bd'], p['s5_cd'], p['s5_ar'], p['s5_ai'], s5_h0)
    y_s5 = _s5_glu(yf, yb, proj, ucol, p['s5_d'], p['s5_glu_w'], p['s5_glu_b'])

    dt4 = proj[..., EV_DT_OFF:EV_DT_OFF + 2 * M2_HEADS].astype(F32).reshape(b, l, 2, M2_HEADS)
    dt_col = jnp.transpose(dt4, (2, 0, 1, 3))
    dt_row = jnp.transpose(dt4, (2, 0, 3, 1))
    ydir, m2_fin = _ssd(proj, dt_col, dt_row, p['m2_dt_bias'], p['m2_a'], m2_h0)
    y_m2 = _m2_gate(ydir, proj, proj, EV_Z_OFF // M2_INNER, p['m2_d'], p['m2_norm_g'])
    return (y_s5, y_m2), (s5_fin, m2_fin)


def kernel(x, c, ctx, c_ctx, mod_w, mod_b, norm_mix_g, norm_mlp_g, mlp_w1, mlp_w2, final_norm_g, ev_in_w, ev_out_w, s5_lam_re, s5_lam_im, s5_log_dt, s5_b_re, s5_b_im, s5_c_re, s5_c_im, s5_d, s5_glu_w, s5_glu_b, m2_conv_w, m2_conv_b, m2_dt_bias, m2_a_log, m2_d, m2_norm_g, hy_in_w, hy_in_b, hy_conv_w, hy_conv_b, hy_f_w1, hy_f_b1, hy_f_freq1, hy_f_w2, hy_f_b2, hy_f_freq2, hy_f_w3, hy_d, hy_out_w, hy_out_b):
    bsz, n, dm = x.shape
    lc = ctx.shape[1]
    x = _add_pe(x, _grid_sincos(n, dm))

    c8 = jnp.concatenate([c, c_ctx[None], jnp.zeros((3, dm), F32)], axis=0)
    mods = _modulation(c8, mod_w, mod_b).reshape(DEPTH, 8, N_MOD, dm)
    final_g = final_norm_g.reshape(1, dm)

    for i in range(DEPTH):
        j = i // 2
        mx = [mods[i, :bsz, k][:, None, :] for k in range(N_MOD)]
        mc = [jnp.broadcast_to(mods[i, bsz, k][None, None, :], (bsz, 1, dm)) for k in range(N_MOD)]
        g_mix = norm_mix_g[i].reshape(1, dm)
        g_mlp = norm_mlp_g[i].reshape(1, dm)
        ctx_later = any(k % 2 == 0 for k in range(i + 1, DEPTH))
        w1 = mlp_w1[i].astype(BF16)
        w2 = mlp_w2[i].astype(BF16)

        if i % 2 == 0:
            in_w = ev_in_w[j]
            o0, o1, o2 = S5_WIDTH, S5_WIDTH + M2_INNER, S5_WIDTH + M2_INNER + M2_XBC
            bd, cd, ar8, ai8 = _s5_prepare(s5_lam_re[j], s5_lam_im[j], s5_log_dt[j], s5_b_re[j], s5_b_im[j],
                                           s5_c_re[j], s5_c_im[j])
            p = dict(
                w_in=jnp.concatenate([in_w[:, o1:o2], in_w[:, o0:o1], in_w[:, :o0],
                                      jnp.pad(in_w[:, o2:], ((0, 0), (0, 128 - 2 * M2_HEADS)))],
                                     axis=1).astype(BF16),
                s5_bd=bd, s5_cd=cd, s5_ar=ar8, s5_ai=ai8,
                s5_d=s5_d[j].reshape(1, S5_WIDTH), s5_glu_w=s5_glu_w[j].astype(BF16),
                s5_glu_b=s5_glu_b[j].reshape(1, S5_WIDTH),
                m2_conv_w=m2_conv_w[j], m2_conv_b=m2_conv_b[j].reshape(1, M2_XBC),
                m2_dt_bias=m2_dt_bias[j], m2_a=-jnp.exp(m2_a_log[j]),
                m2_d=jnp.repeat(m2_d[j], M2_HEAD_DIM).reshape(1, M2_INNER),
                m2_norm_g=m2_norm_g[j].reshape(1, M2_INNER))
            zero_states = (jnp.zeros((8, S5_LANES), F32),
                           jnp.zeros((2, bsz, M2_GROUPS, M2_STATE, M2_GROUP_W), F32))
            ys_c, ctx_states = _even_mixer((ctx, g_mix, mc[0], mc[1]), p, zero_states)
            ys_x, _ = _even_mixer((x, g_mix, mx[0], mx[1]), p, ctx_states)
            out_ws = [ev_out_w[j][:S5_WIDTH].astype(BF16), ev_out_w[j][S5_WIDTH:].astype(BF16)]
            out_b = jnp.zeros((1, dm), F32)
        else:
            fparams = (hy_f_w1[j], hy_f_b1[j], hy_f_freq1[j], hy_f_w2[j], hy_f_b2[j], hy_f_freq2[j], hy_f_w3[j])
            in_w = hy_in_w[j].astype(BF16)
            in_b = hy_in_b[j].reshape(1, -1)
            conv_b = hy_conv_b[j].reshape(1, -1)
            hspec, tab = _hy_spectrum(n, fparams)
            zz = _adaln_mm_conv(x, g_mix, mx[0], mx[1], in_w, in_b, hy_conv_w[j], conv_b, False)
            ys_x = [_hyena_mixer(zz, hspec, tab, hy_d[j])]
            if ctx_later:
                hspec_c, tab_c = _hy_spectrum(lc, fparams)
                zz_c = _adaln_mm_conv(ctx, g_mix, mc[0], mc[1], in_w, in_b, hy_conv_w[j], conv_b, False)
                ys_c = [_hyena_mixer(zz_c, hspec_c, tab_c, hy_d[j])]
            out_ws = [hy_out_w[j].astype(BF16)]
            out_b = hy_out_b[j].reshape(1, dm)

        x = _mix_mlp(x, mx[2], out_b, ys_x, out_ws, g_mlp, mx[3], mx[4], mx[5], w1, w2, final_g, i == DEPTH - 1)
        if ctx_later:
            ctx = _mix_mlp(ctx, mc[2], out_b, ys_c, out_ws, g_mlp, mc[3], mc[4], mc[5], w1, w2, final_g, False)
    return x
```

```python
import functools
import math

import numpy as np
import jax
import jax.numpy as jnp
from jax import lax
from jax.experimental import pallas as pl
from jax.experimental.pallas import tpu as pltpu

F32 = jnp.float32
BF16 = jnp.bfloat16

D_MODEL = 1024
DEPTH = 4
GRID_W = 64
N_MOD = 6
MLP_HIDDEN = 4 * D_MODEL
NORM_EPS = 1e-6

S5_WIDTH = 512
S5_GROUP_CH = 16
S5_GROUPS = 32
S5_STATE = 64
S5_SUPER = 4
S5_SLAB = 2 * 8 * S5_STATE
S5_LANES = S5_SUPER * S5_SLAB

M2_INNER = 1024
M2_HEAD_DIM = 64
M2_HEADS = 16
M2_GROUPS = 4
M2_STATE = 128
M2_BC = M2_GROUPS * M2_STATE
M2_XBC = M2_INNER + 2 * M2_BC
M2_GROUP_W = (M2_HEADS // M2_GROUPS) * M2_HEAD_DIM

HY_WIDTH = 1024
HY_ORDER = 2
HY_BANDS = 16
HY_EMB = 2 * HY_BANDS + 1
HY_HIDDEN = 64
HY_MIN_DECAY = math.log(1e-2) / 1.5
HY_MAX_DECAY = math.log(1e-2) / 0.3

V7X_VMEM_LIMIT_BYTES = 48 * 1024 * 1024


def _cparams(*sem):
    return pltpu.CompilerParams(dimension_semantics=sem, vmem_limit_bytes=V7X_VMEM_LIMIT_BYTES)


def _bdot(a, b):
    return jnp.dot(a.astype(BF16), b.astype(BF16), preferred_element_type=F32)


def _split3(a):
    a1 = a.astype(BF16)
    r1 = a - a1.astype(F32)
    a2 = r1.astype(BF16)
    a3 = (r1 - a2.astype(F32)).astype(BF16)
    return a1, a2, a3


def _dot_exact_rhs(a, b_bf16):
    a1, a2, a3 = _split3(a)
    d = functools.partial(jnp.dot, preferred_element_type=F32)
    return d(a1, b_bf16) + d(a2, b_bf16) + d(a3, b_bf16)


def _dot_split2_rhs(a, b_bf16):
    a1 = a.astype(BF16)
    a2 = (a - a1.astype(F32)).astype(BF16)
    d = functools.partial(jnp.dot, preferred_element_type=F32)
    return d(a1, b_bf16) + d(a2, b_bf16)


def _dot_exact_lhs(a_bf16, b):
    b1, b2, b3 = _split3(b)
    d = functools.partial(jnp.dot, preferred_element_type=F32)
    return d(a_bf16, b1) + d(a_bf16, b2) + d(a_bf16, b3)


def _dot3(a, b):
    a1 = a.astype(BF16)
    a2 = (a - a1.astype(F32)).astype(BF16)
    b1 = b.astype(BF16)
    b2 = (b - b1.astype(F32)).astype(BF16)
    d = functools.partial(jnp.dot, preferred_element_type=F32)
    return d(a1, b1) + d(a1, b2) + d(a2, b1)


def _silu(x):
    return x * (1.0 / (1.0 + jnp.exp(-x)))


def _sigmoid(x):
    return 1.0 / (1.0 + jnp.exp(-x))


def _adaln(x, g, shift, scale):
    ms = jnp.mean(x * x, axis=-1, keepdims=True)
    return (x * lax.rsqrt(ms + NORM_EPS) * g) * (1.0 + scale) + shift


def _mod_kernel(c_ref, w_ref, b_ref, o_ref):
    o_ref[0] = _bdot(_silu(c_ref[...]), w_ref[0]) + b_ref[0]


def _modulation(c8, mod_w, mod_b):
    n = mod_w.shape[-1]
    tn = 1536
    return pl.pallas_call(
        _mod_kernel,
        grid=(DEPTH, n // tn),
        in_specs=[pl.BlockSpec((8, D_MODEL), lambda i, j: (0, 0)),
                  pl.BlockSpec((1, D_MODEL, tn), lambda i, j: (i, 0, j)),
                  pl.BlockSpec((1, 1, tn), lambda i, j: (i, 0, j))],
        out_specs=pl.BlockSpec((1, 8, tn), lambda i, j: (i, 0, j)),
        out_shape=jax.ShapeDtypeStruct((DEPTH, 8, n), F32),
        compiler_params=_cparams("parallel", "parallel"),
        name="modulation",
    )(c8, mod_w, mod_b.reshape(DEPTH, 1, n))


def _add_pe_kernel(x_ref, pe_ref, o_ref):
    o_ref[0] = x_ref[0] + pe_ref[...]


def _add_pe(x, pe):
    b, l, d = x.shape
    tl = min(l, 1024)
    return pl.pallas_call(
        _add_pe_kernel,
        grid=(l // tl, b),
        in_specs=[pl.BlockSpec((1, tl, d), lambda i, bb: (bb, i, 0)),
                  pl.BlockSpec((tl, d), lambda i, bb: (i, 0))],
        out_specs=pl.BlockSpec((1, tl, d), lambda i, bb: (bb, i, 0)),
        out_shape=jax.ShapeDtypeStruct(x.shape, F32),
        compiler_params=_cparams("parallel", "parallel"),
        name="add_pe",
    )(x, pe)


CONV_COLS = 512


def _adaln_mm_conv_kernel(ncv, act, x_ref, xp_ref, xn_ref, g_ref, sh_ref, sc_ref, w_ref, b_ref, cw_ref, cb_ref,
                          o_ref):
    i = pl.program_id(1)
    tl = x_ref.shape[1]
    rows = tl + 16
    xe = jnp.concatenate([xp_ref[0], x_ref[0], xn_ref[0]], axis=0)
    h = _adaln(xe, g_ref[...], sh_ref[0], sc_ref[0]).astype(BF16)
    n = w_ref.shape[1]
    r = lax.broadcasted_iota(jnp.int32, (tl, 1), 0)
    first = (r == 0) & (i == 0)
    last = (r == tl - 1) & (i == pl.num_programs(1) - 1)
    for c0 in range(0, n, CONV_COLS):
        cols = slice(c0, min(c0 + CONV_COLS, n))
        p = jnp.dot(h, w_ref[:, cols], preferred_element_type=F32) + b_ref[:, cols]
        mid = p[8:8 + tl]
        if c0 < ncv:
            prev = jnp.where(first, 0.0, pltpu.roll(p, 1, axis=0)[8:8 + tl])
            nxt = jnp.where(last, 0.0, pltpu.roll(p, rows - 1, axis=0)[8:8 + tl])
            cw = cw_ref[:, cols]
            mid = prev * cw[0:1] + mid * cw[1:2] + nxt * cw[2:3] + cb_ref[:, cols]
            if act:
                mid = _silu(mid)
        o_ref[0, :, cols] = mid.astype(o_ref.dtype)


def _adaln_mm_conv(x, g, shift, scale, w_bf16, bias, conv_w, conv_b, act):
    b, l, d = x.shape
    n = w_bf16.shape[1]
    ncv = conv_w.shape[1]
    assert ncv % CONV_COLS == 0
    tl = min(l, 512)
    t8 = tl // 8
    nblk8 = l // 8
    cst = lambda bb, i: (0, 0)
    vec = pl.BlockSpec((1, 1, d), lambda bb, i: (bb, 0, 0))
    return pl.pallas_call(
        functools.partial(_adaln_mm_conv_kernel, ncv, act),
        grid=(b, l // tl),
        in_specs=[pl.BlockSpec((1, tl, d), lambda bb, i: (bb, i, 0)),
                  pl.BlockSpec((1, 8, d), lambda bb, i: (bb, jnp.maximum(i * t8 - 1, 0), 0)),
                  pl.BlockSpec((1, 8, d), lambda bb, i: (bb, jnp.minimum((i + 1) * t8, nblk8 - 1), 0)),
                  pl.BlockSpec((1, d), cst), vec, vec,
                  pl.BlockSpec((d, n), cst), pl.BlockSpec((1, n), cst),
                  pl.BlockSpec((3, ncv), cst), pl.BlockSpec((1, ncv), cst)],
        out_specs=pl.BlockSpec((1, tl, n), lambda bb, i: (bb, i, 0)),
        out_shape=jax.ShapeDtypeStruct((b, l, n), BF16),
        compiler_params=_cparams("parallel", "parallel"),
        name="adaln_mm_conv",
    )(x, x, x, g, shift, scale, w_bf16, bias, conv_w, conv_b)


MLP_CHUNK = 512


def _mix_mlp_kernel(n_in, final_norm, *refs):
    x_ref, ga_ref, bo_ref = refs[0:3]
    y_refs = refs[3:3 + n_in]
    wo_refs = refs[3 + n_in:3 + 2 * n_in]
    g_ref, sh_ref, sc_ref, gf_ref, w1_ref, w2_ref, fg_ref, o_ref, a_scr = refs[3 + 2 * n_in:]
    mix = bo_ref[...] + _bdot(y_refs[0][0], wo_refs[0][...])
    for y_ref, wo_ref in zip(y_refs[1:], wo_refs[1:]):
        mix = mix + _bdot(y_ref[0], wo_ref[...])
    x1 = x_ref[0] + ga_ref[0] * mix
    h = _adaln(x1, g_ref[...], sh_ref[0], sc_ref[0]).astype(BF16)
    for c in range(w1_ref.shape[1] // MLP_CHUNK):
        cols = slice(c * MLP_CHUNK, (c + 1) * MLP_CHUNK)
        a = jnp.dot(h, w1_ref[:, cols], preferred_element_type=F32)
        a_scr[:, cols] = jnp.square(jnp.maximum(a, 0.0)).astype(BF16)
    y = x1 + gf_ref[0] * jnp.dot(a_scr[...], w2_ref[...], preferred_element_type=F32)
    if final_norm:
        ms = jnp.mean(y * y, axis=-1, keepdims=True)
        y = y * lax.rsqrt(ms + NORM_EPS) * fg_ref[...]
    o_ref[0] = y


def _mix_mlp(x, gate_a, out_b, ys, out_ws, g, shift, scale, gate_f, w1_bf16, w2_bf16, final_g, final_norm):
    b, l, d = x.shape
    hdim = w1_bf16.shape[1]
    tl = min(l, 512)
    n_in = len(ys)
    row = lambda bb, i: (bb, i, 0)
    vec = pl.BlockSpec((1, 1, d), lambda bb, i: (bb, 0, 0))
    cvec = pl.BlockSpec((1, d), lambda bb, i: (0, 0))
    resident = lambda a: pl.BlockSpec(a.shape, lambda bb, i: (0, 0), pipeline_mode=pl.Buffered(1))
    in_specs = [pl.BlockSpec((1, tl, d), row), vec, cvec]
    in_specs += [pl.BlockSpec((1, tl, y.shape[-1]), row) for y in ys]
    in_specs += [resident(w) for w in out_ws]
    in_specs += [cvec, vec, vec, vec, resident(w1_bf16), resident(w2_bf16), cvec]
    return pl.pallas_call(
        functools.partial(_mix_mlp_kernel, n_in, final_norm),
        grid=(b, l // tl),
        in_specs=in_specs,
        out_specs=pl.BlockSpec((1, tl, d), row),
        out_shape=jax.ShapeDtypeStruct(x.shape, F32),
        scratch_shapes=[pltpu.VMEM((tl, hdim), BF16)],
        compiler_params=_cparams("parallel", "parallel"),
        name="mix_mlp",
    )(x, gate_a, out_b, *ys, *out_ws, g, shift, scale, gate_f, w1_bf16, w2_bf16, final_g)


def _dwconv_kernel(act, x_ref, w_ref, b_ref, o_ref):
    x = x_ref[0].astype(F32)
    l = x.shape[0]
    t = lax.broadcasted_iota(jnp.int32, x.shape, 0)
    prev = jnp.where(t == 0, 0.0, pltpu.roll(x, 1, axis=0))
    nxt = jnp.where(t == l - 1, 0.0, pltpu.roll(x, l - 1, axis=0))
    w = w_ref[...]
    y = prev * w[0:1] + x * w[1:2] + nxt * w[2:3] + b_ref[...]
    if act:
        y = _silu(y)
    o_ref[0] = y.astype(o_ref.dtype)


def _dwconv(x, w, bias, act, col0=0, ncols=None):
    b, l, c = x.shape
    ncols = c if ncols is None else ncols
    tc = 256 if l > 1024 else 512
    j0 = col0 // tc
    return pl.pallas_call(
        functools.partial(_dwconv_kernel, act),
        grid=(b, ncols // tc),
        in_specs=[pl.BlockSpec((1, l, tc), lambda bb, j: (bb, 0, j0 + j)),
                  pl.BlockSpec((3, tc), lambda bb, j: (0, j0 + j)),
                  pl.BlockSpec((1, tc), lambda bb, j: (0, j0 + j))],
        out_specs=pl.BlockSpec((1, l, tc), lambda bb, j: (bb, 0, j)),
        out_shape=jax.ShapeDtypeStruct((b, l, ncols), BF16),
        compiler_params=_cparams("parallel", "parallel"),
        name="dwconv",
    )(x, w, bias)


def _s5_prepare(lam_re, lam_im, log_dt, b_re, b_im, c_re, c_im):
    step = jnp.exp(log_dt)[..., None]
    mag = jnp.exp(lam_re * step)
    ar = mag * jnp.cos(lam_im * step)
    ai = mag * jnp.sin(lam_im * step)
    den = lam_re * lam_re + lam_im * lam_im
    fr = ((ar - 1.0) * lam_re + ai * lam_im) / den
    fi = (ai * lam_re - (ar - 1.0) * lam_im) / den
    eye = jnp.eye(8, dtype=F32)

    def blockdiag_in(b):
        b4 = b.reshape(S5_SUPER, 8, S5_STATE, S5_GROUP_CH)
        return jnp.einsum('sgpk,gh->sgkhp', b4, eye).reshape(S5_SUPER, 128, 512)

    bd = jnp.concatenate([blockdiag_in(b_re), blockdiag_in(b_im)], axis=-1).astype(BF16)

    cpr = c_re[None] * fr[:, :, None, :] - c_im[None] * fi[:, :, None, :]
    cpi = c_re[None] * fi[:, :, None, :] + c_im[None] * fr[:, :, None, :]

    def blockdiag_out(c):
        c5 = c.reshape(2, S5_SUPER, 8, S5_GROUP_CH, S5_STATE)
        return jnp.einsum('dsgkp,gh->dsgphk', c5, eye).reshape(2, S5_SUPER, 512, 128)

    cd = jnp.concatenate([blockdiag_out(cpr), blockdiag_out(-cpi)], axis=2)
    cd = jnp.concatenate([cd[0], cd[1]], axis=-1).astype(BF16)
    rows = lambda a: jnp.repeat(a.reshape(2, S5_SUPER * 512), 4, axis=0)
    return bd, cd, rows(ar), rows(ai)


S5_ROWS = 256


def _s5_scan_kernel(tc, uf_ref, ub_ref, bd_ref, cd_ref, ar_ref, ai_ref, h0_ref, flip_ref,
                    yf_ref, yb_ref, hfin_ref, u_scr, g_scr, y_scr, h_scr):
    c = pl.program_id(0)

    @pl.when(c == 0)
    def _():
        h_scr[...] = h0_ref[...]

    flip = flip_ref[...]
    rows = 8 * tc
    for b in range(4):
        uf = uf_ref[b].astype(F32)
        ub = jnp.dot(flip, ub_ref[b].astype(BF16), preferred_element_type=F32)
        for sg in range(S5_SUPER):
            u_scr[sg, pl.ds(b, tc, stride=8), :] = uf[:, sg * 128:(sg + 1) * 128]
            u_scr[sg, pl.ds(4 + b, tc, stride=8), :] = ub[:, sg * 128:(sg + 1) * 128]

    half = S5_SLAB // 256
    for sg in range(S5_SUPER):
        for r0 in range(0, rows, S5_ROWS):
            g_scr[r0:r0 + S5_ROWS, sg * S5_SLAB:(sg + 1) * S5_SLAB] = jnp.dot(
                u_scr[sg, r0:r0 + S5_ROWS, :].astype(BF16), bd_ref[sg], preferred_element_type=F32)

        ar = [ar_ref[:, (sg * half + k) * 128:(sg * half + k + 1) * 128] for k in range(half)]
        ai = [ai_ref[:, (sg * half + k) * 128:(sg * half + k + 1) * 128] for k in range(half)]
        l_re = [sg * S5_SLAB + k * 128 for k in range(half)]
        l_im = [sg * S5_SLAB + (half + k) * 128 for k in range(half)]
        hr = [h_scr[:, l_re[k]:l_re[k] + 128] for k in range(half)]
        hi = [h_scr[:, l_im[k]:l_im[k] + 128] for k in range(half)]
        for j in range(tc):
            r0 = j * 8
            for k in range(half):
                nr = ar[k] * hr[k] - ai[k] * hi[k] + g_scr[r0:r0 + 8, l_re[k]:l_re[k] + 128]
                ni = ar[k] * hi[k] + ai[k] * hr[k] + g_scr[r0:r0 + 8, l_im[k]:l_im[k] + 128]
                g_scr[r0:r0 + 8, l_re[k]:l_re[k] + 128] = nr
                g_scr[r0:r0 + 8, l_im[k]:l_im[k] + 128] = ni
                hr[k], hi[k] = nr, ni
        for k in range(half):
            h_scr[:, l_re[k]:l_re[k] + 128] = hr[k]
            h_scr[:, l_im[k]:l_im[k] + 128] = hi[k]

        for r0 in range(0, rows, S5_ROWS):
            y = jnp.dot(g_scr[r0:r0 + S5_ROWS, sg * S5_SLAB:(sg + 1) * S5_SLAB].astype(BF16), cd_ref[sg],
                        preferred_element_type=F32)
            y_scr[2 * sg, r0:r0 + S5_ROWS, :] = y[:, :128]
            y_scr[2 * sg + 1, r0:r0 + S5_ROWS, :] = y[:, 128:]

    for b in range(4):
        yf_ref[b] = jnp.concatenate(
            [y_scr[2 * sg, pl.ds(b, tc, stride=8), :] for sg in range(S5_SUPER)], axis=1)
        yb = jnp.concatenate(
            [y_scr[2 * sg + 1, pl.ds(4 + b, tc, stride=8), :] for sg in range(S5_SUPER)], axis=1)
        yb_ref[b] = _dot_exact_lhs(flip, yb)

    hfin_ref[...] = h_scr[...]


def _s5_scan(u, ucol, bd, cd, ar8, ai8, h0):
    b, l, _ = u.shape
    w = S5_WIDTH
    assert b == 4
    tc = 128
    nc = l // tc
    flip = jnp.asarray(np.eye(tc, dtype=np.float32)[::-1], dtype=BF16)
    full = lambda shape: pl.BlockSpec(shape, lambda c: (0,) * len(shape))
    y_shape = jax.ShapeDtypeStruct((b, l, w), F32)
    return pl.pallas_call(
        functools.partial(_s5_scan_kernel, tc),
        grid=(nc,),
        in_specs=[pl.BlockSpec((4, tc, w), lambda c: (0, c, ucol)),
                  pl.BlockSpec((4, tc, w), lambda c: (0, nc - 1 - c, ucol)),
                  full(bd.shape), full(cd.shape), full(ar8.shape), full(ai8.shape),
                  full(h0.shape), full(flip.shape)],
        out_specs=[pl.BlockSpec((4, tc, w), lambda c: (0, c, 0)),
                   pl.BlockSpec((4, tc, w), lambda c: (0, nc - 1 - c, 0)),
                   full((8, S5_LANES))],
        out_shape=[y_shape, y_shape, jax.ShapeDtypeStruct((8, S5_LANES), F32)],
        scratch_shapes=[pltpu.VMEM((S5_SUPER, 8 * tc, 128), F32), pltpu.VMEM((8 * tc, S5_LANES), F32),
                        pltpu.VMEM((2 * S5_SUPER, 8 * tc, 128), F32), pltpu.VMEM((8, S5_LANES), F32)],
        compiler_params=_cparams("arbitrary"),
        name="s5_scan",
    )(u, u, bd, cd, ar8, ai8, h0, flip)


def _gelu_tanh(x):
    return 0.5 * x * (1.0 + jnp.tanh(math.sqrt(2.0 / math.pi) * (x + 0.044715 * (x * x * x))))


def _s5_glu_kernel(yf_ref, yb_ref, u_ref, d_ref, w_ref, b_ref, o_ref):
    y = _gelu_tanh(yf_ref[0] + yb_ref[0] + d_ref[...] * u_ref[0].astype(F32))
    o_ref[0] = (y * _sigmoid(_bdot(y, w_ref[...]) + b_ref[...])).astype(o_ref.dtype)


def _s5_glu(yf, yb, u, ucol, d_skip, glu_w_bf16, glu_b):
    b, l, w = yf.shape
    tl = min(l, 1024)
    row = lambda bb, i: (bb, i, 0)
    cst = lambda bb, i: (0, 0)
    return pl.pallas_call(
        _s5_glu_kernel,
        grid=(b, l // tl),
        in_specs=[pl.BlockSpec((1, tl, w), row)] * 2 + [
            pl.BlockSpec((1, tl, w), lambda bb, i: (bb, i, ucol)),
            pl.BlockSpec((1, w), cst), pl.BlockSpec((w, w), cst), pl.BlockSpec((1, w), cst)],
        out_specs=pl.BlockSpec((1, tl, w), row),
        out_shape=jax.ShapeDtypeStruct(yf.shape, BF16),
        compiler_params=_cparams("parallel", "parallel"),
        name="s5_glu",
    )(yf, yb, u, d_skip, glu_w_bf16, glu_b)


SSD_T = 128
SSD_NB = 4


def _softplus(x):
    return jnp.maximum(x, 0.0) + jnp.log(1.0 + jnp.exp(-jnp.abs(x)))


def _ssd_kernel(xs_ref, bm_ref, cm_ref, dtc_ref, dtr_ref, bias_c_ref, bias_r_ref, a_c_ref, a_r_ref,
                tri_ref, trit_ref, exp_ref, h0_ref, y_ref, hfin_ref, s_scr):
    c = pl.program_id(2)

    @pl.when(c == 0)
    def _():
        s_scr[...] = h0_ref[0]

    tri = tri_ref[0]
    expand = exp_ref[...]
    mask = tri > 0
    for bi in range(SSD_NB):
        dt_c = _softplus(dtc_ref[0, bi] + bias_c_ref[0])
        dt_r = _softplus(dtr_ref[0, bi] + bias_r_ref[0])
        cum_c = _dot_exact_lhs(tri, dt_c * a_c_ref[0])
        cum_r = _dot_exact_rhs(dt_r * a_r_ref[0], trit_ref[0])
        tot_c = jnp.min(cum_c, axis=0, keepdims=True)

        t = dt_c.shape[0]
        ex = _dot_split2_rhs(jnp.concatenate([dt_c, jnp.exp(cum_c), jnp.exp(tot_c - cum_c)], axis=0), expand)
        dt_x = ex[:t]
        in_x = ex[t:2 * t]
        out_x = ex[2 * t:]
        tot_x = jnp.where(pl.program_id(0) == 0, in_x[t - 1:t], in_x[0:1])

        xdt = xs_ref[bi].astype(F32) * dt_x
        xout = (xdt * out_x).astype(BF16)
        xdt = xdt.astype(BF16)
        for g in range(M2_GROUPS):
            bg = bm_ref[bi, :, g * M2_STATE:(g + 1) * M2_STATE].astype(BF16)
            cg = cm_ref[bi, :, g * M2_STATE:(g + 1) * M2_STATE].astype(BF16)
            cb = lax.dot_general(cg, bg, (((1,), (1,)), ((), ())), preferred_element_type=F32)
            gc = slice(g * M2_GROUP_W, (g + 1) * M2_GROUP_W)
            s_prev = s_scr[bi, g]
            y_off = jnp.dot(cg, s_prev.astype(BF16), preferred_element_type=F32) * in_x[:, gc]
            y_heads = []
            for r in range(M2_HEADS // M2_GROUPS):
                h = g * (M2_HEADS // M2_GROUPS) + r
                seg = cum_c[:, h:h + 1] - cum_r[h:h + 1, :]
                m = (cb * jnp.exp(jnp.where(mask, seg, -jnp.inf))).astype(BF16)
                hc = slice(h * M2_HEAD_DIM, (h + 1) * M2_HEAD_DIM)
                y_heads.append(jnp.dot(m, xdt[:, hc], preferred_element_type=F32))
            y_ref[0, bi, :, gc] = (jnp.concatenate(y_heads, axis=1) + y_off).astype(y_ref.dtype)
            s_scr[bi, g] = s_prev * tot_x[:, gc] + lax.dot_general(
                bg, xout[:, gc], (((0,), (0,)), ((), ())), preferred_element_type=F32)

    @pl.when(c == pl.num_programs(2) - 1)
    def _():
        hfin_ref[0] = s_scr[...]


def _ssd(xbc, dt_col, dt_row, dt_bias, a_coef, h0):
    b, l, _ = xbc.shape
    t = SSD_T
    nc = l // t
    tri_f = np.tril(np.ones((t, t), np.float32))
    tri = jnp.asarray(np.stack([tri_f, tri_f.T]), dtype=BF16)
    trit = jnp.asarray(np.stack([tri_f.T, tri_f]), dtype=BF16)
    expand = jnp.asarray(np.kron(np.eye(M2_HEADS, dtype=np.float32),
                                 np.ones((1, M2_HEAD_DIM), np.float32)), dtype=BF16)
    chunk = lambda d, c: c + d * (nc - 1 - 2 * c)
    nb = SSD_NB
    state = pl.BlockSpec((1, nb, M2_GROUPS, M2_STATE, M2_GROUP_W), lambda d, bb, c: (d, bb, 0, 0, 0))
    return pl.pallas_call(
        _ssd_kernel,
        grid=(2, b // nb, nc),
        in_specs=[pl.BlockSpec((nb, t, M2_INNER), lambda d, bb, c: (bb, chunk(d, c), 0)),
                  pl.BlockSpec((nb, t, M2_BC), lambda d, bb, c: (bb, chunk(d, c), 2)),
                  pl.BlockSpec((nb, t, M2_BC), lambda d, bb, c: (bb, chunk(d, c), 3)),
                  pl.BlockSpec((1, nb, t, M2_HEADS), lambda d, bb, c: (d, bb, chunk(d, c), 0)),
                  pl.BlockSpec((1, nb, M2_HEADS, t), lambda d, bb, c: (d, bb, 0, chunk(d, c))),
                  pl.BlockSpec((1, 1, M2_HEADS), lambda d, bb, c: (d, 0, 0)),
                  pl.BlockSpec((1, M2_HEADS, 1), lambda d, bb, c: (d, 0, 0)),
                  pl.BlockSpec((1, 1, M2_HEADS), lambda d, bb, c: (d, 0, 0)),
                  pl.BlockSpec((1, M2_HEADS, 1), lambda d, bb, c: (d, 0, 0)),
                  pl.BlockSpec((1, t, t), lambda d, bb, c: (d, 0, 0)),
                  pl.BlockSpec((1, t, t), lambda d, bb, c: (d, 0, 0)),
                  pl.BlockSpec((M2_HEADS, M2_INNER), lambda d, bb, c: (0, 0)),
                  state],
        out_specs=[pl.BlockSpec((1, nb, t, M2_INNER), lambda d, bb, c: (d, bb, chunk(d, c), 0)), state],
        out_shape=[jax.ShapeDtypeStruct((2, b, l, M2_INNER), BF16),
                   jax.ShapeDtypeStruct((2, b, M2_GROUPS, M2_STATE, M2_GROUP_W), F32)],
        scratch_shapes=[pltpu.VMEM((nb, M2_GROUPS, M2_STATE, M2_GROUP_W), F32)],
        compiler_params=_cparams("parallel", "parallel", "arbitrary"),
        name="ssd",
    )(xbc, xbc, xbc, dt_col, dt_row, dt_bias.reshape(2, 1, M2_HEADS), dt_bias.reshape(2, M2_HEADS, 1),
      a_coef.reshape(2, 1, M2_HEADS), a_coef.reshape(2, M2_HEADS, 1), tri, trit, expand, h0)


def _m2_gate_kernel(yf_ref, yb_ref, xs_ref, z_ref, d_ref, g_ref, o_ref):
    y = (yf_ref[0, 0].astype(F32) + yb_ref[0, 0].astype(F32)
         + d_ref[...] * xs_ref[0].astype(F32)) * _silu(z_ref[0].astype(F32))
    ms = jnp.mean(y * y, axis=-1, keepdims=True)
    o_ref[0] = (y * lax.rsqrt(ms + NORM_EPS) * g_ref[...]).astype(o_ref.dtype)


def _m2_gate(ydir, xbc, z, zcol, d_x, norm_g):
    _, b, l, w = ydir.shape
    tl = min(l, 512)
    row = lambda bb, i: (bb, i, 0)
    cst = lambda bb, i: (0, 0)
    return pl.pallas_call(
        _m2_gate_kernel,
        grid=(b, l // tl),
        in_specs=[pl.BlockSpec((1, 1, tl, w), lambda bb, i: (0, bb, i, 0)),
                  pl.BlockSpec((1, 1, tl, w), lambda bb, i: (1, bb, i, 0)),
                  pl.BlockSpec((1, tl, w), row), pl.BlockSpec((1, tl, w), lambda bb, i: (bb, i, zcol)),
                  pl.BlockSpec((1, w), cst), pl.BlockSpec((1, w), cst)],
        out_specs=pl.BlockSpec((1, tl, w), row),
        out_shape=jax.ShapeDtypeStruct((b, l, w), BF16),
        compiler_params=_cparams("parallel", "parallel"),
        name="m2_gate",
    )(ydir, ydir, xbc, z, d_x, norm_g)


HY_N2 = 128


def _round8(n):
    return (n + 7) // 8 * 8


def _dft2_tables(n1, n2):
    n = n1 * n2
    k1n = n1 // 2 + 1
    k1p = _round8(k1n)
    k1 = np.arange(k1p, dtype=np.float64)[:, None]
    valid = (k1 < k1n).astype(np.float64)
    i1 = np.arange(n1, dtype=np.float64)[None, :]
    th1 = 2.0 * np.pi * k1 * i1 / n1
    stage1 = np.concatenate([np.cos(th1) * valid, -np.sin(th1) * valid], axis=0)
    t = n2 * i1[None] + np.arange(n2, dtype=np.float64)[:, None, None]
    th = 2.0 * np.pi * k1[None] * t / n
    stage1_tw = np.concatenate([np.cos(th) * valid[None], -np.sin(th) * valid[None]], axis=1)
    i2 = np.arange(n2, dtype=np.float64)
    tht = 2.0 * np.pi * i2 / n
    tw1 = np.stack([np.broadcast_to(np.cos(tht)[:, None], (n2, 128)),
                    np.broadcast_to(-np.sin(tht)[:, None], (n2, 128))])
    k2 = np.arange(n2, dtype=np.float64)[:, None]
    th2 = 2.0 * np.pi * k2 * i2[None, :] / n2
    fc, fs = np.cos(th2), np.sin(th2)
    fwd2 = np.block([[fc, fs], [-fs, fc]])
    inv2 = np.block([[fc, -fs], [fs, fc]])
    wk = np.where((k1 == 0) | (k1 == n1 // 2), 1.0, 2.0) * valid
    o1 = np.arange(n1 // 2, dtype=np.float64)[:, None]
    tho = 2.0 * np.pi * o1 * k1.T / n1
    last = np.concatenate([np.cos(tho) * wk.T, -np.sin(tho) * wk.T], axis=1) / n
    f = lambda a: jnp.asarray(a, dtype=F32)
    return dict(k1n=k1n, k1p=k1p, stage1=f(stage1), stage1_tw=f(stage1_tw), tw1=f(tw1),
                fwd2=f(fwd2), inv2=f(inv2), last=f(last))


def _dft1_tables(l):
    n = 2 * l
    kn = l + 1
    kp = _round8(kn)
    k = np.arange(kp, dtype=np.float64)[:, None]
    valid = (k < kn).astype(np.float64)
    t = np.arange(n, dtype=np.float64)[None, :]
    th = 2.0 * np.pi * k * t / n
    fwd = np.concatenate([np.cos(th) * valid, -np.sin(th) * valid], axis=0)
    wk = np.where((k == 0) | (k == l), 1.0, 2.0) * valid
    o = np.arange(l, dtype=np.float64)[:, None]
    tho = 2.0 * np.pi * o * k.T / n
    inv = np.concatenate([np.cos(tho) * wk.T, -np.sin(tho) * wk.T], axis=1) / n
    f = lambda a: jnp.asarray(a, dtype=F32)
    return dict(kp=kp, fwd=f(fwd), inv=f(inv))


def _rowdft_kernel(hi, f_ref, x_ref, o_ref):
    dot = _dot3 if hi else _bdot
    o_ref[0] = dot(f_ref[...], x_ref[0])


def _rowdft(fmat, x, hi):
    bx, r, n = x.shape
    m = fmat.shape[0]
    tn = min(n, 4096)
    return pl.pallas_call(
        functools.partial(_rowdft_kernel, hi),
        grid=(bx, n // tn),
        in_specs=[pl.BlockSpec((m, r), lambda b, j: (0, 0)),
                  pl.BlockSpec((1, r, tn), lambda b, j: (b, 0, j))],
        out_specs=pl.BlockSpec((1, m, tn), lambda b, j: (b, 0, j)),
        out_shape=jax.ShapeDtypeStruct((bx, m, n), F32),
        compiler_params=_cparams("parallel", "parallel"),
        name="hy_rowdft",
    )(fmat if hi else fmat.astype(BF16), x)


HY_TS = 8


HY_PITCH = HY_N2 + 8
HY_CONV_VMEM_BYTES = 58 * 1024 * 1024
HY_STAGE2_UNROLL = 4


def _hy_conv_kernel(k1n, k1p, n2, v_ref, gate_ref, h_ref, st1_ref, tw1_ref, fwd_ref, inv_ref, last_ref,
                    d_ref, o_ref, a_scr, t_scr):
    n1h = v_ref.shape[0] // n2
    st1 = st1_ref[...]
    last = last_ref[...]

    def copy_in(i, _):
        t_scr[pl.ds(pl.multiple_of(i * HY_PITCH, 8), n2), :] = (
            v_ref[pl.ds(pl.multiple_of(i * n2, n2), n2), :].astype(F32))
        return 0
    lax.fori_loop(0, n1h, copy_in, 0)

    def stage1(j, _):
        s = 2 * j
        xs = jnp.concatenate([t_scr[pl.ds(s, n1h, stride=HY_PITCH), :],
                              t_scr[pl.ds(s + 1, n1h, stride=HY_PITCH), :]], axis=1).astype(BF16)
        y = jnp.dot(st1, xs, preferred_element_type=F32)
        a_scr[pl.ds(s, 2 * k1p, stride=HY_PITCH), :] = y[:, :128]
        a_scr[pl.ds(s + 1, 2 * k1p, stride=HY_PITCH), :] = y[:, 128:]
        return 0
    lax.fori_loop(0, n2 // 2, stage1, 0, unroll=8)

    def cmul(ar, ai, br, bi):
        return ar * br - ai * bi, ar * bi + ai * br

    def stage2(j, tw):
        t1r, t1i = tw1_ref[0], tw1_ref[1]
        tw_a = tw
        tw_b = cmul(*tw_a, t1r, t1i)
        ks = (2 * j, 2 * j + 1)
        rows_r = [pl.ds(pl.multiple_of(k * HY_PITCH, 8), n2) for k in ks]
        rows_i = [pl.ds(pl.multiple_of((k1p + k) * HY_PITCH, 8), n2) for k in ks]
        pr, pi = [], []
        for k, twk, rr, ri in zip(ks, (tw_a, tw_b), rows_r, rows_i):
            a, b = cmul(a_scr[rr, :], a_scr[ri, :], *twk)
            pr.append(a)
            pi.append(b)
        p = jnp.concatenate([jnp.concatenate(pr, axis=1), jnp.concatenate(pi, axis=1)], axis=0)
        x = jnp.dot(fwd_ref[...], p.astype(BF16), preferred_element_type=F32)
        xr, xi = x[:n2], x[n2:]
        hr = jnp.concatenate([h_ref[0, ks[0]], h_ref[0, ks[1]]], axis=1).astype(F32)
        hi = jnp.concatenate([h_ref[1, ks[0]], h_ref[1, ks[1]]], axis=1).astype(F32)
        qr, qi = cmul(xr, xi, hr, hi)
        z = jnp.dot(inv_ref[...], jnp.concatenate([qr, qi], axis=0).astype(BF16), preferred_element_type=F32)
        for n, (twk, rr, ri) in enumerate(zip((tw_a, tw_b), rows_r, rows_i)):
            zr, zi = z[:n2, n * 128:(n + 1) * 128], z[n2:, n * 128:(n + 1) * 128]
            a_scr[rr, :] = zr * twk[0] + zi * twk[1]
            a_scr[ri, :] = zi * twk[0] - zr * twk[1]
        return cmul(*tw_b, t1r, t1i)

    pairs = k1p // 2
    lax.fori_loop(0, pairs, stage2, (jnp.ones((n2, 128), F32), jnp.zeros((n2, 128), F32)),
                  unroll=HY_STAGE2_UNROLL)

    def stage3(j, _):
        s = 2 * j
        zs = jnp.concatenate([a_scr[pl.ds(s, 2 * k1p, stride=HY_PITCH), :],
                              a_scr[pl.ds(s + 1, 2 * k1p, stride=HY_PITCH), :]], axis=1).astype(BF16)
        y = jnp.dot(last, zs, preferred_element_type=F32)
        t_scr[pl.ds(s, n1h, stride=HY_PITCH), :] = y[:, :128]
        t_scr[pl.ds(s + 1, n1h, stride=HY_PITCH), :] = y[:, 128:]
        return 0
    lax.fori_loop(0, n2 // 2, stage3, 0, unroll=8)

    def finish(i, _):
        rows = pl.ds(pl.multiple_of(i * n2, n2), n2)
        y = t_scr[pl.ds(pl.multiple_of(i * HY_PITCH, 8), n2), :]
        o_ref[rows, :] = (gate_ref[rows, :].astype(F32)
                          * (y + d_ref[...] * v_ref[rows, :].astype(F32))).astype(o_ref.dtype)
        return 0
    lax.fori_loop(0, n1h, finish, 0)


def _hy_long_conv2(v, vpart, gate, gpart, hspec, order, d_vec, tab):
    b, l, _ = v.shape
    c = HY_WIDTH
    n2 = HY_N2
    n1h = l // n2
    k1p = tab['k1p']
    tiles = c // 128
    seq = pl.BlockSpec((None, l, 128), lambda j, bb: (bb, 0, j))
    vseq = pl.BlockSpec((None, l, 128), lambda j, bb: (bb, 0, vpart * tiles + j))
    gseq = pl.BlockSpec((None, l, 128), lambda j, bb: (bb, 0, gpart * tiles + j))
    cst2 = lambda a: pl.BlockSpec(a.shape, lambda j, bb: (0, 0))
    st1 = tab['stage1'][:, :n1h].astype(BF16)
    fwd2, inv2, last = tab['fwd2'].astype(BF16), tab['inv2'].astype(BF16), tab['last'].astype(BF16)
    return pl.pallas_call(
        functools.partial(_hy_conv_kernel, tab['k1n'], k1p, n2),
        grid=(tiles, b),
        in_specs=[vseq, gseq,
                  pl.BlockSpec((2, k1p, n2, 128), lambda j, bb: (0, 0, 0, order * tiles + j)),
                  cst2(st1),
                  pl.BlockSpec((2, n2, 128), lambda j, bb: (0, 0, 0)),
                  cst2(fwd2), cst2(inv2), cst2(last),
                  pl.BlockSpec((1, 128), lambda j, bb: (0, j))],
        out_specs=seq,
        out_shape=jax.ShapeDtypeStruct((b, l, c), BF16),
        scratch_shapes=[pltpu.VMEM((2 * k1p * HY_PITCH, 128), F32), pltpu.VMEM((n1h * HY_PITCH, 128), F32)],
        compiler_params=pltpu.CompilerParams(dimension_semantics=("parallel", "parallel"),
                                             vmem_limit_bytes=HY_CONV_VMEM_BYTES),
        name="hy_conv",
    )(v, gate, hspec, st1, tab['tw1'], fwd2, inv2, last, d_vec.reshape(1, c))


def _hy_filt_first_kernel(k1p, feat_ref, t_ref, keep_ref, w1_ref, b1_ref, q1_ref, w2_ref, b2_ref, q2_ref,
                          w3_ref, nd_ref, f_ref, o_ref, sum_ref):
    @pl.when(pl.program_id(0) == 0)
    def _():
        sum_ref[...] = jnp.zeros_like(sum_ref)

    n1 = feat_ref.shape[1]
    half = n1 // 2
    feats = feat_ref[...].reshape(HY_TS * n1, feat_ref.shape[2])
    hid = jnp.sin(q1_ref[...] * (_bdot(feats, w1_ref[...]) + b1_ref[...]))
    hid = jnp.sin(q2_ref[...] * (_bdot(hid, w2_ref[...]) + b2_ref[...]))
    acc = jnp.zeros(sum_ref.shape, F32)
    for s in range(HY_TS):
        hs = hid[s * n1:(s + 1) * n1]
        f = jnp.concatenate([_bdot(hs[:half], w3_ref[0]), _bdot(hs[half:], w3_ref[1])], axis=0)
        f = f * jnp.exp(t_ref[s] * nd_ref[...]) * keep_ref[s]
        acc = acc + jnp.sum(jnp.abs(f), axis=0, keepdims=True)
        y = _dot3(f_ref[s], f)
        o_ref[0, :, s, :] = y[:k1p]
        o_ref[1, :, s, :] = y[k1p:]
    sum_ref[...] += acc


def _hy_filt_first(l, tab, f_w1, f_b1, f_freq1, f_w2, f_b2, f_freq2, f_w3):
    c2 = HY_ORDER * HY_WIDTH
    n2 = HY_N2
    n1 = 2 * l // n2
    k1p = tab['k1p']
    t = np.linspace(0.0, 1.0, l, dtype=np.float32)[:, None]
    bands = np.linspace(1e-4, HY_BANDS - 1, HY_BANDS, dtype=np.float32)
    ang = np.float32(2.0 * math.pi / l) * np.arange(l, dtype=np.float32)[:, None] * bands
    feats = np.concatenate([t, np.cos(ang), -np.sin(ang)], axis=-1)
    rows = (n2 * np.arange(n1)[None, :] + np.arange(n2)[:, None]).reshape(-1)
    pos = np.where(rows < l, rows, np.where(rows == l, 0, 2 * l - rows))
    feats_t = jnp.asarray(np.pad(feats[pos], ((0, 0), (0, 128 - HY_EMB))).reshape(n2, n1, 128))
    t_t = jnp.asarray(t[pos].reshape(n2, n1, 1))
    keep_t = jnp.asarray((rows != l).astype(np.float32).reshape(n2, n1, 1))
    w1p = jnp.pad(f_w1, ((0, 128 - HY_EMB), (0, 0)))
    w3d = jnp.transpose(f_w3.reshape(HY_HIDDEN, HY_ORDER, 2, HY_WIDTH), (2, 0, 1, 3)).reshape(2, HY_HIDDEN, c2)
    deltas = jnp.abs(jnp.linspace(HY_MIN_DECAY, HY_MAX_DECAY, HY_WIDTH, dtype=F32))
    negd = jnp.tile(-deltas, HY_ORDER)[None, :]
    cst = lambda i: (0, 0)
    vec = lambda a: a.reshape(1, HY_HIDDEN)
    tile3 = lambda w: pl.BlockSpec((HY_TS, n1, w), lambda i: (i, 0, 0))
    return pl.pallas_call(
        functools.partial(_hy_filt_first_kernel, k1p),
        grid=(n2 // HY_TS,),
        in_specs=[tile3(128), tile3(1), tile3(1),
                  pl.BlockSpec((128, HY_HIDDEN), cst), pl.BlockSpec((1, HY_HIDDEN), cst),
                  pl.BlockSpec((1, HY_HIDDEN), cst),
                  pl.BlockSpec((HY_HIDDEN, HY_HIDDEN), cst), pl.BlockSpec((1, HY_HIDDEN), cst),
                  pl.BlockSpec((1, HY_HIDDEN), cst),
                  pl.BlockSpec((2, HY_HIDDEN, c2), lambda i: (0, 0, 0)),
                  pl.BlockSpec((1, c2), cst),
                  pl.BlockSpec((HY_TS, 2 * k1p, n1), lambda i: (i, 0, 0))],
        out_specs=[pl.BlockSpec((None, 2, k1p, HY_TS, c2), lambda i: (0, 0, 0, i, 0)),
                   pl.BlockSpec((1, c2), cst)],
        out_shape=[jax.ShapeDtypeStruct((1, 2, k1p, n2, c2), F32), jax.ShapeDtypeStruct((1, c2), F32)],
        compiler_params=_cparams("arbitrary"),
        name="hy_filt_first",
    )(feats_t, t_t, keep_t, w1p, vec(f_b1), vec(f_freq1), f_w2, vec(f_b2), vec(f_freq2), w3d, negd,
      tab['stage1_tw'])


def _hy_filt_mid_kernel(n2, a_ref, sum_ref, fwd_ref, o_ref):
    c2 = a_ref.shape[-1]
    x = _dot3(fwd_ref[...], a_ref[0, :, 0].reshape(2 * n2, c2)) * (1.0 / (sum_ref[...] + 1e-6))
    o_ref[0, 0] = x[:n2].astype(BF16)
    o_ref[1, 0] = x[n2:].astype(BF16)


def _hy_filt_mid(a5, tab, colsum):
    _, _, k1p, n2, c2 = a5.shape
    return pl.pallas_call(
        functools.partial(_hy_filt_mid_kernel, n2),
        grid=(k1p,),
        in_specs=[pl.BlockSpec((1, 2, 1, n2, c2), lambda k: (0, 0, k, 0, 0)),
                  pl.BlockSpec((1, c2), lambda k: (0, 0)),
                  pl.BlockSpec((2 * n2, 2 * n2), lambda k: (0, 0))],
        out_specs=pl.BlockSpec((2, 1, n2, c2), lambda k: (0, k, 0, 0)),
        out_shape=jax.ShapeDtypeStruct((2, k1p, n2, c2), BF16),
        compiler_params=_cparams("parallel"),
        name="hy_filt_mid",
    )(a5, colsum, tab['fwd2'])


def _hy_ctx_kernel(kp, fw_ref, inv_ref, h_ref, v_ref, gate_ref, d_ref, o_ref):
    v = v_ref[0].astype(F32)
    u = _bdot(fw_ref[...], v)
    ur, ui = u[:kp], u[kp:]
    hr, hi = h_ref[0], h_ref[1]
    q = jnp.concatenate([ur * hr - ui * hi, ur * hi + ui * hr], axis=0)
    o_ref[0] = (gate_ref[0].astype(F32) * (_bdot(inv_ref[...], q) + d_ref[...] * v)).astype(o_ref.dtype)


def _hy_long_conv1(v, vpart, gate, gpart, hspec, order, d_vec, tab):
    b, l, _ = v.shape
    c = HY_WIDTH
    kp = tab['kp']
    row = lambda bb: (bb, 0, 0)
    return pl.pallas_call(
        functools.partial(_hy_ctx_kernel, kp),
        grid=(b,),
        in_specs=[pl.BlockSpec((2 * kp, l), lambda bb: (0, 0)),
                  pl.BlockSpec((l, 2 * kp), lambda bb: (0, 0)),
                  pl.BlockSpec((2, kp, c), lambda bb: (0, 0, order)),
                  pl.BlockSpec((1, l, c), lambda bb: (bb, 0, vpart)),
                  pl.BlockSpec((1, l, c), lambda bb: (bb, 0, gpart)),
                  pl.BlockSpec((1, c), lambda bb: (0, 0))],
        out_specs=pl.BlockSpec((1, l, c), row),
        out_shape=jax.ShapeDtypeStruct((b, l, c), BF16),
        compiler_params=_cparams("parallel"),
        name="hy_ctx_conv",
    )(tab['fwd'][:, :l].astype(BF16), tab['inv'].astype(BF16), hspec, v, gate, d_vec.reshape(1, c))


def _hy_filter_kernel(feat_ref, t_ref, keep_ref, w1_ref, b1_ref, q1_ref, w2_ref, b2_ref, q2_ref,
                      w3_ref, nd_ref, f_ref, sum_ref):
    @pl.when(pl.program_id(0) == 0)
    def _():
        sum_ref[...] = jnp.zeros_like(sum_ref)

    hid = jnp.sin(q1_ref[...] * (_bdot(feat_ref[...], w1_ref[...]) + b1_ref[...]))
    hid = jnp.sin(q2_ref[...] * (_bdot(hid, w2_ref[...]) + b2_ref[...]))
    f = _bdot(hid, w3_ref[0]) * jnp.exp(t_ref[...] * nd_ref[...]) * keep_ref[...]
    f_ref[...] = f
    sum_ref[...] += jnp.sum(jnp.abs(f), axis=0, keepdims=True)


def _hy_filters(l, f_w1, f_b1, f_freq1, f_w2, f_b2, f_freq2, f_w3):
    c2 = HY_ORDER * HY_WIDTH
    t = np.linspace(0.0, 1.0, l, dtype=np.float32)[:, None]
    bands = np.linspace(1e-4, HY_BANDS - 1, HY_BANDS, dtype=np.float32)
    ang = np.float32(2.0 * math.pi / l) * np.arange(l, dtype=np.float32)[:, None] * bands
    feats = np.concatenate([t, np.cos(ang), -np.sin(ang)], axis=-1)
    rows = np.arange(2 * l)
    pos = np.where(rows < l, rows, np.where(rows == l, 0, 2 * l - rows))
    feats2 = jnp.asarray(np.pad(feats[pos], ((0, 0), (0, 128 - HY_EMB))))
    keep = jnp.asarray((rows != l).astype(np.float32))[:, None]
    t2 = jnp.asarray(t[pos])
    w1p = jnp.pad(f_w1, ((0, 128 - HY_EMB), (0, 0)))
    w3d = jnp.transpose(f_w3.reshape(HY_HIDDEN, HY_ORDER, 2, HY_WIDTH), (2, 0, 1, 3)).reshape(2, HY_HIDDEN, c2)
    deltas = jnp.abs(jnp.linspace(HY_MIN_DECAY, HY_MAX_DECAY, HY_WIDTH, dtype=F32))
    negd = jnp.tile(-deltas, HY_ORDER)[None, :]
    tr = min(l, 512)
    per_dir = l // tr
    cst = lambda i: (0, 0)
    vec = lambda a: a.reshape(1, HY_HIDDEN)
    return pl.pallas_call(
        _hy_filter_kernel,
        grid=(2 * l // tr,),
        in_specs=[pl.BlockSpec((tr, 128), lambda i: (i, 0)),
                  pl.BlockSpec((tr, 1), lambda i: (i, 0)),
                  pl.BlockSpec((tr, 1), lambda i: (i, 0)),
                  pl.BlockSpec((128, HY_HIDDEN), cst), pl.BlockSpec((1, HY_HIDDEN), cst),
                  pl.BlockSpec((1, HY_HIDDEN), cst),
                  pl.BlockSpec((HY_HIDDEN, HY_HIDDEN), cst), pl.BlockSpec((1, HY_HIDDEN), cst),
                  pl.BlockSpec((1, HY_HIDDEN), cst),
                  pl.BlockSpec((1, HY_HIDDEN, c2), lambda i: (i // per_dir, 0, 0)),
                  pl.BlockSpec((1, c2), cst)],
        out_specs=[pl.BlockSpec((tr, c2), lambda i: (i, 0)), pl.BlockSpec((1, c2), cst)],
        out_shape=[jax.ShapeDtypeStruct((2 * l, c2), F32), jax.ShapeDtypeStruct((1, c2), F32)],
        compiler_params=_cparams("arbitrary"),
        name="hy_filter",
    )(feats2, t2, keep, w1p, vec(f_b1), vec(f_freq1), f_w2, vec(f_b2), vec(f_freq2), w3d, negd)


def _col_scale_kernel(x_ref, sum_ref, o_ref):
    o_ref[...] = x_ref[...] * (1.0 / (sum_ref[...] + 1e-6))


def _hy_spectrum(l, fparams):
    if 2 * l >= 16 * HY_N2:
        tab = _dft2_tables(2 * l // HY_N2, HY_N2)
        a, colsum = _hy_filt_first(l, tab, *fparams)
        return _hy_filt_mid(a, tab, colsum), tab
    filt, colsum = _hy_filters(l, *fparams)
    c2 = filt.shape[1]
    tab = _dft1_tables(l)
    spec = _rowdft(tab['fwd'], filt[None], True)[0]
    spec = pl.pallas_call(
        _col_scale_kernel,
        in_specs=[pl.BlockSpec(spec.shape, lambda: (0, 0)), pl.BlockSpec((1, c2), lambda: (0, 0))],
        out_specs=pl.BlockSpec(spec.shape, lambda: (0, 0)),
        out_shape=jax.ShapeDtypeStruct(spec.shape, F32),
        name="hy_col_scale",
    )(spec, colsum)
    return spec.reshape(2, tab['kp'], c2), tab


def _hyena_mixer(zz, hspec, tab, d_skip):
    conv = _hy_long_conv2 if 'k1p' in tab else _hy_long_conv1
    y = conv(zz, 0, zz, 1, hspec, 0, d_skip[0], tab)
    return conv(y, 0, zz, 2, hspec, 1, d_skip[1], tab)


def _grid_sincos(n, dm):
    rows = n // GRID_W
    quarter = dm // 4
    omega = 1.0 / (10000.0 ** (jnp.arange(quarter, dtype=F32) / quarter))
    ang_r = jnp.arange(rows, dtype=F32)[:, None] * omega
    ang_c = jnp.arange(GRID_W, dtype=F32)[:, None] * omega
    emb_r = jnp.concatenate([jnp.sin(ang_r), jnp.cos(ang_r)], axis=-1)
    emb_c = jnp.concatenate([jnp.sin(ang_c), jnp.cos(ang_c)], axis=-1)
    half = emb_r.shape[-1]
    pe = jnp.concatenate([jnp.broadcast_to(emb_r[:, None, :], (rows, GRID_W, half)),
                          jnp.broadcast_to(emb_c[None, :, :], (rows, GRID_W, half))], axis=-1)
    return pe.reshape(rows * GRID_W, 2 * half)


EV_Z_OFF = M2_XBC
EV_U_OFF = EV_Z_OFF + M2_INNER
EV_DT_OFF = EV_U_OFF + S5_WIDTH
EV_PROJ_W = EV_DT_OFF + 128


def _even_mixer(h_in, p, states):
    x, g, shift, scale = h_in
    b, l, _ = x.shape
    proj = _adaln_mm_conv(x, g, shift, scale, p['w_in'], jnp.zeros((1, EV_PROJ_W), F32),
                          p['m2_conv_w'], p['m2_conv_b'], True)
    s5_h0, m2_h0 = states

    ucol = EV_U_OFF // S5_WIDTH
    yf, yb, s5_fin = _s5_scan(proj, ucol, p['s5_bd'], p['s5_cd'], p['s5_ar'], p['s5_ai'], s5_h0)
    y_s5 = _s5_glu(yf, yb, proj, ucol, p['s5_d'], p['s5_glu_w'], p['s5_glu_b'])

    dt4 = proj[..., EV_DT_OFF:EV_DT_OFF + 2 * M2_HEADS].astype(F32).reshape(b, l, 2, M2_HEADS)
    dt_col = jnp.transpose(dt4, (2, 0, 1, 3))
    dt_row = jnp.transpose(dt4, (2, 0, 3, 1))
    ydir, m2_fin = _ssd(proj, dt_col, dt_row, p['m2_dt_bias'], p['m2_a'], m2_h0)
    y_m2 = _m2_gate(ydir, proj, proj, EV_Z_OFF // M2_INNER, p['m2_d'], p['m2_norm_g'])
    return (y_s5, y_m2), (s5_fin, m2_fin)


def kernel(x, c, ctx, c_ctx, mod_w, mod_b, norm_mix_g, norm_mlp_g, mlp_w1, mlp_w2, final_norm_g, ev_in_w, ev_out_w, s5_lam_re, s5_lam_im, s5_log_dt, s5_b_re, s5_b_im, s5_c_re, s5_c_im, s5_d, s5_glu_w, s5_glu_b, m2_conv_w, m2_conv_b, m2_dt_bias, m2_a_log, m2_d, m2_norm_g, hy_in_w, hy_in_b, hy_conv_w, hy_conv_b, hy_f_w1, hy_f_b1, hy_f_freq1, hy_f_w2, hy_f_b2, hy_f_freq2, hy_f_w3, hy_d, hy_out_w, hy_out_b):
    bsz, n, dm = x.shape
    lc = ctx.shape[1]
    x = _add_pe(x, _grid_sincos(n, dm))

    c8 = jnp.concatenate([c, c_ctx[None], jnp.zeros((3, dm), F32)], axis=0)
    mods = _modulation(c8, mod_w, mod_b).reshape(DEPTH, 8, N_MOD, dm)
    final_g = final_norm_g.reshape(1, dm)

    for i in range(DEPTH):
        j = i // 2
        mx = [mods[i, :bsz, k][:, None, :] for k in range(N_MOD)]
        mc = [jnp.broadcast_to(mods[i, bsz, k][None, None, :], (bsz, 1, dm)) for k in range(N_MOD)]
        g_mix = norm_mix_g[i].reshape(1, dm)
        g_mlp = norm_mlp_g[i].reshape(1, dm)
        ctx_later = any(k % 2 == 0 for k in range(i + 1, DEPTH))
        w1 = mlp_w1[i].astype(BF16)
        w2 = mlp_w2[i].astype(BF16)

        if i % 2 == 0:
            in_w = ev_in_w[j]
            o0, o1, o2 = S5_WIDTH, S5_WIDTH + M2_INNER, S5_WIDTH + M2_INNER + M2_XBC
            bd, cd, ar8, ai8 = _s5_prepare(s5_lam_re[j], s5_lam_im[j], s5_log_dt[j], s5_b_re[j], s5_b_im[j],
                                           s5_c_re[j], s5_c_im[j])
            p = dict(
                w_in=jnp.concatenate([in_w[:, o1:o2], in_w[:, o0:o1], in_w[:, :o0],
                                      jnp.pad(in_w[:, o2:], ((0, 0), (0, 128 - 2 * M2_HEADS)))],
                                     axis=1).astype(BF16),
                s5_bd=bd, s5_cd=cd, s5_ar=ar8, s5_ai=ai8,
                s5_d=s5_d[j].reshape(1, S5_WIDTH), s5_glu_w=s5_glu_w[j].astype(BF16),
                s5_glu_b=s5_glu_b[j].reshape(1, S5_WIDTH),
                m2_conv_w=m2_conv_w[j], m2_conv_b=m2_conv_b[j].reshape(1, M2_XBC),
                m2_dt_bias=m2_dt_bias[j], m2_a=-jnp.exp(m2_a_log[j]),
                m2_d=jnp.repeat(m2_d[j], M2_HEAD_DIM).reshape(1, M2_INNER),
                m2_norm_g=m2_norm_g[j].reshape(1, M2_INNER))
            zero_states = (jnp.zeros((8, S5_LANES), F32),
                           jnp.zeros((2, bsz, M2_GROUPS, M2_STATE, M2_GROUP_W), F32))
            ys_c, ctx_states = _even_mixer((ctx, g_mix, mc[0], mc[1]), p, zero_states)
            ys_x, _ = _even_mixer((x, g_mix, mx[0], mx[1]), p, ctx_states)
            out_ws = [ev_out_w[j][:S5_WIDTH].astype(BF16), ev_out_w[j][S5_WIDTH:].astype(BF16)]
            out_b = jnp.zeros((1, dm), F32)
        else:
            fparams = (hy_f_w1[j], hy_f_b1[j], hy_f_freq1[j], hy_f_w2[j], hy_f_b2[j], hy_f_freq2[j], hy_f_w3[j])
            in_w = hy_in_w[j].astype(BF16)
            in_b = hy_in_b[j].reshape(1, -1)
            conv_b = hy_conv_b[j].reshape(1, -1)
            hspec, tab = _hy_spectrum(n, fparams)
            zz = _adaln_mm_conv(x, g_mix, mx[0], mx[1], in_w, in_b, hy_conv_w[j], conv_b, False)
            ys_x = [_hyena_mixer(zz, hspec, tab, hy_d[j])]
            if ctx_later:
                hspec_c, tab_c = _hy_spectrum(lc, fparams)
                zz_c = _adaln_mm_conv(ctx, g_mix, mc[0], mc[1], in_w, in_b, hy_conv_w[j], conv_b, False)
                ys_c = [_hyena_mixer(zz_c, hspec_c, tab_c, hy_d[j])]
            out_ws = [hy_out_w[j].astype(BF16)]
            out_b = hy_out_b[j].reshape(1, dm)

        x = _mix_mlp(x, mx[2], out_b, ys_x, out_ws, g_mlp, mx[3], mx[4], mx[5], w1, w2, final_g, i == DEPTH - 1)
        if ctx_later:
            ctx = _mix_mlp(ctx, mc[2], out_b, ys_c, out_ws, g_mlp, mc[3], mc[4], mc[5], w1, w2, final_g, False)
    return x
```

```python
import functools
import math

import numpy as np
import jax
import jax.numpy as jnp
from jax import lax
from jax.experimental import pallas as pl
from jax.experimental.pallas import tpu as pltpu

F32 = jnp.float32
BF16 = jnp.bfloat16

D_MODEL = 1024
DEPTH = 4
GRID_W = 64
N_MOD = 6
MLP_HIDDEN = 4 * D_MODEL
NORM_EPS = 1e-6

S5_WIDTH = 512
S5_GROUP_CH = 16
S5_GROUPS = 32
S5_STATE = 64
S5_SUPER = 4
S5_SLAB = 2 * 8 * S5_STATE
S5_LANES = S5_SUPER * S5_SLAB

M2_INNER = 1024
M2_HEAD_DIM = 64
M2_HEADS = 16
M2_GROUPS = 4
M2_STATE = 128
M2_BC = M2_GROUPS * M2_STATE
M2_XBC = M2_INNER + 2 * M2_BC
M2_GROUP_W = (M2_HEADS // M2_GROUPS) * M2_HEAD_DIM

HY_WIDTH = 1024
HY_ORDER = 2
HY_BANDS = 16
HY_EMB = 2 * HY_BANDS + 1
HY_HIDDEN = 64
HY_MIN_DECAY = math.log(1e-2) / 1.5
HY_MAX_DECAY = math.log(1e-2) / 0.3

V7X_VMEM_LIMIT_BYTES = 48 * 1024 * 1024


def _cparams(*sem):
    return pltpu.CompilerParams(dimension_semantics=sem, vmem_limit_bytes=V7X_VMEM_LIMIT_BYTES)


def _bdot(a, b):
    return jnp.dot(a.astype(BF16), b.astype(BF16), preferred_element_type=F32)


def _split3(a):
    a1 = a.astype(BF16)
    r1 = a - a1.astype(F32)
    a2 = r1.astype(BF16)
    a3 = (r1 - a2.astype(F32)).astype(BF16)
    return a1, a2, a3


def _dot_exact_rhs(a, b_bf16):
    a1, a2, a3 = _split3(a)
    d = functools.partial(jnp.dot, preferred_element_type=F32)
    return d(a1, b_bf16) + d(a2, b_bf16) + d(a3, b_bf16)


def _dot_split2_rhs(a, b_bf16):
    a1 = a.astype(BF16)
    a2 = (a - a1.astype(F32)).astype(BF16)
    d = functools.partial(jnp.dot, preferred_element_type=F32)
    return d(a1, b_bf16) + d(a2, b_bf16)


def _dot_exact_lhs(a_bf16, b):
    b1, b2, b3 = _split3(b)
    d = functools.partial(jnp.dot, preferred_element_type=F32)
    return d(a_bf16, b1) + d(a_bf16, b2) + d(a_bf16, b3)


def _dot3(a, b):
    a1 = a.astype(BF16)
    a2 = (a - a1.astype(F32)).astype(BF16)
    b1 = b.astype(BF16)
    b2 = (b - b1.astype(F32)).astype(BF16)
    d = functools.partial(jnp.dot, preferred_element_type=F32)
    return d(a1, b1) + d(a1, b2) + d(a2, b1)


def _silu(x):
    return x * (1.0 / (1.0 + jnp.exp(-x)))


def _sigmoid(x):
    return 1.0 / (1.0 + jnp.exp(-x))


def _adaln(x, g, shift, scale):
    ms = jnp.mean(x * x, axis=-1, keepdims=True)
    return (x * lax.rsqrt(ms + NORM_EPS) * g) * (1.0 + scale) + shift


def _mod_kernel(c_ref, w_ref, b_ref, o_ref):
    o_ref[0] = _bdot(_silu(c_ref[...]), w_ref[0]) + b_ref[0]


def _modulation(c8, mod_w, mod_b):
    n = mod_w.shape[-1]
    tn = 1536
    return pl.pallas_call(
        _mod_kernel,
        grid=(DEPTH, n // tn),
        in_specs=[pl.BlockSpec((8, D_MODEL), lambda i, j: (0, 0)),
                  pl.BlockSpec((1, D_MODEL, tn), lambda i, j: (i, 0, j)),
                  pl.BlockSpec((1, 1, tn), lambda i, j: (i, 0, j))],
        out_specs=pl.BlockSpec((1, 8, tn), lambda i, j: (i, 0, j)),
        out_shape=jax.ShapeDtypeStruct((DEPTH, 8, n), F32),
        compiler_params=_cparams("parallel", "parallel"),
        name="modulation",
    )(c8, mod_w, mod_b.reshape(DEPTH, 1, n))


def _add_pe_kernel(x_ref, pe_ref, o_ref):
    o_ref[0] = x_ref[0] + pe_ref[...]


def _add_pe(x, pe):
    b, l, d = x.shape
    tl = min(l, 1024)
    return pl.pallas_call(
        _add_pe_kernel,
        grid=(l // tl, b),
        in_specs=[pl.BlockSpec((1, tl, d), lambda i, bb: (bb, i, 0)),
                  pl.BlockSpec((tl, d), lambda i, bb: (i, 0))],
        out_specs=pl.BlockSpec((1, tl, d), lambda i, bb: (bb, i, 0)),
        out_shape=jax.ShapeDtypeStruct(x.shape, F32),
        compiler_params=_cparams("parallel", "parallel"),
        name="add_pe",
    )(x, pe)


CONV_COLS = 512


def _adaln_mm_conv_kernel(ncv, act, x_ref, xp_ref, xn_ref, g_ref, sh_ref, sc_ref, w_ref, b_ref, cw_ref, cb_ref,
                          o_ref):
    i = pl.program_id(1)
    tl = x_ref.shape[1]
    rows = tl + 16
    xe = jnp.concatenate([xp_ref[0], x_ref[0], xn_ref[0]], axis=0)
    h = _adaln(xe, g_ref[...], sh_ref[0], sc_ref[0]).astype(BF16)
    n = w_ref.shape[1]
    r = lax.broadcasted_iota(jnp.int32, (tl, 1), 0)
    first = (r == 0) & (i == 0)
    last = (r == tl - 1) & (i == pl.num_programs(1) - 1)
    for c0 in range(0, n, CONV_COLS):
        cols = slice(c0, min(c0 + CONV_COLS, n))
        p = jnp.dot(h, w_ref[:, cols], preferred_element_type=F32) + b_ref[:, cols]
        mid = p[8:8 + tl]
        if c0 < ncv:
            prev = jnp.where(first, 0.0, pltpu.roll(p, 1, axis=0)[8:8 + tl])
            nxt = jnp.where(last, 0.0, pltpu.roll(p, rows - 1, axis=0)[8:8 + tl])
            cw = cw_ref[:, cols]
            mid = prev * cw[0:1] + mid * cw[1:2] + nxt * cw[2:3] + cb_ref[:, cols]
            if act:
                mid = _silu(mid)
        o_ref[0, :, cols] = mid.astype(o_ref.dtype)


def _adaln_mm_conv(x, g, shift, scale, w_bf16, bias, conv_w, conv_b, act):
    b, l, d = x.shape
    n = w_bf16.shape[1]
    ncv = conv_w.shape[1]
    assert ncv % CONV_COLS == 0
    tl = min(l, 512)
    t8 = tl // 8
    nblk8 = l // 8
    cst = lambda bb, i: (0, 0)
    vec = pl.BlockSpec((1, 1, d), lambda bb, i: (bb, 0, 0))
    return pl.pallas_call(
        functools.partial(_adaln_mm_conv_kernel, ncv, act),
        grid=(b, l // tl),
        in_specs=[pl.BlockSpec((1, tl, d), lambda bb, i: (bb, i, 0)),
                  pl.BlockSpec((1, 8, d), lambda bb, i: (bb, jnp.maximum(i * t8 - 1, 0), 0)),
                  pl.BlockSpec((1, 8, d), lambda bb, i: (bb, jnp.minimum((i + 1) * t8, nblk8 - 1), 0)),
                  pl.BlockSpec((1, d), cst), vec, vec,
                  pl.BlockSpec((d, n), cst), pl.BlockSpec((1, n), cst),
                  pl.BlockSpec((3, ncv), cst), pl.BlockSpec((1, ncv), cst)],
        out_specs=pl.BlockSpec((1, tl, n), lambda bb, i: (bb, i, 0)),
        out_shape=jax.ShapeDtypeStruct((b, l, n), BF16),
        compiler_params=_cparams("parallel", "parallel"),
        name="adaln_mm_conv",
    )(x, x, x, g, shift, scale, w_bf16, bias, conv_w, conv_b)


MLP_CHUNK = 512


def _mix_mlp_kernel(n_in, final_norm, *refs):
    x_ref, ga_ref, bo_ref = refs[0:3]
    y_refs = refs[3:3 + n_in]
    wo_refs = refs[3 + n_in:3 + 2 * n_in]
    g_ref, sh_ref, sc_ref, gf_ref, w1_ref, w2_ref, fg_ref, o_ref, a_scr = refs[3 + 2 * n_in:]
    mix = bo_ref[...] + _bdot(y_refs[0][0], wo_refs[0][...])
    for y_ref, wo_ref in zip(y_refs[1:], wo_refs[1:]):
        mix = mix + _bdot(y_ref[0], wo_ref[...])
    x1 = x_ref[0] + ga_ref[0] * mix
    h = _adaln(x1, g_ref[...], sh_ref[0], sc_ref[0]).astype(BF16)
    for c in range(w1_ref.shape[1] // MLP_CHUNK):
        cols = slice(c * MLP_CHUNK, (c + 1) * MLP_CHUNK)
        a = jnp.dot(h, w1_ref[:, cols], preferred_element_type=F32)
        a_scr[:, cols] = jnp.square(jnp.maximum(a, 0.0)).astype(BF16)
    y = x1 + gf_ref[0] * jnp.dot(a_scr[...], w2_ref[...], preferred_element_type=F32)
    if final_norm:
        ms = jnp.mean(y * y, axis=-1, keepdims=True)
        y = y * lax.rsqrt(ms + NORM_EPS) * fg_ref[...]
    o_ref[0] = y


def _mix_mlp(x, gate_a, out_b, ys, out_ws, g, shift, scale, gate_f, w1_bf16, w2_bf16, final_g, final_norm):
    b, l, d = x.shape
    hdim = w1_bf16.shape[1]
    tl = min(l, 512)
    n_in = len(ys)
    row = lambda bb, i: (bb, i, 0)
    vec = pl.BlockSpec((1, 1, d), lambda bb, i: (bb, 0, 0))
    cvec = pl.BlockSpec((1, d), lambda bb, i: (0, 0))
    resident = lambda a: pl.BlockSpec(a.shape, lambda bb, i: (0, 0), pipeline_mode=pl.Buffered(1))
    in_specs = [pl.BlockSpec((1, tl, d), row), vec, cvec]
    in_specs += [pl.BlockSpec((1, tl, y.shape[-1]), row) for y in ys]
    in_specs += [resident(w) for w in out_ws]
    in_specs += [cvec, vec, vec, vec, resident(w1_bf16), resident(w2_bf16), cvec]
    return pl.pallas_call(
        functools.partial(_mix_mlp_kernel, n_in, final_norm),
        grid=(b, l // tl),
        in_specs=in_specs,
        out_specs=pl.BlockSpec((1, tl, d), row),
        out_shape=jax.ShapeDtypeStruct(x.shape, F32),
        scratch_shapes=[pltpu.VMEM((tl, hdim), BF16)],
        compiler_params=_cparams("parallel", "parallel"),
        name="mix_mlp",
    )(x, gate_a, out_b, *ys, *out_ws, g, shift, scale, gate_f, w1_bf16, w2_bf16, final_g)


def _dwconv_kernel(act, x_ref, w_ref, b_ref, o_ref):
    x = x_ref[0].astype(F32)
    l = x.shape[0]
    t = lax.broadcasted_iota(jnp.int32, x.shape, 0)
    prev = jnp.where(t == 0, 0.0, pltpu.roll(x, 1, axis=0))
    nxt = jnp.where(t == l - 1, 0.0, pltpu.roll(x, l - 1, axis=0))
    w = w_ref[...]
    y = prev * w[0:1] + x * w[1:2] + nxt * w[2:3] + b_ref[...]
    if act:
        y = _silu(y)
    o_ref[0] = y.astype(o_ref.dtype)


def _dwconv(x, w, bias, act, col0=0, ncols=None):
    b, l, c = x.shape
    ncols = c if ncols is None else ncols
    tc = 256 if l > 1024 else 512
    j0 = col0 // tc
    return pl.pallas_call(
        functools.partial(_dwconv_kernel, act),
        grid=(b, ncols // tc),
        in_specs=[pl.BlockSpec((1, l, tc), lambda bb, j: (bb, 0, j0 + j)),
                  pl.BlockSpec((3, tc), lambda bb, j: (0, j0 + j)),
                  pl.BlockSpec((1, tc), lambda bb, j: (0, j0 + j))],
        out_specs=pl.BlockSpec((1, l, tc), lambda bb, j: (bb, 0, j)),
        out_shape=jax.ShapeDtypeStruct((b, l, ncols), BF16),
        compiler_params=_cparams("parallel", "parallel"),
        name="dwconv",
    )(x, w, bias)


def _s5_prepare(lam_re, lam_im, log_dt, b_re, b_im, c_re, c_im):
    step = jnp.exp(log_dt)[..., None]
    mag = jnp.exp(lam_re * step)
    ar = mag * jnp.cos(lam_im * step)
    ai = mag * jnp.sin(lam_im * step)
    den = lam_re * lam_re + lam_im * lam_im
    fr = ((ar - 1.0) * lam_re + ai * lam_im) / den
    fi = (ai * lam_re - (ar - 1.0) * lam_im) / den
    eye = jnp.eye(8, dtype=F32)

    def blockdiag_in(b):
        b4 = b.reshape(S5_SUPER, 8, S5_STATE, S5_GROUP_CH)
        return jnp.einsum('sgpk,gh->sgkhp', b4, eye).reshape(S5_SUPER, 128, 512)

    bd = jnp.concatenate([blockdiag_in(b_re), blockdiag_in(b_im)], axis=-1).astype(BF16)

    cpr = c_re[None] * fr[:, :, None, :] - c_im[None] * fi[:, :, None, :]
    cpi = c_re[None] * fi[:, :, None, :] + c_im[None] * fr[:, :, None, :]

    def blockdiag_out(c):
        c5 = c.reshape(2, S5_SUPER, 8, S5_GROUP_CH, S5_STATE)
        return jnp.einsum('dsgkp,gh->dsgphk', c5, eye).reshape(2, S5_SUPER, 512, 128)

    cd = jnp.concatenate([blockdiag_out(cpr), blockdiag_out(-cpi)], axis=2)
    cd = jnp.concatenate([cd[0], cd[1]], axis=-1).astype(BF16)
    rows = lambda a: jnp.repeat(a.reshape(2, S5_SUPER * 512), 4, axis=0)
    return bd, cd, rows(ar), rows(ai)


S5_ROWS = 256


def _s5_scan_kernel(tc, uf_ref, ub_ref, bd_ref, cd_ref, ar_ref, ai_ref, h0_ref, flip_ref,
                    yf_ref, yb_ref, hfin_ref, u_scr, g_scr, y_scr, h_scr):
    c = pl.program_id(0)

    @pl.when(c == 0)
    def _():
        h_scr[...] = h0_ref[...]

    flip = flip_ref[...]
    rows = 8 * tc
    for b in range(4):
        uf = uf_ref[b].astype(F32)
        ub = jnp.dot(flip, ub_ref[b].astype(BF16), preferred_element_type=F32)
        for sg in range(S5_SUPER):
            u_scr[sg, pl.ds(b, tc, stride=8), :] = uf[:, sg * 128:(sg + 1) * 128]
            u_scr[sg, pl.ds(4 + b, tc, stride=8), :] = ub[:, sg * 128:(sg + 1) * 128]

    half = S5_SLAB // 256
    for sg in range(S5_SUPER):
        for r0 in range(0, rows, S5_ROWS):
            g_scr[r0:r0 + S5_ROWS, sg * S5_SLAB:(sg + 1) * S5_SLAB] = jnp.dot(
                u_scr[sg, r0:r0 + S5_ROWS, :].astype(BF16), bd_ref[sg], preferred_element_type=F32)

        ar = [ar_ref[:, (sg * half + k) * 128:(sg * half + k + 1) * 128] for k in range(half)]
        ai = [ai_ref[:, (sg * half + k) * 128:(sg * half + k + 1) * 128] for k in range(half)]
        l_re = [sg * S5_SLAB + k * 128 for k in range(half)]
        l_im = [sg * S5_SLAB + (half + k) * 128 for k in range(half)]
        hr = [h_scr[:, l_re[k]:l_re[k] + 128] for k in range(half)]
        hi = [h_scr[:, l_im[k]:l_im[k] + 128] for k in range(half)]
        for j in range(tc):
            r0 = j * 8
            for k in range(half):
                nr = ar[k] * hr[k] - ai[k] * hi[k] + g_scr[r0:r0 + 8, l_re[k]:l_re[k] + 128]
                ni = ar[k] * hi[k] + ai[k] * hr[k] + g_scr[r0:r0 + 8, l_im[k]:l_im[k] + 128]
                g_scr[r0:r0 + 8, l_re[k]:l_re[k] + 128] = nr
                g_scr[r0:r0 + 8, l_im[k]:l_im[k] + 128] = ni
                hr[k], hi[k] = nr, ni
        for k in range(half):
            h_scr[:, l_re[k]:l_re[k] + 128] = hr[k]
            h_scr[:, l_im[k]:l_im[k] + 128] = hi[k]

        for r0 in range(0, rows, S5_ROWS):
            y = jnp.dot(g_scr[r0:r0 + S5_ROWS, sg * S5_SLAB:(sg + 1) * S5_SLAB].astype(BF16), cd_ref[sg],
                        preferred_element_type=F32)
            y_scr[2 * sg, r0:r0 + S5_ROWS, :] = y[:, :128]
            y_scr[2 * sg + 1, r0:r0 + S5_ROWS, :] = y[:, 128:]

    for b in range(4):
        yf_ref[b] = jnp.concatenate(
            [y_scr[2 * sg, pl.ds(b, tc, stride=8), :] for sg in range(S5_SUPER)], axis=1)
        yb = jnp.concatenate(
            [y_scr[2 * sg + 1, pl.ds(4 + b, tc, stride=8), :] for sg in range(S5_SUPER)], axis=1)
        yb_ref[b] = _dot_exact_lhs(flip, yb)

    hfin_ref[...] = h_scr[...]


def _s5_scan(u, ucol, bd, cd, ar8, ai8, h0):
    b, l, _ = u.shape
    w = S5_WIDTH
    assert b == 4
    tc = 128
    nc = l // tc
    flip = jnp.asarray(np.eye(tc, dtype=np.float32)[::-1], dtype=BF16)
    full = lambda shape: pl.BlockSpec(shape, lambda c: (0,) * len(shape))
    y_shape = jax.ShapeDtypeStruct((b, l, w), F32)
    return pl.pallas_call(
        functools.partial(_s5_scan_kernel, tc),
        grid=(nc,),
        in_specs=[pl.BlockSpec((4, tc, w), lambda c: (0, c, ucol)),
                  pl.BlockSpec((4, tc, w), lambda c: (0, nc - 1 - c, ucol)),
                  full(bd.shape), full(cd.shape), full(ar8.shape), full(ai8.shape),
                  full(h0.shape), full(flip.shape)],
        out_specs=[pl.BlockSpec((4, tc, w), lambda c: (0, c, 0)),
                   pl.BlockSpec((4, tc, w), lambda c: (0, nc - 1 - c, 0)),
                   full((8, S5_LANES))],
        out_shape=[y_shape, y_shape, jax.ShapeDtypeStruct((8, S5_LANES), F32)],
        scratch_shapes=[pltpu.VMEM((S5_SUPER, 8 * tc, 128), F32), pltpu.VMEM((8 * tc, S5_LANES), F32),
                        pltpu.VMEM((2 * S5_SUPER, 8 * tc, 128), F32), pltpu.VMEM((8, S5_LANES), F32)],
        compiler_params=_cparams("arbitrary"),
        name="s5_scan",
    )(u, u, bd, cd, ar8, ai8, h0, flip)


def _gelu_tanh(x):
    return 0.5 * x * (1.0 + jnp.tanh(math.sqrt(2.0 / math.pi) * (x + 0.044715 * (x * x * x))))


def _s5_glu_kernel(yf_ref, yb_ref, u_ref, d_ref, w_ref, b_ref, o_ref):
    y = _gelu_tanh(yf_ref[0] + yb_ref[0] + d_ref[...] * u_ref[0].astype(F32))
    o_ref[0] = (y * _sigmoid(_bdot(y, w_ref[...]) + b_ref[...])).astype(o_ref.dtype)


def _s5_glu(yf, yb, u, ucol, d_skip, glu_w_bf16, glu_b):
    b, l, w = yf.shape
    tl = min(l, 1024)
    row = lambda bb, i: (bb, i, 0)
    cst = lambda bb, i: (0, 0)
    return pl.pallas_call(
        _s5_glu_kernel,
        grid=(b, l // tl),
        in_specs=[pl.BlockSpec((1, tl, w), row)] * 2 + [
            pl.BlockSpec((1, tl, w), lambda bb, i: (bb, i, ucol)),
            pl.BlockSpec((1, w), cst), pl.BlockSpec((w, w), cst), pl.BlockSpec((1, w), cst)],
        out_specs=pl.BlockSpec((1, tl, w), row),
        out_shape=jax.ShapeDtypeStruct(yf.shape, BF16),
        compiler_params=_cparams("parallel", "parallel"),
        name="s5_glu",
    )(yf, yb, u, d_skip, glu_w_bf16, glu_b)


SSD_T = 128
SSD_NB = 4


def _softplus(x):
    return jnp.maximum(x, 0.0) + jnp.log(1.0 + jnp.exp(-jnp.abs(x)))


def _ssd_kernel(xs_ref, bm_ref, cm_ref, dtc_ref, dtr_ref, bias_c_ref, bias_r_ref, a_c_ref, a_r_ref,
                tri_ref, trit_ref, exp_ref, h0_ref, y_ref, hfin_ref, s_scr):
    c = pl.program_id(2)

    @pl.when(c == 0)
    def _():
        s_scr[...] = h0_ref[0]

    tri = tri_ref[0]
    expand = exp_ref[...]
    mask = tri > 0
    for bi in range(SSD_NB):
        dt_c = _softplus(dtc_ref[0, bi] + bias_c_ref[0])
        dt_r = _softplus(dtr_ref[0, bi] + bias_r_ref[0])
        cum_c = _dot_exact_lhs(tri, dt_c * a_c_ref[0])
        cum_r = _dot_exact_rhs(dt_r * a_r_ref[0], trit_ref[0])
        tot_c = jnp.min(cum_c, axis=0, keepdims=True)

        t = dt_c.shape[0]
        ex = _dot_split2_rhs(jnp.concatenate([dt_c, jnp.exp(cum_c), jnp.exp(tot_c - cum_c)], axis=0), expand)
        dt_x = ex[:t]
        in_x = ex[t:2 * t]
        out_x = ex[2 * t:]
        tot_x = jnp.where(pl.program_id(0) == 0, in_x[t - 1:t], in_x[0:1])

        xdt = xs_ref[bi].astype(F32) * dt_x
        xout = (xdt * out_x).astype(BF16)
        xdt = xdt.astype(BF16)
        for g in range(M2_GROUPS):
            bg = bm_ref[bi, :, g * M2_STATE:(g + 1) * M2_STATE].astype(BF16)
            cg = cm_ref[bi, :, g * M2_STATE:(g + 1) * M2_STATE].astype(BF16)
            cb = lax.dot_general(cg, bg, (((1,), (1,)), ((), ())), preferred_element_type=F32)
            gc = slice(g * M2_GROUP_W, (g + 1) * M2_GROUP_W)
            s_prev = s_scr[bi, g]
            y_off = jnp.dot(cg, s_prev.astype(BF16), preferred_element_type=F32) * in_x[:, gc]
            y_heads = []
            for r in range(M2_HEADS // M2_GROUPS):
                h = g * (M2_HEADS // M2_GROUPS) + r
                seg = cum_c[:, h:h + 1] - cum_r[h:h + 1, :]
                m = (cb * jnp.exp(jnp.where(mask, seg, -jnp.inf))).astype(BF16)
                hc = slice(h * M2_HEAD_DIM, (h + 1) * M2_HEAD_DIM)
                y_heads.append(jnp.dot(m, xdt[:, hc], preferred_element_type=F32))
            y_ref[0, bi, :, gc] = (jnp.concatenate(y_heads, axis=1) + y_off).astype(y_ref.dtype)
            s_scr[bi, g] = s_prev * tot_x[:, gc] + lax.dot_general(
                bg, xout[:, gc], (((0,), (0,)), ((), ())), preferred_element_type=F32)

    @pl.when(c == pl.num_programs(2) - 1)
    def _():
        hfin_ref[0] = s_scr[...]


def _ssd(xbc, dt_col, dt_row, dt_bias, a_coef, h0):
    b, l, _ = xbc.shape
    t = SSD_T
    nc = l // t
    tri_f = np.tril(np.ones((t, t), np.float32))
    tri = jnp.asarray(np.stack([tri_f, tri_f.T]), dtype=BF16)
    trit = jnp.asarray(np.stack([tri_f.T, tri_f]), dtype=BF16)
    expand = jnp.asarray(np.kron(np.eye(M2_HEADS, dtype=np.float32),
                                 np.ones((1, M2_HEAD_DIM), np.float32)), dtype=BF16)
    chunk = lambda d, c: c + d * (nc - 1 - 2 * c)
    nb = SSD_NB
    state = pl.BlockSpec((1, nb, M2_GROUPS, M2_STATE, M2_GROUP_W), lambda d, bb, c: (d, bb, 0, 0, 0))
    return pl.pallas_call(
        _ssd_kernel,
        grid=(2, b // nb, nc),
        in_specs=[pl.BlockSpec((nb, t, M2_INNER), lambda d, bb, c: (bb, chunk(d, c), 0)),
                  pl.BlockSpec((nb, t, M2_BC), lambda d, bb, c: (bb, chunk(d, c), 2)),
                  pl.BlockSpec((nb, t, M2_BC), lambda d, bb, c: (bb, chunk(d, c), 3)),
                  pl.BlockSpec((1, nb, t, M2_HEADS), lambda d, bb, c: (d, bb, chunk(d, c), 0)),
                  pl.BlockSpec((1, nb, M2_HEADS, t), lambda d, bb, c: (d, bb, 0, chunk(d, c))),
                  pl.BlockSpec((1, 1, M2_HEADS), lambda d, bb, c: (d, 0, 0)),
                  pl.BlockSpec((1, M2_HEADS, 1), lambda d, bb, c: (d, 0, 0)),
                  pl.BlockSpec((1, 1, M2_HEADS), lambda d, bb, c: (d, 0, 0)),
                  pl.BlockSpec((1, M2_HEADS, 1), lambda d, bb, c: (d, 0, 0)),
                  pl.BlockSpec((1, t, t), lambda d, bb, c: (d, 0, 0)),
                  pl.BlockSpec((1, t, t), lambda d, bb, c: (d, 0, 0)),
                  pl.BlockSpec((M2_HEADS, M2_INNER), lambda d, bb, c: (0, 0)),
                  state],
        out_specs=[pl.BlockSpec((1, nb, t, M2_INNER), lambda d, bb, c: (d, bb, chunk(d, c), 0)), state],
        out_shape=[jax.ShapeDtypeStruct((2, b, l, M2_INNER), BF16),
                   jax.ShapeDtypeStruct((2, b, M2_GROUPS, M2_STATE, M2_GROUP_W), F32)],
        scratch_shapes=[pltpu.VMEM((nb, M2_GROUPS, M2_STATE, M2_GROUP_W), F32)],
        compiler_params=_cparams("parallel", "parallel", "arbitrary"),
        name="ssd",
    )(xbc, xbc, xbc, dt_col, dt_row, dt_bias.reshape(2, 1, M2_HEADS), dt_bias.reshape(2, M2_HEADS, 1),
      a_coef.reshape(2, 1, M2_HEADS), a_coef.reshape(2, M2_HEADS, 1), tri, trit, expand, h0)


def _m2_gate_kernel(yf_ref, yb_ref, xs_ref, z_ref, d_ref, g_ref, o_ref):
    y = (yf_ref[0, 0].astype(F32) + yb_ref[0, 0].astype(F32)
         + d_ref[...] * xs_ref[0].astype(F32)) * _silu(z_ref[0].astype(F32))
    ms = jnp.mean(y * y, axis=-1, keepdims=True)
    o_ref[0] = (y * lax.rsqrt(ms + NORM_EPS) * g_ref[...]).astype(o_ref.dtype)


def _m2_gate(ydir, xbc, z, zcol, d_x, norm_g):
    _, b, l, w = ydir.shape
    tl = min(l, 512)
    row = lambda bb, i: (bb, i, 0)
    cst = lambda bb, i: (0, 0)
    return pl.pallas_call(
        _m2_gate_kernel,
        grid=(b, l // tl),
        in_specs=[pl.BlockSpec((1, 1, tl, w), lambda bb, i: (0, bb, i, 0)),
                  pl.BlockSpec((1, 1, tl, w), lambda bb, i: (1, bb, i, 0)),
                  pl.BlockSpec((1, tl, w), row), pl.BlockSpec((1, tl, w), lambda bb, i: (bb, i, zcol)),
                  pl.BlockSpec((1, w), cst), pl.BlockSpec((1, w), cst)],
        out_specs=pl.BlockSpec((1, tl, w), row),
        out_shape=jax.ShapeDtypeStruct((b, l, w), BF16),
        compiler_params=_cparams("parallel", "parallel"),
        name="m2_gate",
    )(ydir, ydir, xbc, z, d_x, norm_g)


HY_N2 = 128


def _round8(n):
    return (n + 7) // 8 * 8


def _dft2_tables(n1, n2):
    n = n1 * n2
    k1n = n1 // 2 + 1
    k1p = _round8(k1n)
    k1 = np.arange(k1p, dtype=np.float64)[:, None]
    valid = (k1 < k1n).astype(np.float64)
    i1 = np.arange(n1, dtype=np.float64)[None, :]
    t = n2 * i1[None] + np.arange(n2, dtype=np.float64)[:, None, None]
    th = 2.0 * np.pi * k1[None] * t / n
    stage1_tw = np.concatenate([np.cos(th) * valid[None], -np.sin(th) * valid[None]], axis=1)
    i2 = np.arange(n2, dtype=np.float64)
    k2 = np.arange(n2, dtype=np.float64)[:, None]
    th2 = 2.0 * np.pi * k2 * i2[None, :] / n2
    fc, fs = np.cos(th2), np.sin(th2)
    fwd2 = np.block([[fc, fs], [-fs, fc]])
    inv2 = np.block([[fc, -fs], [fs, fc]])
    wk = (np.where((k1 == 0) | (k1 == n1 // 2), 1.0, 2.0) * valid).T[None]
    tho = np.transpose(th[:, :, :n1 // 2], (0, 2, 1))
    last_tw = np.concatenate([np.cos(tho) * wk, -np.sin(tho) * wk], axis=2) / n
    f = lambda a: jnp.asarray(a, dtype=F32)
    return dict(k1n=k1n, k1p=k1p, stage1_tw=f(stage1_tw), fwd2=f(fwd2), inv2=f(inv2), last_tw=f(last_tw))


def _dft1_tables(l):
    n = 2 * l
    kn = l + 1
    kp = _round8(kn)
    k = np.arange(kp, dtype=np.float64)[:, None]
    valid = (k < kn).astype(np.float64)
    t = np.arange(n, dtype=np.float64)[None, :]
    th = 2.0 * np.pi * k * t / n
    fwd = np.concatenate([np.cos(th) * valid, -np.sin(th) * valid], axis=0)
    wk = np.where((k == 0) | (k == l), 1.0, 2.0) * valid
    o = np.arange(l, dtype=np.float64)[:, None]
    tho = 2.0 * np.pi * o * k.T / n
    inv = np.concatenate([np.cos(tho) * wk.T, -np.sin(tho) * wk.T], axis=1) / n
    f = lambda a: jnp.asarray(a, dtype=F32)
    return dict(kp=kp, fwd=f(fwd), inv=f(inv))


def _rowdft_kernel(hi, f_ref, x_ref, o_ref):
    dot = _dot3 if hi else _bdot
    o_ref[0] = dot(f_ref[...], x_ref[0])


def _rowdft(fmat, x, hi):
    bx, r, n = x.shape
    m = fmat.shape[0]
    tn = min(n, 4096)
    return pl.pallas_call(
        functools.partial(_rowdft_kernel, hi),
        grid=(bx, n // tn),
        in_specs=[pl.BlockSpec((m, r), lambda b, j: (0, 0)),
                  pl.BlockSpec((1, r, tn), lambda b, j: (b, 0, j))],
        out_specs=pl.BlockSpec((1, m, tn), lambda b, j: (b, 0, j)),
        out_shape=jax.ShapeDtypeStruct((bx, m, n), F32),
        compiler_params=_cparams("parallel", "parallel"),
        name="hy_rowdft",
    )(fmat if hi else fmat.astype(BF16), x)


HY_TS = 8


HY_PITCH = HY_N2 + 8
HY_CONV_VMEM_BYTES = 58 * 1024 * 1024
HY_STAGE2_UNROLL = 9


def _hy_conv_kernel(k1n, k1p, n2, v_ref, gate_ref, h_ref, st1_ref, fwd_ref, inv_ref, last_ref,
                    d_ref, o_ref, a_scr, t_scr):
    n1h = v_ref.shape[0] // n2

    def copy_in(i, _):
        t_scr[pl.ds(pl.multiple_of(i * HY_PITCH, 8), n2), :] = (
            v_ref[pl.ds(pl.multiple_of(i * n2, n2), n2), :].astype(F32))
        return 0
    lax.fori_loop(0, n1h, copy_in, 0)

    zero = jnp.zeros((n1h, 128), BF16)

    def stage1(j, _):
        s = 2 * j
        xa = t_scr[pl.ds(s, n1h, stride=HY_PITCH), :].astype(BF16)
        xb = t_scr[pl.ds(s + 1, n1h, stride=HY_PITCH), :].astype(BF16)
        xs = jnp.concatenate([jnp.concatenate([xa, zero], axis=1), jnp.concatenate([zero, xb], axis=1)], axis=0)
        y = jnp.dot(st1_ref[j], xs, preferred_element_type=F32)
        a_scr[pl.ds(s, 2 * k1p, stride=HY_PITCH), :] = y[:, :128]
        a_scr[pl.ds(s + 1, 2 * k1p, stride=HY_PITCH), :] = y[:, 128:]
        return 0
    lax.fori_loop(0, n2 // 2, stage1, 0, unroll=16)

    def cmul(ar, ai, br, bi):
        return ar * br - ai * bi, ar * bi + ai * br

    def stage2(j, _):
        ks = (2 * j, 2 * j + 1)
        rows_r = [pl.ds(pl.multiple_of(k * HY_PITCH, 8), n2) for k in ks]
        rows_i = [pl.ds(pl.multiple_of((k1p + k) * HY_PITCH, 8), n2) for k in ks]
        p = jnp.concatenate([jnp.concatenate([a_scr[rr, :] for rr in rows_r], axis=1),
                             jnp.concatenate([a_scr[ri, :] for ri in rows_i], axis=1)], axis=0)
        x = jnp.dot(fwd_ref[...], p.astype(BF16), preferred_element_type=F32)
        hr = jnp.concatenate([h_ref[0, ks[0]], h_ref[0, ks[1]]], axis=1).astype(F32)
        hi = jnp.concatenate([h_ref[1, ks[0]], h_ref[1, ks[1]]], axis=1).astype(F32)
        qr, qi = cmul(x[:n2], x[n2:], hr, hi)
        z = jnp.dot(inv_ref[...], jnp.concatenate([qr, qi], axis=0).astype(BF16), preferred_element_type=F32)
        for n, (rr, ri) in enumerate(zip(rows_r, rows_i)):
            a_scr[rr, :] = z[:n2, n * 128:(n + 1) * 128]
            a_scr[ri, :] = z[n2:, n * 128:(n + 1) * 128]
        return 0

    pairs = k1p // 2
    unroll = max(u for u in range(1, HY_STAGE2_UNROLL + 1) if pairs % u == 0)
    lax.fori_loop(0, pairs, stage2, 0, unroll=unroll)

    def stage3(j, _):
        s = 2 * j
        zs = jnp.concatenate([a_scr[pl.ds(s, 2 * k1p, stride=HY_PITCH), :],
                              a_scr[pl.ds(s + 1, 2 * k1p, stride=HY_PITCH), :]], axis=1).astype(BF16)
        y = jnp.dot(last_ref[j], zs, preferred_element_type=F32)
        t_scr[pl.ds(s, n1h, stride=HY_PITCH), :] = y[:n1h, :128]
        t_scr[pl.ds(s + 1, n1h, stride=HY_PITCH), :] = y[n1h:, 128:]
        return 0
    lax.fori_loop(0, n2 // 2, stage3, 0, unroll=16)

    def finish(i, _):
        rows = pl.ds(pl.multiple_of(i * n2, n2), n2)
        y = t_scr[pl.ds(pl.multiple_of(i * HY_PITCH, 8), n2), :]
        o_ref[rows, :] = (gate_ref[rows, :].astype(F32)
                          * (y + d_ref[...] * v_ref[rows, :].astype(F32))).astype(o_ref.dtype)
        return 0
    lax.fori_loop(0, n1h, finish, 0)


def _hy_long_conv2(v, vpart, gate, gpart, hspec, order, d_vec, tab):
    b, l, _ = v.shape
    c = HY_WIDTH
    n2 = HY_N2
    n1h = l // n2
    k1p = tab['k1p']
    tiles = c // 128
    seq = pl.BlockSpec((None, l, 128), lambda j, bb: (bb, 0, j))
    vseq = pl.BlockSpec((None, l, 128), lambda j, bb: (bb, 0, vpart * tiles + j))
    gseq = pl.BlockSpec((None, l, 128), lambda j, bb: (bb, 0, gpart * tiles + j))
    resident = lambda a: pl.BlockSpec(a.shape, lambda j, bb: (0,) * a.ndim, pipeline_mode=pl.Buffered(1))
    s1 = tab['stage1_tw'][:, :, :n1h]
    st1 = jnp.concatenate([s1[0::2], s1[1::2]], axis=2).astype(BF16)
    lt = tab['last_tw']
    last = jnp.concatenate([lt[0::2], lt[1::2]], axis=1).astype(BF16)
    fwd2, inv2 = tab['fwd2'].astype(BF16), tab['inv2'].astype(BF16)
    return pl.pallas_call(
        functools.partial(_hy_conv_kernel, tab['k1n'], k1p, n2),
        grid=(tiles, b),
        in_specs=[vseq, gseq,
                  pl.BlockSpec((2, k1p, n2, 128), lambda j, bb: (0, 0, 0, order * tiles + j)),
                  resident(st1), resident(fwd2), resident(inv2), resident(last),
                  pl.BlockSpec((1, 128), lambda j, bb: (0, j))],
        out_specs=seq,
        out_shape=jax.ShapeDtypeStruct((b, l, c), BF16),
        scratch_shapes=[pltpu.VMEM((2 * k1p * HY_PITCH, 128), F32), pltpu.VMEM((n1h * HY_PITCH, 128), F32)],
        compiler_params=pltpu.CompilerParams(dimension_semantics=("parallel", "parallel"),
                                             vmem_limit_bytes=HY_CONV_VMEM_BYTES),
        name="hy_conv",
    )(v, gate, hspec, st1, fwd2, inv2, last, d_vec.reshape(1, c))


def _hy_filt_first_kernel(k1p, feat_ref, t_ref, keep_ref, w1_ref, b1_ref, q1_ref, w2_ref, b2_ref, q2_ref,
                          w3_ref, nd_ref, f_ref, o_ref, sum_ref):
    @pl.when(pl.program_id(0) == 0)
    def _():
        sum_ref[...] = jnp.zeros_like(sum_ref)

    n1 = feat_ref.shape[1]
    half = n1 // 2
    feats = feat_ref[...].reshape(HY_TS * n1, feat_ref.shape[2])
    hid = jnp.sin(q1_ref[...] * (_bdot(feats, w1_ref[...]) + b1_ref[...]))
    hid = jnp.sin(q2_ref[...] * (_bdot(hid, w2_ref[...]) + b2_ref[...]))
    acc = jnp.zeros(sum_ref.shape, F32)
    for s in range(HY_TS):
        hs = hid[s * n1:(s + 1) * n1]
        f = jnp.concatenate([_bdot(hs[:half], w3_ref[0]), _bdot(hs[half:], w3_ref[1])], axis=0)
        f = f * jnp.exp(t_ref[s] * nd_ref[...]) * keep_ref[s]
        acc = acc + jnp.sum(jnp.abs(f), axis=0, keepdims=True)
        y = _dot3(f_ref[s], f)
        o_ref[0, :, s, :] = y[:k1p]
        o_ref[1, :, s, :] = y[k1p:]
    sum_ref[...] += acc


def _hy_filt_first(l, tab, f_w1, f_b1, f_freq1, f_w2, f_b2, f_freq2, f_w3):
    c2 = HY_ORDER * HY_WIDTH
    n2 = HY_N2
    n1 = 2 * l // n2
    k1p = tab['k1p']
    t = np.linspace(0.0, 1.0, l, dtype=np.float32)[:, None]
    bands = np.linspace(1e-4, HY_BANDS - 1, HY_BANDS, dtype=np.float32)
    ang = np.float32(2.0 * math.pi / l) * np.arange(l, dtype=np.float32)[:, None] * bands
    feats = np.concatenate([t, np.cos(ang), -np.sin(ang)], axis=-1)
    rows = (n2 * np.arange(n1)[None, :] + np.arange(n2)[:, None]).reshape(-1)
    pos = np.where(rows < l, rows, np.where(rows == l, 0, 2 * l - rows))
    feats_t = jnp.asarray(np.pad(feats[pos], ((0, 0), (0, 128 - HY_EMB))).reshape(n2, n1, 128))
    t_t = jnp.asarray(t[pos].reshape(n2, n1, 1))
    keep_t = jnp.asarray((rows != l).astype(np.float32).reshape(n2, n1, 1))
    w1p = jnp.pad(f_w1, ((0, 128 - HY_EMB), (0, 0)))
    w3d = jnp.transpose(f_w3.reshape(HY_HIDDEN, HY_ORDER, 2, HY_WIDTH), (2, 0, 1, 3)).reshape(2, HY_HIDDEN, c2)
    deltas = jnp.abs(jnp.linspace(HY_MIN_DECAY, HY_MAX_DECAY, HY_WIDTH, dtype=F32))
    negd = jnp.tile(-deltas, HY_ORDER)[None, :]
    cst = lambda i: (0, 0)
    vec = lambda a: a.reshape(1, HY_HIDDEN)
    tile3 = lambda w: pl.BlockSpec((HY_TS, n1, w), lambda i: (i, 0, 0))
    return pl.pallas_call(
        functools.partial(_hy_filt_first_kernel, k1p),
        grid=(n2 // HY_TS,),
        in_specs=[tile3(128), tile3(1), tile3(1),
                  pl.BlockSpec((128, HY_HIDDEN), cst), pl.BlockSpec((1, HY_HIDDEN), cst),
                  pl.BlockSpec((1, HY_HIDDEN), cst),
                  pl.BlockSpec((HY_HIDDEN, HY_HIDDEN), cst), pl.BlockSpec((1, HY_HIDDEN), cst),
                  pl.BlockSpec((1, HY_HIDDEN), cst),
                  pl.BlockSpec((2, HY_HIDDEN, c2), lambda i: (0, 0, 0)),
                  pl.BlockSpec((1, c2), cst),
                  pl.BlockSpec((HY_TS, 2 * k1p, n1), lambda i: (i, 0, 0))],
        out_specs=[pl.BlockSpec((None, 2, k1p, HY_TS, c2), lambda i: (0, 0, 0, i, 0)),
                   pl.BlockSpec((1, c2), cst)],
        out_shape=[jax.ShapeDtypeStruct((1, 2, k1p, n2, c2), F32), jax.ShapeDtypeStruct((1, c2), F32)],
        compiler_params=_cparams("arbitrary"),
        name="hy_filt_first",
    )(feats_t, t_t, keep_t, w1p, vec(f_b1), vec(f_freq1), f_w2, vec(f_b2), vec(f_freq2), w3d, negd,
      tab['stage1_tw'])


def _hy_filt_mid_kernel(n2, a_ref, sum_ref, fwd_ref, o_ref):
    c2 = a_ref.shape[-1]
    x = _dot3(fwd_ref[...], a_ref[0, :, 0].reshape(2 * n2, c2)) * (1.0 / (sum_ref[...] + 1e-6))
    o_ref[0, 0] = x[:n2].astype(BF16)
    o_ref[1, 0] = x[n2:].astype(BF16)


def _hy_filt_mid(a5, tab, colsum):
    _, _, k1p, n2, c2 = a5.shape
    return pl.pallas_call(
        functools.partial(_hy_filt_mid_kernel, n2),
        grid=(k1p,),
        in_specs=[pl.BlockSpec((1, 2, 1, n2, c2), lambda k: (0, 0, k, 0, 0)),
                  pl.BlockSpec((1, c2), lambda k: (0, 0)),
                  pl.BlockSpec((2 * n2, 2 * n2), lambda k: (0, 0))],
        out_specs=pl.BlockSpec((2, 1, n2, c2), lambda k: (0, k, 0, 0)),
        out_shape=jax.ShapeDtypeStruct((2, k1p, n2, c2), BF16),
        compiler_params=_cparams("parallel"),
        name="hy_filt_mid",
    )(a5, colsum, tab['fwd2'])


def _hy_ctx_kernel(kp, fw_ref, inv_ref, h_ref, v_ref, gate_ref, d_ref, o_ref):
    v = v_ref[0].astype(F32)
    u = _bdot(fw_ref[...], v)
    ur, ui = u[:kp], u[kp:]
    hr, hi = h_ref[0], h_ref[1]
    q = jnp.concatenate([ur * hr - ui * hi, ur * hi + ui * hr], axis=0)
    o_ref[0] = (gate_ref[0].astype(F32) * (_bdot(inv_ref[...], q) + d_ref[...] * v)).astype(o_ref.dtype)


def _hy_long_conv1(v, vpart, gate, gpart, hspec, order, d_vec, tab):
    b, l, _ = v.shape
    c = HY_WIDTH
    kp = tab['kp']
    row = lambda bb: (bb, 0, 0)
    return pl.pallas_call(
        functools.partial(_hy_ctx_kernel, kp),
        grid=(b,),
        in_specs=[pl.BlockSpec((2 * kp, l), lambda bb: (0, 0)),
                  pl.BlockSpec((l, 2 * kp), lambda bb: (0, 0)),
                  pl.BlockSpec((2, kp, c), lambda bb: (0, 0, order)),
                  pl.BlockSpec((1, l, c), lambda bb: (bb, 0, vpart)),
                  pl.BlockSpec((1, l, c), lambda bb: (bb, 0, gpart)),
                  pl.BlockSpec((1, c), lambda bb: (0, 0))],
        out_specs=pl.BlockSpec((1, l, c), row),
        out_shape=jax.ShapeDtypeStruct((b, l, c), BF16),
        compiler_params=_cparams("parallel"),
        name="hy_ctx_conv",
    )(tab['fwd'][:, :l].astype(BF16), tab['inv'].astype(BF16), hspec, v, gate, d_vec.reshape(1, c))


def _hy_filter_kernel(feat_ref, t_ref, keep_ref, w1_ref, b1_ref, q1_ref, w2_ref, b2_ref, q2_ref,
                      w3_ref, nd_ref, f_ref, sum_ref):
    @pl.when(pl.program_id(0) == 0)
    def _():
        sum_ref[...] = jnp.zeros_like(sum_ref)

    hid = jnp.sin(q1_ref[...] * (_bdot(feat_ref[...], w1_ref[...]) + b1_ref[...]))
    hid = jnp.sin(q2_ref[...] * (_bdot(hid, w2_ref[...]) + b2_ref[...]))
    f = _bdot(hid, w3_ref[0]) * jnp.exp(t_ref[...] * nd_ref[...]) * keep_ref[...]
    f_ref[...] = f
    sum_ref[...] += jnp.sum(jnp.abs(f), axis=0, keepdims=True)


def _hy_filters(l, f_w1, f_b1, f_freq1, f_w2, f_b2, f_freq2, f_w3):
    c2 = HY_ORDER * HY_WIDTH
    t = np.linspace(0.0, 1.0, l, dtype=np.float32)[:, None]
    bands = np.linspace(1e-4, HY_BANDS - 1, HY_BANDS, dtype=np.float32)
    ang = np.float32(2.0 * math.pi / l) * np.arange(l, dtype=np.float32)[:, None] * bands
    feats = np.concatenate([t, np.cos(ang), -np.sin(ang)], axis=-1)
    rows = np.arange(2 * l)
    pos = np.where(rows < l, rows, np.where(rows == l, 0, 2 * l - rows))
    feats2 = jnp.asarray(np.pad(feats[pos], ((0, 0), (0, 128 - HY_EMB))))
    keep = jnp.asarray((rows != l).astype(np.float32))[:, None]
    t2 = jnp.asarray(t[pos])
    w1p = jnp.pad(f_w1, ((0, 128 - HY_EMB), (0, 0)))
    w3d = jnp.transpose(f_w3.reshape(HY_HIDDEN, HY_ORDER, 2, HY_WIDTH), (2, 0, 1, 3)).reshape(2, HY_HIDDEN, c2)
    deltas = jnp.abs(jnp.linspace(HY_MIN_DECAY, HY_MAX_DECAY, HY_WIDTH, dtype=F32))
    negd = jnp.tile(-deltas, HY_ORDER)[None, :]
    tr = min(l, 512)
    per_dir = l // tr
    cst = lambda i: (0, 0)
    vec = lambda a: a.reshape(1, HY_HIDDEN)
    return pl.pallas_call(
        _hy_filter_kernel,
        grid=(2 * l // tr,),
        in_specs=[pl.BlockSpec((tr, 128), lambda i: (i, 0)),
                  pl.BlockSpec((tr, 1), lambda i: (i, 0)),
                  pl.BlockSpec((tr, 1), lambda i: (i, 0)),
                  pl.BlockSpec((128, HY_HIDDEN), cst), pl.BlockSpec((1, HY_HIDDEN), cst),
                  pl.BlockSpec((1, HY_HIDDEN), cst),
                  pl.BlockSpec((HY_HIDDEN, HY_HIDDEN), cst), pl.BlockSpec((1, HY_HIDDEN), cst),
                  pl.BlockSpec((1, HY_HIDDEN), cst),
                  pl.BlockSpec((1, HY_HIDDEN, c2), lambda i: (i // per_dir, 0, 0)),
                  pl.BlockSpec((1, c2), cst)],
        out_specs=[pl.BlockSpec((tr, c2), lambda i: (i, 0)), pl.BlockSpec((1, c2), cst)],
        out_shape=[jax.ShapeDtypeStruct((2 * l, c2), F32), jax.ShapeDtypeStruct((1, c2), F32)],
        compiler_params=_cparams("arbitrary"),
        name="hy_filter",
    )(feats2, t2, keep, w1p, vec(f_b1), vec(f_freq1), f_w2, vec(f_b2), vec(f_freq2), w3d, negd)


def _col_scale_kernel(x_ref, sum_ref, o_ref):
    o_ref[...] = x_ref[...] * (1.0 / (sum_ref[...] + 1e-6))


def _hy_spectrum(l, fparams):
    if 2 * l >= 16 * HY_N2:
        tab = _dft2_tables(2 * l // HY_N2, HY_N2)
        a, colsum = _hy_filt_first(l, tab, *fparams)
        return _hy_filt_mid(a, tab, colsum), tab
    filt, colsum = _hy_filters(l, *fparams)
    c2 = filt.shape[1]
    tab = _dft1_tables(l)
    spec = _rowdft(tab['fwd'], filt[None], True)[0]
    spec = pl.pallas_call(
        _col_scale_kernel,
        in_specs=[pl.BlockSpec(spec.shape, lambda: (0, 0)), pl.BlockSpec((1, c2), lambda: (0, 0))],
        out_specs=pl.BlockSpec(spec.shape, lambda: (0, 0)),
        out_shape=jax.ShapeDtypeStruct(spec.shape, F32),
        name="hy_col_scale",
    )(spec, colsum)
    return spec.reshape(2, tab['kp'], c2), tab


def _hyena_mixer(zz, hspec, tab, d_skip):
    conv = _hy_long_conv2 if 'k1p' in tab else _hy_long_conv1
    y = conv(zz, 0, zz, 1, hspec, 0, d_skip[0], tab)
    return conv(y, 0, zz, 2, hspec, 1, d_skip[1], tab)


def _grid_sincos(n, dm):
    rows = n // GRID_W
    quarter = dm // 4
    omega = 1.0 / (10000.0 ** (jnp.arange(quarter, dtype=F32) / quarter))
    ang_r = jnp.arange(rows, dtype=F32)[:, None] * omega
    ang_c = jnp.arange(GRID_W, dtype=F32)[:, None] * omega
    emb_r = jnp.concatenate([jnp.sin(ang_r), jnp.cos(ang_r)], axis=-1)
    emb_c = jnp.concatenate([jnp.sin(ang_c), jnp.cos(ang_c)], axis=-1)
    half = emb_r.shape[-1]
    pe = jnp.concatenate([jnp.broadcast_to(emb_r[:, None, :], (rows, GRID_W, half)),
                          jnp.broadcast_to(emb_c[None, :, :], (rows, GRID_W, half))], axis=-1)
    return pe.reshape(rows * GRID_W, 2 * half)


EV_Z_OFF = M2_XBC
EV_U_OFF = EV_Z_OFF + M2_INNER
EV_DT_OFF = EV_U_OFF + S5_WIDTH
EV_PROJ_W = EV_DT_OFF + 128


def _even_mixer(h_in, p, states):
    x, g, shift, scale = h_in
    b, l, _ = x.shape
    proj = _adaln_mm_conv(x, g, shift, scale, p['w_in'], jnp.zeros((1, EV_PROJ_W), F32),
                          p['m2_conv_w'], p['m2_conv_b'], True)
    s5_h0, m2_h0 = states

    ucol = EV_U_OFF // S5_WIDTH
    yf, yb, s5_fin = _s5_scan(proj, ucol, p['s5_bd'], p['s5_cd'], p['s5_ar'], p['s5_ai'], s5_h0)
    y_s5 = _s5_glu(yf, yb, proj, ucol, p['s5_d'], p['s5_glu_w'], p['s5_glu_b'])

    dt4 = proj[..., EV_DT_OFF:EV_DT_OFF + 2 * M2_HEADS].astype(F32).reshape(b, l, 2, M2_HEADS)
    dt_col = jnp.transpose(dt4, (2, 0, 1, 3))
    dt_row = jnp.transpose(dt4, (2, 0, 3, 1))
    ydir, m2_fin = _ssd(proj, dt_col, dt_row, p['m2_dt_bias'], p['m2_a'], m2_h0)
    y_m2 = _m2_gate(ydir, proj, proj, EV_Z_OFF // M2_INNER, p['m2_d'], p['m2_norm_g'])
    return (y_s5, y_m2), (s5_fin, m2_fin)


def kernel(x, c, ctx, c_ctx, mod_w, mod_b, norm_mix_g, norm_mlp_g, mlp_w1, mlp_w2, final_norm_g, ev_in_w, ev_out_w, s5_lam_re, s5_lam_im, s5_log_dt, s5_b_re, s5_b_im, s5_c_re, s5_c_im, s5_d, s5_glu_w, s5_glu_b, m2_conv_w, m2_conv_b, m2_dt_bias, m2_a_log, m2_d, m2_norm_g, hy_in_w, hy_in_b, hy_conv_w, hy_conv_b, hy_f_w1, hy_f_b1, hy_f_freq1, hy_f_w2, hy_f_b2, hy_f_freq2, hy_f_w3, hy_d, hy_out_w, hy_out_b):
    bsz, n, dm = x.shape
    lc = ctx.shape[1]
    x = _add_pe(x, _grid_sincos(n, dm))

    c8 = jnp.concatenate([c, c_ctx[None], jnp.zeros((3, dm), F32)], axis=0)
    mods = _modulation(c8, mod_w, mod_b).reshape(DEPTH, 8, N_MOD, dm)
    final_g = final_norm_g.reshape(1, dm)

    for i in range(DEPTH):
        j = i // 2
        mx = [mods[i, :bsz, k][:, None, :] for k in range(N_MOD)]
        mc = [jnp.broadcast_to(mods[i, bsz, k][None, None, :], (bsz, 1, dm)) for k in range(N_MOD)]
        g_mix = norm_mix_g[i].reshape(1, dm)
        g_mlp = norm_mlp_g[i].reshape(1, dm)
        ctx_later = any(k % 2 == 0 for k in range(i + 1, DEPTH))
        w1 = mlp_w1[i].astype(BF16)
        w2 = mlp_w2[i].astype(BF16)

        if i % 2 == 0:
            in_w = ev_in_w[j]
            o0, o1, o2 = S5_WIDTH, S5_WIDTH + M2_INNER, S5_WIDTH + M2_INNER + M2_XBC
            bd, cd, ar8, ai8 = _s5_prepare(s5_lam_re[j], s5_lam_im[j], s5_log_dt[j], s5_b_re[j], s5_b_im[j],
                                           s5_c_re[j], s5_c_im[j])
            p = dict(
                w_in=jnp.concatenate([in_w[:, o1:o2], in_w[:, o0:o1], in_w[:, :o0],
                                      jnp.pad(in_w[:, o2:], ((0, 0), (0, 128 - 2 * M2_HEADS)))],
                                     axis=1).astype(BF16),
                s5_bd=bd, s5_cd=cd, s5_ar=ar8, s5_ai=ai8,
                s5_d=s5_d[j].reshape(1, S5_WIDTH), s5_glu_w=s5_glu_w[j].astype(BF16),
                s5_glu_b=s5_glu_b[j].reshape(1, S5_WIDTH),
                m2_conv_w=m2_conv_w[j], m2_conv_b=m2_conv_b[j].reshape(1, M2_XBC),
                m2_dt_bias=m2_dt_bias[j], m2_a=-jnp.exp(m2_a_log[j]),
                m2_d=jnp.repeat(m2_d[j], M2_HEAD_DIM).reshape(1, M2_INNER),
                m2_norm_g=m2_norm_g[j].reshape(1, M2_INNER))
            zero_states = (jnp.zeros((8, S5_LANES), F32),
                           jnp.zeros((2, bsz, M2_GROUPS, M2_STATE, M2_GROUP_W), F32))
            ys_c, ctx_states = _even_mixer((ctx, g_mix, mc[0], mc[1]), p, zero_states)
            ys_x, _ = _even_mixer((x, g_mix, mx[0], mx[1]), p, ctx_states)
            out_ws = [ev_out_w[j][:S5_WIDTH].astype(BF16), ev_out_w[j][S5_WIDTH:].astype(BF16)]
            out_b = jnp.zeros((1, dm), F32)
        else:
            fparams = (hy_f_w1[j], hy_f_b1[j], hy_f_freq1[j], hy_f_w2[j], hy_f_b2[j], hy_f_freq2[j], hy_f_w3[j])
            in_w = hy_in_w[j].astype(BF16)
            in_b = hy_in_b[j].reshape(1, -1)
            conv_b = hy_conv_b[j].reshape(1, -1)
            hspec, tab = _hy_spectrum(n, fparams)
            zz = _adaln_mm_conv(x, g_mix, mx[0], mx[1], in_w, in_b, hy_conv_w[j], conv_b, False)
            ys_x = [_hyena_mixer(zz, hspec, tab, hy_d[j])]
            if ctx_later:
                hspec_c, tab_c = _hy_spectrum(lc, fparams)
                zz_c = _adaln_mm_conv(ctx, g_mix, mc[0], mc[1], in_w, in_b, hy_conv_w[j], conv_b, False)
                ys_c = [_hyena_mixer(zz_c, hspec_c, tab_c, hy_d[j])]
            out_ws = [hy_out_w[j].astype(BF16)]
            out_b = hy_out_b[j].reshape(1, dm)

        x = _mix_mlp(x, mx[2], out_b, ys_x, out_ws, g_mlp, mx[3], mx[4], mx[5], w1, w2, final_g, i == DEPTH - 1)
        if ctx_later:
            ctx = _mix_mlp(ctx, mc[2], out_b, ys_c, out_ws, g_mlp, mc[3], mc[4], mc[5], w1, w2, final_g, False)
    return x
```

```python
import functools
import math

import numpy as np
import jax
import jax.numpy as jnp
from jax import lax
from jax.experimental import pallas as pl
from jax.experimental.pallas import tpu as pltpu

F32 = jnp.float32
BF16 = jnp.bfloat16

D_MODEL = 1024
DEPTH = 4
GRID_W = 64
N_MOD = 6
NORM_EPS = 1e-6

S5_WIDTH = 512
S5_GROUP_CH = 16
S5_STATE = 64
S5_SUPER = 4
S5_SLAB = 2 * 8 * S5_STATE
S5_LANES = S5_SUPER * S5_SLAB

M2_INNER = 1024
M2_HEAD_DIM = 64
M2_HEADS = 16
M2_GROUPS = 4
M2_STATE = 128
M2_BC = M2_GROUPS * M2_STATE
M2_XBC = M2_INNER + 2 * M2_BC
M2_GROUP_W = (M2_HEADS // M2_GROUPS) * M2_HEAD_DIM

HY_WIDTH = 1024
HY_ORDER = 2
HY_BANDS = 16
HY_EMB = 2 * HY_BANDS + 1
HY_HIDDEN = 64
HY_MIN_DECAY = math.log(1e-2) / 1.5
HY_MAX_DECAY = math.log(1e-2) / 0.3

V7X_VMEM_LIMIT_BYTES = 48 * 1024 * 1024


def _cparams(*sem):
    return pltpu.CompilerParams(dimension_semantics=sem, vmem_limit_bytes=V7X_VMEM_LIMIT_BYTES)


def _bdot(a, b):
    return jnp.dot(a.astype(BF16), b.astype(BF16), preferred_element_type=F32)


def _split3(a):
    a1 = a.astype(BF16)
    r1 = a - a1.astype(F32)
    a2 = r1.astype(BF16)
    a3 = (r1 - a2.astype(F32)).astype(BF16)
    return a1, a2, a3


def _dot_exact_rhs(a, b_bf16):
    a1, a2, a3 = _split3(a)
    d = functools.partial(jnp.dot, preferred_element_type=F32)
    return d(a1, b_bf16) + d(a2, b_bf16) + d(a3, b_bf16)


def _dot_split2_rhs(a, b_bf16):
    a1 = a.astype(BF16)
    a2 = (a - a1.astype(F32)).astype(BF16)
    d = functools.partial(jnp.dot, preferred_element_type=F32)
    return d(a1, b_bf16) + d(a2, b_bf16)


def _dot_exact_lhs(a_bf16, b):
    b1, b2, b3 = _split3(b)
    d = functools.partial(jnp.dot, preferred_element_type=F32)
    return d(a_bf16, b1) + d(a_bf16, b2) + d(a_bf16, b3)


def _dot3(a, b):
    a1 = a.astype(BF16)
    a2 = (a - a1.astype(F32)).astype(BF16)
    b1 = b.astype(BF16)
    b2 = (b - b1.astype(F32)).astype(BF16)
    d = functools.partial(jnp.dot, preferred_element_type=F32)
    return d(a1, b1) + d(a1, b2) + d(a2, b1)


def _silu(x):
    return x * (1.0 / (1.0 + jnp.exp(-x)))


def _sigmoid(x):
    return 1.0 / (1.0 + jnp.exp(-x))


def _adaln(x, g, shift, scale):
    ms = jnp.mean(x * x, axis=-1, keepdims=True)
    return (x * lax.rsqrt(ms + NORM_EPS) * g) * (1.0 + scale) + shift


def _mod_kernel(c_ref, w_ref, b_ref, o_ref):
    o_ref[0] = _bdot(_silu(c_ref[...]), w_ref[0]) + b_ref[0]


def _modulation(c8, mod_w, mod_b):
    n = mod_w.shape[-1]
    tn = 1536
    return pl.pallas_call(
        _mod_kernel,
        grid=(DEPTH, n // tn),
        in_specs=[pl.BlockSpec((8, D_MODEL), lambda i, j: (0, 0)),
                  pl.BlockSpec((1, D_MODEL, tn), lambda i, j: (i, 0, j)),
                  pl.BlockSpec((1, 1, tn), lambda i, j: (i, 0, j))],
        out_specs=pl.BlockSpec((1, 8, tn), lambda i, j: (i, 0, j)),
        out_shape=jax.ShapeDtypeStruct((DEPTH, 8, n), F32),
        compiler_params=_cparams("parallel", "parallel"),
        name="modulation",
    )(c8, mod_w, mod_b.reshape(DEPTH, 1, n))


def _add_pe_kernel(x_ref, pe_ref, o_ref):
    o_ref[0] = x_ref[0] + pe_ref[...]


def _add_pe(x, pe):
    b, l, d = x.shape
    tl = min(l, 1024)
    return pl.pallas_call(
        _add_pe_kernel,
        grid=(l // tl, b),
        in_specs=[pl.BlockSpec((1, tl, d), lambda i, bb: (bb, i, 0)),
                  pl.BlockSpec((tl, d), lambda i, bb: (i, 0))],
        out_specs=pl.BlockSpec((1, tl, d), lambda i, bb: (bb, i, 0)),
        out_shape=jax.ShapeDtypeStruct(x.shape, F32),
        compiler_params=_cparams("parallel", "parallel"),
        name="add_pe",
    )(x, pe)


CONV_COLS = 512


def _adaln_mm_conv_kernel(ncv, act, x_ref, xp_ref, xn_ref, g_ref, sh_ref, sc_ref, w_ref, b_ref, cw_ref, cb_ref,
                          o_ref):
    i = pl.program_id(1)
    tl = x_ref.shape[1]
    rows = tl + 16
    xe = jnp.concatenate([xp_ref[0], x_ref[0], xn_ref[0]], axis=0)
    h = _adaln(xe, g_ref[...], sh_ref[0], sc_ref[0]).astype(BF16)
    n = w_ref.shape[1]
    r = lax.broadcasted_iota(jnp.int32, (tl, 1), 0)
    first = (r == 0) & (i == 0)
    last = (r == tl - 1) & (i == pl.num_programs(1) - 1)
    for c0 in range(0, n, CONV_COLS):
        cols = slice(c0, min(c0 + CONV_COLS, n))
        p = jnp.dot(h, w_ref[:, cols], preferred_element_type=F32) + b_ref[:, cols]
        mid = p[8:8 + tl]
        if c0 < ncv:
            prev = jnp.where(first, 0.0, pltpu.roll(p, 1, axis=0)[8:8 + tl])
            nxt = jnp.where(last, 0.0, pltpu.roll(p, rows - 1, axis=0)[8:8 + tl])
            cw = cw_ref[:, cols]
            mid = prev * cw[0:1] + mid * cw[1:2] + nxt * cw[2:3] + cb_ref[:, cols]
            if act:
                mid = _silu(mid)
        o_ref[0, :, cols] = mid.astype(o_ref.dtype)


def _adaln_mm_conv(x, g, shift, scale, w_bf16, bias, conv_w, conv_b, act):
    b, l, d = x.shape
    n = w_bf16.shape[1]
    ncv = conv_w.shape[1]
    assert ncv % CONV_COLS == 0
    tl = min(l, 512)
    t8 = tl // 8
    nblk8 = l // 8
    cst = lambda bb, i: (0, 0)
    vec = pl.BlockSpec((1, 1, d), lambda bb, i: (bb, 0, 0))
    return pl.pallas_call(
        functools.partial(_adaln_mm_conv_kernel, ncv, act),
        grid=(b, l // tl),
        in_specs=[pl.BlockSpec((1, tl, d), lambda bb, i: (bb, i, 0)),
                  pl.BlockSpec((1, 8, d), lambda bb, i: (bb, jnp.maximum(i * t8 - 1, 0), 0)),
                  pl.BlockSpec((1, 8, d), lambda bb, i: (bb, jnp.minimum((i + 1) * t8, nblk8 - 1), 0)),
                  pl.BlockSpec((1, d), cst), vec, vec,
                  pl.BlockSpec((d, n), cst), pl.BlockSpec((1, n), cst),
                  pl.BlockSpec((3, ncv), cst), pl.BlockSpec((1, ncv), cst)],
        out_specs=pl.BlockSpec((1, tl, n), lambda bb, i: (bb, i, 0)),
        out_shape=jax.ShapeDtypeStruct((b, l, n), BF16),
        compiler_params=_cparams("parallel", "parallel"),
        name="adaln_mm_conv",
    )(x, x, x, g, shift, scale, w_bf16, bias, conv_w, conv_b)


MLP_CHUNK = 512


def _mix_mlp_kernel(n_in, final_norm, *refs):
    x_ref, ga_ref, bo_ref = refs[0:3]
    y_refs = refs[3:3 + n_in]
    wo_refs = refs[3 + n_in:3 + 2 * n_in]
    g_ref, sh_ref, sc_ref, gf_ref, w1_ref, w2_ref, fg_ref, o_ref, a_scr = refs[3 + 2 * n_in:]
    mix = bo_ref[...] + _bdot(y_refs[0][0], wo_refs[0][...])
    for y_ref, wo_ref in zip(y_refs[1:], wo_refs[1:]):
        mix = mix + _bdot(y_ref[0], wo_ref[...])
    x1 = x_ref[0] + ga_ref[0] * mix
    h = _adaln(x1, g_ref[...], sh_ref[0], sc_ref[0]).astype(BF16)
    for c in range(w1_ref.shape[1] // MLP_CHUNK):
        cols = slice(c * MLP_CHUNK, (c + 1) * MLP_CHUNK)
        a = jnp.dot(h, w1_ref[:, cols], preferred_element_type=F32)
        a_scr[:, cols] = jnp.square(jnp.maximum(a, 0.0)).astype(BF16)
    y = x1 + gf_ref[0] * jnp.dot(a_scr[...], w2_ref[...], preferred_element_type=F32)
    if final_norm:
        ms = jnp.mean(y * y, axis=-1, keepdims=True)
        y = y * lax.rsqrt(ms + NORM_EPS) * fg_ref[...]
    o_ref[0] = y


def _mix_mlp(x, gate_a, out_b, ys, out_ws, g, shift, scale, gate_f, w1_bf16, w2_bf16, final_g, final_norm):
    b, l, d = x.shape
    hdim = w1_bf16.shape[1]
    tl = min(l, 512)
    n_in = len(ys)
    row = lambda bb, i: (bb, i, 0)
    vec = pl.BlockSpec((1, 1, d), lambda bb, i: (bb, 0, 0))
    cvec = pl.BlockSpec((1, d), lambda bb, i: (0, 0))
    resident = lambda a: pl.BlockSpec(a.shape, lambda bb, i: (0, 0), pipeline_mode=pl.Buffered(1))
    in_specs = [pl.BlockSpec((1, tl, d), row), vec, cvec]
    in_specs += [pl.BlockSpec((1, tl, y.shape[-1]), row) for y in ys]
    in_specs += [resident(w) for w in out_ws]
    in_specs += [cvec, vec, vec, vec, resident(w1_bf16), resident(w2_bf16), cvec]
    return pl.pallas_call(
        functools.partial(_mix_mlp_kernel, n_in, final_norm),
        grid=(b, l // tl),
        in_specs=in_specs,
        out_specs=pl.BlockSpec((1, tl, d), row),
        out_shape=jax.ShapeDtypeStruct(x.shape, F32),
        scratch_shapes=[pltpu.VMEM((tl, hdim), BF16)],
        compiler_params=_cparams("parallel", "parallel"),
        name="mix_mlp",
    )(x, gate_a, out_b, *ys, *out_ws, g, shift, scale, gate_f, w1_bf16, w2_bf16, final_g)


def _s5_prepare(lam_re, lam_im, log_dt, b_re, b_im, c_re, c_im):
    step = jnp.exp(log_dt)[..., None]
    mag = jnp.exp(lam_re * step)
    ar = mag * jnp.cos(lam_im * step)
    ai = mag * jnp.sin(lam_im * step)
    den = lam_re * lam_re + lam_im * lam_im
    fr = ((ar - 1.0) * lam_re + ai * lam_im) / den
    fi = (ai * lam_re - (ar - 1.0) * lam_im) / den
    eye = jnp.eye(8, dtype=F32)

    def blockdiag_in(b):
        b4 = b.reshape(S5_SUPER, 8, S5_STATE, S5_GROUP_CH)
        return jnp.einsum('sgpk,gh->sgkhp', b4, eye).reshape(S5_SUPER, 128, 512)

    bd = jnp.concatenate([blockdiag_in(b_re), blockdiag_in(b_im)], axis=-1).astype(BF16)

    cpr = c_re[None] * fr[:, :, None, :] - c_im[None] * fi[:, :, None, :]
    cpi = c_re[None] * fi[:, :, None, :] + c_im[None] * fr[:, :, None, :]

    def blockdiag_out(c):
        c5 = c.reshape(2, S5_SUPER, 8, S5_GROUP_CH, S5_STATE)
        return jnp.einsum('dsgkp,gh->dsgphk', c5, eye).reshape(2, S5_SUPER, 512, 128)

    cd = jnp.concatenate([blockdiag_out(cpr), blockdiag_out(-cpi)], axis=2)
    cd = jnp.concatenate([cd[0], cd[1]], axis=-1).astype(BF16)
    rows = lambda a: jnp.repeat(a.reshape(2, S5_SUPER * 512), 4, axis=0)
    return bd, cd, rows(ar), rows(ai)


S5_ROWS = 256


def _s5_scan_kernel(tc, uf_ref, ub_ref, bd_ref, cd_ref, ar_ref, ai_ref, h0_ref, flip_ref,
                    yf_ref, yb_ref, hfin_ref, u_scr, g_scr, y_scr, h_scr):
    c = pl.program_id(0)

    @pl.when(c == 0)
    def _():
        h_scr[...] = h0_ref[...]

    flip = flip_ref[...]
    rows = 8 * tc
    for b in range(4):
        uf = uf_ref[b].astype(F32)
        ub = jnp.dot(flip, ub_ref[b].astype(BF16), preferred_element_type=F32)
        for sg in range(S5_SUPER):
            u_scr[sg, pl.ds(b, tc, stride=8), :] = uf[:, sg * 128:(sg + 1) * 128]
            u_scr[sg, pl.ds(4 + b, tc, stride=8), :] = ub[:, sg * 128:(sg + 1) * 128]

    half = S5_SLAB // 256
    for sg in range(S5_SUPER):
        for r0 in range(0, rows, S5_ROWS):
            g_scr[r0:r0 + S5_ROWS, sg * S5_SLAB:(sg + 1) * S5_SLAB] = jnp.dot(
                u_scr[sg, r0:r0 + S5_ROWS, :].astype(BF16), bd_ref[sg], preferred_element_type=F32)

        ar = [ar_ref[:, (sg * half + k) * 128:(sg * half + k + 1) * 128] for k in range(half)]
        ai = [ai_ref[:, (sg * half + k) * 128:(sg * half + k + 1) * 128] for k in range(half)]
        l_re = [sg * S5_SLAB + k * 128 for k in range(half)]
        l_im = [sg * S5_SLAB + (half + k) * 128 for k in range(half)]
        hr = [h_scr[:, l_re[k]:l_re[k] + 128] for k in range(half)]
        hi = [h_scr[:, l_im[k]:l_im[k] + 128] for k in range(half)]
        for j in range(tc):
            r0 = j * 8
            for k in range(half):
                nr = ar[k] * hr[k] - ai[k] * hi[k] + g_scr[r0:r0 + 8, l_re[k]:l_re[k] + 128]
                ni = ar[k] * hi[k] + ai[k] * hr[k] + g_scr[r0:r0 + 8, l_im[k]:l_im[k] + 128]
                g_scr[r0:r0 + 8, l_re[k]:l_re[k] + 128] = nr
                g_scr[r0:r0 + 8, l_im[k]:l_im[k] + 128] = ni
                hr[k], hi[k] = nr, ni
        for k in range(half):
            h_scr[:, l_re[k]:l_re[k] + 128] = hr[k]
            h_scr[:, l_im[k]:l_im[k] + 128] = hi[k]

        for r0 in range(0, rows, S5_ROWS):
            y = jnp.dot(g_scr[r0:r0 + S5_ROWS, sg * S5_SLAB:(sg + 1) * S5_SLAB].astype(BF16), cd_ref[sg],
                        preferred_element_type=F32)
            y_scr[2 * sg, r0:r0 + S5_ROWS, :] = y[:, :128]
            y_scr[2 * sg + 1, r0:r0 + S5_ROWS, :] = y[:, 128:]

    for b in range(4):
        yf_ref[b] = jnp.concatenate(
            [y_scr[2 * sg, pl.ds(b, tc, stride=8), :] for sg in range(S5_SUPER)], axis=1)
        yb = jnp.concatenate(
            [y_scr[2 * sg + 1, pl.ds(4 + b, tc, stride=8), :] for sg in range(S5_SUPER)], axis=1)
        yb_ref[b] = _dot_exact_lhs(flip, yb)

    hfin_ref[...] = h_scr[...]


def _s5_scan(u, ucol, bd, cd, ar8, ai8, h0):
    b, l, _ = u.shape
    w = S5_WIDTH
    assert b == 4
    tc = 128
    nc = l // tc
    flip = jnp.asarray(np.eye(tc, dtype=np.float32)[::-1], dtype=BF16)
    full = lambda shape: pl.BlockSpec(shape, lambda c: (0,) * len(shape))
    y_shape = jax.ShapeDtypeStruct((b, l, w), F32)
    return pl.pallas_call(
        functools.partial(_s5_scan_kernel, tc),
        grid=(nc,),
        in_specs=[pl.BlockSpec((4, tc, w), lambda c: (0, c, ucol)),
                  pl.BlockSpec((4, tc, w), lambda c: (0, nc - 1 - c, ucol)),
                  full(bd.shape), full(cd.shape), full(ar8.shape), full(ai8.shape),
                  full(h0.shape), full(flip.shape)],
        out_specs=[pl.BlockSpec((4, tc, w), lambda c: (0, c, 0)),
                   pl.BlockSpec((4, tc, w), lambda c: (0, nc - 1 - c, 0)),
                   full((8, S5_LANES))],
        out_shape=[y_shape, y_shape, jax.ShapeDtypeStruct((8, S5_LANES), F32)],
        scratch_shapes=[pltpu.VMEM((S5_SUPER, 8 * tc, 128), F32), pltpu.VMEM((8 * tc, S5_LANES), F32),
                        pltpu.VMEM((2 * S5_SUPER, 8 * tc, 128), F32), pltpu.VMEM((8, S5_LANES), F32)],
        compiler_params=_cparams("arbitrary"),
        name="s5_scan",
    )(u, u, bd, cd, ar8, ai8, h0, flip)


def _gelu_tanh(x):
    return 0.5 * x * (1.0 + jnp.tanh(math.sqrt(2.0 / math.pi) * (x + 0.044715 * (x * x * x))))


def _s5_glu_kernel(yf_ref, yb_ref, u_ref, d_ref, w_ref, b_ref, o_ref):
    y = _gelu_tanh(yf_ref[0] + yb_ref[0] + d_ref[...] * u_ref[0].astype(F32))
    o_ref[0] = (y * _sigmoid(_bdot(y, w_ref[...]) + b_ref[...])).astype(o_ref.dtype)


def _s5_glu(yf, yb, u, ucol, d_skip, glu_w_bf16, glu_b):
    b, l, w = yf.shape
    tl = min(l, 1024)
    row = lambda bb, i: (bb, i, 0)
    cst = lambda bb, i: (0, 0)
    return pl.pallas_call(
        _s5_glu_kernel,
        grid=(b, l // tl),
        in_specs=[pl.BlockSpec((1, tl, w), row)] * 2 + [
            pl.BlockSpec((1, tl, w), lambda bb, i: (bb, i, ucol)),
            pl.BlockSpec((1, w), cst), pl.BlockSpec((w, w), cst), pl.BlockSpec((1, w), cst)],
        out_specs=pl.BlockSpec((1, tl, w), row),
        out_shape=jax.ShapeDtypeStruct(yf.shape, BF16),
        compiler_params=_cparams("parallel", "parallel"),
        name="s5_glu",
    )(yf, yb, u, d_skip, glu_w_bf16, glu_b)


SSD_T = 128
SSD_NB = 4


def _softplus(x):
    return jnp.maximum(x, 0.0) + jnp.log(1.0 + jnp.exp(-jnp.abs(x)))


def _ssd_kernel(xs_ref, bm_ref, cm_ref, dtc_ref, dtr_ref, bias_c_ref, bias_r_ref, a_c_ref, a_r_ref,
                tri_ref, trit_ref, exp_ref, h0_ref, y_ref, hfin_ref, s_scr):
    c = pl.program_id(2)

    @pl.when(c == 0)
    def _():
        s_scr[...] = h0_ref[0]

    tri = tri_ref[0]
    expand = exp_ref[...]
    mask = tri > 0
    for bi in range(SSD_NB):
        dt_c = _softplus(dtc_ref[0, bi] + bias_c_ref[0])
        dt_r = _softplus(dtr_ref[0, bi] + bias_r_ref[0])
        cum_c = _dot_exact_lhs(tri, dt_c * a_c_ref[0])
        cum_r = _dot_exact_rhs(dt_r * a_r_ref[0], trit_ref[0])
        tot_c = jnp.min(cum_c, axis=0, keepdims=True)

        t = dt_c.shape[0]
        ex = _dot_split2_rhs(jnp.concatenate([dt_c, jnp.exp(cum_c), jnp.exp(tot_c - cum_c)], axis=0), expand)
        dt_x = ex[:t]
        in_x = ex[t:2 * t]
        out_x = ex[2 * t:]
        tot_x = jnp.where(pl.program_id(0) == 0, in_x[t - 1:t], in_x[0:1])

        xdt = xs_ref[bi].astype(F32) * dt_x
        xout = (xdt * out_x).astype(BF16)
        xdt = xdt.astype(BF16)
        for g in range(M2_GROUPS):
            bg = bm_ref[bi, :, g * M2_STATE:(g + 1) * M2_STATE].astype(BF16)
            cg = cm_ref[bi, :, g * M2_STATE:(g + 1) * M2_STATE].astype(BF16)
            cb = lax.dot_general(cg, bg, (((1,), (1,)), ((), ())), preferred_element_type=F32)
            gc = slice(g * M2_GROUP_W, (g + 1) * M2_GROUP_W)
            s_prev = s_scr[bi, g]
            y_off = jnp.dot(cg, s_prev.astype(BF16), preferred_element_type=F32) * in_x[:, gc]
            y_heads = []
            for r in range(M2_HEADS // M2_GROUPS):
                h = g * (M2_HEADS // M2_GROUPS) + r
                seg = cum_c[:, h:h + 1] - cum_r[h:h + 1, :]
                m = (cb * jnp.exp(jnp.where(mask, seg, -jnp.inf))).astype(BF16)
                hc = slice(h * M2_HEAD_DIM, (h + 1) * M2_HEAD_DIM)
                y_heads.append(jnp.dot(m, xdt[:, hc], preferred_element_type=F32))
            y_ref[0, bi, :, gc] = (jnp.concatenate(y_heads, axis=1) + y_off).astype(y_ref.dtype)
            s_scr[bi, g] = s_prev * tot_x[:, gc] + lax.dot_general(
                bg, xout[:, gc], (((0,), (0,)), ((), ())), preferred_element_type=F32)

    @pl.when(c == pl.num_programs(2) - 1)
    def _():
        hfin_ref[0] = s_scr[...]


def _ssd(xbc, dt_col, dt_row, dt_bias, a_coef, h0):
    b, l, _ = xbc.shape
    t = SSD_T
    nc = l // t
    tri_f = np.tril(np.ones((t, t), np.float32))
    tri = jnp.asarray(np.stack([tri_f, tri_f.T]), dtype=BF16)
    trit = jnp.asarray(np.stack([tri_f.T, tri_f]), dtype=BF16)
    expand = jnp.asarray(np.kron(np.eye(M2_HEADS, dtype=np.float32),
                                 np.ones((1, M2_HEAD_DIM), np.float32)), dtype=BF16)
    chunk = lambda d, c: c + d * (nc - 1 - 2 * c)
    nb = SSD_NB
    state = pl.BlockSpec((1, nb, M2_GROUPS, M2_STATE, M2_GROUP_W), lambda d, bb, c: (d, bb, 0, 0, 0))
    return pl.pallas_call(
        _ssd_kernel,
        grid=(2, b // nb, nc),
        in_specs=[pl.BlockSpec((nb, t, M2_INNER), lambda d, bb, c: (bb, chunk(d, c), 0)),
                  pl.BlockSpec((nb, t, M2_BC), lambda d, bb, c: (bb, chunk(d, c), 2)),
                  pl.BlockSpec((nb, t, M2_BC), lambda d, bb, c: (bb, chunk(d, c), 3)),
                  pl.BlockSpec((1, nb, t, M2_HEADS), lambda d, bb, c: (d, bb, chunk(d, c), 0)),
                  pl.BlockSpec((1, nb, M2_HEADS, t), lambda d, bb, c: (d, bb, 0, chunk(d, c))),
                  pl.BlockSpec((1, 1, M2_HEADS), lambda d, bb, c: (d, 0, 0)),
                  pl.BlockSpec((1, M2_HEADS, 1), lambda d, bb, c: (d, 0, 0)),
                  pl.BlockSpec((1, 1, M2_HEADS), lambda d, bb, c: (d, 0, 0)),
                  pl.BlockSpec((1, M2_HEADS, 1), lambda d, bb, c: (d, 0, 0)),
                  pl.BlockSpec((1, t, t), lambda d, bb, c: (d, 0, 0)),
                  pl.BlockSpec((1, t, t), lambda d, bb, c: (d, 0, 0)),
                  pl.BlockSpec((M2_HEADS, M2_INNER), lambda d, bb, c: (0, 0)),
                  state],
        out_specs=[pl.BlockSpec((1, nb, t, M2_INNER), lambda d, bb, c: (d, bb, chunk(d, c), 0)), state],
        out_shape=[jax.ShapeDtypeStruct((2, b, l, M2_INNER), BF16),
                   jax.ShapeDtypeStruct((2, b, M2_GROUPS, M2_STATE, M2_GROUP_W), F32)],
        scratch_shapes=[pltpu.VMEM((nb, M2_GROUPS, M2_STATE, M2_GROUP_W), F32)],
        compiler_params=_cparams("parallel", "parallel", "arbitrary"),
        name="ssd",
    )(xbc, xbc, xbc, dt_col, dt_row, dt_bias.reshape(2, 1, M2_HEADS), dt_bias.reshape(2, M2_HEADS, 1),
      a_coef.reshape(2, 1, M2_HEADS), a_coef.reshape(2, M2_HEADS, 1), tri, trit, expand, h0)


def _m2_gate_kernel(yf_ref, yb_ref, xs_ref, z_ref, d_ref, g_ref, o_ref):
    y = (yf_ref[0, 0].astype(F32) + yb_ref[0, 0].astype(F32)
         + d_ref[...] * xs_ref[0].astype(F32)) * _silu(z_ref[0].astype(F32))
    ms = jnp.mean(y * y, axis=-1, keepdims=True)
    o_ref[0] = (y * lax.rsqrt(ms + NORM_EPS) * g_ref[...]).astype(o_ref.dtype)


def _m2_gate(ydir, xbc, z, zcol, d_x, norm_g):
    _, b, l, w = ydir.shape
    tl = min(l, 512)
    row = lambda bb, i: (bb, i, 0)
    cst = lambda bb, i: (0, 0)
    return pl.pallas_call(
        _m2_gate_kernel,
        grid=(b, l // tl),
        in_specs=[pl.BlockSpec((1, 1, tl, w), lambda bb, i: (0, bb, i, 0)),
                  pl.BlockSpec((1, 1, tl, w), lambda bb, i: (1, bb, i, 0)),
                  pl.BlockSpec((1, tl, w), row), pl.BlockSpec((1, tl, w), lambda bb, i: (bb, i, zcol)),
                  pl.BlockSpec((1, w), cst), pl.BlockSpec((1, w), cst)],
        out_specs=pl.BlockSpec((1, tl, w), row),
        out_shape=jax.ShapeDtypeStruct((b, l, w), BF16),
        compiler_params=_cparams("parallel", "parallel"),
        name="m2_gate",
    )(ydir, ydir, xbc, z, d_x, norm_g)


HY_N2 = 128


def _round8(n):
    return (n + 7) // 8 * 8


def _dft2_tables(n1, n2):
    n = n1 * n2
    k1n = n1 // 2 + 1
    k1p = _round8(k1n)
    k1 = np.arange(k1p, dtype=np.float64)[:, None]
    valid = (k1 < k1n).astype(np.float64)
    i1 = np.arange(n1, dtype=np.float64)[None, :]
    t = n2 * i1[None] + np.arange(n2, dtype=np.float64)[:, None, None]
    th = 2.0 * np.pi * k1[None] * t / n
    stage1_tw = np.concatenate([np.cos(th) * valid[None], -np.sin(th) * valid[None]], axis=1)
    i2 = np.arange(n2, dtype=np.float64)
    k2 = np.arange(n2, dtype=np.float64)[:, None]
    th2 = 2.0 * np.pi * k2 * i2[None, :] / n2
    fc, fs = np.cos(th2), np.sin(th2)
    fwd2 = np.block([[fc, fs], [-fs, fc]])
    inv2 = np.block([[fc, -fs], [fs, fc]])
    wk = (np.where((k1 == 0) | (k1 == n1 // 2), 1.0, 2.0) * valid).T[None]
    tho = np.transpose(th[:, :, :n1 // 2], (0, 2, 1))
    last_tw = np.concatenate([np.cos(tho) * wk, -np.sin(tho) * wk], axis=2) / n
    f = lambda a: jnp.asarray(a, dtype=F32)
    return dict(k1n=k1n, k1p=k1p, stage1_tw=f(stage1_tw), fwd2=f(fwd2), inv2=f(inv2), last_tw=f(last_tw))


def _dft1_tables(l):
    n = 2 * l
    kn = l + 1
    kp = _round8(kn)
    k = np.arange(kp, dtype=np.float64)[:, None]
    valid = (k < kn).astype(np.float64)
    t = np.arange(n, dtype=np.float64)[None, :]
    th = 2.0 * np.pi * k * t / n
    fwd = np.concatenate([np.cos(th) * valid, -np.sin(th) * valid], axis=0)
    wk = np.where((k == 0) | (k == l), 1.0, 2.0) * valid
    o = np.arange(l, dtype=np.float64)[:, None]
    tho = 2.0 * np.pi * o * k.T / n
    inv = np.concatenate([np.cos(tho) * wk.T, -np.sin(tho) * wk.T], axis=1) / n
    f = lambda a: jnp.asarray(a, dtype=F32)
    return dict(kp=kp, fwd=f(fwd), inv=f(inv))


def _rowdft_kernel(hi, f_ref, x_ref, o_ref):
    dot = _dot3 if hi else _bdot
    o_ref[0] = dot(f_ref[...], x_ref[0])


def _rowdft(fmat, x, hi):
    bx, r, n = x.shape
    m = fmat.shape[0]
    tn = min(n, 4096)
    return pl.pallas_call(
        functools.partial(_rowdft_kernel, hi),
        grid=(bx, n // tn),
        in_specs=[pl.BlockSpec((m, r), lambda b, j: (0, 0)),
                  pl.BlockSpec((1, r, tn), lambda b, j: (b, 0, j))],
        out_specs=pl.BlockSpec((1, m, tn), lambda b, j: (b, 0, j)),
        out_shape=jax.ShapeDtypeStruct((bx, m, n), F32),
        compiler_params=_cparams("parallel", "parallel"),
        name="hy_rowdft",
    )(fmat if hi else fmat.astype(BF16), x)


HY_TS = 8


HY_PITCH = HY_N2 + 8
HY_CONV_VMEM_BYTES = 58 * 1024 * 1024
HY_STAGE2_UNROLL = 9


def _hy_conv_kernel(k1n, k1p, n2, v_ref, gate_ref, h_ref, st1_ref, fwd_ref, inv_ref, last_ref,
                    d_ref, o_ref, a_scr, t_scr):
    n1h = v_ref.shape[0] // n2

    def copy_in(i, _):
        t_scr[pl.ds(pl.multiple_of(i * HY_PITCH, 8), n2), :] = (
            v_ref[pl.ds(pl.multiple_of(i * n2, n2), n2), :].astype(F32))
        return 0
    lax.fori_loop(0, n1h, copy_in, 0, unroll=4)

    zero = jnp.zeros((n1h, 128), BF16)

    def stage1(j, _):
        s = 2 * j
        xa = t_scr[pl.ds(s, n1h, stride=HY_PITCH), :].astype(BF16)
        xb = t_scr[pl.ds(s + 1, n1h, stride=HY_PITCH), :].astype(BF16)
        xs = jnp.concatenate([jnp.concatenate([xa, zero], axis=1), jnp.concatenate([zero, xb], axis=1)], axis=0)
        y = jnp.dot(st1_ref[j], xs, preferred_element_type=F32)
        a_scr[pl.ds(s, 2 * k1p, stride=HY_PITCH), :] = y[:, :128]
        a_scr[pl.ds(s + 1, 2 * k1p, stride=HY_PITCH), :] = y[:, 128:]
        return 0
    lax.fori_loop(0, n2 // 2, stage1, 0, unroll=16)

    def cmul(ar, ai, br, bi):
        return ar * br - ai * bi, ar * bi + ai * br

    def stage2(j, _):
        ks = (2 * j, 2 * j + 1)
        rows_r = [pl.ds(pl.multiple_of(k * HY_PITCH, 8), n2) for k in ks]
        rows_i = [pl.ds(pl.multiple_of((k1p + k) * HY_PITCH, 8), n2) for k in ks]
        p = jnp.concatenate([jnp.concatenate([a_scr[rr, :] for rr in rows_r], axis=1),
                             jnp.concatenate([a_scr[ri, :] for ri in rows_i], axis=1)], axis=0)
        x = jnp.dot(fwd_ref[...], p.astype(BF16), preferred_element_type=F32)
        hr = jnp.concatenate([h_ref[0, ks[0]], h_ref[0, ks[1]]], axis=1).astype(F32)
        hi = jnp.concatenate([h_ref[1, ks[0]], h_ref[1, ks[1]]], axis=1).astype(F32)
        qr, qi = cmul(x[:n2], x[n2:], hr, hi)
        z = jnp.dot(inv_ref[...], jnp.concatenate([qr, qi], axis=0).astype(BF16), preferred_element_type=F32)
        for n, (rr, ri) in enumerate(zip(rows_r, rows_i)):
            a_scr[rr, :] = z[:n2, n * 128:(n + 1) * 128]
            a_scr[ri, :] = z[n2:, n * 128:(n + 1) * 128]
        return 0

    pairs = k1p // 2
    unroll = max(u for u in range(1, HY_STAGE2_UNROLL + 1) if pairs % u == 0)
    lax.fori_loop(0, pairs, stage2, 0, unroll=unroll)

    def stage3(j, _):
        s = 2 * j
        zs = jnp.concatenate([a_scr[pl.ds(s, 2 * k1p, stride=HY_PITCH), :],
                              a_scr[pl.ds(s + 1, 2 * k1p, stride=HY_PITCH), :]], axis=1).astype(BF16)
        y = jnp.dot(last_ref[j], zs, preferred_element_type=F32)
        t_scr[pl.ds(s, n1h, stride=HY_PITCH), :] = y[:n1h, :128]
        t_scr[pl.ds(s + 1, n1h, stride=HY_PITCH), :] = y[n1h:, 128:]
        return 0
    lax.fori_loop(0, n2 // 2, stage3, 0, unroll=16)

    def finish(i, _):
        rows = pl.ds(pl.multiple_of(i * n2, n2), n2)
        y = t_scr[pl.ds(pl.multiple_of(i * HY_PITCH, 8), n2), :]
        o_ref[rows, :] = (gate_ref[rows, :].astype(F32)
                          * (y + d_ref[...] * v_ref[rows, :].astype(F32))).astype(o_ref.dtype)
        return 0
    lax.fori_loop(0, n1h, finish, 0, unroll=4)


def _hy_long_conv2(v, vpart, gate, gpart, hspec, order, d_vec, tab):
    b, l, _ = v.shape
    c = HY_WIDTH
    n2 = HY_N2
    n1h = l // n2
    k1p = tab['k1p']
    tiles = c // 128
    seq = pl.BlockSpec((None, l, 128), lambda j, bb: (bb, 0, j))
    vseq = pl.BlockSpec((None, l, 128), lambda j, bb: (bb, 0, vpart * tiles + j))
    gseq = pl.BlockSpec((None, l, 128), lambda j, bb: (bb, 0, gpart * tiles + j))
    resident = lambda a: pl.BlockSpec(a.shape, lambda j, bb: (0,) * a.ndim, pipeline_mode=pl.Buffered(1))
    s1 = tab['stage1_tw'][:, :, :n1h]
    st1 = jnp.concatenate([s1[0::2], s1[1::2]], axis=2).astype(BF16)
    lt = tab['last_tw']
    last = jnp.concatenate([lt[0::2], lt[1::2]], axis=1).astype(BF16)
    fwd2, inv2 = tab['fwd2'].astype(BF16), tab['inv2'].astype(BF16)
    return pl.pallas_call(
        functools.partial(_hy_conv_kernel, tab['k1n'], k1p, n2),
        grid=(tiles, b),
        in_specs=[vseq, gseq,
                  pl.BlockSpec((2, k1p, n2, 128), lambda j, bb: (0, 0, 0, order * tiles + j)),
                  resident(st1), resident(fwd2), resident(inv2), resident(last),
                  pl.BlockSpec((1, 128), lambda j, bb: (0, j))],
        out_specs=seq,
        out_shape=jax.ShapeDtypeStruct((b, l, c), BF16),
        scratch_shapes=[pltpu.VMEM((2 * k1p * HY_PITCH, 128), F32), pltpu.VMEM((n1h * HY_PITCH, 128), F32)],
        compiler_params=pltpu.CompilerParams(dimension_semantics=("parallel", "parallel"),
                                             vmem_limit_bytes=HY_CONV_VMEM_BYTES),
        name="hy_conv",
    )(v, gate, hspec, st1, fwd2, inv2, last, d_vec.reshape(1, c))


def _hy_filt_first_kernel(k1p, feat_ref, t_ref, keep_ref, w1_ref, b1_ref, q1_ref, w2_ref, b2_ref, q2_ref,
                          w3_ref, nd_ref, f_ref, o_ref, sum_ref):
    @pl.when(pl.program_id(0) == 0)
    def _():
        sum_ref[...] = jnp.zeros_like(sum_ref)

    n1 = feat_ref.shape[1]
    half = n1 // 2
    feats = feat_ref[...].reshape(HY_TS * n1, feat_ref.shape[2])
    hid = jnp.sin(q1_ref[...] * (_bdot(feats, w1_ref[...]) + b1_ref[...]))
    hid = jnp.sin(q2_ref[...] * (_bdot(hid, w2_ref[...]) + b2_ref[...]))
    acc = jnp.zeros(sum_ref.shape, F32)
    for s in range(HY_TS):
        hs = hid[s * n1:(s + 1) * n1]
        f = jnp.concatenate([_bdot(hs[:half], w3_ref[0]), _bdot(hs[half:], w3_ref[1])], axis=0)
        f = f * jnp.exp(t_ref[s] * nd_ref[...]) * keep_ref[s]
        acc = acc + jnp.sum(jnp.abs(f), axis=0, keepdims=True)
        y = _dot3(f_ref[s], f)
        o_ref[0, :, s, :] = y[:k1p]
        o_ref[1, :, s, :] = y[k1p:]
    sum_ref[...] += acc


def _hy_filt_first(l, tab, f_w1, f_b1, f_freq1, f_w2, f_b2, f_freq2, f_w3):
    c2 = HY_ORDER * HY_WIDTH
    n2 = HY_N2
    n1 = 2 * l // n2
    k1p = tab['k1p']
    t = np.linspace(0.0, 1.0, l, dtype=np.float32)[:, None]
    bands = np.linspace(1e-4, HY_BANDS - 1, HY_BANDS, dtype=np.float32)
    ang = np.float32(2.0 * math.pi / l) * np.arange(l, dtype=np.float32)[:, None] * bands
    feats = np.concatenate([t, np.cos(ang), -np.sin(ang)], axis=-1)
    rows = (n2 * np.arange(n1)[None, :] + np.arange(n2)[:, None]).reshape(-1)
    pos = np.where(rows < l, rows, np.where(rows == l, 0, 2 * l - rows))
    feats_t = jnp.asarray(np.pad(feats[pos], ((0, 0), (0, 128 - HY_EMB))).reshape(n2, n1, 128))
    t_t = jnp.asarray(t[pos].reshape(n2, n1, 1))
    keep_t = jnp.asarray((rows != l).astype(np.float32).reshape(n2, n1, 1))
    w1p = jnp.pad(f_w1, ((0, 128 - HY_EMB), (0, 0)))
    w3d = jnp.transpose(f_w3.reshape(HY_HIDDEN, HY_ORDER, 2, HY_WIDTH), (2, 0, 1, 3)).reshape(2, HY_HIDDEN, c2)
    deltas = jnp.abs(jnp.linspace(HY_MIN_DECAY, HY_MAX_DECAY, HY_WIDTH, dtype=F32))
    negd = jnp.tile(-deltas, HY_ORDER)[None, :]
    cst = lambda i: (0, 0)
    vec = lambda a: a.reshape(1, HY_HIDDEN)
    tile3 = lambda w: pl.BlockSpec((HY_TS, n1, w), lambda i: (i, 0, 0))
    return pl.pallas_call(
        functools.partial(_hy_filt_first_kernel, k1p),
        grid=(n2 // HY_TS,),
        in_specs=[tile3(128), tile3(1), tile3(1),
                  pl.BlockSpec((128, HY_HIDDEN), cst), pl.BlockSpec((1, HY_HIDDEN), cst),
                  pl.BlockSpec((1, HY_HIDDEN), cst),
                  pl.BlockSpec((HY_HIDDEN, HY_HIDDEN), cst), pl.BlockSpec((1, HY_HIDDEN), cst),
                  pl.BlockSpec((1, HY_HIDDEN), cst),
                  pl.BlockSpec((2, HY_HIDDEN, c2), lambda i: (0, 0, 0)),
                  pl.BlockSpec((1, c2), cst),
                  pl.BlockSpec((HY_TS, 2 * k1p, n1), lambda i: (i, 0, 0))],
        out_specs=[pl.BlockSpec((None, 2, k1p, HY_TS, c2), lambda i: (0, 0, 0, i, 0)),
                   pl.BlockSpec((1, c2), cst)],
        out_shape=[jax.ShapeDtypeStruct((1, 2, k1p, n2, c2), F32), jax.ShapeDtypeStruct((1, c2), F32)],
        compiler_params=_cparams("arbitrary"),
        name="hy_filt_first",
    )(feats_t, t_t, keep_t, w1p, vec(f_b1), vec(f_freq1), f_w2, vec(f_b2), vec(f_freq2), w3d, negd,
      tab['stage1_tw'])


def _hy_filt_mid_kernel(n2, a_ref, sum_ref, fwd_ref, o_ref):
    c2 = a_ref.shape[-1]
    x = _dot3(fwd_ref[...], a_ref[0, :, 0].reshape(2 * n2, c2)) * (1.0 / (sum_ref[...] + 1e-6))
    o_ref[0, 0] = x[:n2].astype(BF16)
    o_ref[1, 0] = x[n2:].astype(BF16)


def _hy_filt_mid(a5, tab, colsum):
    _, _, k1p, n2, c2 = a5.shape
    return pl.pallas_call(
        functools.partial(_hy_filt_mid_kernel, n2),
        grid=(k1p,),
        in_specs=[pl.BlockSpec((1, 2, 1, n2, c2), lambda k: (0, 0, k, 0, 0)),
                  pl.BlockSpec((1, c2), lambda k: (0, 0)),
                  pl.BlockSpec((2 * n2, 2 * n2), lambda k: (0, 0))],
        out_specs=pl.BlockSpec((2, 1, n2, c2), lambda k: (0, k, 0, 0)),
        out_shape=jax.ShapeDtypeStruct((2, k1p, n2, c2), BF16),
        compiler_params=_cparams("parallel"),
        name="hy_filt_mid",
    )(a5, colsum, tab['fwd2'])


def _hy_ctx_kernel(kp, fw_ref, inv_ref, h_ref, v_ref, gate_ref, d_ref, o_ref):
    v = v_ref[0].astype(F32)
    u = _bdot(fw_ref[...], v)
    ur, ui = u[:kp], u[kp:]
    hr, hi = h_ref[0], h_ref[1]
    q = jnp.concatenate([ur * hr - ui * hi, ur * hi + ui * hr], axis=0)
    o_ref[0] = (gate_ref[0].astype(F32) * (_bdot(inv_ref[...], q) + d_ref[...] * v)).astype(o_ref.dtype)


def _hy_long_conv1(v, vpart, gate, gpart, hspec, order, d_vec, tab):
    b, l, _ = v.shape
    c = HY_WIDTH
    kp = tab['kp']
    row = lambda bb: (bb, 0, 0)
    return pl.pallas_call(
        functools.partial(_hy_ctx_kernel, kp),
        grid=(b,),
        in_specs=[pl.BlockSpec((2 * kp, l), lambda bb: (0, 0)),
                  pl.BlockSpec((l, 2 * kp), lambda bb: (0, 0)),
                  pl.BlockSpec((2, kp, c), lambda bb: (0, 0, order)),
                  pl.BlockSpec((1, l, c), lambda bb: (bb, 0, vpart)),
                  pl.BlockSpec((1, l, c), lambda bb: (bb, 0, gpart)),
                  pl.BlockSpec((1, c), lambda bb: (0, 0))],
        out_specs=pl.BlockSpec((1, l, c), row),
        out_shape=jax.ShapeDtypeStruct((b, l, c), BF16),
        compiler_params=_cparams("parallel"),
        name="hy_ctx_conv",
    )(tab['fwd'][:, :l].astype(BF16), tab['inv'].astype(BF16), hspec, v, gate, d_vec.reshape(1, c))


def _hy_filter_kernel(feat_ref, t_ref, keep_ref, w1_ref, b1_ref, q1_ref, w2_ref, b2_ref, q2_ref,
                      w3_ref, nd_ref, f_ref, sum_ref):
    @pl.when(pl.program_id(0) == 0)
    def _():
        sum_ref[...] = jnp.zeros_like(sum_ref)

    hid = jnp.sin(q1_ref[...] * (_bdot(feat_ref[...], w1_ref[...]) + b1_ref[...]))
    hid = jnp.sin(q2_ref[...] * (_bdot(hid, w2_ref[...]) + b2_ref[...]))
    f = _bdot(hid, w3_ref[0]) * jnp.exp(t_ref[...] * nd_ref[...]) * keep_ref[...]
    f_ref[...] = f
    sum_ref[...] += jnp.sum(jnp.abs(f), axis=0, keepdims=True)


def _hy_filters(l, f_w1, f_b1, f_freq1, f_w2, f_b2, f_freq2, f_w3):
    c2 = HY_ORDER * HY_WIDTH
    t = np.linspace(0.0, 1.0, l, dtype=np.float32)[:, None]
    bands = np.linspace(1e-4, HY_BANDS - 1, HY_BANDS, dtype=np.float32)
    ang = np.float32(2.0 * math.pi / l) * np.arange(l, dtype=np.float32)[:, None] * bands
    feats = np.concatenate([t, np.cos(ang), -np.sin(ang)], axis=-1)
    rows = np.arange(2 * l)
    pos = np.where(rows < l, rows, np.where(rows == l, 0, 2 * l - rows))
    feats2 = jnp.asarray(np.pad(feats[pos], ((0, 0), (0, 128 - HY_EMB))))
    keep = jnp.asarray((rows != l).astype(np.float32))[:, None]
    t2 = jnp.asarray(t[pos])
    w1p = jnp.pad(f_w1, ((0, 128 - HY_EMB), (0, 0)))
    w3d = jnp.transpose(f_w3.reshape(HY_HIDDEN, HY_ORDER, 2, HY_WIDTH), (2, 0, 1, 3)).reshape(2, HY_HIDDEN, c2)
    deltas = jnp.abs(jnp.linspace(HY_MIN_DECAY, HY_MAX_DECAY, HY_WIDTH, dtype=F32))
    negd = jnp.tile(-deltas, HY_ORDER)[None, :]
    tr = min(l, 512)
    per_dir = l // tr
    cst = lambda i: (0, 0)
    vec = lambda a: a.reshape(1, HY_HIDDEN)
    return pl.pallas_call(
        _hy_filter_kernel,
        grid=(2 * l // tr,),
        in_specs=[pl.BlockSpec((tr, 128), lambda i: (i, 0)),
                  pl.BlockSpec((tr, 1), lambda i: (i, 0)),
                  pl.BlockSpec((tr, 1), lambda i: (i, 0)),
                  pl.BlockSpec((128, HY_HIDDEN), cst), pl.BlockSpec((1, HY_HIDDEN), cst),
                  pl.BlockSpec((1, HY_HIDDEN), cst),
                  pl.BlockSpec((HY_HIDDEN, HY_HIDDEN), cst), pl.BlockSpec((1, HY_HIDDEN), cst),
                  pl.BlockSpec((1, HY_HIDDEN), cst),
                  pl.BlockSpec((1, HY_HIDDEN, c2), lambda i: (i // per_dir, 0, 0)),
                  pl.BlockSpec((1, c2), cst)],
        out_specs=[pl.BlockSpec((tr, c2), lambda i: (i, 0)), pl.BlockSpec((1, c2), cst)],
        out_shape=[jax.ShapeDtypeStruct((2 * l, c2), F32), jax.ShapeDtypeStruct((1, c2), F32)],
        compiler_params=_cparams("arbitrary"),
        name="hy_filter",
    )(feats2, t2, keep, w1p, vec(f_b1), vec(f_freq1), f_w2, vec(f_b2), vec(f_freq2), w3d, negd)


def _col_scale_kernel(x_ref, sum_ref, o_ref):
    o_ref[...] = x_ref[...] * (1.0 / (sum_ref[...] + 1e-6))


def _hy_spectrum(l, fparams):
    if 2 * l >= 16 * HY_N2:
        tab = _dft2_tables(2 * l // HY_N2, HY_N2)
        a, colsum = _hy_filt_first(l, tab, *fparams)
        return _hy_filt_mid(a, tab, colsum), tab
    filt, colsum = _hy_filters(l, *fparams)
    c2 = filt.shape[1]
    tab = _dft1_tables(l)
    spec = _rowdft(tab['fwd'], filt[None], True)[0]
    spec = pl.pallas_call(
        _col_scale_kernel,
        in_specs=[pl.BlockSpec(spec.shape, lambda: (0, 0)), pl.BlockSpec((1, c2), lambda: (0, 0))],
        out_specs=pl.BlockSpec(spec.shape, lambda: (0, 0)),
        out_shape=jax.ShapeDtypeStruct(spec.shape, F32),
        name="hy_col_scale",
    )(spec, colsum)
    return spec.reshape(2, tab['kp'], c2), tab


def _hyena_mixer(zz, hspec, tab, d_skip):
    conv = _hy_long_conv2 if 'k1p' in tab else _hy_long_conv1
    y = conv(zz, 0, zz, 1, hspec, 0, d_skip[0], tab)
    return conv(y, 0, zz, 2, hspec, 1, d_skip[1], tab)


def _grid_sincos(n, dm):
    rows = n // GRID_W
    quarter = dm // 4
    omega = 1.0 / (10000.0 ** (jnp.arange(quarter, dtype=F32) / quarter))
    ang_r = jnp.arange(rows, dtype=F32)[:, None] * omega
    ang_c = jnp.arange(GRID_W, dtype=F32)[:, None] * omega
    emb_r = jnp.concatenate([jnp.sin(ang_r), jnp.cos(ang_r)], axis=-1)
    emb_c = jnp.concatenate([jnp.sin(ang_c), jnp.cos(ang_c)], axis=-1)
    half = emb_r.shape[-1]
    pe = jnp.concatenate([jnp.broadcast_to(emb_r[:, None, :], (rows, GRID_W, half)),
                          jnp.broadcast_to(emb_c[None, :, :], (rows, GRID_W, half))], axis=-1)
    return pe.reshape(rows * GRID_W, 2 * half)


EV_Z_OFF = M2_XBC
EV_U_OFF = EV_Z_OFF + M2_INNER
EV_DT_OFF = EV_U_OFF + S5_WIDTH
EV_PROJ_W = EV_DT_OFF + 128


def _even_mixer(h_in, p, states):
    x, g, shift, scale = h_in
    b, l, _ = x.shape
    proj = _adaln_mm_conv(x, g, shift, scale, p['w_in'], jnp.zeros((1, EV_PROJ_W), F32),
                          p['m2_conv_w'], p['m2_conv_b'], True)
    s5_h0, m2_h0 = states

    ucol = EV_U_OFF // S5_WIDTH
    yf, yb, s5_fin = _s5_scan(proj, ucol, p['s5_bd'], p['s5_cd'], p['s5_ar'], p['s5_ai'], s5_h0)
    y_s5 = _s5_glu(yf, yb, proj, ucol, p['s5_d'], p['s5_glu_w'], p['s5_glu_b'])

    dt4 = proj[..., EV_DT_OFF:EV_DT_OFF + 2 * M2_HEADS].astype(F32).reshape(b, l, 2, M2_HEADS)
    dt_col = jnp.transpose(dt4, (2, 0, 1, 3))
    dt_row = jnp.transpose(dt4, (2, 0, 3, 1))
    ydir, m2_fin = _ssd(proj, dt_col, dt_row, p['m2_dt_bias'], p['m2_a'], m2_h0)
    y_m2 = _m2_gate(ydir, proj, proj, EV_Z_OFF // M2_INNER, p['m2_d'], p['m2_norm_g'])
    return (y_s5, y_m2), (s5_fin, m2_fin)


def kernel(x, c, ctx, c_ctx, mod_w, mod_b, norm_mix_g, norm_mlp_g, mlp_w1, mlp_w2, final_norm_g, ev_in_w, ev_out_w, s5_lam_re, s5_lam_im, s5_log_dt, s5_b_re, s5_b_im, s5_c_re, s5_c_im, s5_d, s5_glu_w, s5_glu_b, m2_conv_w, m2_conv_b, m2_dt_bias, m2_a_log, m2_d, m2_norm_g, hy_in_w, hy_in_b, hy_conv_w, hy_conv_b, hy_f_w1, hy_f_b1, hy_f_freq1, hy_f_w2, hy_f_b2, hy_f_freq2, hy_f_w3, hy_d, hy_out_w, hy_out_b):
    bsz, n, dm = x.shape
    lc = ctx.shape[1]
    x = _add_pe(x, _grid_sincos(n, dm))

    c8 = jnp.concatenate([c, c_ctx[None], jnp.zeros((3, dm), F32)], axis=0)
    mods = _modulation(c8, mod_w, mod_b).reshape(DEPTH, 8, N_MOD, dm)
    final_g = final_norm_g.reshape(1, dm)

    for i in range(DEPTH):
        j = i // 2
        mx = [mods[i, :bsz, k][:, None, :] for k in range(N_MOD)]
        mc = [jnp.broadcast_to(mods[i, bsz, k][None, None, :], (bsz, 1, dm)) for k in range(N_MOD)]
        g_mix = norm_mix_g[i].reshape(1, dm)
        g_mlp = norm_mlp_g[i].reshape(1, dm)
        ctx_later = any(k % 2 == 0 for k in range(i + 1, DEPTH))
        w1 = mlp_w1[i].astype(BF16)
        w2 = mlp_w2[i].astype(BF16)

        if i % 2 == 0:
            in_w = ev_in_w[j]
            o0, o1, o2 = S5_WIDTH, S5_WIDTH + M2_INNER, S5_WIDTH + M2_INNER + M2_XBC
            bd, cd, ar8, ai8 = _s5_prepare(s5_lam_re[j], s5_lam_im[j], s5_log_dt[j], s5_b_re[j], s5_b_im[j],
                                           s5_c_re[j], s5_c_im[j])
            p = dict(
                w_in=jnp.concatenate([in_w[:, o1:o2], in_w[:, o0:o1], in_w[:, :o0],
                                      jnp.pad(in_w[:, o2:], ((0, 0), (0, 128 - 2 * M2_HEADS)))],
                                     axis=1).astype(BF16),
                s5_bd=bd, s5_cd=cd, s5_ar=ar8, s5_ai=ai8,
                s5_d=s5_d[j].reshape(1, S5_WIDTH), s5_glu_w=s5_glu_w[j].astype(BF16),
                s5_glu_b=s5_glu_b[j].reshape(1, S5_WIDTH),
                m2_conv_w=m2_conv_w[j], m2_conv_b=m2_conv_b[j].reshape(1, M2_XBC),
                m2_dt_bias=m2_dt_bias[j], m2_a=-jnp.exp(m2_a_log[j]),
                m2_d=jnp.repeat(m2_d[j], M2_HEAD_DIM).reshape(1, M2_INNER),
                m2_norm_g=m2_norm_g[j].reshape(1, M2_INNER))
            zero_states = (jnp.zeros((8, S5_LANES), F32),
                           jnp.zeros((2, bsz, M2_GROUPS, M2_STATE, M2_GROUP_W), F32))
            ys_c, ctx_states = _even_mixer((ctx, g_mix, mc[0], mc[1]), p, zero_states)
            ys_x, _ = _even_mixer((x, g_mix, mx[0], mx[1]), p, ctx_states)
            out_ws = [ev_out_w[j][:S5_WIDTH].astype(BF16), ev_out_w[j][S5_WIDTH:].astype(BF16)]
            out_b = jnp.zeros((1, dm), F32)
        else:
            fparams = (hy_f_w1[j], hy_f_b1[j], hy_f_freq1[j], hy_f_w2[j], hy_f_b2[j], hy_f_freq2[j], hy_f_w3[j])
            in_w = hy_in_w[j].astype(BF16)
            in_b = hy_in_b[j].reshape(1, -1)
            conv_b = hy_conv_b[j].reshape(1, -1)
            hspec, tab = _hy_spectrum(n, fparams)
            zz = _adaln_mm_conv(x, g_mix, mx[0], mx[1], in_w, in_b, hy_conv_w[j], conv_b, False)
            ys_x = [_hyena_mixer(zz, hspec, tab, hy_d[j])]
            if ctx_later:
                hspec_c, tab_c = _hy_spectrum(lc, fparams)
                zz_c = _adaln_mm_conv(ctx, g_mix, mc[0], mc[1], in_w, in_b, hy_conv_w[j], conv_b, False)
                ys_c = [_hyena_mixer(zz_c, hspec_c, tab_c, hy_d[j])]
            out_ws = [hy_out_w[j].astype(BF16)]
            out_b = hy_out_b[j].reshape(1, dm)

        x = _mix_mlp(x, mx[2], out_b, ys_x, out_ws, g_mlp, mx[3], mx[4], mx[5], w1, w2, final_g, i == DEPTH - 1)
        if ctx_later:
            ctx = _mix_mlp(ctx, mc[2], out_b, ys_c, out_ws, g_mlp, mc[3], mc[4], mc[5], w1, w2, final_g, False)
    return x
```

```python
import functools
import math

import numpy as np
import jax
import jax.numpy as jnp
from jax import lax
from jax.experimental import pallas as pl
from jax.experimental.pallas import tpu as pltpu

F32 = jnp.float32
BF16 = jnp.bfloat16

D_MODEL = 1024
DEPTH = 4
GRID_W = 64
N_MOD = 6
NORM_EPS = 1e-6

S5_WIDTH = 512
S5_GROUP_CH = 16
S5_STATE = 64
S5_SUPER = 4
S5_SLAB = 2 * 8 * S5_STATE
S5_LANES = S5_SUPER * S5_SLAB

M2_INNER = 1024
M2_HEAD_DIM = 64
M2_HEADS = 16
M2_GROUPS = 4
M2_STATE = 128
M2_BC = M2_GROUPS * M2_STATE
M2_XBC = M2_INNER + 2 * M2_BC
M2_GROUP_W = (M2_HEADS // M2_GROUPS) * M2_HEAD_DIM

HY_WIDTH = 1024
HY_ORDER = 2
HY_BANDS = 16
HY_EMB = 2 * HY_BANDS + 1
HY_HIDDEN = 64
HY_MIN_DECAY = math.log(1e-2) / 1.5
HY_MAX_DECAY = math.log(1e-2) / 0.3

V7X_VMEM_LIMIT_BYTES = 48 * 1024 * 1024


def _cparams(*sem):
    return pltpu.CompilerParams(dimension_semantics=sem, vmem_limit_bytes=V7X_VMEM_LIMIT_BYTES)


def _bdot(a, b):
    return jnp.dot(a.astype(BF16), b.astype(BF16), preferred_element_type=F32)


def _split3(a):
    a1 = a.astype(BF16)
    r1 = a - a1.astype(F32)
    a2 = r1.astype(BF16)
    a3 = (r1 - a2.astype(F32)).astype(BF16)
    return a1, a2, a3


def _dot_exact_rhs(a, b_bf16):
    a1, a2, a3 = _split3(a)
    d = functools.partial(jnp.dot, preferred_element_type=F32)
    return d(a1, b_bf16) + d(a2, b_bf16) + d(a3, b_bf16)


def _dot_split2_rhs(a, b_bf16):
    a1 = a.astype(BF16)
    a2 = (a - a1.astype(F32)).astype(BF16)
    d = functools.partial(jnp.dot, preferred_element_type=F32)
    return d(a1, b_bf16) + d(a2, b_bf16)


def _dot_exact_lhs(a_bf16, b):
    b1, b2, b3 = _split3(b)
    d = functools.partial(jnp.dot, preferred_element_type=F32)
    return d(a_bf16, b1) + d(a_bf16, b2) + d(a_bf16, b3)


def _dot3(a, b):
    a1 = a.astype(BF16)
    a2 = (a - a1.astype(F32)).astype(BF16)
    b1 = b.astype(BF16)
    b2 = (b - b1.astype(F32)).astype(BF16)
    d = functools.partial(jnp.dot, preferred_element_type=F32)
    return d(a1, b1) + d(a1, b2) + d(a2, b1)


def _silu(x):
    return x * (1.0 / (1.0 + jnp.exp(-x)))


def _sigmoid(x):
    return 1.0 / (1.0 + jnp.exp(-x))


def _adaln(x, g, shift, scale):
    ms = jnp.mean(x * x, axis=-1, keepdims=True)
    return (x * lax.rsqrt(ms + NORM_EPS) * g) * (1.0 + scale) + shift


def _mod_kernel(c_ref, w_ref, b_ref, o_ref):
    o_ref[0] = _bdot(_silu(c_ref[...]), w_ref[0]) + b_ref[0]


def _modulation(c8, mod_w, mod_b):
    n = mod_w.shape[-1]
    tn = 1536
    return pl.pallas_call(
        _mod_kernel,
        grid=(DEPTH, n // tn),
        in_specs=[pl.BlockSpec((8, D_MODEL), lambda i, j: (0, 0)),
                  pl.BlockSpec((1, D_MODEL, tn), lambda i, j: (i, 0, j)),
                  pl.BlockSpec((1, 1, tn), lambda i, j: (i, 0, j))],
        out_specs=pl.BlockSpec((1, 8, tn), lambda i, j: (i, 0, j)),
        out_shape=jax.ShapeDtypeStruct((DEPTH, 8, n), F32),
        compiler_params=_cparams("parallel", "parallel"),
        name="modulation",
    )(c8, mod_w, mod_b.reshape(DEPTH, 1, n))


def _add_pe_kernel(x_ref, pe_ref, o_ref):
    o_ref[0] = x_ref[0] + pe_ref[...]


def _add_pe(x, pe):
    b, l, d = x.shape
    tl = min(l, 1024)
    return pl.pallas_call(
        _add_pe_kernel,
        grid=(l // tl, b),
        in_specs=[pl.BlockSpec((1, tl, d), lambda i, bb: (bb, i, 0)),
                  pl.BlockSpec((tl, d), lambda i, bb: (i, 0))],
        out_specs=pl.BlockSpec((1, tl, d), lambda i, bb: (bb, i, 0)),
        out_shape=jax.ShapeDtypeStruct(x.shape, F32),
        compiler_params=_cparams("parallel", "parallel"),
        name="add_pe",
    )(x, pe)


CONV_COLS = 512


def _adaln_mm_conv_kernel(ncv, act, x_ref, xp_ref, xn_ref, g_ref, sh_ref, sc_ref, w_ref, b_ref, cw_ref, cb_ref,
                          o_ref):
    i = pl.program_id(1)
    tl = x_ref.shape[1]
    rows = tl + 16
    xe = jnp.concatenate([xp_ref[0], x_ref[0], xn_ref[0]], axis=0)
    h = _adaln(xe, g_ref[...], sh_ref[0], sc_ref[0]).astype(BF16)
    n = w_ref.shape[1]
    r = lax.broadcasted_iota(jnp.int32, (tl, 1), 0)
    first = (r == 0) & (i == 0)
    last = (r == tl - 1) & (i == pl.num_programs(1) - 1)
    for c0 in range(0, n, CONV_COLS):
        cols = slice(c0, min(c0 + CONV_COLS, n))
        p = jnp.dot(h, w_ref[:, cols], preferred_element_type=F32) + b_ref[:, cols]
        mid = p[8:8 + tl]
        if c0 < ncv:
            prev = jnp.where(first, 0.0, pltpu.roll(p, 1, axis=0)[8:8 + tl])
            nxt = jnp.where(last, 0.0, pltpu.roll(p, rows - 1, axis=0)[8:8 + tl])
            cw = cw_ref[:, cols]
            mid = prev * cw[0:1] + mid * cw[1:2] + nxt * cw[2:3] + cb_ref[:, cols]
            if act:
                mid = _silu(mid)
        o_ref[0, :, cols] = mid.astype(o_ref.dtype)


def _adaln_mm_conv(x, g, shift, scale, w_bf16, bias, conv_w, conv_b, act):
    b, l, d = x.shape
    n = w_bf16.shape[1]
    ncv = conv_w.shape[1]
    assert ncv % CONV_COLS == 0
    tl = min(l, 512)
    t8 = tl // 8
    nblk8 = l // 8
    cst = lambda bb, i: (0, 0)
    vec = pl.BlockSpec((1, 1, d), lambda bb, i: (bb, 0, 0))
    return pl.pallas_call(
        functools.partial(_adaln_mm_conv_kernel, ncv, act),
        grid=(b, l // tl),
        in_specs=[pl.BlockSpec((1, tl, d), lambda bb, i: (bb, i, 0)),
                  pl.BlockSpec((1, 8, d), lambda bb, i: (bb, jnp.maximum(i * t8 - 1, 0), 0)),
                  pl.BlockSpec((1, 8, d), lambda bb, i: (bb, jnp.minimum((i + 1) * t8, nblk8 - 1), 0)),
                  pl.BlockSpec((1, d), cst), vec, vec,
                  pl.BlockSpec((d, n), cst), pl.BlockSpec((1, n), cst),
                  pl.BlockSpec((3, ncv), cst), pl.BlockSpec((1, ncv), cst)],
        out_specs=pl.BlockSpec((1, tl, n), lambda bb, i: (bb, i, 0)),
        out_shape=jax.ShapeDtypeStruct((b, l, n), BF16),
        compiler_params=_cparams("parallel", "parallel"),
        name="adaln_mm_conv",
    )(x, x, x, g, shift, scale, w_bf16, bias, conv_w, conv_b)


MLP_CHUNK = 512


def _mix_mlp_kernel(n_in, final_norm, *refs):
    x_ref, ga_ref, bo_ref = refs[0:3]
    y_refs = refs[3:3 + n_in]
    wo_refs = refs[3 + n_in:3 + 2 * n_in]
    g_ref, sh_ref, sc_ref, gf_ref, w1_ref, w2_ref, fg_ref, o_ref, a_scr = refs[3 + 2 * n_in:]
    mix = bo_ref[...] + _bdot(y_refs[0][0], wo_refs[0][...])
    for y_ref, wo_ref in zip(y_refs[1:], wo_refs[1:]):
        mix = mix + _bdot(y_ref[0], wo_ref[...])
    x1 = x_ref[0] + ga_ref[0] * mix
    h = _adaln(x1, g_ref[...], sh_ref[0], sc_ref[0]).astype(BF16)
    for c in range(w1_ref.shape[1] // MLP_CHUNK):
        cols = slice(c * MLP_CHUNK, (c + 1) * MLP_CHUNK)
        a = jnp.dot(h, w1_ref[:, cols], preferred_element_type=F32)
        a_scr[:, cols] = jnp.square(jnp.maximum(a, 0.0)).astype(BF16)
    y = x1 + gf_ref[0] * jnp.dot(a_scr[...], w2_ref[...], preferred_element_type=F32)
    if final_norm:
        ms = jnp.mean(y * y, axis=-1, keepdims=True)
        y = y * lax.rsqrt(ms + NORM_EPS) * fg_ref[...]
    o_ref[0] = y


def _mix_mlp(x, gate_a, out_b, ys, out_ws, g, shift, scale, gate_f, w1_bf16, w2_bf16, final_g, final_norm):
    b, l, d = x.shape
    hdim = w1_bf16.shape[1]
    tl = min(l, 512)
    n_in = len(ys)
    row = lambda bb, i: (bb, i, 0)
    vec = pl.BlockSpec((1, 1, d), lambda bb, i: (bb, 0, 0))
    cvec = pl.BlockSpec((1, d), lambda bb, i: (0, 0))
    resident = lambda a: pl.BlockSpec(a.shape, lambda bb, i: (0, 0), pipeline_mode=pl.Buffered(1))
    in_specs = [pl.BlockSpec((1, tl, d), row), vec, cvec]
    in_specs += [pl.BlockSpec((1, tl, y.shape[-1]), row) for y in ys]
    in_specs += [resident(w) for w in out_ws]
    in_specs += [cvec, vec, vec, vec, resident(w1_bf16), resident(w2_bf16), cvec]
    return pl.pallas_call(
        functools.partial(_mix_mlp_kernel, n_in, final_norm),
        grid=(b, l // tl),
        in_specs=in_specs,
        out_specs=pl.BlockSpec((1, tl, d), row),
        out_shape=jax.ShapeDtypeStruct(x.shape, F32),
        scratch_shapes=[pltpu.VMEM((tl, hdim), BF16)],
        compiler_params=_cparams("parallel", "parallel"),
        name="mix_mlp",
    )(x, gate_a, out_b, *ys, *out_ws, g, shift, scale, gate_f, w1_bf16, w2_bf16, final_g)


def _s5_prepare(lam_re, lam_im, log_dt, b_re, b_im, c_re, c_im):
    step = jnp.exp(log_dt)[..., None]
    mag = jnp.exp(lam_re * step)
    ar = mag * jnp.cos(lam_im * step)
    ai = mag * jnp.sin(lam_im * step)
    den = lam_re * lam_re + lam_im * lam_im
    fr = ((ar - 1.0) * lam_re + ai * lam_im) / den
    fi = (ai * lam_re - (ar - 1.0) * lam_im) / den
    eye = jnp.eye(8, dtype=F32)

    def blockdiag_in(b):
        b4 = b.reshape(S5_SUPER, 8, S5_STATE, S5_GROUP_CH)
        return jnp.einsum('sgpk,gh->sgkhp', b4, eye).reshape(S5_SUPER, 128, 512)

    bd = jnp.concatenate([blockdiag_in(b_re), blockdiag_in(b_im)], axis=-1).astype(BF16)

    cpr = c_re[None] * fr[:, :, None, :] - c_im[None] * fi[:, :, None, :]
    cpi = c_re[None] * fi[:, :, None, :] + c_im[None] * fr[:, :, None, :]

    def blockdiag_out(c):
        c5 = c.reshape(2, S5_SUPER, 8, S5_GROUP_CH, S5_STATE)
        return jnp.einsum('dsgkp,gh->dsgphk', c5, eye).reshape(2, S5_SUPER, 512, 128)

    cd = jnp.concatenate([blockdiag_out(cpr), blockdiag_out(-cpi)], axis=2)
    cd = jnp.concatenate([cd[0], cd[1]], axis=-1).astype(BF16)
    rows = lambda a: jnp.repeat(a.reshape(2, S5_SUPER * 512), 4, axis=0)
    return bd, cd, rows(ar), rows(ai)


S5_ROWS = 256


def _s5_scan_kernel(tc, uf_ref, ub_ref, bd_ref, cd_ref, ar_ref, ai_ref, h0_ref, flip_ref,
                    yf_ref, yb_ref, hfin_ref, u_scr, g_scr, y_scr, h_scr):
    c = pl.program_id(0)

    @pl.when(c == 0)
    def _():
        h_scr[...] = h0_ref[...]

    flip = flip_ref[...]
    rows = 8 * tc
    for b in range(4):
        uf = uf_ref[b].astype(F32)
        ub = jnp.dot(flip, ub_ref[b].astype(BF16), preferred_element_type=F32)
        for sg in range(S5_SUPER):
            u_scr[sg, pl.ds(b, tc, stride=8), :] = uf[:, sg * 128:(sg + 1) * 128]
            u_scr[sg, pl.ds(4 + b, tc, stride=8), :] = ub[:, sg * 128:(sg + 1) * 128]

    half = S5_SLAB // 256
    for sg in range(S5_SUPER):
        for r0 in range(0, rows, S5_ROWS):
            g_scr[r0:r0 + S5_ROWS, sg * S5_SLAB:(sg + 1) * S5_SLAB] = jnp.dot(
                u_scr[sg, r0:r0 + S5_ROWS, :].astype(BF16), bd_ref[sg], preferred_element_type=F32)

        ar = [ar_ref[:, (sg * half + k) * 128:(sg * half + k + 1) * 128] for k in range(half)]
        ai = [ai_ref[:, (sg * half + k) * 128:(sg * half + k + 1) * 128] for k in range(half)]
        l_re = [sg * S5_SLAB + k * 128 for k in range(half)]
        l_im = [sg * S5_SLAB + (half + k) * 128 for k in range(half)]
        hr = [h_scr[:, l_re[k]:l_re[k] + 128] for k in range(half)]
        hi = [h_scr[:, l_im[k]:l_im[k] + 128] for k in range(half)]
        for j in range(tc):
            r0 = j * 8
            for k in range(half):
                nr = ar[k] * hr[k] - ai[k] * hi[k] + g_scr[r0:r0 + 8, l_re[k]:l_re[k] + 128]
                ni = ar[k] * hi[k] + ai[k] * hr[k] + g_scr[r0:r0 + 8, l_im[k]:l_im[k] + 128]
                g_scr[r0:r0 + 8, l_re[k]:l_re[k] + 128] = nr
                g_scr[r0:r0 + 8, l_im[k]:l_im[k] + 128] = ni
                hr[k], hi[k] = nr, ni
        for k in range(half):
            h_scr[:, l_re[k]:l_re[k] + 128] = hr[k]
            h_scr[:, l_im[k]:l_im[k] + 128] = hi[k]

        for r0 in range(0, rows, S5_ROWS):
            y = jnp.dot(g_scr[r0:r0 + S5_ROWS, sg * S5_SLAB:(sg + 1) * S5_SLAB].astype(BF16), cd_ref[sg],
                        preferred_element_type=F32)
            y_scr[2 * sg, r0:r0 + S5_ROWS, :] = y[:, :128]
            y_scr[2 * sg + 1, r0:r0 + S5_ROWS, :] = y[:, 128:]

    for b in range(4):
        yf_ref[b] = jnp.concatenate(
            [y_scr[2 * sg, pl.ds(b, tc, stride=8), :] for sg in range(S5_SUPER)], axis=1)
        yb = jnp.concatenate(
            [y_scr[2 * sg + 1, pl.ds(4 + b, tc, stride=8), :] for sg in range(S5_SUPER)], axis=1)
        yb_ref[b] = _dot_exact_lhs(flip, yb)

    hfin_ref[...] = h_scr[...]


def _s5_scan(u, ucol, bd, cd, ar8, ai8, h0):
    b, l, _ = u.shape
    w = S5_WIDTH
    assert b == 4
    tc = 128
    nc = l // tc
    flip = jnp.asarray(np.eye(tc, dtype=np.float32)[::-1], dtype=BF16)
    full = lambda shape: pl.BlockSpec(shape, lambda c: (0,) * len(shape))
    y_shape = jax.ShapeDtypeStruct((b, l, w), F32)
    return pl.pallas_call(
        functools.partial(_s5_scan_kernel, tc),
        grid=(nc,),
        in_specs=[pl.BlockSpec((4, tc, w), lambda c: (0, c, ucol)),
                  pl.BlockSpec((4, tc, w), lambda c: (0, nc - 1 - c, ucol)),
                  full(bd.shape), full(cd.shape), full(ar8.shape), full(ai8.shape),
                  full(h0.shape), full(flip.shape)],
        out_specs=[pl.BlockSpec((4, tc, w), lambda c: (0, c, 0)),
                   pl.BlockSpec((4, tc, w), lambda c: (0, nc - 1 - c, 0)),
                   full((8, S5_LANES))],
        out_shape=[y_shape, y_shape, jax.ShapeDtypeStruct((8, S5_LANES), F32)],
        scratch_shapes=[pltpu.VMEM((S5_SUPER, 8 * tc, 128), F32), pltpu.VMEM((8 * tc, S5_LANES), F32),
                        pltpu.VMEM((2 * S5_SUPER, 8 * tc, 128), F32), pltpu.VMEM((8, S5_LANES), F32)],
        compiler_params=_cparams("arbitrary"),
        name="s5_scan",
    )(u, u, bd, cd, ar8, ai8, h0, flip)


def _gelu_tanh(x):
    return 0.5 * x * (1.0 + jnp.tanh(math.sqrt(2.0 / math.pi) * (x + 0.044715 * (x * x * x))))


def _s5_glu_kernel(yf_ref, yb_ref, u_ref, d_ref, w_ref, b_ref, o_ref):
    y = _gelu_tanh(yf_ref[0] + yb_ref[0] + d_ref[...] * u_ref[0].astype(F32))
    o_ref[0] = (y * _sigmoid(_bdot(y, w_ref[...]) + b_ref[...])).astype(o_ref.dtype)


def _s5_glu(yf, yb, u, ucol, d_skip, glu_w_bf16, glu_b):
    b, l, w = yf.shape
    tl = min(l, 1024)
    row = lambda bb, i: (bb, i, 0)
    cst = lambda bb, i: (0, 0)
    return pl.pallas_call(
        _s5_glu_kernel,
        grid=(b, l // tl),
        in_specs=[pl.BlockSpec((1, tl, w), row)] * 2 + [
            pl.BlockSpec((1, tl, w), lambda bb, i: (bb, i, ucol)),
            pl.BlockSpec((1, w), cst), pl.BlockSpec((w, w), cst), pl.BlockSpec((1, w), cst)],
        out_specs=pl.BlockSpec((1, tl, w), row),
        out_shape=jax.ShapeDtypeStruct(yf.shape, BF16),
        compiler_params=_cparams("parallel", "parallel"),
        name="s5_glu",
    )(yf, yb, u, d_skip, glu_w_bf16, glu_b)


SSD_T = 128
SSD_NB = 4


def _softplus(x):
    return jnp.maximum(x, 0.0) + jnp.log(1.0 + jnp.exp(-jnp.abs(x)))


def _ssd_kernel(xs_ref, bm_ref, cm_ref, dtc_ref, dtr_ref, bias_c_ref, bias_r_ref, a_c_ref, a_r_ref,
                tri_ref, trit_ref, exp_ref, h0_ref, y_ref, hfin_ref, s_scr):
    c = pl.program_id(2)

    @pl.when(c == 0)
    def _():
        s_scr[...] = h0_ref[0]

    tri = tri_ref[0]
    expand = exp_ref[...]
    mask = tri > 0
    for bi in range(SSD_NB):
        dt_c = _softplus(dtc_ref[0, bi] + bias_c_ref[0])
        dt_r = _softplus(dtr_ref[0, bi] + bias_r_ref[0])
        cum_c = _dot_exact_lhs(tri, dt_c * a_c_ref[0])
        cum_r = _dot_exact_rhs(dt_r * a_r_ref[0], trit_ref[0])
        tot_c = jnp.min(cum_c, axis=0, keepdims=True)

        t = dt_c.shape[0]
        ex = _dot_split2_rhs(jnp.concatenate([dt_c, jnp.exp(cum_c), jnp.exp(tot_c - cum_c)], axis=0), expand)
        dt_x = ex[:t]
        in_x = ex[t:2 * t]
        out_x = ex[2 * t:]
        tot_x = jnp.where(pl.program_id(0) == 0, in_x[t - 1:t], in_x[0:1])

        xdt = xs_ref[bi].astype(F32) * dt_x
        xout = (xdt * out_x).astype(BF16)
        xdt = xdt.astype(BF16)
        for g in range(M2_GROUPS):
            bg = bm_ref[bi, :, g * M2_STATE:(g + 1) * M2_STATE].astype(BF16)
            cg = cm_ref[bi, :, g * M2_STATE:(g + 1) * M2_STATE].astype(BF16)
            cb = lax.dot_general(cg, bg, (((1,), (1,)), ((), ())), preferred_element_type=F32)
            gc = slice(g * M2_GROUP_W, (g + 1) * M2_GROUP_W)
            s_prev = s_scr[bi, g]
            y_off = jnp.dot(cg, s_prev.astype(BF16), preferred_element_type=F32) * in_x[:, gc]
            y_heads = []
            for r in range(M2_HEADS // M2_GROUPS):
                h = g * (M2_HEADS // M2_GROUPS) + r
                seg = cum_c[:, h:h + 1] - cum_r[h:h + 1, :]
                m = (cb * jnp.exp(jnp.where(mask, seg, -jnp.inf))).astype(BF16)
                hc = slice(h * M2_HEAD_DIM, (h + 1) * M2_HEAD_DIM)
                y_heads.append(jnp.dot(m, xdt[:, hc], preferred_element_type=F32))
            y_ref[0, bi, :, gc] = (jnp.concatenate(y_heads, axis=1) + y_off).astype(y_ref.dtype)
            s_scr[bi, g] = s_prev * tot_x[:, gc] + lax.dot_general(
                bg, xout[:, gc], (((0,), (0,)), ((), ())), preferred_element_type=F32)

    @pl.when(c == pl.num_programs(2) - 1)
    def _():
        hfin_ref[0] = s_scr[...]


def _ssd(xbc, dt_col, dt_row, dt_bias, a_coef, h0):
    b, l, _ = xbc.shape
    t = SSD_T
    nc = l // t
    tri_f = np.tril(np.ones((t, t), np.float32))
    tri = jnp.asarray(np.stack([tri_f, tri_f.T]), dtype=BF16)
    trit = jnp.asarray(np.stack([tri_f.T, tri_f]), dtype=BF16)
    expand = jnp.asarray(np.kron(np.eye(M2_HEADS, dtype=np.float32),
                                 np.ones((1, M2_HEAD_DIM), np.float32)), dtype=BF16)
    chunk = lambda d, c: c + d * (nc - 1 - 2 * c)
    nb = SSD_NB
    state = pl.BlockSpec((1, nb, M2_GROUPS, M2_STATE, M2_GROUP_W), lambda d, bb, c: (d, bb, 0, 0, 0))
    return pl.pallas_call(
        _ssd_kernel,
        grid=(2, b // nb, nc),
        in_specs=[pl.BlockSpec((nb, t, M2_INNER), lambda d, bb, c: (bb, chunk(d, c), 0)),
                  pl.BlockSpec((nb, t, M2_BC), lambda d, bb, c: (bb, chunk(d, c), 2)),
                  pl.BlockSpec((nb, t, M2_BC), lambda d, bb, c: (bb, chunk(d, c), 3)),
                  pl.BlockSpec((1, nb, t, M2_HEADS), lambda d, bb, c: (d, bb, chunk(d, c), 0)),
                  pl.BlockSpec((1, nb, M2_HEADS, t), lambda d, bb, c: (d, bb, 0, chunk(d, c))),
                  pl.BlockSpec((1, 1, M2_HEADS), lambda d, bb, c: (d, 0, 0)),
                  pl.BlockSpec((1, M2_HEADS, 1), lambda d, bb, c: (d, 0, 0)),
                  pl.BlockSpec((1, 1, M2_HEADS), lambda d, bb, c: (d, 0, 0)),
                  pl.BlockSpec((1, M2_HEADS, 1), lambda d, bb, c: (d, 0, 0)),
                  pl.BlockSpec((1, t, t), lambda d, bb, c: (d, 0, 0)),
                  pl.BlockSpec((1, t, t), lambda d, bb, c: (d, 0, 0)),
                  pl.BlockSpec((M2_HEADS, M2_INNER), lambda d, bb, c: (0, 0)),
                  state],
        out_specs=[pl.BlockSpec((1, nb, t, M2_INNER), lambda d, bb, c: (d, bb, chunk(d, c), 0)), state],
        out_shape=[jax.ShapeDtypeStruct((2, b, l, M2_INNER), BF16),
                   jax.ShapeDtypeStruct((2, b, M2_GROUPS, M2_STATE, M2_GROUP_W), F32)],
        scratch_shapes=[pltpu.VMEM((nb, M2_GROUPS, M2_STATE, M2_GROUP_W), F32)],
        compiler_params=_cparams("parallel", "parallel", "arbitrary"),
        name="ssd",
    )(xbc, xbc, xbc, dt_col, dt_row, dt_bias.reshape(2, 1, M2_HEADS), dt_bias.reshape(2, M2_HEADS, 1),
      a_coef.reshape(2, 1, M2_HEADS), a_coef.reshape(2, M2_HEADS, 1), tri, trit, expand, h0)


def _m2_gate_kernel(yf_ref, yb_ref, xs_ref, z_ref, d_ref, g_ref, o_ref):
    y = (yf_ref[0, 0].astype(F32) + yb_ref[0, 0].astype(F32)
         + d_ref[...] * xs_ref[0].astype(F32)) * _silu(z_ref[0].astype(F32))
    ms = jnp.mean(y * y, axis=-1, keepdims=True)
    o_ref[0] = (y * lax.rsqrt(ms + NORM_EPS) * g_ref[...]).astype(o_ref.dtype)


def _m2_gate(ydir, xbc, z, zcol, d_x, norm_g):
    _, b, l, w = ydir.shape
    tl = min(l, 512)
    row = lambda bb, i: (bb, i, 0)
    cst = lambda bb, i: (0, 0)
    return pl.pallas_call(
        _m2_gate_kernel,
        grid=(b, l // tl),
        in_specs=[pl.BlockSpec((1, 1, tl, w), lambda bb, i: (0, bb, i, 0)),
                  pl.BlockSpec((1, 1, tl, w), lambda bb, i: (1, bb, i, 0)),
                  pl.BlockSpec((1, tl, w), row), pl.BlockSpec((1, tl, w), lambda bb, i: (bb, i, zcol)),
                  pl.BlockSpec((1, w), cst), pl.BlockSpec((1, w), cst)],
        out_specs=pl.BlockSpec((1, tl, w), row),
        out_shape=jax.ShapeDtypeStruct((b, l, w), BF16),
        compiler_params=_cparams("parallel", "parallel"),
        name="m2_gate",
    )(ydir, ydir, xbc, z, d_x, norm_g)


HY_N2 = 128


def _round8(n):
    return (n + 7) // 8 * 8


def _dft2_tables(n1, n2):
    n = n1 * n2
    k1n = n1 // 2 + 1
    k1p = _round8(k1n)
    k1 = np.arange(k1p, dtype=np.float64)[:, None]
    valid = (k1 < k1n).astype(np.float64)
    i1 = np.arange(n1, dtype=np.float64)[None, :]
    t = n2 * i1[None] + np.arange(n2, dtype=np.float64)[:, None, None]
    th = 2.0 * np.pi * k1[None] * t / n
    stage1_tw = np.concatenate([np.cos(th) * valid[None], -np.sin(th) * valid[None]], axis=1)
    i2 = np.arange(n2, dtype=np.float64)
    k2 = np.arange(n2, dtype=np.float64)[:, None]
    th2 = 2.0 * np.pi * k2 * i2[None, :] / n2
    fc, fs = np.cos(th2), np.sin(th2)
    fwd2 = np.block([[fc, fs], [-fs, fc]])
    inv2 = np.block([[fc, -fs], [fs, fc]])
    wk = (np.where((k1 == 0) | (k1 == n1 // 2), 1.0, 2.0) * valid).T[None]
    tho = np.transpose(th[:, :, :n1 // 2], (0, 2, 1))
    last_tw = np.concatenate([np.cos(tho) * wk, -np.sin(tho) * wk], axis=2) / n
    f = lambda a: jnp.asarray(a, dtype=F32)
    return dict(k1n=k1n, k1p=k1p, stage1_tw=f(stage1_tw), fwd2=f(fwd2), inv2=f(inv2), last_tw=f(last_tw))


def _dft1_tables(l):
    n = 2 * l
    kn = l + 1
    kp = _round8(kn)
    k = np.arange(kp, dtype=np.float64)[:, None]
    valid = (k < kn).astype(np.float64)
    t = np.arange(n, dtype=np.float64)[None, :]
    th = 2.0 * np.pi * k * t / n
    fwd = np.concatenate([np.cos(th) * valid, -np.sin(th) * valid], axis=0)
    wk = np.where((k == 0) | (k == l), 1.0, 2.0) * valid
    o = np.arange(l, dtype=np.float64)[:, None]
    tho = 2.0 * np.pi * o * k.T / n
    inv = np.concatenate([np.cos(tho) * wk.T, -np.sin(tho) * wk.T], axis=1) / n
    f = lambda a: jnp.asarray(a, dtype=F32)
    return dict(kp=kp, fwd=f(fwd), inv=f(inv))


def _rowdft_kernel(hi, f_ref, x_ref, o_ref):
    dot = _dot3 if hi else _bdot
    o_ref[0] = dot(f_ref[...], x_ref[0])


def _rowdft(fmat, x, hi):
    bx, r, n = x.shape
    m = fmat.shape[0]
    tn = min(n, 4096)
    return pl.pallas_call(
        functools.partial(_rowdft_kernel, hi),
        grid=(bx, n // tn),
        in_specs=[pl.BlockSpec((m, r), lambda b, j: (0, 0)),
                  pl.BlockSpec((1, r, tn), lambda b, j: (b, 0, j))],
        out_specs=pl.BlockSpec((1, m, tn), lambda b, j: (b, 0, j)),
        out_shape=jax.ShapeDtypeStruct((bx, m, n), F32),
        compiler_params=_cparams("parallel", "parallel"),
        name="hy_rowdft",
    )(fmat if hi else fmat.astype(BF16), x)


HY_TS = 8


HY_PITCH = HY_N2 + 8
HY_CONV_VMEM_BYTES = 58 * 1024 * 1024
HY_STAGE2_UNROLL = 12


def _hy_conv_kernel(k1n, k1p, n2, v_ref, gate_ref, h_ref, st1_ref, fwd_ref, inv_ref, last_ref,
                    d_ref, o_ref, a_scr, t_scr):
    n1h = v_ref.shape[0] // n2

    def copy_in(i, _):
        t_scr[pl.ds(pl.multiple_of(i * HY_PITCH, 8), n2), :] = (
            v_ref[pl.ds(pl.multiple_of(i * n2, n2), n2), :].astype(F32))
        return 0
    lax.fori_loop(0, n1h, copy_in, 0, unroll=4)

    zero = jnp.zeros((n1h, 128), BF16)

    def stage1(j, _):
        s = 2 * j
        xa = t_scr[pl.ds(s, n1h, stride=HY_PITCH), :].astype(BF16)
        xb = t_scr[pl.ds(s + 1, n1h, stride=HY_PITCH), :].astype(BF16)
        xs = jnp.concatenate([jnp.concatenate([xa, zero], axis=1), jnp.concatenate([zero, xb], axis=1)], axis=0)
        y = jnp.dot(st1_ref[j], xs, preferred_element_type=F32)
        a_scr[pl.ds(s, 2 * k1p, stride=HY_PITCH), :] = y[:, :128]
        a_scr[pl.ds(s + 1, 2 * k1p, stride=HY_PITCH), :] = y[:, 128:]
        return 0
    lax.fori_loop(0, n2 // 2, stage1, 0, unroll=16)

    def cmul(ar, ai, br, bi):
        return ar * br - ai * bi, ar * bi + ai * br

    def stage2(j, _):
        ks = (2 * j, 2 * j + 1)
        rows_r = [pl.ds(pl.multiple_of(k * HY_PITCH, 8), n2) for k in ks]
        rows_i = [pl.ds(pl.multiple_of((k1p + k) * HY_PITCH, 8), n2) for k in ks]
        p = jnp.concatenate([jnp.concatenate([a_scr[rr, :] for rr in rows_r], axis=1),
                             jnp.concatenate([a_scr[ri, :] for ri in rows_i], axis=1)], axis=0)
        x = jnp.dot(fwd_ref[...], p.astype(BF16), preferred_element_type=F32)
        hr = jnp.concatenate([h_ref[0, ks[0]], h_ref[0, ks[1]]], axis=1).astype(F32)
        hi = jnp.concatenate([h_ref[1, ks[0]], h_ref[1, ks[1]]], axis=1).astype(F32)
        qr, qi = cmul(x[:n2], x[n2:], hr, hi)
        z = jnp.dot(inv_ref[...], jnp.concatenate([qr, qi], axis=0).astype(BF16), preferred_element_type=F32)
        for n, (rr, ri) in enumerate(zip(rows_r, rows_i)):
            a_scr[rr, :] = z[:n2, n * 128:(n + 1) * 128]
            a_scr[ri, :] = z[n2:, n * 128:(n + 1) * 128]
        return 0

    pairs = k1p // 2
    unroll = max(u for u in range(1, HY_STAGE2_UNROLL + 1) if pairs % u == 0)
    lax.fori_loop(0, pairs, stage2, 0, unroll=unroll)

    def stage3(j, _):
        s = 2 * j
        zs = jnp.concatenate([a_scr[pl.ds(s, 2 * k1p, stride=HY_PITCH), :],
                              a_scr[pl.ds(s + 1, 2 * k1p, stride=HY_PITCH), :]], axis=1).astype(BF16)
        y = jnp.dot(last_ref[j], zs, preferred_element_type=F32)
        t_scr[pl.ds(s, n1h, stride=HY_PITCH), :] = y[:n1h, :128]
        t_scr[pl.ds(s + 1, n1h, stride=HY_PITCH), :] = y[n1h:, 128:]
        return 0
    lax.fori_loop(0, n2 // 2, stage3, 0, unroll=16)

    def finish(i, _):
        rows = pl.ds(pl.multiple_of(i * n2, n2), n2)
        y = t_scr[pl.ds(pl.multiple_of(i * HY_PITCH, 8), n2), :]
        o_ref[rows, :] = (gate_ref[rows, :].astype(F32)
                          * (y + d_ref[...] * v_ref[rows, :].astype(F32))).astype(o_ref.dtype)
        return 0
    lax.fori_loop(0, n1h, finish, 0, unroll=4)


def _hy_long_conv2(v, vpart, gate, gpart, hspec, order, d_vec, tab):
    b, l, _ = v.shape
    c = HY_WIDTH
    n2 = HY_N2
    n1h = l // n2
    k1p = tab['k1p']
    tiles = c // 128
    seq = pl.BlockSpec((None, l, 128), lambda j, bb: (bb, 0, j))
    vseq = pl.BlockSpec((None, l, 128), lambda j, bb: (bb, 0, vpart * tiles + j))
    gseq = pl.BlockSpec((None, l, 128), lambda j, bb: (bb, 0, gpart * tiles + j))
    resident = lambda a: pl.BlockSpec(a.shape, lambda j, bb: (0,) * a.ndim, pipeline_mode=pl.Buffered(1))
    s1 = tab['stage1_tw'][:, :, :n1h]
    st1 = jnp.concatenate([s1[0::2], s1[1::2]], axis=2).astype(BF16)
    lt = tab['last_tw']
    last = jnp.concatenate([lt[0::2], lt[1::2]], axis=1).astype(BF16)
    fwd2, inv2 = tab['fwd2'].astype(BF16), tab['inv2'].astype(BF16)
    return pl.pallas_call(
        functools.partial(_hy_conv_kernel, tab['k1n'], k1p, n2),
        grid=(tiles, b),
        in_specs=[vseq, gseq,
                  pl.BlockSpec((2, k1p, n2, 128), lambda j, bb: (0, 0, 0, order * tiles + j)),
                  resident(st1), resident(fwd2), resident(inv2), resident(last),
                  pl.BlockSpec((1, 128), lambda j, bb: (0, j))],
        out_specs=seq,
        out_shape=jax.ShapeDtypeStruct((b, l, c), BF16),
        scratch_shapes=[pltpu.VMEM((2 * k1p * HY_PITCH, 128), F32), pltpu.VMEM((n1h * HY_PITCH, 128), F32)],
        compiler_params=pltpu.CompilerParams(dimension_semantics=("parallel", "parallel"),
                                             vmem_limit_bytes=HY_CONV_VMEM_BYTES),
        name="hy_conv",
    )(v, gate, hspec, st1, fwd2, inv2, last, d_vec.reshape(1, c))


def _hy_filt_first_kernel(k1p, feat_ref, t_ref, keep_ref, w1_ref, b1_ref, q1_ref, w2_ref, b2_ref, q2_ref,
                          w3_ref, nd_ref, f_ref, o_ref, sum_ref):
    @pl.when(pl.program_id(0) == 0)
    def _():
        sum_ref[...] = jnp.zeros_like(sum_ref)

    n1 = feat_ref.shape[1]
    half = n1 // 2
    feats = feat_ref[...].reshape(HY_TS * n1, feat_ref.shape[2])
    hid = jnp.sin(q1_ref[...] * (_bdot(feats, w1_ref[...]) + b1_ref[...]))
    hid = jnp.sin(q2_ref[...] * (_bdot(hid, w2_ref[...]) + b2_ref[...]))
    acc = jnp.zeros(sum_ref.shape, F32)
    for s in range(HY_TS):
        hs = hid[s * n1:(s + 1) * n1]
        f = jnp.concatenate([_bdot(hs[:half], w3_ref[0]), _bdot(hs[half:], w3_ref[1])], axis=0)
        f = f * jnp.exp(t_ref[s] * nd_ref[...]) * keep_ref[s]
        acc = acc + jnp.sum(jnp.abs(f), axis=0, keepdims=True)
        y = _dot3(f_ref[s], f)
        o_ref[0, :, s, :] = y[:k1p]
        o_ref[1, :, s, :] = y[k1p:]
    sum_ref[...] += acc


def _hy_filt_first(l, tab, f_w1, f_b1, f_freq1, f_w2, f_b2, f_freq2, f_w3):
    c2 = HY_ORDER * HY_WIDTH
    n2 = HY_N2
    n1 = 2 * l // n2
    k1p = tab['k1p']
    t = np.linspace(0.0, 1.0, l, dtype=np.float32)[:, None]
    bands = np.linspace(1e-4, HY_BANDS - 1, HY_BANDS, dtype=np.float32)
    ang = np.float32(2.0 * math.pi / l) * np.arange(l, dtype=np.float32)[:, None] * bands
    feats = np.concatenate([t, np.cos(ang), -np.sin(ang)], axis=-1)
    rows = (n2 * np.arange(n1)[None, :] + np.arange(n2)[:, None]).reshape(-1)
    pos = np.where(rows < l, rows, np.where(rows == l, 0, 2 * l - rows))
    feats_t = jnp.asarray(np.pad(feats[pos], ((0, 0), (0, 128 - HY_EMB))).reshape(n2, n1, 128))
    t_t = jnp.asarray(t[pos].reshape(n2, n1, 1))
    keep_t = jnp.asarray((rows != l).astype(np.float32).reshape(n2, n1, 1))
    w1p = jnp.pad(f_w1, ((0, 128 - HY_EMB), (0, 0)))
    w3d = jnp.transpose(f_w3.reshape(HY_HIDDEN, HY_ORDER, 2, HY_WIDTH), (2, 0, 1, 3)).reshape(2, HY_HIDDEN, c2)
    deltas = jnp.abs(jnp.linspace(HY_MIN_DECAY, HY_MAX_DECAY, HY_WIDTH, dtype=F32))
    negd = jnp.tile(-deltas, HY_ORDER)[None, :]
    cst = lambda i: (0, 0)
    vec = lambda a: a.reshape(1, HY_HIDDEN)
    tile3 = lambda w: pl.BlockSpec((HY_TS, n1, w), lambda i: (i, 0, 0))
    return pl.pallas_call(
        functools.partial(_hy_filt_first_kernel, k1p),
        grid=(n2 // HY_TS,),
        in_specs=[tile3(128), tile3(1), tile3(1),
                  pl.BlockSpec((128, HY_HIDDEN), cst), pl.BlockSpec((1, HY_HIDDEN), cst),
                  pl.BlockSpec((1, HY_HIDDEN), cst),
                  pl.BlockSpec((HY_HIDDEN, HY_HIDDEN), cst), pl.BlockSpec((1, HY_HIDDEN), cst),
                  pl.BlockSpec((1, HY_HIDDEN), cst),
                  pl.BlockSpec((2, HY_HIDDEN, c2), lambda i: (0, 0, 0)),
                  pl.BlockSpec((1, c2), cst),
                  pl.BlockSpec((HY_TS, 2 * k1p, n1), lambda i: (i, 0, 0))],
        out_specs=[pl.BlockSpec((None, 2, k1p, HY_TS, c2), lambda i: (0, 0, 0, i, 0)),
                   pl.BlockSpec((1, c2), cst)],
        out_shape=[jax.ShapeDtypeStruct((1, 2, k1p, n2, c2), F32), jax.ShapeDtypeStruct((1, c2), F32)],
        compiler_params=_cparams("arbitrary"),
        name="hy_filt_first",
    )(feats_t, t_t, keep_t, w1p, vec(f_b1), vec(f_freq1), f_w2, vec(f_b2), vec(f_freq2), w3d, negd,
      tab['stage1_tw'])


def _hy_filt_mid_kernel(n2, a_ref, sum_ref, fwd_ref, o_ref):
    c2 = a_ref.shape[-1]
    x = _dot3(fwd_ref[...], a_ref[0, :, 0].reshape(2 * n2, c2)) * (1.0 / (sum_ref[...] + 1e-6))
    o_ref[0, 0] = x[:n2].astype(BF16)
    o_ref[1, 0] = x[n2:].astype(BF16)


def _hy_filt_mid(a5, tab, colsum):
    _, _, k1p, n2, c2 = a5.shape
    return pl.pallas_call(
        functools.partial(_hy_filt_mid_kernel, n2),
        grid=(k1p,),
        in_specs=[pl.BlockSpec((1, 2, 1, n2, c2), lambda k: (0, 0, k, 0, 0)),
                  pl.BlockSpec((1, c2), lambda k: (0, 0)),
                  pl.BlockSpec((2 * n2, 2 * n2), lambda k: (0, 0))],
        out_specs=pl.BlockSpec((2, 1, n2, c2), lambda k: (0, k, 0, 0)),
        out_shape=jax.ShapeDtypeStruct((2, k1p, n2, c2), BF16),
        compiler_params=_cparams("parallel"),
        name="hy_filt_mid",
    )(a5, colsum, tab['fwd2'])


def _hy_ctx_kernel(kp, fw_ref, inv_ref, h_ref, v_ref, gate_ref, d_ref, o_ref):
    v = v_ref[0].astype(F32)
    u = _bdot(fw_ref[...], v)
    ur, ui = u[:kp], u[kp:]
    hr, hi = h_ref[0], h_ref[1]
    q = jnp.concatenate([ur * hr - ui * hi, ur * hi + ui * hr], axis=0)
    o_ref[0] = (gate_ref[0].astype(F32) * (_bdot(inv_ref[...], q) + d_ref[...] * v)).astype(o_ref.dtype)


def _hy_long_conv1(v, vpart, gate, gpart, hspec, order, d_vec, tab):
    b, l, _ = v.shape
    c = HY_WIDTH
    kp = tab['kp']
    row = lambda bb: (bb, 0, 0)
    return pl.pallas_call(
        functools.partial(_hy_ctx_kernel, kp),
        grid=(b,),
        in_specs=[pl.BlockSpec((2 * kp, l), lambda bb: (0, 0)),
                  pl.BlockSpec((l, 2 * kp), lambda bb: (0, 0)),
                  pl.BlockSpec((2, kp, c), lambda bb: (0, 0, order)),
                  pl.BlockSpec((1, l, c), lambda bb: (bb, 0, vpart)),
                  pl.BlockSpec((1, l, c), lambda bb: (bb, 0, gpart)),
                  pl.BlockSpec((1, c), lambda bb: (0, 0))],
        out_specs=pl.BlockSpec((1, l, c), row),
        out_shape=jax.ShapeDtypeStruct((b, l, c), BF16),
        compiler_params=_cparams("parallel"),
        name="hy_ctx_conv",
    )(tab['fwd'][:, :l].astype(BF16), tab['inv'].astype(BF16), hspec, v, gate, d_vec.reshape(1, c))


def _hy_filter_kernel(feat_ref, t_ref, keep_ref, w1_ref, b1_ref, q1_ref, w2_ref, b2_ref, q2_ref,
                      w3_ref, nd_ref, f_ref, sum_ref):
    @pl.when(pl.program_id(0) == 0)
    def _():
        sum_ref[...] = jnp.zeros_like(sum_ref)

    hid = jnp.sin(q1_ref[...] * (_bdot(feat_ref[...], w1_ref[...]) + b1_ref[...]))
    hid = jnp.sin(q2_ref[...] * (_bdot(hid, w2_ref[...]) + b2_ref[...]))
    f = _bdot(hid, w3_ref[0]) * jnp.exp(t_ref[...] * nd_ref[...]) * keep_ref[...]
    f_ref[...] = f
    sum_ref[...] += jnp.sum(jnp.abs(f), axis=0, keepdims=True)


def _hy_filters(l, f_w1, f_b1, f_freq1, f_w2, f_b2, f_freq2, f_w3):
    c2 = HY_ORDER * HY_WIDTH
    t = np.linspace(0.0, 1.0, l, dtype=np.float32)[:, None]
    bands = np.linspace(1e-4, HY_BANDS - 1, HY_BANDS, dtype=np.float32)
    ang = np.float32(2.0 * math.pi / l) * np.arange(l, dtype=np.float32)[:, None] * bands
    feats = np.concatenate([t, np.cos(ang), -np.sin(ang)], axis=-1)
    rows = np.arange(2 * l)
    pos = np.where(rows < l, rows, np.where(rows == l, 0, 2 * l - rows))
    feats2 = jnp.asarray(np.pad(feats[pos], ((0, 0), (0, 128 - HY_EMB))))
    keep = jnp.asarray((rows != l).astype(np.float32))[:, None]
    t2 = jnp.asarray(t[pos])
    w1p = jnp.pad(f_w1, ((0, 128 - HY_EMB), (0, 0)))
    w3d = jnp.transpose(f_w3.reshape(HY_HIDDEN, HY_ORDER, 2, HY_WIDTH), (2, 0, 1, 3)).reshape(2, HY_HIDDEN, c2)
    deltas = jnp.abs(jnp.linspace(HY_MIN_DECAY, HY_MAX_DECAY, HY_WIDTH, dtype=F32))
    negd = jnp.tile(-deltas, HY_ORDER)[None, :]
    tr = min(l, 512)
    per_dir = l // tr
    cst = lambda i: (0, 0)
    vec = lambda a: a.reshape(1, HY_HIDDEN)
    return pl.pallas_call(
        _hy_filter_kernel,
        grid=(2 * l // tr,),
        in_specs=[pl.BlockSpec((tr, 128), lambda i: (i, 0)),
                  pl.BlockSpec((tr, 1), lambda i: (i, 0)),
                  pl.BlockSpec((tr, 1), lambda i: (i, 0)),
                  pl.BlockSpec((128, HY_HIDDEN), cst), pl.BlockSpec((1, HY_HIDDEN), cst),
                  pl.BlockSpec((1, HY_HIDDEN), cst),
                  pl.BlockSpec((HY_HIDDEN, HY_HIDDEN), cst), pl.BlockSpec((1, HY_HIDDEN), cst),
                  pl.BlockSpec((1, HY_HIDDEN), cst),
                  pl.BlockSpec((1, HY_HIDDEN, c2), lambda i: (i // per_dir, 0, 0)),
                  pl.BlockSpec((1, c2), cst)],
        out_specs=[pl.BlockSpec((tr, c2), lambda i: (i, 0)), pl.BlockSpec((1, c2), cst)],
        out_shape=[jax.ShapeDtypeStruct((2 * l, c2), F32), jax.ShapeDtypeStruct((1, c2), F32)],
        compiler_params=_cparams("arbitrary"),
        name="hy_filter",
    )(feats2, t2, keep, w1p, vec(f_b1), vec(f_freq1), f_w2, vec(f_b2), vec(f_freq2), w3d, negd)


def _col_scale_kernel(x_ref, sum_ref, o_ref):
    o_ref[...] = x_ref[...] * (1.0 / (sum_ref[...] + 1e-6))


def _hy_spectrum(l, fparams):
    if 2 * l >= 16 * HY_N2:
        tab = _dft2_tables(2 * l // HY_N2, HY_N2)
        a, colsum = _hy_filt_first(l, tab, *fparams)
        return _hy_filt_mid(a, tab, colsum), tab
    filt, colsum = _hy_filters(l, *fparams)
    c2 = filt.shape[1]
    tab = _dft1_tables(l)
    spec = _rowdft(tab['fwd'], filt[None], True)[0]
    spec = pl.pallas_call(
        _col_scale_kernel,
        in_specs=[pl.BlockSpec(spec.shape, lambda: (0, 0)), pl.BlockSpec((1, c2), lambda: (0, 0))],
        out_specs=pl.BlockSpec(spec.shape, lambda: (0, 0)),
        out_shape=jax.ShapeDtypeStruct(spec.shape, F32),
        name="hy_col_scale",
    )(spec, colsum)
    return spec.reshape(2, tab['kp'], c2), tab


def _hyena_mixer(zz, hspec, tab, d_skip):
    conv = _hy_long_conv2 if 'k1p' in tab else _hy_long_conv1
    y = conv(zz, 0, zz, 1, hspec, 0, d_skip[0], tab)
    return conv(y, 0, zz, 2, hspec, 1, d_skip[1], tab)


def _grid_sincos(n, dm):
    rows = n // GRID_W
    quarter = dm // 4
    omega = 1.0 / (10000.0 ** (jnp.arange(quarter, dtype=F32) / quarter))
    ang_r = jnp.arange(rows, dtype=F32)[:, None] * omega
    ang_c = jnp.arange(GRID_W, dtype=F32)[:, None] * omega
    emb_r = jnp.concatenate([jnp.sin(ang_r), jnp.cos(ang_r)], axis=-1)
    emb_c = jnp.concatenate([jnp.sin(ang_c), jnp.cos(ang_c)], axis=-1)
    half = emb_r.shape[-1]
    pe = jnp.concatenate([jnp.broadcast_to(emb_r[:, None, :], (rows, GRID_W, half)),
                          jnp.broadcast_to(emb_c[None, :, :], (rows, GRID_W, half))], axis=-1)
    return pe.reshape(rows * GRID_W, 2 * half)


EV_Z_OFF = M2_XBC
EV_U_OFF = EV_Z_OFF + M2_INNER
EV_DT_OFF = EV_U_OFF + S5_WIDTH
EV_PROJ_W = EV_DT_OFF + 128


def _even_mixer(h_in, p, states):
    x, g, shift, scale = h_in
    b, l, _ = x.shape
    proj = _adaln_mm_conv(x, g, shift, scale, p['w_in'], jnp.zeros((1, EV_PROJ_W), F32),
                          p['m2_conv_w'], p['m2_conv_b'], True)
    s5_h0, m2_h0 = states

    ucol = EV_U_OFF // S5_WIDTH
    yf, yb, s5_fin = _s5_scan(proj, ucol, p['s5_bd'], p['s5_cd'], p['s5_ar'], p['s5_ai'], s5_h0)
    y_s5 = _s5_glu(yf, yb, proj, ucol, p['s5_d'], p['s5_glu_w'], p['s5_glu_b'])

    dt4 = proj[..., EV_DT_OFF:EV_DT_OFF + 2 * M2_HEADS].astype(F32).reshape(b, l, 2, M2_HEADS)
    dt_col = jnp.transpose(dt4, (2, 0, 1, 3))
    dt_row = jnp.transpose(dt4, (2, 0, 3, 1))
    ydir, m2_fin = _ssd(proj, dt_col, dt_row, p['m2_dt_bias'], p['m2_a'], m2_h0)
    y_m2 = _m2_gate(ydir, proj, proj, EV_Z_OFF // M2_INNER, p['m2_d'], p['m2_norm_g'])
    return (y_s5, y_m2), (s5_fin, m2_fin)


def kernel(x, c, ctx, c_ctx, mod_w, mod_b, norm_mix_g, norm_mlp_g, mlp_w1, mlp_w2, final_norm_g, ev_in_w, ev_out_w, s5_lam_re, s5_lam_im, s5_log_dt, s5_b_re, s5_b_im, s5_c_re, s5_c_im, s5_d, s5_glu_w, s5_glu_b, m2_conv_w, m2_conv_b, m2_dt_bias, m2_a_log, m2_d, m2_norm_g, hy_in_w, hy_in_b, hy_conv_w, hy_conv_b, hy_f_w1, hy_f_b1, hy_f_freq1, hy_f_w2, hy_f_b2, hy_f_freq2, hy_f_w3, hy_d, hy_out_w, hy_out_b):
    bsz, n, dm = x.shape
    lc = ctx.shape[1]
    x = _add_pe(x, _grid_sincos(n, dm))

    c8 = jnp.concatenate([c, c_ctx[None], jnp.zeros((3, dm), F32)], axis=0)
    mods = _modulation(c8, mod_w, mod_b).reshape(DEPTH, 8, N_MOD, dm)
    final_g = final_norm_g.reshape(1, dm)

    for i in range(DEPTH):
        j = i // 2
        mx = [mods[i, :bsz, k][:, None, :] for k in range(N_MOD)]
        mc = [jnp.broadcast_to(mods[i, bsz, k][None, None, :], (bsz, 1, dm)) for k in range(N_MOD)]
        g_mix = norm_mix_g[i].reshape(1, dm)
        g_mlp = norm_mlp_g[i].reshape(1, dm)
        ctx_later = any(k % 2 == 0 for k in range(i + 1, DEPTH))
        w1 = mlp_w1[i].astype(BF16)
        w2 = mlp_w2[i].astype(BF16)

        if i % 2 == 0:
            in_w = ev_in_w[j]
            o0, o1, o2 = S5_WIDTH, S5_WIDTH + M2_INNER, S5_WIDTH + M2_INNER + M2_XBC
            bd, cd, ar8, ai8 = _s5_prepare(s5_lam_re[j], s5_lam_im[j], s5_log_dt[j], s5_b_re[j], s5_b_im[j],
                                           s5_c_re[j], s5_c_im[j])
            p = dict(
                w_in=jnp.concatenate([in_w[:, o1:o2], in_w[:, o0:o1], in_w[:, :o0],
                                      jnp.pad(in_w[:, o2:], ((0, 0), (0, 128 - 2 * M2_HEADS)))],
                                     axis=1).astype(BF16),
                s5_bd=bd, s5_cd=cd, s5_ar=ar8, s5_ai=ai8,
                s5_d=s5_d[j].reshape(1, S5_WIDTH), s5_glu_w=s5_glu_w[j].astype(BF16),
                s5_glu_b=s5_glu_b[j].reshape(1, S5_WIDTH),
                m2_conv_w=m2_conv_w[j], m2_conv_b=m2_conv_b[j].reshape(1, M2_XBC),
                m2_dt_bias=m2_dt_bias[j], m2_a=-jnp.exp(m2_a_log[j]),
                m2_d=jnp.repeat(m2_d[j], M2_HEAD_DIM).reshape(1, M2_INNER),
                m2_norm_g=m2_norm_g[j].reshape(1, M2_INNER))
            zero_states = (jnp.zeros((8, S5_LANES), F32),
                           jnp.zeros((2, bsz, M2_GROUPS, M2_STATE, M2_GROUP_W), F32))
            ys_c, ctx_states = _even_mixer((ctx, g_mix, mc[0], mc[1]), p, zero_states)
            ys_x, _ = _even_mixer((x, g_mix, mx[0], mx[1]), p, ctx_states)
            out_ws = [ev_out_w[j][:S5_WIDTH].astype(BF16), ev_out_w[j][S5_WIDTH:].astype(BF16)]
            out_b = jnp.zeros((1, dm), F32)
        else:
            fparams = (hy_f_w1[j], hy_f_b1[j], hy_f_freq1[j], hy_f_w2[j], hy_f_b2[j], hy_f_freq2[j], hy_f_w3[j])
            in_w = hy_in_w[j].astype(BF16)
            in_b = hy_in_b[j].reshape(1, -1)
            conv_b = hy_conv_b[j].reshape(1, -1)
            hspec, tab = _hy_spectrum(n, fparams)
            zz = _adaln_mm_conv(x, g_mix, mx[0], mx[1], in_w, in_b, hy_conv_w[j], conv_b, False)
            ys_x = [_hyena_mixer(zz, hspec, tab, hy_d[j])]
            if ctx_later:
                hspec_c, tab_c = _hy_spectrum(lc, fparams)
                zz_c = _adaln_mm_conv(ctx, g_mix, mc[0], mc[1], in_w, in_b, hy_conv_w[j], conv_b, False)
                ys_c = [_hyena_mixer(zz_c, hspec_c, tab_c, hy_d[j])]
            out_ws = [hy_out_w[j].astype(BF16)]
            out_b = hy_out_b[j].reshape(1, dm)

        x = _mix_mlp(x, mx[2], out_b, ys_x, out_ws, g_mlp, mx[3], mx[4], mx[5], w1, w2, final_g, i == DEPTH - 1)
        if ctx_later:
            ctx = _mix_mlp(ctx, mc[2], out_b, ys_c, out_ws, g_mlp, mc[3], mc[4], mc[5], w1, w2, final_g, False)
    return x
```
